```python
import math, functools
import jax, jax.numpy as jnp
from jax import lax
import numpy as np

D_MODEL = 1024
BATCH = 2
SEQ = 8192
DEPTH = 2

GRID_W = 64
CTX_LEN = 256
DA_HEADS = 4
DA_HEAD_DIM = 64
DA_VDIM = 2 * DA_HEAD_DIM
DA_QK = DA_HEADS * 2 * DA_HEAD_DIM
MLA_HEADS = 8
MLA_NOPE = 64
MLA_ROPE = 32
MLA_V = 64
MLA_Q_RANK = 384
MLA_KV_RANK = 256
HGRN_HEADS = 4
HGRN_DK = 128
HGRN_DV = 128
HGRN_KW = HGRN_HEADS * HGRN_DK
HGRN_VW = HGRN_HEADS * HGRN_DV
SSD_HEADS = 8
SSD_HEAD_DIM = 64
SSD_GROUPS = 2
SSD_STATE = 128
SSD_CONV_W = 5
SSD_INNER = SSD_HEADS * SSD_HEAD_DIM
SSD_BC = SSD_GROUPS * SSD_STATE
SSD_CONV_CH = SSD_INNER + 2 * SSD_BC
MLP_HIDDEN = 4 * D_MODEL
Q_BLOCK = 128
SCAN_CHUNK = 64
ROPE_BASE = 10000.0
NORM_EPS = 1e-6

ATT_SPLITS = (DA_QK, DA_QK, DA_HEADS * DA_VDIM, MLA_Q_RANK, MLA_KV_RANK, MLA_ROPE)
ATT_IN = sum(ATT_SPLITS)
ATT_OUT = DA_HEADS * DA_VDIM + MLA_HEADS * MLA_V
REC_SPLITS = (HGRN_KW, HGRN_KW, HGRN_KW, HGRN_VW, HGRN_VW, SSD_INNER, SSD_CONV_CH, 2 * SSD_HEADS)
REC_IN = sum(REC_SPLITS)
REC_OUT = HGRN_VW + SSD_INNER

kernel_name = 'hybrid_diffattn_mla_hgrn2_ssd_block'


def split_sizes(x, sizes):
    return jnp.split(x, np.cumsum(sizes)[:-1].tolist(), axis=-1)


def rms_norm(x, g):
    xf = x.astype(jnp.float32)
    y = xf * lax.rsqrt(jnp.mean(xf * xf, axis=-1, keepdims=True) + NORM_EPS)
    return (y * g.astype(jnp.float32)).astype(x.dtype)


def modulate(x, g, shift, scale):
    return rms_norm(x, g) * (1 + scale) + shift


def sq_relu_mlp(u, w1, w2):
    return jnp.square(jax.nn.relu(u @ w1)) @ w2


def axial_rope(length, rot_dim):
    rows = length // GRID_W
    row = jnp.repeat(jnp.arange(rows, dtype=jnp.float32), GRID_W)
    col = jnp.tile(jnp.arange(GRID_W, dtype=jnp.float32), rows)
    n_freq = rot_dim // 4
    inv_freq = ROPE_BASE ** (-jnp.arange(n_freq, dtype=jnp.float32) / n_freq)
    ang = jnp.concatenate([row[:, None] * inv_freq, col[:, None] * inv_freq], axis=-1)
    return jnp.cos(ang), jnp.sin(ang)


def apply_rope(x, rope):
    cos, sin = (t.astype(x.dtype) for t in rope)
    half = x.shape[-1] // 2
    x1, x2 = x[..., :half], x[..., half:]
    return jnp.concatenate([x1 * cos - x2 * sin, x1 * sin + x2 * cos], axis=-1)


def block_attention(q, k, v, scale):
    b, m, h, s, dk = q.shape
    qb = q.reshape(b, m, h, s // Q_BLOCK, Q_BLOCK, dk).transpose(3, 0, 1, 2, 4, 5)

    def one_block(qi):
        sc = jnp.einsum('bmhqd,bmhkd->bmhqk', qi, k).astype(jnp.float32) * scale
        p = jax.nn.softmax(sc, axis=-1).astype(v.dtype)
        return jnp.einsum('bmhqk,bhkv->bmhqv', p, v)

    out = lax.map(one_block, qb)
    return out.transpose(1, 2, 3, 0, 4, 5).reshape(b, m, h, s, v.shape[-1])


def att_project(u, w_in, q_norm, w_uq, kv_norm, w_ukv, rope_da, rope_mla):
    b, s, _ = u.shape
    qa, ka, va, cq, ckv, kr = split_sizes(u @ w_in, ATT_SPLITS)
    qa = qa.reshape(b, s, DA_HEADS, 2, DA_HEAD_DIM).transpose(0, 3, 2, 1, 4)
    ka = ka.reshape(b, s, DA_HEADS, 2, DA_HEAD_DIM).transpose(0, 3, 2, 1, 4)
    va = va.reshape(b, s, DA_HEADS, DA_VDIM).transpose(0, 2, 1, 3)
    qm = (rms_norm(cq, q_norm) @ w_uq).reshape(b, s, MLA_HEADS, MLA_NOPE + MLA_ROPE).transpose(0, 2, 1, 3)
    kvm = (rms_norm(ckv, kv_norm) @ w_ukv).reshape(b, s, MLA_HEADS, MLA_NOPE + MLA_V).transpose(0, 2, 1, 3)
    qn, qr = qm[..., :MLA_NOPE], qm[..., MLA_NOPE:]
    kn, vm = kvm[..., :MLA_NOPE], kvm[..., MLA_NOPE:]
    kr = kr[:, None]
    if rope_da is not None:
        qa, ka = apply_rope(qa, rope_da), apply_rope(ka, rope_da)
        qr, kr = apply_rope(qr, rope_mla), apply_rope(kr, rope_mla)
    qm = jnp.concatenate([qn, qr], axis=-1)[:, None]
    km = jnp.concatenate([kn, jnp.broadcast_to(kr, kn.shape[:-1] + (MLA_ROPE,))], axis=-1)[:, None]
    return qa, ka, va, qm, km, vm


def attention_mixer(u_lat, u_ctx, layer, w_in, lam_p, subnorm, q_norm, w_uq, kv_norm, w_ukv, w_out,
                    rope_da, rope_mla, need_ctx):
    lam_init = 0.8 - 0.6 * math.exp(-0.3 * layer)
    lp = lam_p.astype(jnp.float32)
    lam = jnp.exp(jnp.sum(lp[0] * lp[1])) - jnp.exp(jnp.sum(lp[2] * lp[3])) + lam_init
    qa, ka, va, qm, km, vm = att_project(u_lat, w_in, q_norm, w_uq, kv_norm, w_ukv, rope_da, rope_mla)
    qa_c, ka_c, va_c, qm_c, km_c, vm_c = att_project(u_ctx, w_in, q_norm, w_uq, kv_norm, w_ukv, None, None)
    da_scale = DA_HEAD_DIM ** -0.5
    mla_scale = (MLA_NOPE + MLA_ROPE) ** -0.5

    def merge(oa, om):
        b, _, _, s, _ = oa.shape
        ya = rms_norm(oa[:, 0] - lam.astype(oa.dtype) * oa[:, 1], subnorm) * (1 - lam_init)
        ya = ya.transpose(0, 2, 1, 3).reshape(b, s, DA_HEADS * DA_VDIM)
        ym = om[:, 0].transpose(0, 2, 1, 3).reshape(b, s, MLA_HEADS * MLA_V)
        return jnp.concatenate([ya, ym], axis=-1) @ w_out

    oa = block_attention(qa, jnp.concatenate([ka_c, ka], axis=3), jnp.concatenate([va_c, va], axis=2), da_scale)
    om = block_attention(qm, jnp.concatenate([km_c, km], axis=3), jnp.concatenate([vm_c, vm], axis=2), mla_scale)
    y_lat = merge(oa, om)
    y_ctx = None
    if need_ctx:
        y_ctx = merge(block_attention(qa_c, ka_c, va_c, da_scale), block_attention(qm_c, km_c, vm_c, mla_scale))
    return y_lat, y_ctx


def hgrn2_scan(q, k, v, logf, s0):
    dtype = q.dtype
    q, k, v, logf, s0 = (t.astype(jnp.float32) for t in (q, k, v, logf, s0))
    b, L, H, K = q.shape
    V = v.shape[-1]
    nc = L // SCAN_CHUNK

    def chunks(t):
        return t.reshape(b, nc, SCAN_CHUNK, H, t.shape[-1]).transpose(1, 0, 3, 2, 4)

    mask = jnp.tril(jnp.ones((SCAN_CHUNK, SCAN_CHUNK), dtype=bool))[:, :, None]

    def step(S, inp):
        qi, ki, vi, gi = inp
        bc = jnp.cumsum(gi, axis=2)
        rel = bc[:, :, :, None, :] - bc[:, :, None, :, :]
        decay = jnp.exp(jnp.where(mask, rel, -jnp.inf))
        att = jnp.einsum('bhtk,bhtsk,bhsk->bhts', qi, decay, ki)
        o = jnp.einsum('bhts,bhsv->bhtv', att, vi) + jnp.einsum('bhtk,bhkv->bhtv', qi * jnp.exp(bc), S)
        btot = bc[:, :, -1]
        S = jnp.exp(btot)[..., None] * S + jnp.einsum('bhsk,bhsv->bhkv', ki * jnp.exp(btot[:, :, None] - bc), vi)
        return S, o

    S, o = lax.scan(step, s0, (chunks(q), chunks(k), chunks(v), chunks(logf)))
    o = o.transpose(1, 0, 3, 2, 4).reshape(b, L, H, V)
    return S.astype(dtype), o.astype(dtype)


def ssd_scan(x, dt, bm, cm, h0, a_neg):
    dtype = x.dtype
    x, dt, bm, cm, h0, a_neg = (t.astype(jnp.float32) for t in (x, dt, bm, cm, h0, a_neg))
    b, L, H, P = x.shape
    G, N = bm.shape[2], bm.shape[3]
    R = H // G
    nc = L // SCAN_CHUNK
    T = SCAN_CHUNK
    xdt = (x * dt[..., None]).reshape(b, nc, T, G, R, P)
    acum = jnp.cumsum((dt * a_neg).reshape(b, nc, T, G, R).transpose(0, 3, 4, 1, 2), axis=-1)
    bm = bm.reshape(b, nc, T, G, N)
    cm = cm.reshape(b, nc, T, G, N)
    mask = jnp.tril(jnp.ones((T, T), dtype=bool))
    decay = jnp.exp(jnp.where(mask, acum[..., :, None] - acum[..., None, :], -jnp.inf))
    cb = jnp.einsum('bctgn,bcsgn->bgcts', cm, bm)
    y_diag = jnp.einsum('bgrcts,bcsgrp->bctgrp', cb[:, :, None] * decay, xdt)
    w_end = jnp.exp(acum[..., -1:] - acum)
    st = jnp.einsum('bcsgn,bgrcs,bcsgrp->cbgrnp', bm, w_end, xdt)
    a_tot = jnp.moveaxis(jnp.exp(acum[..., -1]), -1, 0)

    def step(h, inp):
        s_c, a_c = inp
        return a_c[..., None, None] * h + s_c, h

    h_fin, h_in = lax.scan(step, h0, (st, a_tot))
    y_off = jnp.einsum('bctgn,cbgrnp,bgrct->bctgrp', cm, h_in, jnp.exp(acum))
    y = (y_diag + y_off).reshape(b, L, H, P)
    return h_fin.astype(dtype), y.astype(dtype)


def bidir(scan_f, scan_b, args_f, args_b, h0_f, h0_b):
    flip = lambda t: jnp.flip(t, axis=1)
    hf, yf = scan_f(*args_f, h0_f)
    hb, yb = scan_b(*[flip(t) for t in args_b], h0_b)
    return yf + flip(yb), hf, hb


def centred_dwconv(x, w, bias):
    pad = SSD_CONV_W // 2
    y = lax.conv_general_dilated(x, w[:, None, :].astype(x.dtype), window_strides=(1,), padding=[(pad, pad)],
                                 dimension_numbers=('NWC', 'WIO', 'NWC'), feature_group_count=x.shape[-1])
    return y + bias


def recurrent_mixer(u_lat, u_ctx, layer, w_in, bound_logits, out_norm, conv_w, conv_b, a_log, dt_bias, skip,
                    ssd_g, w_out, need_ctx):
    gamma = jax.nn.softmax(bound_logits.astype(jnp.float32), axis=0)
    lb = (jnp.cumsum(gamma, axis=0) - gamma[0])[layer]
    lb_f = lb[:HGRN_KW].reshape(HGRN_HEADS, HGRN_DK)
    lb_b = lb[HGRN_KW:].reshape(HGRN_HEADS, HGRN_DK)

    def project(u):
        b, s, _ = u.shape
        q, f_f, f_b, i, g, z, xbc, dt = split_sizes(u @ w_in, REC_SPLITS)
        xbc = jax.nn.silu(centred_dwconv(xbc, conv_w, conv_b))
        xs, bm, cm = split_sizes(xbc, (SSD_INNER, SSD_BC, SSD_BC))
        q = jax.nn.silu(q).reshape(b, s, HGRN_HEADS, HGRN_DK)
        i = i.reshape(b, s, HGRN_HEADS, HGRN_DV)

        def gate_args(f_raw, lower):
            f = lower + (1.0 - lower) * jax.nn.sigmoid(f_raw.astype(jnp.float32).reshape(b, s, HGRN_HEADS, HGRN_DK))
            return (q, 1.0 - f, i, jnp.log(f))

        xs = xs.reshape(b, s, SSD_HEADS, SSD_HEAD_DIM)
        bm = bm.reshape(b, s, SSD_GROUPS, SSD_STATE)
        cm = cm.reshape(b, s, SSD_GROUPS, SSD_STATE)
        ssd_f = (xs, jax.nn.softplus(dt[..., :SSD_HEADS] + dt_bias[0]), bm, cm)
        ssd_b = (xs, jax.nn.softplus(dt[..., SSD_HEADS:] + dt_bias[1]), bm, cm)
        return gate_args(f_f, lb_f), gate_args(f_b, lb_b), ssd_f, ssd_b, (g, z, xs)

    def merge(o, y, extras):
        g, z, xs = extras
        b, s = g.shape[0], g.shape[1]
        o = rms_norm(o, out_norm.reshape(HGRN_HEADS, HGRN_DV)).reshape(b, s, HGRN_VW) * jax.nn.silu(g)
        y = (y + skip[:, None] * xs).reshape(b, s, SSD_INNER) * jax.nn.silu(z)
        y = rms_norm(y.reshape(b, s, SSD_GROUPS, SSD_INNER // SSD_GROUPS),
                     ssd_g.reshape(SSD_GROUPS, SSD_INNER // SSD_GROUPS)).reshape(b, s, SSD_INNER)
        return jnp.concatenate([o, y], axis=-1) @ w_out

    ssd_fwd = functools.partial(ssd_scan, a_neg=-jnp.exp(a_log[0]))
    ssd_bwd = functools.partial(ssd_scan, a_neg=-jnp.exp(a_log[1]))
    hc_f, hc_b, sc_f, sc_b, ex_c = project(u_ctx)
    hl_f, hl_b, sl_f, sl_b, ex_l = project(u_lat)
    b = u_ctx.shape[0]
    zh = jnp.zeros((b, HGRN_HEADS, HGRN_DK, HGRN_DV), u_ctx.dtype)
    zs = jnp.zeros((b, SSD_GROUPS, SSD_HEADS // SSD_GROUPS, SSD_STATE, SSD_HEAD_DIM), u_ctx.dtype)
    oc, st_hf, st_hb = bidir(hgrn2_scan, hgrn2_scan, hc_f, hc_b, zh, zh)
    yc, st_sf, st_sb = bidir(ssd_fwd, ssd_bwd, sc_f, sc_b, zs, zs)
    ol, _, _ = bidir(hgrn2_scan, hgrn2_scan, hl_f, hl_b, st_hf, st_hb)
    yl, _, _ = bidir(ssd_fwd, ssd_bwd, sl_f, sl_b, st_sf, st_sb)
    y_lat = merge(ol, yl, ex_l)
    y_ctx = merge(oc, yc, ex_c) if need_ctx else None
    return y_lat, y_ctx


def setup_inputs(seed: int = 0) -> dict:
    key = jax.random.key(seed)
    ks = iter(jax.random.split(key, 48))
    D = D_MODEL
    n_even, n_odd = (DEPTH + 1) // 2, DEPTH // 2

    def nrm(shape, scale):
        return scale * jax.random.normal(next(ks), shape, jnp.float32)

    def gain(shape):
        return 1.0 + nrm(shape, 0.02)

    inp = {}
    inp['x'] = nrm((BATCH, SEQ, D), 1.0)
    inp['c'] = nrm((BATCH, D), 1.0)
    inp['ctx'] = nrm((BATCH, CTX_LEN, D), 1.0)
    inp['c_ctx'] = nrm((D,), 1.0)
    inp['w_mod'] = nrm((DEPTH, D, 6 * D), 0.5 * D ** -0.5)
    inp['b_mod'] = nrm((DEPTH, 6 * D), 0.01)
    inp['norm_mix'] = gain((DEPTH, D))
    inp['norm_mlp'] = gain((DEPTH, D))
    inp['w_mlp_in'] = nrm((DEPTH, D, MLP_HIDDEN), D ** -0.5)
    inp['w_mlp_out'] = nrm((DEPTH, MLP_HIDDEN, D), MLP_HIDDEN ** -0.5)
    inp['att_w_in'] = nrm((n_even, D, ATT_IN), D ** -0.5)
    inp['att_lambda'] = nrm((n_even, 4, DA_HEAD_DIM), 0.1)
    inp['att_subnorm'] = gain((n_even, DA_VDIM))
    inp['mla_q_norm'] = gain((n_even, MLA_Q_RANK))
    inp['mla_w_uq'] = nrm((n_even, MLA_Q_RANK, MLA_HEADS * (MLA_NOPE + MLA_ROPE)), MLA_Q_RANK ** -0.5)
    inp['mla_kv_norm'] = gain((n_even, MLA_KV_RANK))
    inp['mla_w_ukv'] = nrm((n_even, MLA_KV_RANK, MLA_HEADS * (MLA_NOPE + MLA_V)), MLA_KV_RANK ** -0.5)
    inp['att_w_out'] = nrm((n_even, ATT_OUT, D), ATT_OUT ** -0.5)
    inp['rec_w_in'] = nrm((n_odd, D, REC_IN), D ** -0.5)
    inp['hgrn_bound_logits'] = nrm((DEPTH, 2 * HGRN_KW), 0.5)
    inp['hgrn_out_norm'] = gain((n_odd, HGRN_VW))
    inp['ssd_conv_w'] = nrm((n_odd, SSD_CONV_W, SSD_CONV_CH), SSD_CONV_W ** -0.5)
    inp['ssd_conv_b'] = nrm((n_odd, SSD_CONV_CH), 0.01)
    inp['ssd_a_log'] = jnp.log(jax.random.uniform(next(ks), (n_odd, 2, SSD_HEADS), jnp.float32, 1.0, 16.0))
    dt0 = jnp.exp(jax.random.uniform(next(ks), (n_odd, 2, SSD_HEADS), jnp.float32, math.log(1e-3), math.log(1e-1)))
    inp['ssd_dt_bias'] = dt0 + jnp.log(-jnp.expm1(-dt0))
    inp['ssd_skip'] = 1.0 + nrm((n_odd, SSD_HEADS), 0.1)
    inp['ssd_norm'] = gain((n_odd, SSD_INNER))
    inp['rec_w_out'] = nrm((n_odd, REC_OUT, D), REC_OUT ** -0.5)
    inp['final_norm'] = gain((D,))
    return inp


def reference(x, c, ctx, c_ctx, w_mod, b_mod, norm_mix, norm_mlp, w_mlp_in, w_mlp_out,
              att_w_in, att_lambda, att_subnorm, mla_q_norm, mla_w_uq, mla_kv_norm, mla_w_ukv, att_w_out,
              rec_w_in, hgrn_bound_logits, hgrn_out_norm, ssd_conv_w, ssd_conv_b, ssd_a_log, ssd_dt_bias,
              ssd_skip, ssd_norm, rec_w_out, final_norm):
    L = x.shape[1]
    rope_da = axial_rope(L, DA_HEAD_DIM)
    rope_mla = axial_rope(L, MLA_ROPE)
    h_lat, h_ctx = x, ctx
    for l in range(DEPTH):
        last = l == DEPTH - 1
        j = l // 2
        mod_lat = [m[:, None, :] for m in jnp.split(jax.nn.silu(c) @ w_mod[l] + b_mod[l], 6, axis=-1)]
        mod_ctx = jnp.split(jax.nn.silu(c_ctx) @ w_mod[l] + b_mod[l], 6, axis=-1)
        u_lat = modulate(h_lat, norm_mix[l], mod_lat[0], mod_lat[1])
        u_ctx = modulate(h_ctx, norm_mix[l], mod_ctx[0], mod_ctx[1])
        if l % 2 == 0:
            y_lat, y_ctx = attention_mixer(u_lat, u_ctx, l, att_w_in[j], att_lambda[j], att_subnorm[j],
                                           mla_q_norm[j], mla_w_uq[j], mla_kv_norm[j], mla_w_ukv[j], att_w_out[j],
                                           rope_da, rope_mla, not last)
        else:
            y_lat, y_ctx = recurrent_mixer(u_lat, u_ctx, l, rec_w_in[j], hgrn_bound_logits, hgrn_out_norm[j],
                                           ssd_conv_w[j], ssd_conv_b[j], ssd_a_log[j], ssd_dt_bias[j],
                                           ssd_skip[j], ssd_norm[j], rec_w_out[j], not last)
        h_lat = h_lat + mod_lat[2] * y_lat
        h_lat = h_lat + mod_lat[5] * sq_relu_mlp(modulate(h_lat, norm_mlp[l], mod_lat[3], mod_lat[4]),
                                                 w_mlp_in[l], w_mlp_out[l])
        if not last:
            h_ctx = h_ctx + mod_ctx[2] * y_ctx
            h_ctx = h_ctx + mod_ctx[5] * sq_relu_mlp(modulate(h_ctx, norm_mlp[l], mod_ctx[3], mod_ctx[4]),
                                                     w_mlp_in[l], w_mlp_out[l])
    return rms_norm(h_lat, final_norm)
```

```python
import functools
import math

import numpy as np
import jax
import jax.numpy as jnp
from jax import lax
from jax.experimental import pallas as pl
from jax.experimental.pallas import tpu as pltpu

F32 = jnp.float32
BF16 = jnp.bfloat16

D_MODEL = 1024
GRID_W = 64
DA_HEADS = 4
DA_HEAD_DIM = 64
MLA_HEADS = 8
MLA_NOPE = 64
MLA_ROPE = 32
MLA_V = 64
MLA_Q_RANK = 384
MLA_KV_RANK = 256
HGRN_HEADS = 4
HGRN_DK = 128
SSD_HEADS = 8
SSD_HEAD_DIM = 64
SSD_GROUPS = 2
SSD_STATE = 128
SSD_CONV_W = 5
SSD_INNER = SSD_HEADS * SSD_HEAD_DIM
MLP_HIDDEN = 4 * D_MODEL
ROPE_BASE = 10000.0
NORM_EPS = 1e-6
CHUNK = 64

LANES = 128
VMEM_LIMIT = 56 * 1024 * 1024


def _cparams(sem):
    return pltpu.CompilerParams(dimension_semantics=sem, vmem_limit_bytes=VMEM_LIMIT)


def _rms(x):
    return x * lax.rsqrt(jnp.mean(x * x, axis=-1, keepdims=True) + NORM_EPS)


def _modulate(x, g, shift, scale):
    return (_rms(x) * g) * (1.0 + scale) + shift


def _silu(x):
    return x * jax.nn.sigmoid(x)


def _bdot(a, b):
    return jnp.dot(a.astype(BF16), b.astype(BF16), preferred_element_type=F32)


def _dot_nt(a, b):
    return lax.dot_general(a, b, (((1,), (1,)), ((), ())), preferred_element_type=F32)


def _dot_tn(a, b):
    return lax.dot_general(a, b, (((0,), (0,)), ((), ())), preferred_element_type=F32)


def _split2(x):
    hi = x.astype(BF16)
    lo = (x - hi.astype(F32)).astype(BF16)
    return hi, lo


def _exact_left(m01, x):
    hi, lo = _split2(x)
    return (jnp.dot(m01, hi, preferred_element_type=F32) + jnp.dot(m01, lo, preferred_element_type=F32))


def _exact_right(x, m01):
    hi, lo = _split2(x)
    return (jnp.dot(hi, m01, preferred_element_type=F32) + jnp.dot(lo, m01, preferred_element_type=F32))


def _mod_kernel(c_ref, w_ref, b_ref, o_ref):
    a = _silu(c_ref[...]).astype(BF16)
    o_ref[...] = jnp.dot(a, w_ref[...].astype(BF16), preferred_element_type=F32) + b_ref[...]


def _mod_vectors(cc, w_mod, b_mod):
    depth, d, n = w_mod.shape
    tn = 1024
    return pl.pallas_call(
        _mod_kernel,
        grid=(depth, n // tn),
        in_specs=[
            pl.BlockSpec((8, d), lambda l, j: (0, 0)),
            pl.BlockSpec((None, d, tn), lambda l, j: (l, 0, j)),
            pl.BlockSpec((None, 1, tn), lambda l, j: (l, 0, j)),
        ],
        out_specs=pl.BlockSpec((None, 8, tn), lambda l, j: (l, 0, j)),
        out_shape=jax.ShapeDtypeStruct((depth, 8, n), F32),
        compiler_params=_cparams(("arbitrary", "arbitrary")),
        name="mod_vectors",
    )(cc, w_mod, b_mod.reshape(depth, 1, n))


def _rope_partner(x, half):
    n = x.shape[-1]
    lane = lax.broadcasted_iota(jnp.int32, x.shape, x.ndim - 1)
    up = pltpu.roll(x, n - half, x.ndim - 1)
    dn = pltpu.roll(x, half, x.ndim - 1)
    return jnp.where((lane % (2 * half)) < half, up, dn)


def _tile_lanes(x, n):
    return jnp.concatenate([x] * n, axis=-1)


def _proj0_kernel(use_rope, *refs):
    if use_rope:
        (h_ref, mod_ref, g_ref, win_ref, qn_ref, wuq_ref, kvn_ref, wk_ref, wv_ref,
         ca_ref, sa_ref, cq_ref, sq_ref,
         qa_ref, ka_ref, va_ref, qm_ref, km_ref, vm_ref) = refs
    else:
        (h_ref, mod_ref, g_ref, win_ref, qn_ref, wuq_ref, kvn_ref, wk_ref, wv_ref,
         qa_ref, ka_ref, va_ref, qm_ref, km_ref, vm_ref) = refs
    mod = mod_ref[...]
    ub = _modulate(h_ref[...], g_ref[...], mod[0:1], mod[1:2]).astype(BF16)

    def proj(lo, hi):
        return jnp.dot(ub, win_ref[:, lo:hi], preferred_element_type=F32)

    qa = proj(0, 512) * (DA_HEAD_DIM ** -0.5)
    ka = proj(512, 1024)
    va_ref[...] = proj(1024, 1536).astype(BF16)
    cq = proj(1536, 1920)
    ckv = proj(1920, 2176)
    kr = proj(2176, 2304)
    cqn = (_rms(cq) * qn_ref[...]).astype(BF16)
    qm = jnp.dot(cqn, wuq_ref[...], preferred_element_type=F32) * ((MLA_NOPE + MLA_ROPE) ** -0.5)
    ckvn = (_rms(ckv) * kvn_ref[...]).astype(BF16)
    kn = jnp.dot(ckvn, wk_ref[...], preferred_element_type=F32)
    vm_ref[...] = jnp.dot(ckvn, wv_ref[...], preferred_element_type=F32).astype(BF16)
    if use_rope:
        ca = _tile_lanes(ca_ref[...], DA_HEADS)
        sa = _tile_lanes(sa_ref[...], DA_HEADS)
        qa = qa * ca + _rope_partner(qa, DA_HEAD_DIM // 2) * sa
        ka = ka * ca + _rope_partner(ka, DA_HEAD_DIM // 2) * sa
        cq1 = cq_ref[...]
        sq1 = sq_ref[...]
        qm = qm * _tile_lanes(cq1, MLA_HEADS) + _rope_partner(qm, MLA_ROPE // 2) * _tile_lanes(sq1, MLA_HEADS)
        kr = kr * cq1 + _rope_partner(kr, MLA_ROPE // 2) * sq1
    qa_ref[...] = qa.astype(BF16)
    ka_ref[...] = ka.astype(BF16)
    qm_ref[...] = qm.astype(BF16)
    km_ref[...] = (kn + _tile_lanes(kr, MLA_HEADS)).astype(BF16)


def _proj0(h, mods, gain, w, tables, tr):
    g_, r_, d = h.shape
    use_rope = tables is not None
    row = lambda c: pl.BlockSpec((None, tr, c), lambda b, t: (b, t, 0))
    full = lambda a: pl.BlockSpec(a.shape, lambda b, t: (0,) * a.ndim)
    in_specs = [row(d), pl.BlockSpec((None, 8, d), lambda b, t: (b, 0, 0)), full(gain),
                full(w["win"]), full(w["qn"]), full(w["wuq"]), full(w["kvn"]), full(w["wk"]), full(w["wv"])]
    args = [h, mods, gain, w["win"], w["qn"], w["wuq"], w["kvn"], w["wk"], w["wv"]]
    if use_rope:
        in_specs += [pl.BlockSpec((tr, LANES), lambda b, t: (t, 0))] * 4
        args += list(tables)
    widths = (512, 512, 512, 1024, 1024, 512)
    return pl.pallas_call(
        functools.partial(_proj0_kernel, use_rope),
        grid=(g_, r_ // tr),
        in_specs=in_specs,
        out_specs=[row(c) for c in widths],
        out_shape=[jax.ShapeDtypeStruct((g_, r_, c), BF16) for c in widths],
        compiler_params=_cparams(("arbitrary", "arbitrary")),
        name="proj0_rope" if use_rope else "proj0_ctx",
    )(*args)


def _attn_kernel(mode, has_lat, tk, nk, lam_init, *refs):
    refs = list(refs)
    q_ref, kc_ref, vc_ref = refs[:3]
    refs = refs[3:]
    if has_lat:
        kl_ref, vl_ref = refs[:2]
        refs = refs[2:]
    if mode == "da":
        lam_ref, sub_ref = refs[:2]
        refs = refs[2:]
    o_ref, ma_ref, la_ref, acca_ref, mb_ref, lb_ref, accb_ref = refs

    q = q_ref[...]
    if mode == "da":
        lane = lax.broadcasted_iota(jnp.int32, q.shape, 1)
        zero = jnp.zeros_like(q)
        qs = (jnp.where(lane < DA_HEAD_DIM, q, zero), jnp.where(lane >= DA_HEAD_DIM, q, zero))
    else:
        qs = (q[:, :LANES], q[:, LANES:])
    stats = ((ma_ref, la_ref, acca_ref), (mb_ref, lb_ref, accb_ref))
    for m_ref, l_ref, acc_ref in stats:
        m_ref[...] = jnp.full(m_ref.shape, -jnp.inf, F32)
        l_ref[...] = jnp.zeros(l_ref.shape, F32)
        acc_ref[...] = jnp.zeros(acc_ref.shape, F32)

    def step(k, v):
        ks = (k, k) if mode == "da" else (k[:, :LANES], k[:, LANES:])
        for i in range(2):
            m_ref, l_ref, acc_ref = stats[i]
            s = _dot_nt(qs[i], ks[i])
            m_prev = m_ref[...]
            m_new = jnp.maximum(m_prev, jnp.max(s, axis=1, keepdims=True))
            alpha = jnp.exp(m_prev - m_new)
            p = jnp.exp(s - m_new)
            l_ref[...] = alpha * l_ref[...] + jnp.sum(p, axis=1, keepdims=True)
            acc_ref[...] = alpha * acc_ref[...] + jnp.dot(p.astype(BF16), v, preferred_element_type=F32)
            m_ref[...] = m_new

    step(kc_ref[...], vc_ref[...])
    if has_lat:
        def body(i, carry):
            off = pl.multiple_of(i * tk, tk)
            step(kl_ref[pl.ds(off, tk), :], vl_ref[pl.ds(off, tk), :])
            return carry

        lax.fori_loop(0, nk, body, 0)

    oa = acca_ref[...] / la_ref[...]
    ob = accb_ref[...] / lb_ref[...]
    if mode == "da":
        lp = lam_ref[...]
        lam = (jnp.exp(jnp.sum(lp[0:1] * lp[1:2], axis=1, keepdims=True))
               - jnp.exp(jnp.sum(lp[2:3] * lp[3:4], axis=1, keepdims=True)) + lam_init)
        dlt = oa - lam * ob
        o_ref[...] = ((_rms(dlt) * sub_ref[...]) * (1.0 - lam_init)).astype(o_ref.dtype)
    else:
        lane = lax.broadcasted_iota(jnp.int32, oa.shape, 1)
        o_ref[...] = jnp.where(lane < MLA_V, oa, ob).astype(o_ref.dtype)


def _attention(mode, q, kc, vc, kl, vl, extra, tq, tk, lam_init):
    b_, sq, _ = q.shape
    has_lat = kl is not None
    w = LANES if mode == "da" else 2 * LANES
    nh = 4
    blk = lambda rows, c: pl.BlockSpec((None, rows, c), lambda b, h, t: (b, 0, h))
    in_specs = [pl.BlockSpec((None, tq, w), lambda b, h, t: (b, t, h)),
                blk(kc.shape[1], w), blk(vc.shape[1], LANES)]
    args = [q, kc, vc]
    nk = 0
    if has_lat:
        in_specs += [blk(kl.shape[1], w), blk(vl.shape[1], LANES)]
        args += [kl, vl]
        nk = kl.shape[1] // tk
    if mode == "da":
        lam_p, subnorm = extra
        in_specs += [pl.BlockSpec(lam_p.shape, lambda b, h, t: (0, 0)),
                     pl.BlockSpec(subnorm.shape, lambda b, h, t: (0, 0))]
        args += [lam_p, subnorm]
    scratch = []
    for _ in range(2):
        scratch += [pltpu.VMEM((tq, 1), F32), pltpu.VMEM((tq, 1), F32), pltpu.VMEM((tq, LANES), F32)]
    return pl.pallas_call(
        functools.partial(_attn_kernel, mode, has_lat, tk, nk, lam_init),
        grid=(b_, nh, sq // tq),
        in_specs=in_specs,
        out_specs=pl.BlockSpec((None, tq, LANES), lambda b, h, t: (b, t, h)),
        out_shape=jax.ShapeDtypeStruct((b_, sq, nh * LANES), BF16),
        scratch_shapes=scratch,
        compiler_params=_cparams(("arbitrary", "arbitrary", "arbitrary")),
        name=f"attn_{mode}_{'lat' if has_lat else 'ctx'}",
    )(*args)


def _outproj0_kernel(ya_ref, ym_ref, h_ref, mod_ref, wa_ref, wm_ref, o_ref):
    y = (jnp.dot(ya_ref[...], wa_ref[...], preferred_element_type=F32)
         + jnp.dot(ym_ref[...], wm_ref[...], preferred_element_type=F32))
    o_ref[...] = h_ref[...] + mod_ref[2:3, :] * y


def _outproj0(ya, ym, h, mods, wa, wm, tr):
    g_, r_, d = h.shape
    row = lambda c: pl.BlockSpec((None, tr, c), lambda b, t: (b, t, 0))
    full = lambda a: pl.BlockSpec(a.shape, lambda b, t: (0,) * a.ndim)
    return pl.pallas_call(
        _outproj0_kernel,
        grid=(g_, r_ // tr),
        in_specs=[row(512), row(512), row(d), pl.BlockSpec((None, 8, d), lambda b, t: (b, 0, 0)),
                  full(wa), full(wm)],
        out_specs=row(d),
        out_shape=jax.ShapeDtypeStruct(h.shape, F32),
        compiler_params=_cparams(("arbitrary", "arbitrary")),
        name="outproj0",
    )(ya, ym, h, mods, wa, wm)


def _mlp_kernel(final, *refs):
    if final:
        h_ref, mod_ref, g_ref, w1_ref, w2_ref, fg_ref, o_ref, u_ref, acc_ref = refs
    else:
        h_ref, mod_ref, g_ref, w1_ref, w2_ref, o_ref, u_ref, acc_ref = refs
    j = pl.program_id(2)

    @pl.when(j == 0)
    def _():
        u_ref[...] = _modulate(h_ref[...], g_ref[...], mod_ref[3:4, :], mod_ref[4:5, :]).astype(BF16)
        acc_ref[...] = jnp.zeros(acc_ref.shape, F32)

    a = jnp.dot(u_ref[...], w1_ref[...], preferred_element_type=F32)
    a = jnp.square(jnp.maximum(a, 0.0)).astype(BF16)
    acc_ref[...] += jnp.dot(a, w2_ref[...], preferred_element_type=F32)

    @pl.when(j == pl.num_programs(2) - 1)
    def _():
        out = h_ref[...] + mod_ref[5:6, :] * acc_ref[...]
        if final:
            out = _rms(out) * fg_ref[...]
        o_ref[...] = out


def _mlp(h, mods, gain, w1, w2, final_gain, tr, th=1024):
    g_, r_, d = h.shape
    hid = w1.shape[1]
    final = final_gain is not None
    row = pl.BlockSpec((None, tr, d), lambda b, t, j: (b, t, 0))
    in_specs = [row, pl.BlockSpec((None, 8, d), lambda b, t, j: (b, 0, 0)),
                pl.BlockSpec(gain.shape, lambda b, t, j: (0, 0)),
                pl.BlockSpec((d, th), lambda b, t, j: (0, j)),
                pl.BlockSpec((th, d), lambda b, t, j: (j, 0))]
    args = [h, mods, gain, w1, w2]
    if final:
        in_specs.append(pl.BlockSpec(final_gain.shape, lambda b, t, j: (0, 0)))
        args.append(final_gain)
    return pl.pallas_call(
        functools.partial(_mlp_kernel, final),
        grid=(g_, r_ // tr, hid // th),
        in_specs=in_specs,
        out_specs=row,
        out_shape=jax.ShapeDtypeStruct(h.shape, F32),
        scratch_shapes=[pltpu.VMEM((tr, d), BF16), pltpu.VMEM((tr, d), F32)],
        compiler_params=_cparams(("arbitrary", "arbitrary", "arbitrary")),
        name="mlp_final" if final else "mlp",
    )(*args)


def _softplus(x):
    return jnp.maximum(x, 0.0) + jnp.log1p(jnp.exp(-jnp.abs(x)))


def _proj1_kernel(h_ref, mod_ref, g_ref, w_ref, wdt_ref, wdtt_ref, bl_ref, dtb_ref, dtbt_ref,
                  q_ref, gf_ref, gb_ref, i_ref, sg_ref, sz_ref, xbc_ref, dt_ref, dtt_ref):
    mod = mod_ref[...]
    ub = _modulate(h_ref[...], g_ref[...], mod[0:1], mod[1:2]).astype(BF16)

    def proj(lo, hi):
        return jnp.dot(ub, w_ref[:, lo:hi], preferred_element_type=F32)

    bl = bl_ref[...]
    e = jnp.exp(bl - jnp.max(bl, axis=0, keepdims=True))
    gamma = e / jnp.sum(e, axis=0, keepdims=True)
    lb = (gamma[0:1] + gamma[1:2]) - gamma[0:1]
    q_ref[...] = _silu(proj(0, 512)).astype(BF16)
    for k, out in ((0, gf_ref), (1, gb_ref)):
        lbk = lb[:, 512 * k:512 * (k + 1)]
        f = lbk + (1.0 - lbk) * jax.nn.sigmoid(proj(512 * (k + 1), 512 * (k + 2)))
        out[...] = jnp.log(f)
    i_ref[...] = proj(1536, 2048).astype(BF16)
    sg_ref[...] = _silu(proj(2048, 2560))
    sz_ref[...] = _silu(proj(2560, 3072))
    xbc_ref[...] = proj(3072, 4096)
    dt_ref[...] = _softplus(jnp.dot(ub, wdt_ref[...], preferred_element_type=F32) + dtb_ref[...])
    dtt_ref[...] = _softplus(_dot_nt(wdtt_ref[...], ub) + dtbt_ref[...])


def _proj1(h, mods, gain, w, tr):
    g_, r_, d = h.shape
    row = lambda c: pl.BlockSpec((None, tr, c), lambda b, t: (b, t, 0))
    full = lambda a: pl.BlockSpec(a.shape, lambda b, t: (0,) * a.ndim)
    names = ("w", "wdt", "wdtt", "bl", "dtb", "dtbt")
    widths = (512, 512, 512, 512, 512, 512, 1024, 2 * LANES)
    dts = (BF16, F32, F32, BF16, F32, F32, F32, F32)
    return pl.pallas_call(
        _proj1_kernel,
        grid=(g_, r_ // tr),
        in_specs=[row(d), pl.BlockSpec((None, 8, d), lambda b, t: (b, 0, 0)), full(gain)]
        + [full(w[n]) for n in names],
        out_specs=[row(c) for c in widths] + [pl.BlockSpec((None, 16, tr), lambda b, t: (b, 0, t))],
        out_shape=[jax.ShapeDtypeStruct((g_, r_, c), t_) for c, t_ in zip(widths, dts)]
        + [jax.ShapeDtypeStruct((g_, 16, r_), F32)],
        compiler_params=_cparams(("arbitrary", "arbitrary")),
        name="proj1",
    )(h, mods, gain, *[w[n] for n in names])


def _conv_kernel(nt, x_ref, p_ref, n_ref, w_ref, b_ref, xs_ref, bm_ref, cm_ref, pad_ref):
    t = pl.program_id(1)
    tr = x_ref.shape[0]
    pad = SSD_CONV_W // 2
    pad_ref[0:8, :] = jnp.where(t > 0, p_ref[...], 0.0)
    pad_ref[8:8 + tr, :] = x_ref[...]
    pad_ref[8 + tr:16 + tr, :] = jnp.where(t < nt - 1, n_ref[...], 0.0)
    y = jnp.zeros(x_ref.shape, F32) + b_ref[...]
    for j in range(SSD_CONV_W):
        y = y + pad_ref[8 - pad + j:8 - pad + j + tr, :] * w_ref[j:j + 1, :]
    y = _silu(y)
    xs_ref[...] = y[:, :SSD_INNER]
    bm_ref[...] = y[:, SSD_INNER:SSD_INNER + 256].astype(BF16)
    cm_ref[...] = y[:, SSD_INNER + 256:].astype(BF16)


def _conv(xbc, w, b, tr):
    g_, r_, c = xbc.shape
    nt = r_ // tr
    nb = tr // 8
    row = lambda cc: pl.BlockSpec((None, tr, cc), lambda g, t: (g, t, 0))
    return pl.pallas_call(
        functools.partial(_conv_kernel, nt),
        grid=(g_, nt),
        in_specs=[row(c),
                  pl.BlockSpec((None, 8, c), lambda g, t: (g, jnp.maximum(t * nb - 1, 0), 0)),
                  pl.BlockSpec((None, 8, c), lambda g, t: (g, jnp.minimum((t + 1) * nb, nt * nb - 1), 0)),
                  pl.BlockSpec(w.shape, lambda g, t: (0, 0)),
                  pl.BlockSpec(b.shape, lambda g, t: (0, 0))],
        out_specs=[row(512), row(256), row(256)],
        out_shape=[jax.ShapeDtypeStruct((g_, r_, 512), F32),
                   jax.ShapeDtypeStruct((g_, r_, 256), BF16),
                   jax.ShapeDtypeStruct((g_, r_, 256), BF16)],
        scratch_shapes=[pltpu.VMEM((tr + 16, c), F32)],
        compiler_params=_cparams(("arbitrary", "arbitrary")),
        name="ssd_conv",
    )(xbc, xbc, xbc, w, b)


def _tri(rev, strict=False):
    i = np.arange(CHUNK)
    if rev:
        m = i[None, :] > i[:, None] if strict else i[None, :] >= i[:, None]
    else:
        m = i[None, :] < i[:, None] if strict else i[None, :] <= i[:, None]
    return m


def _ssd_consts(rev):
    tri = _tri(rev).astype(np.float32)
    expand = np.zeros((LANES, SSD_INNER), np.float32)
    for h in range(SSD_HEADS):
        expand[h, h * SSD_HEAD_DIM:(h + 1) * SSD_HEAD_DIM] = 1.0
    return jnp.asarray(tri, BF16), jnp.asarray(tri.T.copy(), BF16), jnp.asarray(expand, BF16)


def _ssd_kernel(rev, nchunk, dir_, x_ref, bm_ref, cm_ref, dt_ref, dtt_ref, an_ref, ant_ref, tri_ref, trit_ref,
                ex_ref, h0_ref, y_ref, hT_ref, h_ref):
    s = pl.program_id(1)

    @pl.when(s == 0)
    def _():
        h_ref[...] = h0_ref[...]

    tri = tri_ref[...]
    trit = trit_ref[...]
    ex = ex_ref[...]
    an = an_ref[...]
    ant = ant_ref[...]
    ti = lax.broadcasted_iota(jnp.int32, (CHUNK, CHUNK), 0)
    si = lax.broadcasted_iota(jnp.int32, (CHUNK, CHUNK), 1)
    mask = (si >= ti) if rev else (si <= ti)
    lane = lax.broadcasted_iota(jnp.int32, (CHUNK, LANES), 1)
    last = 0 if rev else CHUNK - 1
    order = range(nchunk - 1, -1, -1) if rev else range(nchunk)
    for c in order:
        r0 = c * CHUNK
        x = x_ref[r0:r0 + CHUNK, :]
        bm = bm_ref[r0:r0 + CHUNK, :]
        cm = cm_ref[r0:r0 + CHUNK, :]
        dt = dt_ref[r0:r0 + CHUNK, dir_ * LANES:(dir_ + 1) * LANES]
        dtt = dtt_ref[8 * dir_:8 * dir_ + 8, r0:r0 + CHUNK]
        acol = _exact_left(tri, dt * an)
        arow = _exact_right(dtt * ant, trit)
        dte = _exact_right(dt, ex)
        ae = _exact_right(acol, ex)
        ae_last = ae[last:last + 1, :]
        xdt = x * dte
        xw = (xdt * jnp.exp(ae_last - ae)).astype(BF16)
        xdtb = xdt.astype(BF16)
        ys = []
        for g in range(SSD_GROUPS):
            bg = bm[:, g * SSD_STATE:(g + 1) * SSD_STATE]
            cg = cm[:, g * SSD_STATE:(g + 1) * SSD_STATE]
            cb = _dot_nt(cg, bg)
            hs = h_ref[:, g * 256:(g + 1) * 256]
            yoff = jnp.dot(cg, hs.astype(BF16), preferred_element_type=F32) * jnp.exp(ae[:, g * 256:(g + 1) * 256])
            for pr in range(2):
                xp = xdtb[:, g * 256 + pr * LANES:g * 256 + (pr + 1) * LANES]
                res = []
                for k in range(2):
                    hd = g * 4 + pr * 2 + k
                    dmat = acol[:, hd:hd + 1] - arow[hd:hd + 1, :]
                    lmat = jnp.exp(jnp.where(mask, dmat, -jnp.inf))
                    res.append(jnp.dot((cb * lmat).astype(BF16), xp, preferred_element_type=F32))
                ys.append(jnp.where(lane < SSD_HEAD_DIM, res[0], res[1]) + yoff[:, pr * LANES:(pr + 1) * LANES])
            h_ref[:, g * 256:(g + 1) * 256] = (hs * jnp.exp(ae_last[:, g * 256:(g + 1) * 256])
                                               + _dot_tn(bg, xw[:, g * 256:(g + 1) * 256]))
        y_ref[r0:r0 + CHUNK, :] = jnp.concatenate(ys, axis=1)

    @pl.when(s == pl.num_programs(1) - 1)
    def _():
        hT_ref[...] = h_ref[...]


def _ssd_scan(rev, xs, bm, cm, dt, dtt, a_log_d, h0, rb):
    b_, r_, _ = xs.shape
    nblk = r_ // rb
    dir_ = 1 if rev else 0
    tri, trit, ex = _ssd_consts(rev)
    a_neg = -jnp.exp(a_log_d.astype(F32))
    an = jnp.zeros((1, LANES), F32).at[0, :SSD_HEADS].set(a_neg)
    ant = a_neg.reshape(SSD_HEADS, 1)
    blk = (lambda s: nblk - 1 - s) if rev else (lambda s: s)
    row = lambda c: pl.BlockSpec((None, rb, c), lambda b, s: (b, blk(s), 0))
    full = lambda a: pl.BlockSpec(a.shape, lambda b, s: (0,) * a.ndim)
    st = pl.BlockSpec((None, SSD_STATE, SSD_INNER), lambda b, s: (b, 0, 0))
    return pl.pallas_call(
        functools.partial(_ssd_kernel, rev, rb // CHUNK, dir_),
        grid=(b_, nblk),
        in_specs=[row(512), row(256), row(256), row(2 * LANES),
                  pl.BlockSpec((None, 16, rb), lambda b, s: (b, 0, blk(s))),
                  full(an), full(ant), full(tri), full(trit), full(ex), st],
        out_specs=[row(512), st],
        out_shape=[jax.ShapeDtypeStruct((b_, r_, SSD_INNER), F32),
                   jax.ShapeDtypeStruct((b_, SSD_STATE, SSD_INNER), F32)],
        scratch_shapes=[pltpu.VMEM((SSD_STATE, SSD_INNER), F32)],
        compiler_params=_cparams(("arbitrary", "arbitrary")),
        name="ssd_bwd" if rev else "ssd_fwd",
    )(xs, bm, cm, dt, dtt, an, ant, tri, trit, ex, h0)


_HGRN_LEVELS = (64, 32, 16, 8, 4, 2, 1)


def _hgrn_consts(rev):
    i = np.arange(CHUNK)
    before = _tri(rev)
    after_strict = ~before
    mats, masks = [], []
    for c in _HGRN_LEVELS:
        same = (i[:, None] // c) == (i[None, :] // c)
        mats.append(same & before)
        mats.append(same & after_strict)
        if c < CHUNK:
            blk = i // c
            first = (blk % 2 == 1) if rev else (blk % 2 == 0)
            pair = first[None, :] & (~first)[:, None] & ((blk[:, None] // 2) == (blk[None, :] // 2))
            masks.append(pair)
    masks.append(np.eye(CHUNK, dtype=bool))
    m = np.concatenate(mats, axis=0).astype(np.float32)
    mk = np.stack(masks, axis=0).astype(np.float32)
    return jnp.asarray(m, BF16), jnp.asarray(mk, F32)


def _hgrn_kernel(rev, nchunk, q_ref, g_ref, v_ref, m_ref, mk_ref, s0_ref, o_ref, sT_ref, s_ref):
    s = pl.program_id(1)

    @pl.when(s == 0)
    def _():
        s_ref[...] = s0_ref[...]

    mall = m_ref[...]
    nl = len(_HGRN_LEVELS)
    last = 0 if rev else CHUNK - 1
    order = range(nchunk - 1, -1, -1) if rev else range(nchunk)
    for c in order:
        r0 = c * CHUNK
        g = g_ref[r0:r0 + CHUNK, :]
        eall = jnp.exp(_exact_left(mall, g))
        kk = 1.0 - jnp.exp(g)
        qq = q_ref[r0:r0 + CHUNK, :].astype(F32)
        vv = v_ref[r0:r0 + CHUNK, :]
        outs = []
        for h in range(HGRN_HEADS):
            ls = slice(h * LANES, (h + 1) * LANES)
            qh, kh, vh = qq[:, ls], kk[:, ls], vv[:, ls]

            def fac(idx, ls=ls):
                return eall[idx * CHUNK:(idx + 1) * CHUNK, ls]

            att = _dot_nt(qh.astype(BF16), kh.astype(BF16)) * mk_ref[nl - 1]
            for l in range(1, nl):
                qt = (qh * fac(2 * l)).astype(BF16)
                kt = (kh * fac(2 * l + 1)).astype(BF16)
                att = att + _dot_nt(qt, kt) * mk_ref[l - 1]
            st = s_ref[h]
            qs_ = (qh * fac(0)).astype(BF16)
            o = _dot_nt(qs_, st.astype(BF16)) + jnp.dot(att.astype(BF16), vh, preferred_element_type=F32)
            outs.append(o)
            ks_ = (kh * fac(1)).astype(BF16)
            dec = fac(0)[last:last + 1, :]
            s_ref[h] = st * dec + _dot_tn(vh, ks_)
        o_ref[r0:r0 + CHUNK, :] = jnp.concatenate(outs, axis=1)

    @pl.when(s == pl.num_programs(1) - 1)
    def _():
        sT_ref[...] = s_ref[...]


def _hgrn_scan(rev, q, g, v, s0, rb):
    b_, r_, _ = q.shape
    nblk = r_ // rb
    mall, mk = _hgrn_consts(rev)
    blk = (lambda s: nblk - 1 - s) if rev else (lambda s: s)
    row = pl.BlockSpec((None, rb, 512), lambda b, s: (b, blk(s), 0))
    full = lambda a: pl.BlockSpec(a.shape, lambda b, s: (0,) * a.ndim)
    st = pl.BlockSpec((None, HGRN_HEADS, LANES, HGRN_DK), lambda b, s: (b, 0, 0, 0))
    return pl.pallas_call(
        functools.partial(_hgrn_kernel, rev, rb // CHUNK),
        grid=(b_, nblk),
        in_specs=[row, row, row, full(mall), full(mk), st],
        out_specs=[row, st],
        out_shape=[jax.ShapeDtypeStruct((b_, r_, 512), F32),
                   jax.ShapeDtypeStruct((b_, HGRN_HEADS, LANES, HGRN_DK), F32)],
        scratch_shapes=[pltpu.VMEM((HGRN_HEADS, LANES, HGRN_DK), F32)],
        compiler_params=_cparams(("arbitrary", "arbitrary")),
        name="hgrn_bwd" if rev else "hgrn_fwd",
    )(q, g, v, mall, mk, s0)


def _outproj1_kernel(of_ref, ob_ref, yf_ref, yb_ref, sg_ref, sz_ref, xs_ref, h_ref, mod_ref,
                     on_ref, sk_ref, sn_ref, wo_ref, wy_ref, o_ref):
    o = of_ref[...] + ob_ref[...]
    on = on_ref[...]
    parts = []
    for hh in range(HGRN_HEADS):
        ls = slice(hh * LANES, (hh + 1) * LANES)
        parts.append(_rms(o[:, ls]) * on[:, ls])
    o = jnp.concatenate(parts, axis=1) * sg_ref[...]
    y = (yf_ref[...] + yb_ref[...] + sk_ref[...] * xs_ref[...]) * sz_ref[...]
    sn = sn_ref[...]
    gw = SSD_INNER // SSD_GROUPS
    parts = []
    for gg in range(SSD_GROUPS):
        ls = slice(gg * gw, (gg + 1) * gw)
        parts.append(_rms(y[:, ls]) * sn[:, ls])
    y = jnp.concatenate(parts, axis=1)
    out = (jnp.dot(o.astype(BF16), wo_ref[...], preferred_element_type=F32)
           + jnp.dot(y.astype(BF16), wy_ref[...], preferred_element_type=F32))
    o_ref[...] = h_ref[...] + mod_ref[2:3, :] * out


def _outproj1(of, ob, yf, yb, sg, sz, xs, h, mods, on, sk, sn, wo, wy, tr):
    g_, r_, d = h.shape
    row = lambda c: pl.BlockSpec((None, tr, c), lambda b, t: (b, t, 0))
    full = lambda a: pl.BlockSpec(a.shape, lambda b, t: (0,) * a.ndim)
    return pl.pallas_call(
        _outproj1_kernel,
        grid=(g_, r_ // tr),
        in_specs=[row(512)] * 7 + [row(d), pl.BlockSpec((None, 8, d), lambda b, t: (b, 0, 0)),
                                   full(on), full(sk), full(sn), full(wo), full(wy)],
        out_specs=row(d),
        out_shape=jax.ShapeDtypeStruct(h.shape, F32),
        compiler_params=_cparams(("arbitrary", "arbitrary")),
        name="outproj1",
    )(of, ob, yf, yb, sg, sz, xs, h, mods, on, sk, sn, wo, wy)


def _rope_tables(length):
    rows = length // GRID_W
    row = jnp.repeat(jnp.arange(rows, dtype=F32), GRID_W)
    col = jnp.tile(jnp.arange(GRID_W, dtype=F32), rows)

    def cs(rot_dim):
        n_freq = rot_dim // 4
        inv_freq = ROPE_BASE ** (-jnp.arange(n_freq, dtype=F32) / n_freq)
        ang = jnp.concatenate([row[:, None] * inv_freq, col[:, None] * inv_freq], axis=-1)
        c, s = jnp.cos(ang), jnp.sin(ang)
        return jnp.concatenate([c, c], axis=-1), jnp.concatenate([-s, s], axis=-1)

    ca, sa = cs(DA_HEAD_DIM)
    cq, sq = cs(MLA_ROPE)
    ones = jnp.ones((length, MLA_NOPE), F32)
    pad1 = jnp.ones((length, LANES - MLA_NOPE - MLA_ROPE), F32)
    cqt = jnp.concatenate([ones, cq, pad1], axis=-1)
    sqt = jnp.concatenate([0.0 * ones, sq, 0.0 * pad1], axis=-1)
    return jnp.tile(ca, (1, 2)), jnp.tile(sa, (1, 2)), cqt, sqt


def _att_weights(att_w_in, mla_q_norm, mla_w_uq, mla_kv_norm, mla_w_ukv):
    d = att_w_in.shape[0]
    kr = att_w_in[:, 2176:2208]
    kr_blk = jnp.concatenate([jnp.zeros((d, MLA_NOPE), F32), kr,
                              jnp.zeros((d, LANES - MLA_NOPE - MLA_ROPE), F32)], axis=1)
    win = jnp.concatenate([att_w_in[:, :2176], kr_blk], axis=1).astype(BF16)
    wq = mla_w_uq.reshape(MLA_Q_RANK, MLA_HEADS, MLA_NOPE + MLA_ROPE)
    wq = jnp.pad(wq, ((0, 0), (0, 0), (0, LANES - MLA_NOPE - MLA_ROPE))).reshape(MLA_Q_RANK, MLA_HEADS * LANES)
    wkv = mla_w_ukv.reshape(MLA_KV_RANK, MLA_HEADS, MLA_NOPE + MLA_V)
    wk = jnp.pad(wkv[:, :, :MLA_NOPE], ((0, 0), (0, 0), (0, LANES - MLA_NOPE))).reshape(MLA_KV_RANK, MLA_HEADS * LANES)
    wv = wkv[:, :, MLA_NOPE:].reshape(MLA_KV_RANK, MLA_HEADS * MLA_V)
    return dict(win=win, qn=mla_q_norm.reshape(1, -1), wuq=wq.astype(BF16), kvn=mla_kv_norm.reshape(1, -1),
                wk=wk.astype(BF16), wv=wv.astype(BF16))


def _rec_weights(rec_w_in, bound_logits, dt_bias):
    d = rec_w_in.shape[0]
    w = rec_w_in[:, :4096].astype(BF16)
    wdt = rec_w_in[:, 4096:4112]
    pad = jnp.zeros((d, LANES - SSD_HEADS), F32)
    wdt_rows = jnp.concatenate([wdt[:, :SSD_HEADS], pad, wdt[:, SSD_HEADS:], pad], axis=1).astype(BF16)
    zb = jnp.zeros((LANES - SSD_HEADS,), F32)
    dtb = jnp.concatenate([dt_bias[0], zb, dt_bias[1], zb]).reshape(1, 2 * LANES)
    return dict(w=w, wdt=wdt_rows, wdtt=wdt.T.astype(BF16), bl=bound_logits,
                dtb=dtb, dtbt=dt_bias.reshape(2 * SSD_HEADS, 1))


def kernel(x, c, ctx, c_ctx, w_mod, b_mod, norm_mix, norm_mlp, w_mlp_in, w_mlp_out, att_w_in, att_lambda, att_subnorm, mla_q_norm, mla_w_uq, mla_kv_norm, mla_w_ukv, att_w_out, rec_w_in, hgrn_bound_logits, hgrn_out_norm, ssd_conv_w, ssd_conv_b, ssd_a_log, ssd_dt_bias, ssd_skip, ssd_norm, rec_w_out, final_norm):
    b_, length, d = x.shape
    n_ctx = ctx.shape[1]
    assert w_mod.shape[0] == 2 and d == D_MODEL

    cc = jnp.zeros((8, d), F32).at[:b_].set(c).at[b_].set(c_ctx)
    mods = _mod_vectors(cc, w_mod, b_mod)
    mods = jnp.pad(mods.reshape(2, 8, 6, d), ((0, 0), (0, 0), (0, 2), (0, 0)))
    mods_lat = [mods[l, :b_] for l in range(2)]
    mods_ctx = [jnp.broadcast_to(mods[l, b_], (b_, 8, d)) for l in range(2)]
    row = lambda v: v.reshape(1, -1)

    tr = min(512, length)
    trc = min(256, n_ctx)
    w0 = _att_weights(att_w_in[0], mla_q_norm[0], mla_w_uq[0], mla_kv_norm[0], mla_w_ukv[0])
    tables = _rope_tables(length)
    qa, ka, va, qm, km, vm = _proj0(x, mods_lat[0], row(norm_mix[0]), w0, tables, tr)
    qa_c, ka_c, va_c, qm_c, km_c, vm_c = _proj0(ctx, mods_ctx[0], row(norm_mix[0]), w0, None, trc)
    lam_init = 0.8 - 0.6 * math.exp(-0.3 * 0)
    da_extra = (att_lambda[0], row(att_subnorm[0]))
    tq, tk = min(512, length), min(512, length)
    ya = _attention("da", qa, ka_c, va_c, ka, va, da_extra, tq, tk, lam_init)
    ym = _attention("mla", qm, km_c, vm_c, km, vm, None, tq, tk, lam_init)
    ya_c = _attention("da", qa_c, ka_c, va_c, None, None, da_extra, trc, tk, lam_init)
    ym_c = _attention("mla", qm_c, km_c, vm_c, None, None, None, trc, tk, lam_init)
    wout = att_w_out[0].astype(BF16)
    w1 = w_mlp_in.astype(BF16)
    w2 = w_mlp_out.astype(BF16)
    trm = min(1024, length)
    h_lat = _outproj0(ya, ym, x, mods_lat[0], wout[:512], wout[512:], tr)
    h_lat = _mlp(h_lat, mods_lat[0], row(norm_mlp[0]), w1[0], w2[0], None, trm)
    h_ctx = _outproj0(ya_c, ym_c, ctx, mods_ctx[0], wout[:512], wout[512:], trc)
    h_ctx = _mlp(h_ctx, mods_ctx[0], row(norm_mlp[0]), w1[0], w2[0], None, trc)

    w1r = _rec_weights(rec_w_in[0], hgrn_bound_logits, ssd_dt_bias[0])
    cw = ssd_conv_w[0]
    cb = row(ssd_conv_b[0])
    pc = _proj1(h_ctx, mods_ctx[1], row(norm_mix[1]), w1r, trc)
    pl_ = _proj1(h_lat, mods_lat[1], row(norm_mix[1]), w1r, tr)
    xs_c, bm_c, cm_c = _conv(pc[6], cw, cb, trc)
    xs_l, bm_l, cm_l = _conv(pl_[6], cw, cb, tr)
    rb = 256
    zs = jnp.zeros((b_, SSD_STATE, SSD_INNER), F32)
    zh = jnp.zeros((b_, HGRN_HEADS, LANES, HGRN_DK), F32)
    ys, os_ = [], []
    for rev in (False, True):
        dr = 1 if rev else 0
        _, hs = _ssd_scan(rev, xs_c, bm_c, cm_c, pc[7], pc[8], ssd_a_log[0, dr], zs, rb)
        y, _ = _ssd_scan(rev, xs_l, bm_l, cm_l, pl_[7], pl_[8], ssd_a_log[0, dr], hs, rb)
        ys.append(y)
        _, ss = _hgrn_scan(rev, pc[0], pc[1 + dr], pc[3], zh, rb)
        o, _ = _hgrn_scan(rev, pl_[0], pl_[1 + dr], pl_[3], ss, rb)
        os_.append(o)
    skip = row(jnp.repeat(ssd_skip[0], SSD_HEAD_DIM))
    wro = rec_w_out[0].astype(BF16)
    h_lat = _outproj1(os_[0], os_[1], ys[0], ys[1], pl_[4], pl_[5], xs_l, h_lat, mods_lat[1],
                      row(hgrn_out_norm[0]), skip, row(ssd_norm[0]), wro[:512], wro[512:], tr)
    return _mlp(h_lat, mods_lat[1], row(norm_mlp[1]), w1[1], w2[1], row(final_norm), trm)
```

```python
import functools
import math

import numpy as np
import jax
import jax.numpy as jnp
from jax import lax
from jax.experimental import pallas as pl
from jax.experimental.pallas import tpu as pltpu

F32 = jnp.float32
BF16 = jnp.bfloat16

D_MODEL = 1024
GRID_W = 64
DA_HEADS = 4
DA_HEAD_DIM = 64
MLA_HEADS = 8
MLA_NOPE = 64
MLA_ROPE = 32
MLA_V = 64
MLA_Q_RANK = 384
MLA_KV_RANK = 256
HGRN_HEADS = 4
HGRN_DK = 128
SSD_HEADS = 8
SSD_HEAD_DIM = 64
SSD_GROUPS = 2
SSD_STATE = 128
SSD_CONV_W = 5
SSD_INNER = SSD_HEADS * SSD_HEAD_DIM
MLP_HIDDEN = 4 * D_MODEL
ROPE_BASE = 10000.0
NORM_EPS = 1e-6
CHUNK = 64

LANES = 128
LOG2E = 1.4426950408889634
VMEM_LIMIT = 56 * 1024 * 1024


def _cparams(sem):
    return pltpu.CompilerParams(dimension_semantics=sem, vmem_limit_bytes=VMEM_LIMIT)


def _rms(x):
    return x * lax.rsqrt(jnp.mean(x * x, axis=-1, keepdims=True) + NORM_EPS)


def _modulate(x, g, shift, scale):
    return (_rms(x) * g) * (1.0 + scale) + shift


def _silu(x):
    return x * jax.nn.sigmoid(x)


def _bdot(a, b):
    return jnp.dot(a.astype(BF16), b.astype(BF16), preferred_element_type=F32)


def _dot_nt(a, b):
    return lax.dot_general(a, b, (((1,), (1,)), ((), ())), preferred_element_type=F32)


def _dot_tn(a, b):
    return lax.dot_general(a, b, (((0,), (0,)), ((), ())), preferred_element_type=F32)


def _split2(x):
    hi = x.astype(BF16)
    lo = (x - hi.astype(F32)).astype(BF16)
    return hi, lo


def _exact_left(m01, x):
    hi, lo = _split2(x)
    return (jnp.dot(m01, hi, preferred_element_type=F32) + jnp.dot(m01, lo, preferred_element_type=F32))


def _exact_right(x, m01):
    hi, lo = _split2(x)
    return (jnp.dot(hi, m01, preferred_element_type=F32) + jnp.dot(lo, m01, preferred_element_type=F32))


def _mod_kernel(c_ref, w_ref, b_ref, o_ref):
    a = _silu(c_ref[...]).astype(BF16)
    o_ref[...] = jnp.dot(a, w_ref[...].astype(BF16), preferred_element_type=F32) + b_ref[...]


def _mod_vectors(cc, w_mod, b_mod):
    depth, d, n = w_mod.shape
    tn = 1024
    return pl.pallas_call(
        _mod_kernel,
        grid=(depth, n // tn),
        in_specs=[
            pl.BlockSpec((8, d), lambda l, j: (0, 0)),
            pl.BlockSpec((None, d, tn), lambda l, j: (l, 0, j)),
            pl.BlockSpec((None, 1, tn), lambda l, j: (l, 0, j)),
        ],
        out_specs=pl.BlockSpec((None, 8, tn), lambda l, j: (l, 0, j)),
        out_shape=jax.ShapeDtypeStruct((depth, 8, n), F32),
        compiler_params=_cparams(("arbitrary", "arbitrary")),
        name="mod_vectors",
    )(cc, w_mod, b_mod.reshape(depth, 1, n))


def _rope_partner(x, half):
    n = x.shape[-1]
    lane = lax.broadcasted_iota(jnp.int32, x.shape, x.ndim - 1)
    up = pltpu.roll(x, n - half, x.ndim - 1)
    dn = pltpu.roll(x, half, x.ndim - 1)
    return jnp.where((lane % (2 * half)) < half, up, dn)


def _tile_lanes(x, n):
    return jnp.concatenate([x] * n, axis=-1)


def _proj0_kernel(use_rope, *refs):
    if use_rope:
        (h_ref, mod_ref, g_ref, win_ref, wva_ref, qn_ref, wuq_ref, kvn_ref, wk_ref, wv_ref,
         ca_ref, sa_ref, cq_ref, sq_ref,
         qa_ref, ka_ref, va_ref, qm_ref, km_ref, vm_ref) = refs
    else:
        (h_ref, mod_ref, g_ref, win_ref, wva_ref, qn_ref, wuq_ref, kvn_ref, wk_ref, wv_ref,
         qa_ref, ka_ref, va_ref, qm_ref, km_ref, vm_ref) = refs
    mod = mod_ref[...]
    ub = _modulate(h_ref[...], g_ref[...], mod[0:1], mod[1:2]).astype(BF16)

    def proj(lo, hi):
        return jnp.dot(ub, win_ref[:, lo:hi], preferred_element_type=F32)

    qa = proj(0, 512) * (DA_HEAD_DIM ** -0.5 * LOG2E)
    ka = proj(512, 1024)
    va_ref[...] = _dot_nt(wva_ref[...], ub).astype(BF16)
    cq = proj(1024, 1408)
    ckv = proj(1408, 1664)
    kr = proj(1664, 1792)
    cqn = (_rms(cq) * qn_ref[...]).astype(BF16)
    qm = jnp.dot(cqn, wuq_ref[...], preferred_element_type=F32) * ((MLA_NOPE + MLA_ROPE) ** -0.5 * LOG2E)
    ckvn = (_rms(ckv) * kvn_ref[...]).astype(BF16)
    kn = jnp.dot(ckvn, wk_ref[...], preferred_element_type=F32)
    vm_ref[...] = _dot_nt(wv_ref[...], ckvn).astype(BF16)
    if use_rope:
        ca = _tile_lanes(ca_ref[...], DA_HEADS)
        sa = _tile_lanes(sa_ref[...], DA_HEADS)
        qa = qa * ca + _rope_partner(qa, DA_HEAD_DIM // 2) * sa
        ka = ka * ca + _rope_partner(ka, DA_HEAD_DIM // 2) * sa
        cq1 = cq_ref[...]
        sq1 = sq_ref[...]
        qm = qm * _tile_lanes(cq1, MLA_HEADS) + _rope_partner(qm, MLA_ROPE // 2) * _tile_lanes(sq1, MLA_HEADS)
        kr = kr * cq1 + _rope_partner(kr, MLA_ROPE // 2) * sq1
    qa_ref[...] = qa.astype(BF16)
    ka_ref[...] = ka.astype(BF16)
    qm_ref[...] = qm.astype(BF16)
    km_ref[...] = (kn + _tile_lanes(kr, MLA_HEADS)).astype(BF16)


def _proj0(h, mods, gain, w, tables, tr):
    g_, r_, d = h.shape
    use_rope = tables is not None
    row = lambda c: pl.BlockSpec((None, tr, c), lambda b, t: (b, t, 0))
    full = lambda a: pl.BlockSpec(a.shape, lambda b, t: (0,) * a.ndim)
    in_specs = [row(d), pl.BlockSpec((None, 8, d), lambda b, t: (b, 0, 0)), full(gain),
                full(w["win"]), full(w["wva"]), full(w["qn"]), full(w["wuq"]), full(w["kvn"]), full(w["wk"]),
                full(w["wv"])]
    args = [h, mods, gain, w["win"], w["wva"], w["qn"], w["wuq"], w["kvn"], w["wk"], w["wv"]]
    if use_rope:
        in_specs += [pl.BlockSpec((tr, LANES), lambda b, t: (t, 0))] * 4
        args += list(tables)
    col = pl.BlockSpec((None, 512, tr), lambda b, t: (b, 0, t))
    widths = (512, 512, None, 1024, 1024, None)
    return pl.pallas_call(
        functools.partial(_proj0_kernel, use_rope),
        grid=(g_, r_ // tr),
        in_specs=in_specs,
        out_specs=[col if c is None else row(c) for c in widths],
        out_shape=[jax.ShapeDtypeStruct((g_, 512, r_) if c is None else (g_, r_, c), BF16) for c in widths],
        compiler_params=_cparams(("arbitrary", "arbitrary")),
        name="proj0_rope" if use_rope else "proj0_ctx",
    )(*args)


def _attn_kernel(mode, has_lat, tk, nk, lam_init, *refs):
    refs = list(refs)
    q_ref, kc_ref, vc_ref = refs[:3]
    refs = refs[3:]
    if has_lat:
        kl_ref, vl_ref = refs[:2]
        refs = refs[2:]
    if mode == "da":
        lam_ref, sub_ref = refs[:2]
        refs = refs[2:]
    o_ref, ma_ref, la_ref, acca_ref, mb_ref, lb_ref, accb_ref = refs[:7]
    if has_lat:
        p_ref, al_ref = refs[7:]

    q = q_ref[...]
    if mode == "da":
        lane = lax.broadcasted_iota(jnp.int32, q.shape, 1)
        zero = jnp.zeros_like(q)
        qs = (jnp.where(lane < DA_HEAD_DIM, q, zero), jnp.where(lane >= DA_HEAD_DIM, q, zero))
    else:
        qs = (q[:, :LANES], q[:, LANES:])
    stats = ((ma_ref, la_ref, acca_ref), (mb_ref, lb_ref, accb_ref))
    for m_ref, l_ref, acc_ref in stats:
        m_ref[...] = jnp.full(m_ref.shape, -jnp.inf, F32)
        l_ref[...] = jnp.zeros(l_ref.shape, F32)
        acc_ref[...] = jnp.zeros(acc_ref.shape, F32)

    def scores(k):
        ks = (k, k) if mode == "da" else (k[:, :LANES], k[:, LANES:])
        return [_dot_nt(ks[i], qs[i]) for i in range(2)]

    def softmax(i, s):
        m_ref, l_ref, _ = stats[i]
        m_prev = m_ref[...]
        m_new = jnp.maximum(m_prev, jnp.max(s, axis=0, keepdims=True))
        alpha = jnp.exp2(m_prev - m_new)
        p = jnp.exp2(s - m_new)
        l_ref[...] = alpha * l_ref[...] + jnp.sum(p, axis=0, keepdims=True)
        m_ref[...] = m_new
        return p.astype(BF16), alpha

    def accumulate(i, vt, p, alpha):
        acc_ref = stats[i][2]
        acc_ref[...] = alpha * acc_ref[...] + jnp.dot(vt, p, preferred_element_type=F32)

    ss = scores(kc_ref[...])
    pa = [softmax(i, ss[i]) for i in range(2)]
    for i in range(2):
        accumulate(i, vc_ref[...], *pa[i])
    if has_lat:
        def chunk(c):
            return pl.ds(c * tk if isinstance(c, int) else pl.multiple_of(c * tk, tk), tk)

        def stage(c_qk, c_pv, slot_in, slot_out):
            ss = scores(kl_ref[chunk(c_qk), :])
            if c_pv is not None:
                vt = vl_ref[:, chunk(c_pv)]
                for i in range(2):
                    accumulate(i, vt, p_ref[slot_in, i], al_ref[slot_in, i])
            for i in range(2):
                p, alpha = softmax(i, ss[i])
                p_ref[slot_out, i] = p
                al_ref[slot_out, i] = alpha

        stage(0, None, None, 0)
        npair = (nk - 1) // 2

        def body(j, carry):
            stage(2 * j + 1, 2 * j, 0, 1)
            stage(2 * j + 2, 2 * j + 1, 1, 0)
            return carry

        lax.fori_loop(0, npair, body, 0)
        last = 2 * npair
        if last < nk - 1:
            stage(nk - 1, nk - 2, 0, 1)
            last, slot = nk - 1, 1
        else:
            slot = 0
        vt = vl_ref[:, chunk(last)]
        for i in range(2):
            accumulate(i, vt, p_ref[slot, i], al_ref[slot, i])

    oa = acca_ref[...] / la_ref[...]
    ob = accb_ref[...] / lb_ref[...]
    if mode == "da":
        lp = lam_ref[...]
        lam = (jnp.exp(jnp.sum(lp[0:1] * lp[1:2], axis=1, keepdims=True))
               - jnp.exp(jnp.sum(lp[2:3] * lp[3:4], axis=1, keepdims=True)) + lam_init)
        dlt = (oa - lam * ob).T
        o_ref[...] = ((_rms(dlt) * sub_ref[...]) * (1.0 - lam_init)).astype(o_ref.dtype)
    else:
        chan = lax.broadcasted_iota(jnp.int32, oa.shape, 0)
        o_ref[...] = jnp.where(chan < MLA_V, oa, ob).T.astype(o_ref.dtype)


def _attention(mode, q, kc, vc, kl, vl, extra, tq, tk, lam_init):
    b_, sq, _ = q.shape
    has_lat = kl is not None
    w = LANES if mode == "da" else 2 * LANES
    nh = 4
    kblk = lambda a: pl.BlockSpec((None, a.shape[1], w), lambda b, h, t: (b, 0, h))
    vblk = lambda a: pl.BlockSpec((None, LANES, a.shape[2]), lambda b, h, t: (b, h, 0))
    in_specs = [pl.BlockSpec((None, tq, w), lambda b, h, t: (b, t, h)), kblk(kc), vblk(vc)]
    args = [q, kc, vc]
    nk = 0
    if has_lat:
        in_specs += [kblk(kl), vblk(vl)]
        args += [kl, vl]
        nk = kl.shape[1] // tk
    if mode == "da":
        lam_p, subnorm = extra
        in_specs += [pl.BlockSpec(lam_p.shape, lambda b, h, t: (0, 0)),
                     pl.BlockSpec(subnorm.shape, lambda b, h, t: (0, 0))]
        args += [lam_p, subnorm]
    scratch = []
    for _ in range(2):
        scratch += [pltpu.VMEM((1, tq), F32), pltpu.VMEM((1, tq), F32), pltpu.VMEM((LANES, tq), F32)]
    if has_lat:
        scratch += [pltpu.VMEM((2, 2, tk, tq), BF16), pltpu.VMEM((2, 2, 1, tq), F32)]
    return pl.pallas_call(
        functools.partial(_attn_kernel, mode, has_lat, tk, nk, lam_init),
        grid=(b_, nh, sq // tq),
        in_specs=in_specs,
        out_specs=pl.BlockSpec((None, tq, LANES), lambda b, h, t: (b, t, h)),
        out_shape=jax.ShapeDtypeStruct((b_, sq, nh * LANES), BF16),
        scratch_shapes=scratch,
        compiler_params=_cparams(("arbitrary", "arbitrary", "arbitrary")),
        name=f"attn_{mode}_{'lat' if has_lat else 'ctx'}",
    )(*args)


def _outproj0_kernel(ya_ref, ym_ref, h_ref, mod_ref, wa_ref, wm_ref, o_ref):
    y = (jnp.dot(ya_ref[...], wa_ref[...], preferred_element_type=F32)
         + jnp.dot(ym_ref[...], wm_ref[...], preferred_element_type=F32))
    o_ref[...] = h_ref[...] + mod_ref[2:3, :] * y


def _outproj0(ya, ym, h, mods, wa, wm, tr):
    g_, r_, d = h.shape
    row = lambda c: pl.BlockSpec((None, tr, c), lambda b, t: (b, t, 0))
    full = lambda a: pl.BlockSpec(a.shape, lambda b, t: (0,) * a.ndim)
    return pl.pallas_call(
        _outproj0_kernel,
        grid=(g_, r_ // tr),
        in_specs=[row(512), row(512), row(d), pl.BlockSpec((None, 8, d), lambda b, t: (b, 0, 0)),
                  full(wa), full(wm)],
        out_specs=row(d),
        out_shape=jax.ShapeDtypeStruct(h.shape, F32),
        compiler_params=_cparams(("arbitrary", "arbitrary")),
        name="outproj0",
    )(ya, ym, h, mods, wa, wm)


def _mlp_kernel(final, *refs):
    if final:
        h_ref, mod_ref, g_ref, w1_ref, w2_ref, fg_ref, o_ref, u_ref, acc_ref = refs
    else:
        h_ref, mod_ref, g_ref, w1_ref, w2_ref, o_ref, u_ref, acc_ref = refs
    j = pl.program_id(2)

    @pl.when(j == 0)
    def _():
        u_ref[...] = _modulate(h_ref[...], g_ref[...], mod_ref[3:4, :], mod_ref[4:5, :]).astype(BF16)
        acc_ref[...] = jnp.zeros(acc_ref.shape, F32)

    a = jnp.dot(u_ref[...], w1_ref[...], preferred_element_type=F32)
    a = jnp.square(jnp.maximum(a, 0.0)).astype(BF16)
    acc_ref[...] += jnp.dot(a, w2_ref[...], preferred_element_type=F32)

    @pl.when(j == pl.num_programs(2) - 1)
    def _():
        out = h_ref[...] + mod_ref[5:6, :] * acc_ref[...]
        if final:
            out = _rms(out) * fg_ref[...]
        o_ref[...] = out


def _mlp(h, mods, gain, w1, w2, final_gain, tr, th=1024):
    g_, r_, d = h.shape
    hid = w1.shape[1]
    final = final_gain is not None
    row = pl.BlockSpec((None, tr, d), lambda b, t, j: (b, t, 0))
    in_specs = [row, pl.BlockSpec((None, 8, d), lambda b, t, j: (b, 0, 0)),
                pl.BlockSpec(gain.shape, lambda b, t, j: (0, 0)),
                pl.BlockSpec((d, th), lambda b, t, j: (0, j)),
                pl.BlockSpec((th, d), lambda b, t, j: (j, 0))]
    args = [h, mods, gain, w1, w2]
    if final:
        in_specs.append(pl.BlockSpec(final_gain.shape, lambda b, t, j: (0, 0)))
        args.append(final_gain)
    return pl.pallas_call(
        functools.partial(_mlp_kernel, final),
        grid=(g_, r_ // tr, hid // th),
        in_specs=in_specs,
        out_specs=row,
        out_shape=jax.ShapeDtypeStruct(h.shape, F32),
        scratch_shapes=[pltpu.VMEM((tr, d), BF16), pltpu.VMEM((tr, d), F32)],
        compiler_params=_cparams(("arbitrary", "arbitrary", "arbitrary")),
        name="mlp_final" if final else "mlp",
    )(*args)


def _softplus(x):
    return jnp.maximum(x, 0.0) + jnp.log1p(jnp.exp(-jnp.abs(x)))


def _proj1_kernel(h_ref, mod_ref, g_ref, w_ref, wdt_ref, wdtt_ref, bl_ref, dtb_ref, dtbt_ref,
                  q_ref, gf_ref, gb_ref, i_ref, sg_ref, sz_ref, xbc_ref, dt_ref, dtt_ref):
    mod = mod_ref[...]
    ub = _modulate(h_ref[...], g_ref[...], mod[0:1], mod[1:2]).astype(BF16)

    def proj(lo, hi):
        return jnp.dot(ub, w_ref[:, lo:hi], preferred_element_type=F32)

    bl = bl_ref[...]
    e = jnp.exp(bl - jnp.max(bl, axis=0, keepdims=True))
    gamma = e / jnp.sum(e, axis=0, keepdims=True)
    lb = (gamma[0:1] + gamma[1:2]) - gamma[0:1]
    q_ref[...] = _silu(proj(0, 512)).astype(BF16)
    for k, out in ((0, gf_ref), (1, gb_ref)):
        lbk = lb[:, 512 * k:512 * (k + 1)]
        f = lbk + (1.0 - lbk) * jax.nn.sigmoid(proj(512 * (k + 1), 512 * (k + 2)))
        out[...] = jnp.log(f)
    i_ref[...] = proj(1536, 2048).astype(BF16)
    sg_ref[...] = _silu(proj(2048, 2560))
    sz_ref[...] = _silu(proj(2560, 3072))
    xbc_ref[...] = proj(3072, 4096)
    dt_ref[...] = _softplus(jnp.dot(ub, wdt_ref[...], preferred_element_type=F32) + dtb_ref[...])
    dtt_ref[...] = _softplus(_dot_nt(wdtt_ref[...], ub) + dtbt_ref[...])


def _proj1(h, mods, gain, w, tr):
    g_, r_, d = h.shape
    row = lambda c: pl.BlockSpec((None, tr, c), lambda b, t: (b, t, 0))
    full = lambda a: pl.BlockSpec(a.shape, lambda b, t: (0,) * a.ndim)
    names = ("w", "wdt", "wdtt", "bl", "dtb", "dtbt")
    widths = (512, 512, 512, 512, 512, 512, 1024, 2 * LANES)
    dts = (BF16, F32, F32, BF16, F32, F32, F32, F32)
    return pl.pallas_call(
        _proj1_kernel,
        grid=(g_, r_ // tr),
        in_specs=[row(d), pl.BlockSpec((None, 8, d), lambda b, t: (b, 0, 0)), full(gain)]
        + [full(w[n]) for n in names],
        out_specs=[row(c) for c in widths] + [pl.BlockSpec((None, 16, tr), lambda b, t: (b, 0, t))],
        out_shape=[jax.ShapeDtypeStruct((g_, r_, c), t_) for c, t_ in zip(widths, dts)]
        + [jax.ShapeDtypeStruct((g_, 16, r_), F32)],
        compiler_params=_cparams(("arbitrary", "arbitrary")),
        name="proj1",
    )(h, mods, gain, *[w[n] for n in names])


def _conv_kernel(nt, x_ref, p_ref, n_ref, w_ref, b_ref, xs_ref, bm_ref, cm_ref, pad_ref):
    t = pl.program_id(1)
    tr = x_ref.shape[0]
    pad = SSD_CONV_W // 2
    pad_ref[0:8, :] = jnp.where(t > 0, p_ref[...], 0.0)
    pad_ref[8:8 + tr, :] = x_ref[...]
    pad_ref[8 + tr:16 + tr, :] = jnp.where(t < nt - 1, n_ref[...], 0.0)
    y = jnp.zeros(x_ref.shape, F32) + b_ref[...]
    for j in range(SSD_CONV_W):
        y = y + pad_ref[8 - pad + j:8 - pad + j + tr, :] * w_ref[j:j + 1, :]
    y = _silu(y)
    xs_ref[...] = y[:, :SSD_INNER]
    bm_ref[...] = y[:, SSD_INNER:SSD_INNER + 256].astype(BF16)
    cm_ref[...] = y[:, SSD_INNER + 256:].astype(BF16)


def _conv(xbc, w, b, tr):
    g_, r_, c = xbc.shape
    nt = r_ // tr
    nb = tr // 8
    row = lambda cc: pl.BlockSpec((None, tr, cc), lambda g, t: (g, t, 0))
    return pl.pallas_call(
        functools.partial(_conv_kernel, nt),
        grid=(g_, nt),
        in_specs=[row(c),
                  pl.BlockSpec((None, 8, c), lambda g, t: (g, jnp.maximum(t * nb - 1, 0), 0)),
                  pl.BlockSpec((None, 8, c), lambda g, t: (g, jnp.minimum((t + 1) * nb, nt * nb - 1), 0)),
                  pl.BlockSpec(w.shape, lambda g, t: (0, 0)),
                  pl.BlockSpec(b.shape, lambda g, t: (0, 0))],
        out_specs=[row(512), row(256), row(256)],
        out_shape=[jax.ShapeDtypeStruct((g_, r_, 512), F32),
                   jax.ShapeDtypeStruct((g_, r_, 256), BF16),
                   jax.ShapeDtypeStruct((g_, r_, 256), BF16)],
        scratch_shapes=[pltpu.VMEM((tr + 16, c), F32)],
        compiler_params=_cparams(("arbitrary", "arbitrary")),
        name="ssd_conv",
    )(xbc, xbc, xbc, w, b)


def _tri(rev, strict=False):
    i = np.arange(CHUNK)
    if rev:
        m = i[None, :] > i[:, None] if strict else i[None, :] >= i[:, None]
    else:
        m = i[None, :] < i[:, None] if strict else i[None, :] <= i[:, None]
    return m


def _ssd_consts(rev):
    tri = _tri(rev).astype(np.float32)
    expand = np.zeros((LANES, SSD_INNER), np.float32)
    for h in range(SSD_HEADS):
        expand[h, h * SSD_HEAD_DIM:(h + 1) * SSD_HEAD_DIM] = 1.0
    return jnp.asarray(tri, BF16), jnp.asarray(tri.T.copy(), BF16), jnp.asarray(expand, BF16)


def _ssd_kernel(rev, nchunk, dir_, x_ref, bm_ref, cm_ref, dt_ref, dtt_ref, an_ref, ant_ref, tri_ref, trit_ref,
                ex_ref, h0_ref, y_ref, hT_ref, h_ref):
    s = pl.program_id(1)

    @pl.when(s == 0)
    def _():
        h_ref[...] = h0_ref[...]

    tri = tri_ref[...]
    trit = trit_ref[...]
    ex = ex_ref[...]
    an = an_ref[...]
    ant = ant_ref[...]
    ti = lax.broadcasted_iota(jnp.int32, (CHUNK, CHUNK), 0)
    si = lax.broadcasted_iota(jnp.int32, (CHUNK, CHUNK), 1)
    mask = (si >= ti) if rev else (si <= ti)
    lane = lax.broadcasted_iota(jnp.int32, (CHUNK, LANES), 1)
    last = 0 if rev else CHUNK - 1
    order = range(nchunk - 1, -1, -1) if rev else range(nchunk)
    for c in order:
        r0 = c * CHUNK
        x = x_ref[r0:r0 + CHUNK, :]
        bm = bm_ref[r0:r0 + CHUNK, :]
        cm = cm_ref[r0:r0 + CHUNK, :]
        dt = dt_ref[r0:r0 + CHUNK, dir_ * LANES:(dir_ + 1) * LANES]
        dtt = dtt_ref[8 * dir_:8 * dir_ + 8, r0:r0 + CHUNK]
        acol = _exact_left(tri, dt * an)
        arow = _exact_right(dtt * ant, trit)
        dte = _exact_right(dt, ex)
        ae = _exact_right(acol, ex)
        ae_last = ae[last:last + 1, :]
        xdt = x * dte
        xw = (xdt * jnp.exp(ae_last - ae)).astype(BF16)
        xdtb = xdt.astype(BF16)
        ys = []
        for g in range(SSD_GROUPS):
            bg = bm[:, g * SSD_STATE:(g + 1) * SSD_STATE]
            cg = cm[:, g * SSD_STATE:(g + 1) * SSD_STATE]
            cb = _dot_nt(cg, bg)
            hs = h_ref[:, g * 256:(g + 1) * 256]
            yoff = jnp.dot(cg, hs.astype(BF16), preferred_element_type=F32) * jnp.exp(ae[:, g * 256:(g + 1) * 256])
            for pr in range(2):
                xp = xdtb[:, g * 256 + pr * LANES:g * 256 + (pr + 1) * LANES]
                res = []
                for k in range(2):
                    hd = g * 4 + pr * 2 + k
                    dmat = acol[:, hd:hd + 1] - arow[hd:hd + 1, :]
                    lmat = jnp.exp(jnp.where(mask, dmat, -jnp.inf))
                    res.append(jnp.dot((cb * lmat).astype(BF16), xp, preferred_element_type=F32))
                ys.append(jnp.where(lane < SSD_HEAD_DIM, res[0], res[1]) + yoff[:, pr * LANES:(pr + 1) * LANES])
            h_ref[:, g * 256:(g + 1) * 256] = (hs * jnp.exp(ae_last[:, g * 256:(g + 1) * 256])
                                               + _dot_tn(bg, xw[:, g * 256:(g + 1) * 256]))
        y_ref[r0:r0 + CHUNK, :] = jnp.concatenate(ys, axis=1)

    @pl.when(s == pl.num_programs(1) - 1)
    def _():
        hT_ref[...] = h_ref[...]


def _ssd_scan(rev, xs, bm, cm, dt, dtt, a_log_d, h0, rb):
    b_, r_, _ = xs.shape
    nblk = r_ // rb
    dir_ = 1 if rev else 0
    tri, trit, ex = _ssd_consts(rev)
    a_neg = -jnp.exp(a_log_d.astype(F32))
    an = jnp.zeros((1, LANES), F32).at[0, :SSD_HEADS].set(a_neg)
    ant = a_neg.reshape(SSD_HEADS, 1)
    blk = (lambda s: nblk - 1 - s) if rev else (lambda s: s)
    row = lambda c: pl.BlockSpec((None, rb, c), lambda b, s: (b, blk(s), 0))
    full = lambda a: pl.BlockSpec(a.shape, lambda b, s: (0,) * a.ndim)
    st = pl.BlockSpec((None, SSD_STATE, SSD_INNER), lambda b, s: (b, 0, 0))
    return pl.pallas_call(
        functools.partial(_ssd_kernel, rev, rb // CHUNK, dir_),
        grid=(b_, nblk),
        in_specs=[row(512), row(256), row(256), row(2 * LANES),
                  pl.BlockSpec((None, 16, rb), lambda b, s: (b, 0, blk(s))),
                  full(an), full(ant), full(tri), full(trit), full(ex), st],
        out_specs=[row(512), st],
        out_shape=[jax.ShapeDtypeStruct((b_, r_, SSD_INNER), F32),
                   jax.ShapeDtypeStruct((b_, SSD_STATE, SSD_INNER), F32)],
        scratch_shapes=[pltpu.VMEM((SSD_STATE, SSD_INNER), F32)],
        compiler_params=_cparams(("arbitrary", "arbitrary")),
        name="ssd_bwd" if rev else "ssd_fwd",
    )(xs, bm, cm, dt, dtt, an, ant, tri, trit, ex, h0)


_HGRN_LEVELS = (64, 32, 16, 8, 4, 2, 1)


def _hgrn_consts(rev):
    i = np.arange(CHUNK)
    before = _tri(rev)
    after_strict = ~before
    mats, masks = [], []
    for c in _HGRN_LEVELS:
        same = (i[:, None] // c) == (i[None, :] // c)
        mats.append(same & before)
        mats.append(same & after_strict)
        if c < CHUNK:
            blk = i // c
            first = (blk % 2 == 1) if rev else (blk % 2 == 0)
            pair = first[None, :] & (~first)[:, None] & ((blk[:, None] // 2) == (blk[None, :] // 2))
            masks.append(pair)
    masks.append(np.eye(CHUNK, dtype=bool))
    m = np.concatenate(mats, axis=0).astype(np.float32)
    mk = np.stack(masks, axis=0).astype(np.float32)
    return jnp.asarray(m, BF16), jnp.asarray(mk, F32)


def _hgrn_kernel(rev, nchunk, q_ref, g_ref, v_ref, m_ref, mk_ref, s0_ref, o_ref, sT_ref, s_ref):
    s = pl.program_id(1)

    @pl.when(s == 0)
    def _():
        s_ref[...] = s0_ref[...]

    mall = m_ref[...]
    nl = len(_HGRN_LEVELS)
    last = 0 if rev else CHUNK - 1
    order = range(nchunk - 1, -1, -1) if rev else range(nchunk)
    for c in order:
        r0 = c * CHUNK
        g = g_ref[r0:r0 + CHUNK, :]
        eall = jnp.exp(_exact_left(mall, g))
        kk = 1.0 - jnp.exp(g)
        qq = q_ref[r0:r0 + CHUNK, :].astype(F32)
        vv = v_ref[r0:r0 + CHUNK, :]
        outs = []
        for h in range(HGRN_HEADS):
            ls = slice(h * LANES, (h + 1) * LANES)
            qh, kh, vh = qq[:, ls], kk[:, ls], vv[:, ls]

            def fac(idx, ls=ls):
                return eall[idx * CHUNK:(idx + 1) * CHUNK, ls]

            att = _dot_nt(qh.astype(BF16), kh.astype(BF16)) * mk_ref[nl - 1]
            for l in range(1, nl):
                qt = (qh * fac(2 * l)).astype(BF16)
                kt = (kh * fac(2 * l + 1)).astype(BF16)
                att = att + _dot_nt(qt, kt) * mk_ref[l - 1]
            st = s_ref[h]
            qs_ = (qh * fac(0)).astype(BF16)
            o = _dot_nt(qs_, st.astype(BF16)) + jnp.dot(att.astype(BF16), vh, preferred_element_type=F32)
            outs.append(o)
            ks_ = (kh * fac(1)).astype(BF16)
            dec = fac(0)[last:last + 1, :]
            s_ref[h] = st * dec + _dot_tn(vh, ks_)
        o_ref[r0:r0 + CHUNK, :] = jnp.concatenate(outs, axis=1)

    @pl.when(s == pl.num_programs(1) - 1)
    def _():
        sT_ref[...] = s_ref[...]


def _hgrn_scan(rev, q, g, v, s0, rb):
    b_, r_, _ = q.shape
    nblk = r_ // rb
    mall, mk = _hgrn_consts(rev)
    blk = (lambda s: nblk - 1 - s) if rev else (lambda s: s)
    row = pl.BlockSpec((None, rb, 512), lambda b, s: (b, blk(s), 0))
    full = lambda a: pl.BlockSpec(a.shape, lambda b, s: (0,) * a.ndim)
    st = pl.BlockSpec((None, HGRN_HEADS, LANES, HGRN_DK), lambda b, s: (b, 0, 0, 0))
    return pl.pallas_call(
        functools.partial(_hgrn_kernel, rev, rb // CHUNK),
        grid=(b_, nblk),
        in_specs=[row, row, row, full(mall), full(mk), st],
        out_specs=[row, st],
        out_shape=[jax.ShapeDtypeStruct((b_, r_, 512), F32),
                   jax.ShapeDtypeStruct((b_, HGRN_HEADS, LANES, HGRN_DK), F32)],
        scratch_shapes=[pltpu.VMEM((HGRN_HEADS, LANES, HGRN_DK), F32)],
        compiler_params=_cparams(("arbitrary", "arbitrary")),
        name="hgrn_bwd" if rev else "hgrn_fwd",
    )(q, g, v, mall, mk, s0)


def _outproj1_kernel(of_ref, ob_ref, yf_ref, yb_ref, sg_ref, sz_ref, xs_ref, h_ref, mod_ref,
                     on_ref, sk_ref, sn_ref, wo_ref, wy_ref, o_ref):
    o = of_ref[...] + ob_ref[...]
    on = on_ref[...]
    parts = []
    for hh in range(HGRN_HEADS):
        ls = slice(hh * LANES, (hh + 1) * LANES)
        parts.append(_rms(o[:, ls]) * on[:, ls])
    o = jnp.concatenate(parts, axis=1) * sg_ref[...]
    y = (yf_ref[...] + yb_ref[...] + sk_ref[...] * xs_ref[...]) * sz_ref[...]
    sn = sn_ref[...]
    gw = SSD_INNER // SSD_GROUPS
    parts = []
    for gg in range(SSD_GROUPS):
        ls = slice(gg * gw, (gg + 1) * gw)
        parts.append(_rms(y[:, ls]) * sn[:, ls])
    y = jnp.concatenate(parts, axis=1)
    out = (jnp.dot(o.astype(BF16), wo_ref[...], preferred_element_type=F32)
           + jnp.dot(y.astype(BF16), wy_ref[...], preferred_element_type=F32))
    o_ref[...] = h_ref[...] + mod_ref[2:3, :] * out


def _outproj1(of, ob, yf, yb, sg, sz, xs, h, mods, on, sk, sn, wo, wy, tr):
    g_, r_, d = h.shape
    row = lambda c: pl.BlockSpec((None, tr, c), lambda b, t: (b, t, 0))
    full = lambda a: pl.BlockSpec(a.shape, lambda b, t: (0,) * a.ndim)
    return pl.pallas_call(
        _outproj1_kernel,
        grid=(g_, r_ // tr),
        in_specs=[row(512)] * 7 + [row(d), pl.BlockSpec((None, 8, d), lambda b, t: (b, 0, 0)),
                                   full(on), full(sk), full(sn), full(wo), full(wy)],
        out_specs=row(d),
        out_shape=jax.ShapeDtypeStruct(h.shape, F32),
        compiler_params=_cparams(("arbitrary", "arbitrary")),
        name="outproj1",
    )(of, ob, yf, yb, sg, sz, xs, h, mods, on, sk, sn, wo, wy)


def _rope_tables(length):
    rows = length // GRID_W
    row = jnp.repeat(jnp.arange(rows, dtype=F32), GRID_W)
    col = jnp.tile(jnp.arange(GRID_W, dtype=F32), rows)

    def cs(rot_dim):
        n_freq = rot_dim // 4
        inv_freq = ROPE_BASE ** (-jnp.arange(n_freq, dtype=F32) / n_freq)
        ang = jnp.concatenate([row[:, None] * inv_freq, col[:, None] * inv_freq], axis=-1)
        c, s = jnp.cos(ang), jnp.sin(ang)
        return jnp.concatenate([c, c], axis=-1), jnp.concatenate([-s, s], axis=-1)

    ca, sa = cs(DA_HEAD_DIM)
    cq, sq = cs(MLA_ROPE)
    ones = jnp.ones((length, MLA_NOPE), F32)
    pad1 = jnp.ones((length, LANES - MLA_NOPE - MLA_ROPE), F32)
    cqt = jnp.concatenate([ones, cq, pad1], axis=-1)
    sqt = jnp.concatenate([0.0 * ones, sq, 0.0 * pad1], axis=-1)
    return jnp.tile(ca, (1, 2)), jnp.tile(sa, (1, 2)), cqt, sqt


def _att_weights(att_w_in, mla_q_norm, mla_w_uq, mla_kv_norm, mla_w_ukv):
    d = att_w_in.shape[0]
    kr = att_w_in[:, 2176:2208]
    kr_blk = jnp.concatenate([jnp.zeros((d, MLA_NOPE), F32), kr,
                              jnp.zeros((d, LANES - MLA_NOPE - MLA_ROPE), F32)], axis=1)
    win = jnp.concatenate([att_w_in[:, :1024], att_w_in[:, 1536:2176], kr_blk], axis=1).astype(BF16)
    wva = att_w_in[:, 1024:1536].T.astype(BF16)
    wq = mla_w_uq.reshape(MLA_Q_RANK, MLA_HEADS, MLA_NOPE + MLA_ROPE)
    wq = jnp.pad(wq, ((0, 0), (0, 0), (0, LANES - MLA_NOPE - MLA_ROPE))).reshape(MLA_Q_RANK, MLA_HEADS * LANES)
    wkv = mla_w_ukv.reshape(MLA_KV_RANK, MLA_HEADS, MLA_NOPE + MLA_V)
    wk = jnp.pad(wkv[:, :, :MLA_NOPE], ((0, 0), (0, 0), (0, LANES - MLA_NOPE))).reshape(MLA_KV_RANK, MLA_HEADS * LANES)
    wv = wkv[:, :, MLA_NOPE:].reshape(MLA_KV_RANK, MLA_HEADS * MLA_V)
    return dict(win=win, wva=wva, qn=mla_q_norm.reshape(1, -1), wuq=wq.astype(BF16),
                kvn=mla_kv_norm.reshape(1, -1), wk=wk.astype(BF16), wv=wv.T.astype(BF16))


def _rec_weights(rec_w_in, bound_logits, dt_bias):
    d = rec_w_in.shape[0]
    w = rec_w_in[:, :4096].astype(BF16)
    wdt = rec_w_in[:, 4096:4112]
    pad = jnp.zeros((d, LANES - SSD_HEADS), F32)
    wdt_rows = jnp.concatenate([wdt[:, :SSD_HEADS], pad, wdt[:, SSD_HEADS:], pad], axis=1).astype(BF16)
    zb = jnp.zeros((LANES - SSD_HEADS,), F32)
    dtb = jnp.concatenate([dt_bias[0], zb, dt_bias[1], zb]).reshape(1, 2 * LANES)
    return dict(w=w, wdt=wdt_rows, wdtt=wdt.T.astype(BF16), bl=bound_logits,
                dtb=dtb, dtbt=dt_bias.reshape(2 * SSD_HEADS, 1))


def kernel(x, c, ctx, c_ctx, w_mod, b_mod, norm_mix, norm_mlp, w_mlp_in, w_mlp_out, att_w_in, att_lambda, att_subnorm, mla_q_norm, mla_w_uq, mla_kv_norm, mla_w_ukv, att_w_out, rec_w_in, hgrn_bound_logits, hgrn_out_norm, ssd_conv_w, ssd_conv_b, ssd_a_log, ssd_dt_bias, ssd_skip, ssd_norm, rec_w_out, final_norm):
    b_, length, d = x.shape
    n_ctx = ctx.shape[1]
    assert w_mod.shape[0] == 2 and d == D_MODEL

    cc = jnp.zeros((8, d), F32).at[:b_].set(c).at[b_].set(c_ctx)
    mods = _mod_vectors(cc, w_mod, b_mod)
    mods = jnp.pad(mods.reshape(2, 8, 6, d), ((0, 0), (0, 0), (0, 2), (0, 0)))
    mods_lat = [mods[l, :b_] for l in range(2)]
    mods_ctx = [jnp.broadcast_to(mods[l, b_], (b_, 8, d)) for l in range(2)]
    row = lambda v: v.reshape(1, -1)

    tr = min(512, length)
    trc = min(256, n_ctx)
    w0 = _att_weights(att_w_in[0], mla_q_norm[0], mla_w_uq[0], mla_kv_norm[0], mla_w_ukv[0])
    tables = _rope_tables(length)
    qa, ka, va, qm, km, vm = _proj0(x, mods_lat[0], row(norm_mix[0]), w0, tables, tr)
    qa_c, ka_c, va_c, qm_c, km_c, vm_c = _proj0(ctx, mods_ctx[0], row(norm_mix[0]), w0, None, trc)
    lam_init = 0.8 - 0.6 * math.exp(-0.3 * 0)
    da_extra = (att_lambda[0], row(att_subnorm[0]))
    tq, tk = min(512, length), min(512, length)
    ya = _attention("da", qa, ka_c, va_c, ka, va, da_extra, tq, tk, lam_init)
    ym = _attention("mla", qm, km_c, vm_c, km, vm, None, tq, tk, lam_init)
    ya_c = _attention("da", qa_c, ka_c, va_c, None, None, da_extra, trc, tk, lam_init)
    ym_c = _attention("mla", qm_c, km_c, vm_c, None, None, None, trc, tk, lam_init)
    wout = att_w_out[0].astype(BF16)
    w1 = w_mlp_in.astype(BF16)
    w2 = w_mlp_out.astype(BF16)
    trm = min(1024, length)
    h_lat = _outproj0(ya, ym, x, mods_lat[0], wout[:512], wout[512:], tr)
    h_lat = _mlp(h_lat, mods_lat[0], row(norm_mlp[0]), w1[0], w2[0], None, trm)
    h_ctx = _outproj0(ya_c, ym_c, ctx, mods_ctx[0], wout[:512], wout[512:], trc)
    h_ctx = _mlp(h_ctx, mods_ctx[0], row(norm_mlp[0]), w1[0], w2[0], None, trc)

    w1r = _rec_weights(rec_w_in[0], hgrn_bound_logits, ssd_dt_bias[0])
    cw = ssd_conv_w[0]
    cb = row(ssd_conv_b[0])
    pc = _proj1(h_ctx, mods_ctx[1], row(norm_mix[1]), w1r, trc)
    pl_ = _proj1(h_lat, mods_lat[1], row(norm_mix[1]), w1r, tr)
    xs_c, bm_c, cm_c = _conv(pc[6], cw, cb, trc)
    xs_l, bm_l, cm_l = _conv(pl_[6], cw, cb, tr)
    rb = 256
    zs = jnp.zeros((b_, SSD_STATE, SSD_INNER), F32)
    zh = jnp.zeros((b_, HGRN_HEADS, LANES, HGRN_DK), F32)
    ys, os_ = [], []
    for rev in (False, True):
        dr = 1 if rev else 0
        _, hs = _ssd_scan(rev, xs_c, bm_c, cm_c, pc[7], pc[8], ssd_a_log[0, dr], zs, rb)
        y, _ = _ssd_scan(rev, xs_l, bm_l, cm_l, pl_[7], pl_[8], ssd_a_log[0, dr], hs, rb)
        ys.append(y)
        _, ss = _hgrn_scan(rev, pc[0], pc[1 + dr], pc[3], zh, rb)
        o, _ = _hgrn_scan(rev, pl_[0], pl_[1 + dr], pl_[3], ss, rb)
        os_.append(o)
    skip = row(jnp.repeat(ssd_skip[0], SSD_HEAD_DIM))
    wro = rec_w_out[0].astype(BF16)
    h_lat = _outproj1(os_[0], os_[1], ys[0], ys[1], pl_[4], pl_[5], xs_l, h_lat, mods_lat[1],
                      row(hgrn_out_norm[0]), skip, row(ssd_norm[0]), wro[:512], wro[512:], tr)
    return _mlp(h_lat, mods_lat[1], row(norm_mlp[1]), w1[1], w2[1], row(final_norm), trm)
```

```python
import functools
import math

import numpy as np
import jax
import jax.numpy as jnp
from jax import lax
from jax.experimental import pallas as pl
from jax.experimental.pallas import tpu as pltpu

F32 = jnp.float32
BF16 = jnp.bfloat16

D_MODEL = 1024
GRID_W = 64
DA_HEADS = 4
DA_HEAD_DIM = 64
MLA_HEADS = 8
MLA_NOPE = 64
MLA_ROPE = 32
MLA_V = 64
MLA_Q_RANK = 384
MLA_KV_RANK = 256
HGRN_HEADS = 4
HGRN_DK = 128
SSD_HEADS = 8
SSD_HEAD_DIM = 64
SSD_GROUPS = 2
SSD_STATE = 128
SSD_CONV_W = 5
SSD_INNER = SSD_HEADS * SSD_HEAD_DIM
MLP_HIDDEN = 4 * D_MODEL
ROPE_BASE = 10000.0
NORM_EPS = 1e-6
CHUNK = 64

LANES = 128
LOG2E = 1.4426950408889634
VMEM_LIMIT = 56 * 1024 * 1024


def _cparams(sem):
    return pltpu.CompilerParams(dimension_semantics=sem, vmem_limit_bytes=VMEM_LIMIT)


def _rms(x):
    return x * lax.rsqrt(jnp.mean(x * x, axis=-1, keepdims=True) + NORM_EPS)


def _modulate(x, g, shift, scale):
    return (_rms(x) * g) * (1.0 + scale) + shift


def _silu(x):
    return x * jax.nn.sigmoid(x)


def _bdot(a, b):
    return jnp.dot(a.astype(BF16), b.astype(BF16), preferred_element_type=F32)


def _dot_nt(a, b):
    return lax.dot_general(a, b, (((1,), (1,)), ((), ())), preferred_element_type=F32)


def _dot_tn(a, b):
    return lax.dot_general(a, b, (((0,), (0,)), ((), ())), preferred_element_type=F32)


def _split2(x):
    hi = x.astype(BF16)
    lo = (x - hi.astype(F32)).astype(BF16)
    return hi, lo


def _exact_left(m01, x):
    hi, lo = _split2(x)
    return (jnp.dot(m01, hi, preferred_element_type=F32) + jnp.dot(m01, lo, preferred_element_type=F32))


def _exact_right(x, m01):
    hi, lo = _split2(x)
    return (jnp.dot(hi, m01, preferred_element_type=F32) + jnp.dot(lo, m01, preferred_element_type=F32))


def _mod_kernel(c_ref, w_ref, b_ref, o_ref):
    a = _silu(c_ref[...]).astype(BF16)
    o_ref[...] = jnp.dot(a, w_ref[...].astype(BF16), preferred_element_type=F32) + b_ref[...]


def _mod_vectors(cc, w_mod, b_mod):
    depth, d, n = w_mod.shape
    tn = 1024
    return pl.pallas_call(
        _mod_kernel,
        grid=(depth, n // tn),
        in_specs=[
            pl.BlockSpec((8, d), lambda l, j: (0, 0)),
            pl.BlockSpec((None, d, tn), lambda l, j: (l, 0, j)),
            pl.BlockSpec((None, 1, tn), lambda l, j: (l, 0, j)),
        ],
        out_specs=pl.BlockSpec((None, 8, tn), lambda l, j: (l, 0, j)),
        out_shape=jax.ShapeDtypeStruct((depth, 8, n), F32),
        compiler_params=_cparams(("arbitrary", "arbitrary")),
        name="mod_vectors",
    )(cc, w_mod, b_mod.reshape(depth, 1, n))


def _rope_partner(x, half):
    n = x.shape[-1]
    lane = lax.broadcasted_iota(jnp.int32, x.shape, x.ndim - 1)
    up = pltpu.roll(x, n - half, x.ndim - 1)
    dn = pltpu.roll(x, half, x.ndim - 1)
    return jnp.where((lane % (2 * half)) < half, up, dn)


def _tile_lanes(x, n):
    return jnp.concatenate([x] * n, axis=-1)


def _proj0_kernel(use_rope, *refs):
    if use_rope:
        (h_ref, mod_ref, g_ref, win_ref, wva_ref, qn_ref, wuq_ref, kvn_ref, wk_ref, wv_ref,
         ca_ref, sa_ref, cq_ref, sq_ref,
         qa_ref, ka_ref, va_ref, qm_ref, km_ref, vm_ref) = refs
    else:
        (h_ref, mod_ref, g_ref, win_ref, wva_ref, qn_ref, wuq_ref, kvn_ref, wk_ref, wv_ref,
         qa_ref, ka_ref, va_ref, qm_ref, km_ref, vm_ref) = refs
    mod = mod_ref[...]
    ub = _modulate(h_ref[...], g_ref[...], mod[0:1], mod[1:2]).astype(BF16)

    def proj(lo, hi):
        return jnp.dot(ub, win_ref[:, lo:hi], preferred_element_type=F32)

    qa = proj(0, 512) * (DA_HEAD_DIM ** -0.5 * LOG2E)
    ka = proj(512, 1024)
    va_ref[...] = _dot_nt(wva_ref[...], ub).astype(BF16)
    cq = proj(1024, 1408)
    ckv = proj(1408, 1664)
    kr = proj(1664, 1792)
    cqn = (_rms(cq) * qn_ref[...]).astype(BF16)
    qm = jnp.dot(cqn, wuq_ref[...], preferred_element_type=F32) * ((MLA_NOPE + MLA_ROPE) ** -0.5 * LOG2E)
    ckvn = (_rms(ckv) * kvn_ref[...]).astype(BF16)
    kn = jnp.dot(ckvn, wk_ref[...], preferred_element_type=F32)
    vm_ref[...] = _dot_nt(wv_ref[...], ckvn).astype(BF16)
    if use_rope:
        ca = _tile_lanes(ca_ref[...], DA_HEADS)
        sa = _tile_lanes(sa_ref[...], DA_HEADS)
        qa = qa * ca + _rope_partner(qa, DA_HEAD_DIM // 2) * sa
        ka = ka * ca + _rope_partner(ka, DA_HEAD_DIM // 2) * sa
        cq1 = cq_ref[...]
        sq1 = sq_ref[...]
        qm = qm * _tile_lanes(cq1, MLA_HEADS) + _rope_partner(qm, MLA_ROPE // 2) * _tile_lanes(sq1, MLA_HEADS)
        kr = kr * cq1 + _rope_partner(kr, MLA_ROPE // 2) * sq1
    qa_ref[...] = qa.astype(BF16)
    ka_ref[...] = ka.astype(BF16)
    qm_ref[...] = qm.astype(BF16)
    km_ref[...] = (kn + _tile_lanes(kr, MLA_HEADS)).astype(BF16)


def _proj0(h, mods, gain, w, tables, tr):
    g_, r_, d = h.shape
    use_rope = tables is not None
    row = lambda c: pl.BlockSpec((None, tr, c), lambda b, t: (b, t, 0))
    full = lambda a: pl.BlockSpec(a.shape, lambda b, t: (0,) * a.ndim)
    in_specs = [row(d), pl.BlockSpec((None, 8, d), lambda b, t: (b, 0, 0)), full(gain),
                full(w["win"]), full(w["wva"]), full(w["qn"]), full(w["wuq"]), full(w["kvn"]), full(w["wk"]),
                full(w["wv"])]
    args = [h, mods, gain, w["win"], w["wva"], w["qn"], w["wuq"], w["kvn"], w["wk"], w["wv"]]
    if use_rope:
        in_specs += [pl.BlockSpec((tr, LANES), lambda b, t: (t, 0))] * 4
        args += list(tables)
    col = pl.BlockSpec((None, 512, tr), lambda b, t: (b, 0, t))
    widths = (512, 512, None, 1024, 1024, None)
    return pl.pallas_call(
        functools.partial(_proj0_kernel, use_rope),
        grid=(g_, r_ // tr),
        in_specs=in_specs,
        out_specs=[col if c is None else row(c) for c in widths],
        out_shape=[jax.ShapeDtypeStruct((g_, 512, r_) if c is None else (g_, r_, c), BF16) for c in widths],
        compiler_params=_cparams(("arbitrary", "arbitrary")),
        name="proj0_rope" if use_rope else "proj0_ctx",
    )(*args)


def _attn_kernel(mode, has_lat, tk, nk, lam_init, *refs):
    refs = list(refs)
    q_ref, kc_ref, vc_ref = refs[:3]
    refs = refs[3:]
    if has_lat:
        kl_ref, vl_ref = refs[:2]
        refs = refs[2:]
    if mode == "da":
        lam_ref, sub_ref = refs[:2]
        refs = refs[2:]
    o_ref, ma_ref, la_ref, acca_ref, mb_ref, lb_ref, accb_ref = refs[:7]
    if has_lat:
        p_ref, al_ref = refs[7:]

    q = q_ref[...]
    if mode == "da":
        lane = lax.broadcasted_iota(jnp.int32, q.shape, 1)
        zero = jnp.zeros_like(q)
        qs = (jnp.where(lane < DA_HEAD_DIM, q, zero), jnp.where(lane >= DA_HEAD_DIM, q, zero))
    else:
        qs = (q[:, :LANES], q[:, LANES:])
    stats = ((ma_ref, la_ref, acca_ref), (mb_ref, lb_ref, accb_ref))
    for m_ref, l_ref, acc_ref in stats:
        m_ref[...] = jnp.full(m_ref.shape, -jnp.inf, F32)
        l_ref[...] = jnp.zeros(l_ref.shape, F32)
        acc_ref[...] = jnp.zeros(acc_ref.shape, F32)

    def scores(k):
        ks = (k, k) if mode == "da" else (k[:, :LANES], k[:, LANES:])
        return [_dot_nt(ks[i], qs[i]) for i in range(2)]

    def softmax(i, s):
        m_ref, l_ref, _ = stats[i]
        m_prev = m_ref[...]
        m_new = jnp.maximum(m_prev, jnp.max(s, axis=0, keepdims=True))
        alpha = jnp.exp2(m_prev - m_new)
        p = jnp.exp2(s - m_new)
        l_ref[...] = alpha * l_ref[...] + jnp.sum(p, axis=0, keepdims=True)
        m_ref[...] = m_new
        return p.astype(BF16), alpha

    def accumulate(i, vt, p, alpha):
        acc_ref = stats[i][2]
        acc_ref[...] = alpha * acc_ref[...] + jnp.dot(vt, p, preferred_element_type=F32)

    if not has_lat:
        ss = scores(kc_ref[...])
        pa = [softmax(i, ss[i]) for i in range(2)]
        for i in range(2):
            accumulate(i, vc_ref[...], *pa[i])
    else:
        s_ref, mx_ref = p_ref, al_ref
        nc = kc_ref.shape[0]

        def chunk(c):
            return pl.ds(c * tk if isinstance(c, int) else pl.multiple_of(c * tk, tk), tk)

        def keys(st):
            return kc_ref[...] if isinstance(st, int) and st == 0 else kl_ref[chunk(st - 1), :]

        def vals(st):
            return vc_ref[...] if isinstance(st, int) and st == 0 else vl_ref[:, chunk(st - 1)]

        def lookahead(st, slot, n):
            ss = scores(keys(st))
            for i in range(2):
                s_ref[slot, i, 0:n, :] = ss[i]
                mx_ref[1, i] = jnp.maximum(mx_ref[0, i], jnp.max(ss[i], axis=0, keepdims=True))

        def consume(st, slot, n):
            vt = vals(st)
            for i in range(2):
                m_ref, l_ref, acc_ref = stats[i]
                m_cur = mx_ref[0, i]
                alpha = jnp.exp2(m_ref[...] - m_cur)
                p = jnp.exp2(s_ref[slot, i, 0:n, :] - m_cur)
                l_ref[...] = alpha * l_ref[...] + jnp.sum(p, axis=0, keepdims=True)
                acc_ref[...] = alpha * acc_ref[...] + jnp.dot(vt, p.astype(BF16), preferred_element_type=F32)
                m_ref[...] = m_cur

        def advance():
            for i in range(2):
                mx_ref[0, i] = mx_ref[1, i]

        def run(st, slot, has_next):
            n = nc if isinstance(st, int) and st == 0 else tk
            if has_next:
                lookahead(st + 1, 1 - slot, tk)
            consume(st, slot, n)
            if has_next:
                advance()

        for i in range(2):
            mx_ref[0, i] = jnp.full((1, mx_ref.shape[-1]), -jnp.inf, F32)
        lookahead(0, 0, nc)
        advance()
        run(0, 0, True)
        npair = (nk - 1) // 2

        def body(j, carry):
            run(2 * j + 1, 1, True)
            run(2 * j + 2, 0, True)
            return carry

        lax.fori_loop(0, npair, body, 0)
        for st in range(2 * npair + 1, nk + 1):
            run(st, st % 2, st < nk)

    oa = acca_ref[...] / la_ref[...]
    ob = accb_ref[...] / lb_ref[...]
    if mode == "da":
        lp = lam_ref[...]
        lam = (jnp.exp(jnp.sum(lp[0:1] * lp[1:2], axis=1, keepdims=True))
               - jnp.exp(jnp.sum(lp[2:3] * lp[3:4], axis=1, keepdims=True)) + lam_init)
        dlt = (oa - lam * ob).T
        o_ref[...] = ((_rms(dlt) * sub_ref[...]) * (1.0 - lam_init)).astype(o_ref.dtype)
    else:
        chan = lax.broadcasted_iota(jnp.int32, oa.shape, 0)
        o_ref[...] = jnp.where(chan < MLA_V, oa, ob).T.astype(o_ref.dtype)


def _attention(mode, q, kc, vc, kl, vl, extra, tq, tk, lam_init):
    b_, sq, _ = q.shape
    has_lat = kl is not None
    w = LANES if mode == "da" else 2 * LANES
    nh = 4
    kblk = lambda a: pl.BlockSpec((None, a.shape[1], w), lambda b, h, t: (b, 0, h))
    vblk = lambda a: pl.BlockSpec((None, LANES, a.shape[2]), lambda b, h, t: (b, h, 0))
    in_specs = [pl.BlockSpec((None, tq, w), lambda b, h, t: (b, t, h)), kblk(kc), vblk(vc)]
    args = [q, kc, vc]
    nk = 0
    if has_lat:
        in_specs += [kblk(kl), vblk(vl)]
        args += [kl, vl]
        nk = kl.shape[1] // tk
    if mode == "da":
        lam_p, subnorm = extra
        in_specs += [pl.BlockSpec(lam_p.shape, lambda b, h, t: (0, 0)),
                     pl.BlockSpec(subnorm.shape, lambda b, h, t: (0, 0))]
        args += [lam_p, subnorm]
    scratch = []
    for _ in range(2):
        scratch += [pltpu.VMEM((1, tq), F32), pltpu.VMEM((1, tq), F32), pltpu.VMEM((LANES, tq), F32)]
    if has_lat:
        scratch += [pltpu.VMEM((2, 2, tk, tq), F32), pltpu.VMEM((2, 2, 1, tq), F32)]
    return pl.pallas_call(
        functools.partial(_attn_kernel, mode, has_lat, tk, nk, lam_init),
        grid=(b_, nh, sq // tq),
        in_specs=in_specs,
        out_specs=pl.BlockSpec((None, tq, LANES), lambda b, h, t: (b, t, h)),
        out_shape=jax.ShapeDtypeStruct((b_, sq, nh * LANES), BF16),
        scratch_shapes=scratch,
        compiler_params=_cparams(("arbitrary", "arbitrary", "arbitrary")),
        name=f"attn_{mode}_{'lat' if has_lat else 'ctx'}",
    )(*args)


def _outproj0_kernel(ya_ref, ym_ref, h_ref, mod_ref, wa_ref, wm_ref, o_ref):
    y = (jnp.dot(ya_ref[...], wa_ref[...], preferred_element_type=F32)
         + jnp.dot(ym_ref[...], wm_ref[...], preferred_element_type=F32))
    o_ref[...] = h_ref[...] + mod_ref[2:3, :] * y


def _outproj0(ya, ym, h, mods, wa, wm, tr):
    g_, r_, d = h.shape
    row = lambda c: pl.BlockSpec((None, tr, c), lambda b, t: (b, t, 0))
    full = lambda a: pl.BlockSpec(a.shape, lambda b, t: (0,) * a.ndim)
    return pl.pallas_call(
        _outproj0_kernel,
        grid=(g_, r_ // tr),
        in_specs=[row(512), row(512), row(d), pl.BlockSpec((None, 8, d), lambda b, t: (b, 0, 0)),
                  full(wa), full(wm)],
        out_specs=row(d),
        out_shape=jax.ShapeDtypeStruct(h.shape, F32),
        compiler_params=_cparams(("arbitrary", "arbitrary")),
        name="outproj0",
    )(ya, ym, h, mods, wa, wm)


def _mlp_kernel(final, *refs):
    if final:
        h_ref, mod_ref, g_ref, w1_ref, w2_ref, fg_ref, o_ref, u_ref, acc_ref = refs
    else:
        h_ref, mod_ref, g_ref, w1_ref, w2_ref, o_ref, u_ref, acc_ref = refs
    j = pl.program_id(2)

    @pl.when(j == 0)
    def _():
        u_ref[...] = _modulate(h_ref[...], g_ref[...], mod_ref[3:4, :], mod_ref[4:5, :]).astype(BF16)
        acc_ref[...] = jnp.zeros(acc_ref.shape, F32)

    a = jnp.dot(u_ref[...], w1_ref[...], preferred_element_type=F32)
    a = jnp.square(jnp.maximum(a, 0.0)).astype(BF16)
    acc_ref[...] += jnp.dot(a, w2_ref[...], preferred_element_type=F32)

    @pl.when(j == pl.num_programs(2) - 1)
    def _():
        out = h_ref[...] + mod_ref[5:6, :] * acc_ref[...]
        if final:
            out = _rms(out) * fg_ref[...]
        o_ref[...] = out


def _mlp(h, mods, gain, w1, w2, final_gain, tr, th=1024):
    g_, r_, d = h.shape
    hid = w1.shape[1]
    final = final_gain is not None
    row = pl.BlockSpec((None, tr, d), lambda b, t, j: (b, t, 0))
    in_specs = [row, pl.BlockSpec((None, 8, d), lambda b, t, j: (b, 0, 0)),
                pl.BlockSpec(gain.shape, lambda b, t, j: (0, 0)),
                pl.BlockSpec((d, th), lambda b, t, j: (0, j)),
                pl.BlockSpec((th, d), lambda b, t, j: (j, 0))]
    args = [h, mods, gain, w1, w2]
    if final:
        in_specs.append(pl.BlockSpec(final_gain.shape, lambda b, t, j: (0, 0)))
        args.append(final_gain)
    return pl.pallas_call(
        functools.partial(_mlp_kernel, final),
        grid=(g_, r_ // tr, hid // th),
        in_specs=in_specs,
        out_specs=row,
        out_shape=jax.ShapeDtypeStruct(h.shape, F32),
        scratch_shapes=[pltpu.VMEM((tr, d), BF16), pltpu.VMEM((tr, d), F32)],
        compiler_params=_cparams(("arbitrary", "arbitrary", "arbitrary")),
        name="mlp_final" if final else "mlp",
    )(*args)


def _softplus(x):
    return jnp.maximum(x, 0.0) + jnp.log1p(jnp.exp(-jnp.abs(x)))


def _proj1_kernel(h_ref, mod_ref, g_ref, w_ref, wdt_ref, wdtt_ref, bl_ref, dtb_ref, dtbt_ref,
                  q_ref, gf_ref, gb_ref, i_ref, sg_ref, sz_ref, xbc_ref, dt_ref, dtt_ref):
    mod = mod_ref[...]
    ub = _modulate(h_ref[...], g_ref[...], mod[0:1], mod[1:2]).astype(BF16)

    def proj(lo, hi):
        return jnp.dot(ub, w_ref[:, lo:hi], preferred_element_type=F32)

    bl = bl_ref[...]
    e = jnp.exp(bl - jnp.max(bl, axis=0, keepdims=True))
    gamma = e / jnp.sum(e, axis=0, keepdims=True)
    lb = (gamma[0:1] + gamma[1:2]) - gamma[0:1]
    q_ref[...] = _silu(proj(0, 512)).astype(BF16)
    for k, out in ((0, gf_ref), (1, gb_ref)):
        lbk = lb[:, 512 * k:512 * (k + 1)]
        f = lbk + (1.0 - lbk) * jax.nn.sigmoid(proj(512 * (k + 1), 512 * (k + 2)))
        out[...] = jnp.log(f)
    i_ref[...] = proj(1536, 2048).astype(BF16)
    sg_ref[...] = _silu(proj(2048, 2560))
    sz_ref[...] = _silu(proj(2560, 3072))
    xbc_ref[...] = proj(3072, 4096)
    dt_ref[...] = _softplus(jnp.dot(ub, wdt_ref[...], preferred_element_type=F32) + dtb_ref[...])
    dtt_ref[...] = _softplus(_dot_nt(wdtt_ref[...], ub) + dtbt_ref[...])


def _proj1(h, mods, gain, w, tr):
    g_, r_, d = h.shape
    row = lambda c: pl.BlockSpec((None, tr, c), lambda b, t: (b, t, 0))
    full = lambda a: pl.BlockSpec(a.shape, lambda b, t: (0,) * a.ndim)
    names = ("w", "wdt", "wdtt", "bl", "dtb", "dtbt")
    widths = (512, 512, 512, 512, 512, 512, 1024, 2 * LANES)
    dts = (BF16, F32, F32, BF16, F32, F32, F32, F32)
    return pl.pallas_call(
        _proj1_kernel,
        grid=(g_, r_ // tr),
        in_specs=[row(d), pl.BlockSpec((None, 8, d), lambda b, t: (b, 0, 0)), full(gain)]
        + [full(w[n]) for n in names],
        out_specs=[row(c) for c in widths] + [pl.BlockSpec((None, 16, tr), lambda b, t: (b, 0, t))],
        out_shape=[jax.ShapeDtypeStruct((g_, r_, c), t_) for c, t_ in zip(widths, dts)]
        + [jax.ShapeDtypeStruct((g_, 16, r_), F32)],
        compiler_params=_cparams(("arbitrary", "arbitrary")),
        name="proj1",
    )(h, mods, gain, *[w[n] for n in names])


def _conv_kernel(nt, x_ref, p_ref, n_ref, w_ref, b_ref, xs_ref, bm_ref, cm_ref, pad_ref):
    t = pl.program_id(1)
    tr = x_ref.shape[0]
    pad = SSD_CONV_W // 2
    pad_ref[0:8, :] = jnp.where(t > 0, p_ref[...], 0.0)
    pad_ref[8:8 + tr, :] = x_ref[...]
    pad_ref[8 + tr:16 + tr, :] = jnp.where(t < nt - 1, n_ref[...], 0.0)
    y = jnp.zeros(x_ref.shape, F32) + b_ref[...]
    for j in range(SSD_CONV_W):
        y = y + pad_ref[8 - pad + j:8 - pad + j + tr, :] * w_ref[j:j + 1, :]
    y = _silu(y)
    xs_ref[...] = y[:, :SSD_INNER]
    bm_ref[...] = y[:, SSD_INNER:SSD_INNER + 256].astype(BF16)
    cm_ref[...] = y[:, SSD_INNER + 256:].astype(BF16)


def _conv(xbc, w, b, tr):
    g_, r_, c = xbc.shape
    nt = r_ // tr
    nb = tr // 8
    row = lambda cc: pl.BlockSpec((None, tr, cc), lambda g, t: (g, t, 0))
    return pl.pallas_call(
        functools.partial(_conv_kernel, nt),
        grid=(g_, nt),
        in_specs=[row(c),
                  pl.BlockSpec((None, 8, c), lambda g, t: (g, jnp.maximum(t * nb - 1, 0), 0)),
                  pl.BlockSpec((None, 8, c), lambda g, t: (g, jnp.minimum((t + 1) * nb, nt * nb - 1), 0)),
                  pl.BlockSpec(w.shape, lambda g, t: (0, 0)),
                  pl.BlockSpec(b.shape, lambda g, t: (0, 0))],
        out_specs=[row(512), row(256), row(256)],
        out_shape=[jax.ShapeDtypeStruct((g_, r_, 512), F32),
                   jax.ShapeDtypeStruct((g_, r_, 256), BF16),
                   jax.ShapeDtypeStruct((g_, r_, 256), BF16)],
        scratch_shapes=[pltpu.VMEM((tr + 16, c), F32)],
        compiler_params=_cparams(("arbitrary", "arbitrary")),
        name="ssd_conv",
    )(xbc, xbc, xbc, w, b)


def _tri(rev, strict=False):
    i = np.arange(CHUNK)
    if rev:
        m = i[None, :] > i[:, None] if strict else i[None, :] >= i[:, None]
    else:
        m = i[None, :] < i[:, None] if strict else i[None, :] <= i[:, None]
    return m


def _ssd_consts(rev):
    tri = _tri(rev).astype(np.float32)
    expand = np.zeros((LANES, SSD_INNER), np.float32)
    for h in range(SSD_HEADS):
        expand[h, h * SSD_HEAD_DIM:(h + 1) * SSD_HEAD_DIM] = 1.0
    return jnp.asarray(tri, BF16), jnp.asarray(tri.T.copy(), BF16), jnp.asarray(expand, BF16)


def _ssd_kernel(rev, nchunk, dir_, x_ref, bm_ref, cm_ref, dt_ref, dtt_ref, an_ref, ant_ref, tri_ref, trit_ref,
                ex_ref, h0_ref, y_ref, hT_ref, h_ref):
    s = pl.program_id(1)

    @pl.when(s == 0)
    def _():
        h_ref[...] = h0_ref[...]

    tri = tri_ref[...]
    trit = trit_ref[...]
    ex = ex_ref[...]
    an = an_ref[...]
    ant = ant_ref[...]
    ti = lax.broadcasted_iota(jnp.int32, (CHUNK, CHUNK), 0)
    si = lax.broadcasted_iota(jnp.int32, (CHUNK, CHUNK), 1)
    mask = (si >= ti) if rev else (si <= ti)
    lane = lax.broadcasted_iota(jnp.int32, (CHUNK, LANES), 1)
    last = 0 if rev else CHUNK - 1
    order = range(nchunk - 1, -1, -1) if rev else range(nchunk)
    for c in order:
        r0 = c * CHUNK
        x = x_ref[r0:r0 + CHUNK, :]
        bm = bm_ref[r0:r0 + CHUNK, :]
        cm = cm_ref[r0:r0 + CHUNK, :]
        dt = dt_ref[r0:r0 + CHUNK, dir_ * LANES:(dir_ + 1) * LANES]
        dtt = dtt_ref[8 * dir_:8 * dir_ + 8, r0:r0 + CHUNK]
        acol = _exact_left(tri, dt * an)
        arow = _exact_right(dtt * ant, trit)
        dte = _exact_right(dt, ex)
        ae = _exact_right(acol, ex)
        ae_last = ae[last:last + 1, :]
        xdt = x * dte
        xw = (xdt * jnp.exp(ae_last - ae)).astype(BF16)
        xdtb = xdt.astype(BF16)
        ys = []
        for g in range(SSD_GROUPS):
            bg = bm[:, g * SSD_STATE:(g + 1) * SSD_STATE]
            cg = cm[:, g * SSD_STATE:(g + 1) * SSD_STATE]
            cb = _dot_nt(cg, bg)
            hs = h_ref[:, g * 256:(g + 1) * 256]
            yoff = jnp.dot(cg, hs.astype(BF16), preferred_element_type=F32) * jnp.exp(ae[:, g * 256:(g + 1) * 256])
            for pr in range(2):
                xp = xdtb[:, g * 256 + pr * LANES:g * 256 + (pr + 1) * LANES]
                res = []
                for k in range(2):
                    hd = g * 4 + pr * 2 + k
                    dmat = acol[:, hd:hd + 1] - arow[hd:hd + 1, :]
                    lmat = jnp.exp(jnp.where(mask, dmat, -jnp.inf))
                    res.append(jnp.dot((cb * lmat).astype(BF16), xp, preferred_element_type=F32))
                ys.append(jnp.where(lane < SSD_HEAD_DIM, res[0], res[1]) + yoff[:, pr * LANES:(pr + 1) * LANES])
            h_ref[:, g * 256:(g + 1) * 256] = (hs * jnp.exp(ae_last[:, g * 256:(g + 1) * 256])
                                               + _dot_tn(bg, xw[:, g * 256:(g + 1) * 256]))
        y_ref[r0:r0 + CHUNK, :] = jnp.concatenate(ys, axis=1)

    @pl.when(s == pl.num_programs(1) - 1)
    def _():
        hT_ref[...] = h_ref[...]


def _ssd_scan(rev, xs, bm, cm, dt, dtt, a_log_d, h0, rb):
    b_, r_, _ = xs.shape
    nblk = r_ // rb
    dir_ = 1 if rev else 0
    tri, trit, ex = _ssd_consts(rev)
    a_neg = -jnp.exp(a_log_d.astype(F32))
    an = jnp.zeros((1, LANES), F32).at[0, :SSD_HEADS].set(a_neg)
    ant = a_neg.reshape(SSD_HEADS, 1)
    blk = (lambda s: nblk - 1 - s) if rev else (lambda s: s)
    row = lambda c: pl.BlockSpec((None, rb, c), lambda b, s: (b, blk(s), 0))
    full = lambda a: pl.BlockSpec(a.shape, lambda b, s: (0,) * a.ndim)
    st = pl.BlockSpec((None, SSD_STATE, SSD_INNER), lambda b, s: (b, 0, 0))
    return pl.pallas_call(
        functools.partial(_ssd_kernel, rev, rb // CHUNK, dir_),
        grid=(b_, nblk),
        in_specs=[row(512), row(256), row(256), row(2 * LANES),
                  pl.BlockSpec((None, 16, rb), lambda b, s: (b, 0, blk(s))),
                  full(an), full(ant), full(tri), full(trit), full(ex), st],
        out_specs=[row(512), st],
        out_shape=[jax.ShapeDtypeStruct((b_, r_, SSD_INNER), F32),
                   jax.ShapeDtypeStruct((b_, SSD_STATE, SSD_INNER), F32)],
        scratch_shapes=[pltpu.VMEM((SSD_STATE, SSD_INNER), F32)],
        compiler_params=_cparams(("arbitrary", "arbitrary")),
        name="ssd_bwd" if rev else "ssd_fwd",
    )(xs, bm, cm, dt, dtt, an, ant, tri, trit, ex, h0)


_HGRN_LEVELS = (64, 32, 16, 8, 4, 2, 1)


def _hgrn_consts(rev):
    i = np.arange(CHUNK)
    before = _tri(rev)
    after_strict = ~before
    mats, masks = [], []
    for c in _HGRN_LEVELS:
        same = (i[:, None] // c) == (i[None, :] // c)
        if c > 1:
            mats.append(same & before)
            mats.append(same & after_strict)
        if c < CHUNK:
            blk = i // c
            first = (blk % 2 == 1) if rev else (blk % 2 == 0)
            pair = first[None, :] & (~first)[:, None] & ((blk[:, None] // 2) == (blk[None, :] // 2))
            masks.append(pair)
    masks.append(np.eye(CHUNK, dtype=bool))
    m = np.concatenate(mats, axis=0).astype(np.float32)
    m = np.concatenate([m, m], axis=1)
    mk = np.stack(masks, axis=0).astype(np.float32)
    return jnp.asarray(m, BF16), jnp.asarray(mk, F32)


def _hgrn_kernel(rev, nchunk, q_ref, g_ref, v_ref, m_ref, mk_ref, s0_ref, o_ref, sT_ref, s_ref):
    s = pl.program_id(1)

    @pl.when(s == 0)
    def _():
        s_ref[...] = s0_ref[...]

    mall = m_ref[...]
    nl = len(_HGRN_LEVELS)
    last = 0 if rev else CHUNK - 1
    order = range(nchunk - 1, -1, -1) if rev else range(nchunk)
    for c in order:
        r0 = c * CHUNK
        g = g_ref[r0:r0 + CHUNK, :]
        g2 = jnp.concatenate(_split2(g), axis=0)
        eall = jnp.exp(jnp.dot(mall, g2, preferred_element_type=F32))
        ff = jnp.exp(g)
        kk = 1.0 - ff
        qq = q_ref[r0:r0 + CHUNK, :].astype(F32)
        vv = v_ref[r0:r0 + CHUNK, :]
        outs = []
        for h in range(HGRN_HEADS):
            ls = slice(h * LANES, (h + 1) * LANES)
            qh, kh, vh = qq[:, ls], kk[:, ls], vv[:, ls]
            khb = kh.astype(BF16)

            def fac(idx, ls=ls):
                return eall[idx * CHUNK:(idx + 1) * CHUNK, ls]

            att = _dot_nt(qh.astype(BF16), khb) * mk_ref[nl - 1]
            att = att + _dot_nt((qh * ff[:, ls]).astype(BF16), khb) * mk_ref[nl - 2]
            for l in range(1, nl - 1):
                qt = (qh * fac(2 * l)).astype(BF16)
                kt = (kh * fac(2 * l + 1)).astype(BF16)
                att = att + _dot_nt(qt, kt) * mk_ref[l - 1]
            st = s_ref[h]
            qs_ = (qh * fac(0)).astype(BF16)
            o = _dot_nt(qs_, st.astype(BF16)) + jnp.dot(att.astype(BF16), vh, preferred_element_type=F32)
            outs.append(o)
            ks_ = (kh * fac(1)).astype(BF16)
            dec = fac(0)[last:last + 1, :]
            s_ref[h] = st * dec + _dot_tn(vh, ks_)
        o_ref[r0:r0 + CHUNK, :] = jnp.concatenate(outs, axis=1)

    @pl.when(s == pl.num_programs(1) - 1)
    def _():
        sT_ref[...] = s_ref[...]


def _hgrn_scan(rev, q, g, v, s0, rb):
    b_, r_, _ = q.shape
    nblk = r_ // rb
    mall, mk = _hgrn_consts(rev)
    blk = (lambda s: nblk - 1 - s) if rev else (lambda s: s)
    row = pl.BlockSpec((None, rb, 512), lambda b, s: (b, blk(s), 0))
    full = lambda a: pl.BlockSpec(a.shape, lambda b, s: (0,) * a.ndim)
    st = pl.BlockSpec((None, HGRN_HEADS, LANES, HGRN_DK), lambda b, s: (b, 0, 0, 0))
    return pl.pallas_call(
        functools.partial(_hgrn_kernel, rev, rb // CHUNK),
        grid=(b_, nblk),
        in_specs=[row, row, row, full(mall), full(mk), st],
        out_specs=[row, st],
        out_shape=[jax.ShapeDtypeStruct((b_, r_, 512), F32),
                   jax.ShapeDtypeStruct((b_, HGRN_HEADS, LANES, HGRN_DK), F32)],
        scratch_shapes=[pltpu.VMEM((HGRN_HEADS, LANES, HGRN_DK), F32)],
        compiler_params=_cparams(("arbitrary", "arbitrary")),
        name="hgrn_bwd" if rev else "hgrn_fwd",
    )(q, g, v, mall, mk, s0)


def _outproj1_kernel(of_ref, ob_ref, yf_ref, yb_ref, sg_ref, sz_ref, xs_ref, h_ref, mod_ref,
                     on_ref, sk_ref, sn_ref, wo_ref, wy_ref, o_ref):
    o = of_ref[...] + ob_ref[...]
    on = on_ref[...]
    parts = []
    for hh in range(HGRN_HEADS):
        ls = slice(hh * LANES, (hh + 1) * LANES)
        parts.append(_rms(o[:, ls]) * on[:, ls])
    o = jnp.concatenate(parts, axis=1) * sg_ref[...]
    y = (yf_ref[...] + yb_ref[...] + sk_ref[...] * xs_ref[...]) * sz_ref[...]
    sn = sn_ref[...]
    gw = SSD_INNER // SSD_GROUPS
    parts = []
    for gg in range(SSD_GROUPS):
        ls = slice(gg * gw, (gg + 1) * gw)
        parts.append(_rms(y[:, ls]) * sn[:, ls])
    y = jnp.concatenate(parts, axis=1)
    out = (jnp.dot(o.astype(BF16), wo_ref[...], preferred_element_type=F32)
           + jnp.dot(y.astype(BF16), wy_ref[...], preferred_element_type=F32))
    o_ref[...] = h_ref[...] + mod_ref[2:3, :] * out


def _outproj1(of, ob, yf, yb, sg, sz, xs, h, mods, on, sk, sn, wo, wy, tr):
    g_, r_, d = h.shape
    row = lambda c: pl.BlockSpec((None, tr, c), lambda b, t: (b, t, 0))
    full = lambda a: pl.BlockSpec(a.shape, lambda b, t: (0,) * a.ndim)
    return pl.pallas_call(
        _outproj1_kernel,
        grid=(g_, r_ // tr),
        in_specs=[row(512)] * 7 + [row(d), pl.BlockSpec((None, 8, d), lambda b, t: (b, 0, 0)),
                                   full(on), full(sk), full(sn), full(wo), full(wy)],
        out_specs=row(d),
        out_shape=jax.ShapeDtypeStruct(h.shape, F32),
        compiler_params=_cparams(("arbitrary", "arbitrary")),
        name="outproj1",
    )(of, ob, yf, yb, sg, sz, xs, h, mods, on, sk, sn, wo, wy)


def _rope_tables(length):
    rows = length // GRID_W
    row = jnp.repeat(jnp.arange(rows, dtype=F32), GRID_W)
    col = jnp.tile(jnp.arange(GRID_W, dtype=F32), rows)

    def cs(rot_dim):
        n_freq = rot_dim // 4
        inv_freq = ROPE_BASE ** (-jnp.arange(n_freq, dtype=F32) / n_freq)
        ang = jnp.concatenate([row[:, None] * inv_freq, col[:, None] * inv_freq], axis=-1)
        c, s = jnp.cos(ang), jnp.sin(ang)
        return jnp.concatenate([c, c], axis=-1), jnp.concatenate([-s, s], axis=-1)

    ca, sa = cs(DA_HEAD_DIM)
    cq, sq = cs(MLA_ROPE)
    ones = jnp.ones((length, MLA_NOPE), F32)
    pad1 = jnp.ones((length, LANES - MLA_NOPE - MLA_ROPE), F32)
    cqt = jnp.concatenate([ones, cq, pad1], axis=-1)
    sqt = jnp.concatenate([0.0 * ones, sq, 0.0 * pad1], axis=-1)
    return jnp.tile(ca, (1, 2)), jnp.tile(sa, (1, 2)), cqt, sqt


def _att_weights(att_w_in, mla_q_norm, mla_w_uq, mla_kv_norm, mla_w_ukv):
    d = att_w_in.shape[0]
    kr = att_w_in[:, 2176:2208]
    kr_blk = jnp.concatenate([jnp.zeros((d, MLA_NOPE), F32), kr,
                              jnp.zeros((d, LANES - MLA_NOPE - MLA_ROPE), F32)], axis=1)
    win = jnp.concatenate([att_w_in[:, :1024], att_w_in[:, 1536:2176], kr_blk], axis=1).astype(BF16)
    wva = att_w_in[:, 1024:1536].T.astype(BF16)
    wq = mla_w_uq.reshape(MLA_Q_RANK, MLA_HEADS, MLA_NOPE + MLA_ROPE)
    wq = jnp.pad(wq, ((0, 0), (0, 0), (0, LANES - MLA_NOPE - MLA_ROPE))).reshape(MLA_Q_RANK, MLA_HEADS * LANES)
    wkv = mla_w_ukv.reshape(MLA_KV_RANK, MLA_HEADS, MLA_NOPE + MLA_V)
    wk = jnp.pad(wkv[:, :, :MLA_NOPE], ((0, 0), (0, 0), (0, LANES - MLA_NOPE))).reshape(MLA_KV_RANK, MLA_HEADS * LANES)
    wv = wkv[:, :, MLA_NOPE:].reshape(MLA_KV_RANK, MLA_HEADS * MLA_V)
    return dict(win=win, wva=wva, qn=mla_q_norm.reshape(1, -1), wuq=wq.astype(BF16),
                kvn=mla_kv_norm.reshape(1, -1), wk=wk.astype(BF16), wv=wv.T.astype(BF16))


def _rec_weights(rec_w_in, bound_logits, dt_bias):
    d = rec_w_in.shape[0]
    w = rec_w_in[:, :4096].astype(BF16)
    wdt = rec_w_in[:, 4096:4112]
    pad = jnp.zeros((d, LANES - SSD_HEADS), F32)
    wdt_rows = jnp.concatenate([wdt[:, :SSD_HEADS], pad, wdt[:, SSD_HEADS:], pad], axis=1).astype(BF16)
    zb = jnp.zeros((LANES - SSD_HEADS,), F32)
    dtb = jnp.concatenate([dt_bias[0], zb, dt_bias[1], zb]).reshape(1, 2 * LANES)
    return dict(w=w, wdt=wdt_rows, wdtt=wdt.T.astype(BF16), bl=bound_logits,
                dtb=dtb, dtbt=dt_bias.reshape(2 * SSD_HEADS, 1))


def kernel(x, c, ctx, c_ctx, w_mod, b_mod, norm_mix, norm_mlp, w_mlp_in, w_mlp_out, att_w_in, att_lambda, att_subnorm, mla_q_norm, mla_w_uq, mla_kv_norm, mla_w_ukv, att_w_out, rec_w_in, hgrn_bound_logits, hgrn_out_norm, ssd_conv_w, ssd_conv_b, ssd_a_log, ssd_dt_bias, ssd_skip, ssd_norm, rec_w_out, final_norm):
    b_, length, d = x.shape
    n_ctx = ctx.shape[1]
    assert w_mod.shape[0] == 2 and d == D_MODEL

    cc = jnp.zeros((8, d), F32).at[:b_].set(c).at[b_].set(c_ctx)
    mods = _mod_vectors(cc, w_mod, b_mod)
    mods = jnp.pad(mods.reshape(2, 8, 6, d), ((0, 0), (0, 0), (0, 2), (0, 0)))
    mods_lat = [mods[l, :b_] for l in range(2)]
    mods_ctx = [jnp.broadcast_to(mods[l, b_], (b_, 8, d)) for l in range(2)]
    row = lambda v: v.reshape(1, -1)

    tr = min(512, length)
    trc = min(256, n_ctx)
    w0 = _att_weights(att_w_in[0], mla_q_norm[0], mla_w_uq[0], mla_kv_norm[0], mla_w_ukv[0])
    tables = _rope_tables(length)
    qa, ka, va, qm, km, vm = _proj0(x, mods_lat[0], row(norm_mix[0]), w0, tables, tr)
    qa_c, ka_c, va_c, qm_c, km_c, vm_c = _proj0(ctx, mods_ctx[0], row(norm_mix[0]), w0, None, trc)
    lam_init = 0.8 - 0.6 * math.exp(-0.3 * 0)
    da_extra = (att_lambda[0], row(att_subnorm[0]))
    tq, tk = min(1024, length), min(1024, length)
    ya = _attention("da", qa, ka_c, va_c, ka, va, da_extra, tq, tk, lam_init)
    ym = _attention("mla", qm, km_c, vm_c, km, vm, None, tq, tk, lam_init)
    ya_c = _attention("da", qa_c, ka_c, va_c, None, None, da_extra, trc, tk, lam_init)
    ym_c = _attention("mla", qm_c, km_c, vm_c, None, None, None, trc, tk, lam_init)
    wout = att_w_out[0].astype(BF16)
    w1 = w_mlp_in.astype(BF16)
    w2 = w_mlp_out.astype(BF16)
    trm = min(1024, length)
    h_lat = _outproj0(ya, ym, x, mods_lat[0], wout[:512], wout[512:], tr)
    h_lat = _mlp(h_lat, mods_lat[0], row(norm_mlp[0]), w1[0], w2[0], None, trm)
    h_ctx = _outproj0(ya_c, ym_c, ctx, mods_ctx[0], wout[:512], wout[512:], trc)
    h_ctx = _mlp(h_ctx, mods_ctx[0], row(norm_mlp[0]), w1[0], w2[0], None, trc)

    w1r = _rec_weights(rec_w_in[0], hgrn_bound_logits, ssd_dt_bias[0])
    cw = ssd_conv_w[0]
    cb = row(ssd_conv_b[0])
    pc = _proj1(h_ctx, mods_ctx[1], row(norm_mix[1]), w1r, trc)
    pl_ = _proj1(h_lat, mods_lat[1], row(norm_mix[1]), w1r, tr)
    xs_c, bm_c, cm_c = _conv(pc[6], cw, cb, trc)
    xs_l, bm_l, cm_l = _conv(pl_[6], cw, cb, tr)
    rb = 256
    zs = jnp.zeros((b_, SSD_STATE, SSD_INNER), F32)
    zh = jnp.zeros((b_, HGRN_HEADS, LANES, HGRN_DK), F32)
    ys, os_ = [], []
    for rev in (False, True):
        dr = 1 if rev else 0
        _, hs = _ssd_scan(rev, xs_c, bm_c, cm_c, pc[7], pc[8], ssd_a_log[0, dr], zs, rb)
        y, _ = _ssd_scan(rev, xs_l, bm_l, cm_l, pl_[7], pl_[8], ssd_a_log[0, dr], hs, rb)
        ys.append(y)
        _, ss = _hgrn_scan(rev, pc[0], pc[1 + dr], pc[3], zh, rb)
        o, _ = _hgrn_scan(rev, pl_[0], pl_[1 + dr], pl_[3], ss, rb)
        os_.append(o)
    skip = row(jnp.repeat(ssd_skip[0], SSD_HEAD_DIM))
    wro = rec_w_out[0].astype(BF16)
    h_lat = _outproj1(os_[0], os_[1], ys[0], ys[1], pl_[4], pl_[5], xs_l, h_lat, mods_lat[1],
                      row(hgrn_out_norm[0]), skip, row(ssd_norm[0]), wro[:512], wro[512:], tr)
    return _mlp(h_lat, mods_lat[1], row(norm_mlp[1]), w1[1], w2[1], row(final_norm), trm)
```

```python
import functools
import math

import numpy as np
import jax
import jax.numpy as jnp
from jax import lax
from jax.experimental import pallas as pl
from jax.experimental.pallas import tpu as pltpu

F32 = jnp.float32
BF16 = jnp.bfloat16

D_MODEL = 1024
GRID_W = 64
DA_HEADS = 4
DA_HEAD_DIM = 64
MLA_HEADS = 8
MLA_NOPE = 64
MLA_ROPE = 32
MLA_V = 64
MLA_Q_RANK = 384
MLA_KV_RANK = 256
HGRN_HEADS = 4
HGRN_DK = 128
SSD_HEADS = 8
SSD_HEAD_DIM = 64
SSD_GROUPS = 2
SSD_STATE = 128
SSD_CONV_W = 5
SSD_INNER = SSD_HEADS * SSD_HEAD_DIM
MLP_HIDDEN = 4 * D_MODEL
ROPE_BASE = 10000.0
NORM_EPS = 1e-6
CHUNK = 64

LANES = 128
LOG2E = 1.4426950408889634
VMEM_LIMIT = 56 * 1024 * 1024


def _cparams(sem):
    return pltpu.CompilerParams(dimension_semantics=sem, vmem_limit_bytes=VMEM_LIMIT)


def _rms(x):
    return x * lax.rsqrt(jnp.mean(x * x, axis=-1, keepdims=True) + NORM_EPS)


def _modulate(x, g, shift, scale):
    return (_rms(x) * g) * (1.0 + scale) + shift


def _silu(x):
    return x * jax.nn.sigmoid(x)


def _bdot(a, b):
    return jnp.dot(a.astype(BF16), b.astype(BF16), preferred_element_type=F32)


def _dot_nt(a, b):
    return lax.dot_general(a, b, (((1,), (1,)), ((), ())), preferred_element_type=F32)


def _dot_tn(a, b):
    return lax.dot_general(a, b, (((0,), (0,)), ((), ())), preferred_element_type=F32)


def _split2(x):
    hi = x.astype(BF16)
    lo = (x - hi.astype(F32)).astype(BF16)
    return hi, lo


def _exact_left(m01, x):
    hi, lo = _split2(x)
    return (jnp.dot(m01, hi, preferred_element_type=F32) + jnp.dot(m01, lo, preferred_element_type=F32))


def _exact_right(x, m01):
    hi, lo = _split2(x)
    return (jnp.dot(hi, m01, preferred_element_type=F32) + jnp.dot(lo, m01, preferred_element_type=F32))


def _mod_kernel(c_ref, w_ref, b_ref, o_ref):
    a = _silu(c_ref[...]).astype(BF16)
    o_ref[...] = jnp.dot(a, w_ref[...].astype(BF16), preferred_element_type=F32) + b_ref[...]


def _mod_vectors(cc, w_mod, b_mod):
    depth, d, n = w_mod.shape
    tn = 1024
    return pl.pallas_call(
        _mod_kernel,
        grid=(depth, n // tn),
        in_specs=[
            pl.BlockSpec((8, d), lambda l, j: (0, 0)),
            pl.BlockSpec((None, d, tn), lambda l, j: (l, 0, j)),
            pl.BlockSpec((None, 1, tn), lambda l, j: (l, 0, j)),
        ],
        out_specs=pl.BlockSpec((None, 8, tn), lambda l, j: (l, 0, j)),
        out_shape=jax.ShapeDtypeStruct((depth, 8, n), F32),
        compiler_params=_cparams(("arbitrary", "arbitrary")),
        name="mod_vectors",
    )(cc, w_mod, b_mod.reshape(depth, 1, n))


def _rope_partner(x, half):
    n = x.shape[-1]
    lane = lax.broadcasted_iota(jnp.int32, x.shape, x.ndim - 1)
    up = pltpu.roll(x, n - half, x.ndim - 1)
    dn = pltpu.roll(x, half, x.ndim - 1)
    return jnp.where((lane % (2 * half)) < half, up, dn)


def _tile_lanes(x, n):
    return jnp.concatenate([x] * n, axis=-1)


def _proj0_kernel(use_rope, *refs):
    if use_rope:
        (h_ref, mod_ref, g_ref, win_ref, wva_ref, qn_ref, wuq_ref, kvn_ref, wk_ref, wv_ref,
         ca_ref, sa_ref, cq_ref, sq_ref,
         qa_ref, ka_ref, va_ref, qm_ref, km_ref, vm_ref) = refs
    else:
        (h_ref, mod_ref, g_ref, win_ref, wva_ref, qn_ref, wuq_ref, kvn_ref, wk_ref, wv_ref,
         qa_ref, ka_ref, va_ref, qm_ref, km_ref, vm_ref) = refs
    mod = mod_ref[...]
    ub = _modulate(h_ref[...], g_ref[...], mod[0:1], mod[1:2]).astype(BF16)

    def proj(lo, hi):
        return jnp.dot(ub, win_ref[:, lo:hi], preferred_element_type=F32)

    qa = proj(0, 512) * (DA_HEAD_DIM ** -0.5 * LOG2E)
    ka = proj(512, 1024)
    va_ref[...] = _dot_nt(wva_ref[...], ub).astype(BF16)
    cq = proj(1024, 1408)
    ckv = proj(1408, 1664)
    kr = proj(1664, 1792)
    cqn = (_rms(cq) * qn_ref[...]).astype(BF16)
    qm = jnp.dot(cqn, wuq_ref[...], preferred_element_type=F32) * ((MLA_NOPE + MLA_ROPE) ** -0.5 * LOG2E)
    ckvn = (_rms(ckv) * kvn_ref[...]).astype(BF16)
    kn = jnp.dot(ckvn, wk_ref[...], preferred_element_type=F32)
    vm_ref[...] = _dot_nt(wv_ref[...], ckvn).astype(BF16)
    if use_rope:
        ca = _tile_lanes(ca_ref[...], DA_HEADS)
        sa = _tile_lanes(sa_ref[...], DA_HEADS)
        qa = qa * ca + _rope_partner(qa, DA_HEAD_DIM // 2) * sa
        ka = ka * ca + _rope_partner(ka, DA_HEAD_DIM // 2) * sa
        cq1 = cq_ref[...]
        sq1 = sq_ref[...]
        qm = qm * _tile_lanes(cq1, MLA_HEADS) + _rope_partner(qm, MLA_ROPE // 2) * _tile_lanes(sq1, MLA_HEADS)
        kr = kr * cq1 + _rope_partner(kr, MLA_ROPE // 2) * sq1
    qa_ref[...] = qa.astype(BF16)
    ka_ref[...] = ka.astype(BF16)
    qm_ref[...] = qm.astype(BF16)
    km_ref[...] = (kn + _tile_lanes(kr, MLA_HEADS)).astype(BF16)


def _proj0(h, mods, gain, w, tables, tr):
    g_, r_, d = h.shape
    use_rope = tables is not None
    row = lambda c: pl.BlockSpec((None, tr, c), lambda b, t: (b, t, 0))
    full = lambda a: pl.BlockSpec(a.shape, lambda b, t: (0,) * a.ndim)
    in_specs = [row(d), pl.BlockSpec((None, 8, d), lambda b, t: (b, 0, 0)), full(gain),
                full(w["win"]), full(w["wva"]), full(w["qn"]), full(w["wuq"]), full(w["kvn"]), full(w["wk"]),
                full(w["wv"])]
    args = [h, mods, gain, w["win"], w["wva"], w["qn"], w["wuq"], w["kvn"], w["wk"], w["wv"]]
    if use_rope:
        in_specs += [pl.BlockSpec((tr, LANES), lambda b, t: (t, 0))] * 4
        args += list(tables)
    col = pl.BlockSpec((None, 512, tr), lambda b, t: (b, 0, t))
    widths = (512, 512, None, 1024, 1024, None)
    return pl.pallas_call(
        functools.partial(_proj0_kernel, use_rope),
        grid=(g_, r_ // tr),
        in_specs=in_specs,
        out_specs=[col if c is None else row(c) for c in widths],
        out_shape=[jax.ShapeDtypeStruct((g_, 512, r_) if c is None else (g_, r_, c), BF16) for c in widths],
        compiler_params=_cparams(("arbitrary", "arbitrary")),
        name="proj0_rope" if use_rope else "proj0_ctx",
    )(*args)


def _attn_kernel(mode, has_lat, tk, nk, lam_init, *refs):
    refs = list(refs)
    q_ref, kc_ref, vc_ref = refs[:3]
    refs = refs[3:]
    if has_lat:
        kl_ref, vl_ref = refs[:2]
        refs = refs[2:]
    if mode == "da":
        lam_ref, sub_ref = refs[:2]
        refs = refs[2:]
    o_ref, ma_ref, la_ref, acca_ref, mb_ref, lb_ref, accb_ref = refs[:7]
    if has_lat:
        p_ref, al_ref = refs[7:]

    q = q_ref[...]
    if mode == "da":
        lane = lax.broadcasted_iota(jnp.int32, q.shape, 1)
        zero = jnp.zeros_like(q)
        qs = (jnp.where(lane < DA_HEAD_DIM, q, zero), jnp.where(lane >= DA_HEAD_DIM, q, zero))
    else:
        qs = (q[:, :LANES], q[:, LANES:])
    stats = ((ma_ref, la_ref, acca_ref), (mb_ref, lb_ref, accb_ref))
    for m_ref, l_ref, acc_ref in stats:
        m_ref[...] = jnp.full(m_ref.shape, -jnp.inf, F32)
        l_ref[...] = jnp.zeros(l_ref.shape, F32)
        acc_ref[...] = jnp.zeros(acc_ref.shape, F32)

    def scores(k):
        ks = (k, k) if mode == "da" else (k[:, :LANES], k[:, LANES:])
        return [_dot_nt(ks[i], qs[i]) for i in range(2)]

    def softmax(i, s):
        m_ref, l_ref, _ = stats[i]
        m_prev = m_ref[...]
        m_new = jnp.maximum(m_prev, jnp.max(s, axis=0, keepdims=True))
        alpha = jnp.exp2(m_prev - m_new)
        p = jnp.exp2(s - m_new)
        l_ref[...] = alpha * l_ref[...] + jnp.sum(p, axis=0, keepdims=True)
        m_ref[...] = m_new
        return p.astype(BF16), alpha

    def accumulate(i, vt, p, alpha):
        acc_ref = stats[i][2]
        acc_ref[...] = alpha * acc_ref[...] + jnp.dot(vt, p, preferred_element_type=F32)

    if not has_lat:
        ss = scores(kc_ref[...])
        pa = [softmax(i, ss[i]) for i in range(2)]
        for i in range(2):
            accumulate(i, vc_ref[...], *pa[i])
    else:
        s_ref, mx_ref = p_ref, al_ref
        nc = kc_ref.shape[0]

        def chunk(c):
            return pl.ds(c * tk if isinstance(c, int) else pl.multiple_of(c * tk, tk), tk)

        def keys(st):
            return kc_ref[...] if isinstance(st, int) and st == 0 else kl_ref[chunk(st - 1), :]

        def vals(st):
            return vc_ref[...] if isinstance(st, int) and st == 0 else vl_ref[:, chunk(st - 1)]

        def lookahead(st, slot, n):
            ss = scores(keys(st))
            for i in range(2):
                s_ref[slot, i, 0:n, :] = ss[i]
                mx_ref[1, i] = jnp.maximum(mx_ref[0, i], jnp.max(ss[i], axis=0, keepdims=True))

        def consume(st, slot, n):
            vt = vals(st)
            for i in range(2):
                m_ref, l_ref, acc_ref = stats[i]
                m_cur = mx_ref[0, i]
                alpha = jnp.exp2(m_ref[...] - m_cur)
                p = jnp.exp2(s_ref[slot, i, 0:n, :] - m_cur)
                l_ref[...] = alpha * l_ref[...] + jnp.sum(p, axis=0, keepdims=True)
                acc_ref[...] = alpha * acc_ref[...] + jnp.dot(vt, p.astype(BF16), preferred_element_type=F32)
                m_ref[...] = m_cur

        def advance():
            for i in range(2):
                mx_ref[0, i] = mx_ref[1, i]

        def run(st, slot, has_next):
            n = nc if isinstance(st, int) and st == 0 else tk
            if has_next:
                lookahead(st + 1, 1 - slot, tk)
            consume(st, slot, n)
            if has_next:
                advance()

        for i in range(2):
            mx_ref[0, i] = jnp.full((1, mx_ref.shape[-1]), -jnp.inf, F32)
        lookahead(0, 0, nc)
        advance()
        run(0, 0, True)
        npair = (nk - 1) // 2

        def body(j, carry):
            run(2 * j + 1, 1, True)
            run(2 * j + 2, 0, True)
            return carry

        lax.fori_loop(0, npair, body, 0)
        for st in range(2 * npair + 1, nk + 1):
            run(st, st % 2, st < nk)

    oa = acca_ref[...] / la_ref[...]
    ob = accb_ref[...] / lb_ref[...]
    if mode == "da":
        lp = lam_ref[...]
        lam = (jnp.exp(jnp.sum(lp[0:1] * lp[1:2], axis=1, keepdims=True))
               - jnp.exp(jnp.sum(lp[2:3] * lp[3:4], axis=1, keepdims=True)) + lam_init)
        dlt = (oa - lam * ob).T
        o_ref[...] = ((_rms(dlt) * sub_ref[...]) * (1.0 - lam_init)).astype(o_ref.dtype)
    else:
        chan = lax.broadcasted_iota(jnp.int32, oa.shape, 0)
        o_ref[...] = jnp.where(chan < MLA_V, oa, ob).T.astype(o_ref.dtype)


def _attention(mode, q, kc, vc, kl, vl, extra, tq, tk, lam_init):
    b_, sq, _ = q.shape
    has_lat = kl is not None
    w = LANES if mode == "da" else 2 * LANES
    nh = 4
    kblk = lambda a: pl.BlockSpec((None, a.shape[1], w), lambda b, h, t: (b, 0, h))
    vblk = lambda a: pl.BlockSpec((None, LANES, a.shape[2]), lambda b, h, t: (b, h, 0))
    in_specs = [pl.BlockSpec((None, tq, w), lambda b, h, t: (b, t, h)), kblk(kc), vblk(vc)]
    args = [q, kc, vc]
    nk = 0
    if has_lat:
        in_specs += [kblk(kl), vblk(vl)]
        args += [kl, vl]
        nk = kl.shape[1] // tk
    if mode == "da":
        lam_p, subnorm = extra
        in_specs += [pl.BlockSpec(lam_p.shape, lambda b, h, t: (0, 0)),
                     pl.BlockSpec(subnorm.shape, lambda b, h, t: (0, 0))]
        args += [lam_p, subnorm]
    scratch = []
    for _ in range(2):
        scratch += [pltpu.VMEM((1, tq), F32), pltpu.VMEM((1, tq), F32), pltpu.VMEM((LANES, tq), F32)]
    if has_lat:
        scratch += [pltpu.VMEM((2, 2, tk, tq), F32), pltpu.VMEM((2, 2, 1, tq), F32)]
    return pl.pallas_call(
        functools.partial(_attn_kernel, mode, has_lat, tk, nk, lam_init),
        grid=(b_, nh, sq // tq),
        in_specs=in_specs,
        out_specs=pl.BlockSpec((None, tq, LANES), lambda b, h, t: (b, t, h)),
        out_shape=jax.ShapeDtypeStruct((b_, sq, nh * LANES), BF16),
        scratch_shapes=scratch,
        compiler_params=_cparams(("arbitrary", "arbitrary", "arbitrary")),
        name=f"attn_{mode}_{'lat' if has_lat else 'ctx'}",
    )(*args)


def _merge_attention(ya_ref, ym_ref, wa_ref, wm_ref):
    return (jnp.dot(ya_ref[...], wa_ref[...], preferred_element_type=F32)
            + jnp.dot(ym_ref[...], wm_ref[...], preferred_element_type=F32))


def _merge_recurrent(of_ref, ob_ref, yf_ref, yb_ref, sg_ref, sz_ref, xs_ref, on_ref, sk_ref, sn_ref, wo_ref, wy_ref):
    o = of_ref[...].astype(F32) + ob_ref[...].astype(F32)
    on = on_ref[...]
    parts = []
    for hh in range(HGRN_HEADS):
        ls = slice(hh * LANES, (hh + 1) * LANES)
        parts.append(_rms(o[:, ls]) * on[:, ls])
    o = jnp.concatenate(parts, axis=1) * sg_ref[...].astype(F32)
    y = ((yf_ref[...].astype(F32) + yb_ref[...].astype(F32) + sk_ref[...] * xs_ref[...])
         * sz_ref[...].astype(F32))
    sn = sn_ref[...]
    gw = SSD_INNER // SSD_GROUPS
    parts = []
    for gg in range(SSD_GROUPS):
        ls = slice(gg * gw, (gg + 1) * gw)
        parts.append(_rms(y[:, ls]) * sn[:, ls])
    y = jnp.concatenate(parts, axis=1)
    return (jnp.dot(o.astype(BF16), wo_ref[...], preferred_element_type=F32)
            + jnp.dot(y.astype(BF16), wy_ref[...], preferred_element_type=F32))


def _mix_mlp_kernel(merge, n_mix, final, *refs):
    mix_refs = refs[:n_mix]
    if final:
        h_ref, mod_ref, g_ref, w1_ref, w2_ref, fg_ref, o_ref, u_ref, acc_ref = refs[n_mix:]
    else:
        h_ref, mod_ref, g_ref, w1_ref, w2_ref, o_ref, u_ref, acc_ref = refs[n_mix:]
    j = pl.program_id(2)

    @pl.when(j == 0)
    def _():
        h1 = h_ref[...] + mod_ref[2:3, :] * merge(*mix_refs)
        o_ref[...] = h1
        u_ref[...] = _modulate(h1, g_ref[...], mod_ref[3:4, :], mod_ref[4:5, :]).astype(BF16)
        acc_ref[...] = jnp.zeros(acc_ref.shape, F32)

    a = jnp.dot(u_ref[...], w1_ref[...], preferred_element_type=F32)
    a = jnp.square(jnp.maximum(a, 0.0)).astype(BF16)
    acc_ref[...] += jnp.dot(a, w2_ref[...], preferred_element_type=F32)

    @pl.when(j == pl.num_programs(2) - 1)
    def _():
        out = o_ref[...] + mod_ref[5:6, :] * acc_ref[...]
        if final:
            out = _rms(out) * fg_ref[...]
        o_ref[...] = out


def _mix_mlp(merge, mix_rows, mix_full, h, mods, gain, w1, w2, final_gain, tr, th):
    g_, r_, d = h.shape
    hid = w1.shape[1]
    final = final_gain is not None
    row = lambda c: pl.BlockSpec((None, tr, c), lambda b, t, j: (b, t, 0))
    full = lambda a: pl.BlockSpec(a.shape, lambda b, t, j: (0,) * a.ndim)
    in_specs = ([row(a.shape[-1]) for a in mix_rows] + [full(a) for a in mix_full]
                + [row(d), pl.BlockSpec((None, 8, d), lambda b, t, j: (b, 0, 0)), full(gain),
                   pl.BlockSpec((d, th), lambda b, t, j: (0, j)),
                   pl.BlockSpec((th, d), lambda b, t, j: (j, 0))])
    args = list(mix_rows) + list(mix_full) + [h, mods, gain, w1, w2]
    if final:
        in_specs.append(full(final_gain))
        args.append(final_gain)
    return pl.pallas_call(
        functools.partial(_mix_mlp_kernel, merge, len(mix_rows) + len(mix_full), final),
        grid=(g_, r_ // tr, hid // th),
        in_specs=in_specs,
        out_specs=row(d),
        out_shape=jax.ShapeDtypeStruct(h.shape, F32),
        scratch_shapes=[pltpu.VMEM((tr, d), BF16), pltpu.VMEM((tr, d), F32)],
        compiler_params=_cparams(("arbitrary", "arbitrary", "arbitrary")),
        name="mix_mlp_final" if final else "mix_mlp",
    )(*args)


def _softplus(x):
    return jnp.maximum(x, 0.0) + jnp.log1p(jnp.exp(-jnp.abs(x)))


def _proj1_kernel(h_ref, mod_ref, g_ref, w_ref, wdt_ref, bl_ref, dtb_ref,
                  q_ref, gf_ref, gb_ref, i_ref, sg_ref, sz_ref, xbc_ref, dt_ref, dtt_ref):
    mod = mod_ref[...]
    ub = _modulate(h_ref[...], g_ref[...], mod[0:1], mod[1:2]).astype(BF16)

    def proj(lo, hi):
        return jnp.dot(ub, w_ref[:, lo:hi], preferred_element_type=F32)

    bl = bl_ref[...]
    e = jnp.exp(bl - jnp.max(bl, axis=0, keepdims=True))
    gamma = e / jnp.sum(e, axis=0, keepdims=True)
    lb = (gamma[0:1] + gamma[1:2]) - gamma[0:1]
    q_ref[...] = _silu(proj(0, 512)).astype(BF16)
    for k, out in ((0, gf_ref), (1, gb_ref)):
        lbk = lb[:, 512 * k:512 * (k + 1)]
        f = lbk + (1.0 - lbk) * jax.nn.sigmoid(proj(512 * (k + 1), 512 * (k + 2)))
        out[...] = jnp.log(f)
    i_ref[...] = proj(1536, 2048).astype(BF16)
    sg_ref[...] = _silu(proj(2048, 2560)).astype(BF16)
    sz_ref[...] = _silu(proj(2560, 3072)).astype(BF16)
    xbc_ref[...] = proj(3072, 4096)
    dt = _softplus(jnp.dot(ub, wdt_ref[...], preferred_element_type=F32) + dtb_ref[...])
    dt_ref[...] = dt
    dtt_ref[...] = jnp.concatenate([dt[:, :LANES].T[:SSD_HEADS], dt[:, LANES:].T[:SSD_HEADS]], axis=0)


def _proj1(h, mods, gain, w, tr):
    g_, r_, d = h.shape
    row = lambda c: pl.BlockSpec((None, tr, c), lambda b, t: (b, t, 0))
    full = lambda a: pl.BlockSpec(a.shape, lambda b, t: (0,) * a.ndim)
    names = ("w", "wdt", "bl", "dtb")
    widths = (512, 512, 512, 512, 512, 512, 1024, 2 * LANES)
    dts = (BF16, F32, F32, BF16, BF16, BF16, F32, F32)
    return pl.pallas_call(
        _proj1_kernel,
        grid=(g_, r_ // tr),
        in_specs=[row(d), pl.BlockSpec((None, 8, d), lambda b, t: (b, 0, 0)), full(gain)]
        + [full(w[n]) for n in names],
        out_specs=[row(c) for c in widths] + [pl.BlockSpec((None, 16, tr), lambda b, t: (b, 0, t))],
        out_shape=[jax.ShapeDtypeStruct((g_, r_, c), t_) for c, t_ in zip(widths, dts)]
        + [jax.ShapeDtypeStruct((g_, 16, r_), F32)],
        compiler_params=_cparams(("arbitrary", "arbitrary")),
        name="proj1",
    )(h, mods, gain, *[w[n] for n in names])


def _conv_kernel(nt, x_ref, p_ref, n_ref, w_ref, b_ref, xs_ref, bm_ref, cm_ref, pad_ref):
    t = pl.program_id(1)
    tr = x_ref.shape[0]
    pad = SSD_CONV_W // 2
    pad_ref[0:8, :] = jnp.where(t > 0, p_ref[...], 0.0)
    pad_ref[8:8 + tr, :] = x_ref[...]
    pad_ref[8 + tr:16 + tr, :] = jnp.where(t < nt - 1, n_ref[...], 0.0)
    y = jnp.zeros(x_ref.shape, F32) + b_ref[...]
    for j in range(SSD_CONV_W):
        y = y + pad_ref[8 - pad + j:8 - pad + j + tr, :] * w_ref[j:j + 1, :]
    y = _silu(y)
    xs_ref[...] = y[:, :SSD_INNER]
    bm_ref[...] = y[:, SSD_INNER:SSD_INNER + 256].astype(BF16)
    cm_ref[...] = y[:, SSD_INNER + 256:].astype(BF16)


def _conv(xbc, w, b, tr):
    g_, r_, c = xbc.shape
    nt = r_ // tr
    nb = tr // 8
    row = lambda cc: pl.BlockSpec((None, tr, cc), lambda g, t: (g, t, 0))
    return pl.pallas_call(
        functools.partial(_conv_kernel, nt),
        grid=(g_, nt),
        in_specs=[row(c),
                  pl.BlockSpec((None, 8, c), lambda g, t: (g, jnp.maximum(t * nb - 1, 0), 0)),
                  pl.BlockSpec((None, 8, c), lambda g, t: (g, jnp.minimum((t + 1) * nb, nt * nb - 1), 0)),
                  pl.BlockSpec(w.shape, lambda g, t: (0, 0)),
                  pl.BlockSpec(b.shape, lambda g, t: (0, 0))],
        out_specs=[row(512), row(256), row(256)],
        out_shape=[jax.ShapeDtypeStruct((g_, r_, 512), F32),
                   jax.ShapeDtypeStruct((g_, r_, 256), BF16),
                   jax.ShapeDtypeStruct((g_, r_, 256), BF16)],
        scratch_shapes=[pltpu.VMEM((tr + 16, c), F32)],
        compiler_params=_cparams(("arbitrary", "arbitrary")),
        name="ssd_conv",
    )(xbc, xbc, xbc, w, b)


def _tri(rev, strict=False):
    i = np.arange(CHUNK)
    if rev:
        m = i[None, :] > i[:, None] if strict else i[None, :] >= i[:, None]
    else:
        m = i[None, :] < i[:, None] if strict else i[None, :] <= i[:, None]
    return m


def _ssd_consts(rev):
    tri = _tri(rev).astype(np.float32)
    expand = np.zeros((LANES, SSD_INNER), np.float32)
    for h in range(SSD_HEADS):
        expand[h, h * SSD_HEAD_DIM:(h + 1) * SSD_HEAD_DIM] = 1.0
    return jnp.asarray(tri, BF16), jnp.asarray(tri.T.copy(), BF16), jnp.asarray(expand, BF16)


def _ssd_chunk(rev, r0, x_ref, bm_ref, cm_ref, dt_ref, dtt_ref, an, ant, tri, trit, ex, h_ref, y_ref):
    dir_ = 1 if rev else 0
    ti = lax.broadcasted_iota(jnp.int32, (CHUNK, CHUNK), 0)
    si = lax.broadcasted_iota(jnp.int32, (CHUNK, CHUNK), 1)
    mask = (si >= ti) if rev else (si <= ti)
    lane = lax.broadcasted_iota(jnp.int32, (CHUNK, LANES), 1)
    last = 0 if rev else CHUNK - 1
    x = x_ref[r0:r0 + CHUNK, :]
    bm = bm_ref[r0:r0 + CHUNK, :]
    cm = cm_ref[r0:r0 + CHUNK, :]
    dt = dt_ref[r0:r0 + CHUNK, dir_ * LANES:(dir_ + 1) * LANES]
    dtt = dtt_ref[8 * dir_:8 * dir_ + 8, r0:r0 + CHUNK]
    acol = _exact_left(tri, dt * an)
    arow = _exact_right(dtt * ant, trit)
    dte = _exact_right(dt, ex)
    ae = _exact_right(acol, ex)
    ae_last = ae[last:last + 1, :]
    xdt = x * dte
    xw = (xdt * jnp.exp(ae_last - ae)).astype(BF16)
    xdtb = xdt.astype(BF16)
    ys = []
    for g in range(SSD_GROUPS):
        bg = bm[:, g * SSD_STATE:(g + 1) * SSD_STATE]
        cg = cm[:, g * SSD_STATE:(g + 1) * SSD_STATE]
        cb = _dot_nt(cg, bg)
        hs = h_ref[:, g * 256:(g + 1) * 256]
        yoff = jnp.dot(cg, hs.astype(BF16), preferred_element_type=F32) * jnp.exp(ae[:, g * 256:(g + 1) * 256])
        for pr in range(2):
            xp = xdtb[:, g * 256 + pr * LANES:g * 256 + (pr + 1) * LANES]
            res = []
            for k in range(2):
                hd = g * 4 + pr * 2 + k
                dmat = acol[:, hd:hd + 1] - arow[hd:hd + 1, :]
                lmat = jnp.exp(jnp.where(mask, dmat, -jnp.inf))
                res.append(jnp.dot((cb * lmat).astype(BF16), xp, preferred_element_type=F32))
            ys.append(jnp.where(lane < SSD_HEAD_DIM, res[0], res[1]) + yoff[:, pr * LANES:(pr + 1) * LANES])
        h_ref[:, g * 256:(g + 1) * 256] = (hs * jnp.exp(ae_last[:, g * 256:(g + 1) * 256])
                                           + _dot_tn(bg, xw[:, g * 256:(g + 1) * 256]))
    y_ref[r0:r0 + CHUNK, :] = jnp.concatenate(ys, axis=1).astype(y_ref.dtype)


def _ssd_kernel(nchunk, xf_ref, bmf_ref, cmf_ref, dtf_ref, dttf_ref, xb_ref, bmb_ref, cmb_ref, dtb_ref, dttb_ref,
                an_ref, ant_ref, tri_ref, trit_ref, ex_ref, h0f_ref, h0b_ref,
                yf_ref, hTf_ref, yb_ref, hTb_ref, hf_ref, hb_ref):
    s = pl.program_id(1)

    @pl.when(s == 0)
    def _():
        hf_ref[...] = h0f_ref[...]
        hb_ref[...] = h0b_ref[...]

    ex = ex_ref[...]
    cf = (an_ref[0], ant_ref[0], tri_ref[0], trit_ref[0], ex)
    cb = (an_ref[1], ant_ref[1], tri_ref[1], trit_ref[1], ex)
    for j in range(nchunk):
        _ssd_chunk(False, j * CHUNK, xf_ref, bmf_ref, cmf_ref, dtf_ref, dttf_ref, *cf, hf_ref, yf_ref)
        _ssd_chunk(True, (nchunk - 1 - j) * CHUNK, xb_ref, bmb_ref, cmb_ref, dtb_ref, dttb_ref, *cb, hb_ref, yb_ref)

    @pl.when(s == pl.num_programs(1) - 1)
    def _():
        hTf_ref[...] = hf_ref[...]
        hTb_ref[...] = hb_ref[...]


def _ssd_scan(xs, bm, cm, dt, dtt, a_log, h0f, h0b, rb):
    b_, r_, _ = xs.shape
    nblk = r_ // rb
    trif, tritf, ex = _ssd_consts(False)
    trib, tritb, _ = _ssd_consts(True)
    tri = jnp.stack([trif, trib])
    trit = jnp.stack([tritf, tritb])
    a_neg = -jnp.exp(a_log.astype(F32))
    an = jnp.zeros((2, 1, LANES), F32).at[:, 0, :SSD_HEADS].set(a_neg)
    ant = a_neg.reshape(2, SSD_HEADS, 1)
    rowf = lambda c: pl.BlockSpec((None, rb, c), lambda b, s: (b, s, 0))
    rowb = lambda c: pl.BlockSpec((None, rb, c), lambda b, s: (b, nblk - 1 - s, 0))
    full = lambda a: pl.BlockSpec(a.shape, lambda b, s: (0,) * a.ndim)
    st = pl.BlockSpec((None, SSD_STATE, SSD_INNER), lambda b, s: (b, 0, 0))
    y_shape = jax.ShapeDtypeStruct((b_, r_, SSD_INNER), BF16)
    h_shape = jax.ShapeDtypeStruct((b_, SSD_STATE, SSD_INNER), F32)
    return pl.pallas_call(
        functools.partial(_ssd_kernel, rb // CHUNK),
        grid=(b_, nblk),
        in_specs=[rowf(512), rowf(256), rowf(256), rowf(2 * LANES),
                  pl.BlockSpec((None, 16, rb), lambda b, s: (b, 0, s)),
                  rowb(512), rowb(256), rowb(256), rowb(2 * LANES),
                  pl.BlockSpec((None, 16, rb), lambda b, s: (b, 0, nblk - 1 - s)),
                  full(an), full(ant), full(tri), full(trit), full(ex), st, st],
        out_specs=[rowf(512), st, rowb(512), st],
        out_shape=[y_shape, h_shape, y_shape, h_shape],
        scratch_shapes=[pltpu.VMEM((SSD_STATE, SSD_INNER), F32)] * 2,
        compiler_params=_cparams(("arbitrary", "arbitrary")),
        name="ssd_scan",
    )(xs, bm, cm, dt, dtt, xs, bm, cm, dt, dtt, an, ant, tri, trit, ex, h0f, h0b)


_HGRN_LEVELS = (64, 32, 16, 8, 4, 2, 1)


def _hgrn_consts(rev):
    i = np.arange(CHUNK)
    before = _tri(rev)
    after_strict = ~before
    mats, masks = [], []
    for c in _HGRN_LEVELS:
        same = (i[:, None] // c) == (i[None, :] // c)
        q_side = same & before
        k_side = same & after_strict
        if c == CHUNK:
            mats += [q_side, k_side]
        else:
            blk = i // c
            first = (blk % 2 == 1) if rev else (blk % 2 == 0)
            if c > 1:
                mats.append(np.where(first[:, None], k_side, q_side))
            pair = first[None, :] & (~first)[:, None] & ((blk[:, None] // 2) == (blk[None, :] // 2))
            masks.append(pair)
    masks.append(np.eye(CHUNK, dtype=bool))
    m = np.concatenate(mats, axis=0).astype(np.float32)
    m = np.concatenate([m, m], axis=1)
    mk = np.stack(masks, axis=0).astype(np.float32)
    return jnp.asarray(m, BF16), jnp.asarray(mk, F32)


def _hgrn_chunk(rev, r0, q_ref, g_ref, v_ref, mall, mk_ref, s_ref, o_ref):
    nl = len(_HGRN_LEVELS)
    last = 0 if rev else CHUNK - 1
    g = g_ref[r0:r0 + CHUNK, :]
    g2 = jnp.concatenate(_split2(g), axis=0)
    eall = jnp.exp(jnp.dot(mall, g2, preferred_element_type=F32))
    ff = jnp.exp(g)
    kk = 1.0 - ff
    qq = q_ref[r0:r0 + CHUNK, :].astype(F32)
    vv = v_ref[r0:r0 + CHUNK, :]
    outs = []
    for h in range(HGRN_HEADS):
        ls = slice(h * LANES, (h + 1) * LANES)
        qh, kh, vh = qq[:, ls], kk[:, ls], vv[:, ls]
        khb = kh.astype(BF16)

        def fac(idx, ls=ls):
            return eall[idx * CHUNK:(idx + 1) * CHUNK, ls]

        att = _dot_nt(qh.astype(BF16), khb) * mk_ref[nl - 1]
        att = att + _dot_nt((qh * ff[:, ls]).astype(BF16), khb) * mk_ref[nl - 2]
        for l in range(1, nl - 1):
            e = fac(l + 1)
            att = att + _dot_nt((qh * e).astype(BF16), (kh * e).astype(BF16)) * mk_ref[l - 1]
        st = s_ref[h]
        qs_ = (qh * fac(0)).astype(BF16)
        o = _dot_nt(qs_, st.astype(BF16)) + jnp.dot(att.astype(BF16), vh, preferred_element_type=F32)
        outs.append(o)
        ks_ = (kh * fac(1)).astype(BF16)
        dec = fac(0)[last:last + 1, :]
        s_ref[h] = st * dec + _dot_tn(vh, ks_)
    o_ref[r0:r0 + CHUNK, :] = jnp.concatenate(outs, axis=1).astype(o_ref.dtype)


def _hgrn_kernel(nchunk, qf_ref, gf_ref, vf_ref, qb_ref, gb_ref, vb_ref, mf_ref, mkf_ref, mb_ref, mkb_ref,
                 s0f_ref, s0b_ref, of_ref, sTf_ref, ob_ref, sTb_ref, sf_ref, sb_ref):
    s = pl.program_id(1)

    @pl.when(s == 0)
    def _():
        sf_ref[...] = s0f_ref[...]
        sb_ref[...] = s0b_ref[...]

    mallf = mf_ref[...]
    mallb = mb_ref[...]
    for j in range(nchunk):
        _hgrn_chunk(False, j * CHUNK, qf_ref, gf_ref, vf_ref, mallf, mkf_ref, sf_ref, of_ref)
        _hgrn_chunk(True, (nchunk - 1 - j) * CHUNK, qb_ref, gb_ref, vb_ref, mallb, mkb_ref, sb_ref, ob_ref)

    @pl.when(s == pl.num_programs(1) - 1)
    def _():
        sTf_ref[...] = sf_ref[...]
        sTb_ref[...] = sb_ref[...]


def _hgrn_scan(q, gf, gb, v, s0f, s0b, rb):
    b_, r_, _ = q.shape
    nblk = r_ // rb
    mallf, mkf = _hgrn_consts(False)
    mallb, mkb = _hgrn_consts(True)
    rowf = pl.BlockSpec((None, rb, 512), lambda b, s: (b, s, 0))
    rowb = pl.BlockSpec((None, rb, 512), lambda b, s: (b, nblk - 1 - s, 0))
    full = lambda a: pl.BlockSpec(a.shape, lambda b, s: (0,) * a.ndim)
    st = pl.BlockSpec((None, HGRN_HEADS, LANES, HGRN_DK), lambda b, s: (b, 0, 0, 0))
    o_shape = jax.ShapeDtypeStruct((b_, r_, 512), BF16)
    s_shape = jax.ShapeDtypeStruct((b_, HGRN_HEADS, LANES, HGRN_DK), F32)
    return pl.pallas_call(
        functools.partial(_hgrn_kernel, rb // CHUNK),
        grid=(b_, nblk),
        in_specs=[rowf, rowf, rowf, rowb, rowb, rowb, full(mallf), full(mkf), full(mallb), full(mkb), st, st],
        out_specs=[rowf, st, rowb, st],
        out_shape=[o_shape, s_shape, o_shape, s_shape],
        scratch_shapes=[pltpu.VMEM((HGRN_HEADS, LANES, HGRN_DK), F32)] * 2,
        compiler_params=_cparams(("arbitrary", "arbitrary")),
        name="hgrn_scan",
    )(q, gf, v, q, gb, v, mallf, mkf, mallb, mkb, s0f, s0b)


def _rope_tables(length):
    rows = length // GRID_W
    row = jnp.repeat(jnp.arange(rows, dtype=F32), GRID_W)
    col = jnp.tile(jnp.arange(GRID_W, dtype=F32), rows)

    def cs(rot_dim):
        n_freq = rot_dim // 4
        inv_freq = ROPE_BASE ** (-jnp.arange(n_freq, dtype=F32) / n_freq)
        ang = jnp.concatenate([row[:, None] * inv_freq, col[:, None] * inv_freq], axis=-1)
        c, s = jnp.cos(ang), jnp.sin(ang)
        return jnp.concatenate([c, c], axis=-1), jnp.concatenate([-s, s], axis=-1)

    ca, sa = cs(DA_HEAD_DIM)
    cq, sq = cs(MLA_ROPE)
    ones = jnp.ones((length, MLA_NOPE), F32)
    pad1 = jnp.ones((length, LANES - MLA_NOPE - MLA_ROPE), F32)
    cqt = jnp.concatenate([ones, cq, pad1], axis=-1)
    sqt = jnp.concatenate([0.0 * ones, sq, 0.0 * pad1], axis=-1)
    return jnp.tile(ca, (1, 2)), jnp.tile(sa, (1, 2)), cqt, sqt


def _att_weights(att_w_in, mla_q_norm, mla_w_uq, mla_kv_norm, mla_w_ukv):
    d = att_w_in.shape[0]
    kr = att_w_in[:, 2176:2208]
    kr_blk = jnp.concatenate([jnp.zeros((d, MLA_NOPE), F32), kr,
                              jnp.zeros((d, LANES - MLA_NOPE - MLA_ROPE), F32)], axis=1)
    win = jnp.concatenate([att_w_in[:, :1024], att_w_in[:, 1536:2176], kr_blk], axis=1).astype(BF16)
    wva = att_w_in[:, 1024:1536].T.astype(BF16)
    wq = mla_w_uq.reshape(MLA_Q_RANK, MLA_HEADS, MLA_NOPE + MLA_ROPE)
    wq = jnp.pad(wq, ((0, 0), (0, 0), (0, LANES - MLA_NOPE - MLA_ROPE))).reshape(MLA_Q_RANK, MLA_HEADS * LANES)
    wkv = mla_w_ukv.reshape(MLA_KV_RANK, MLA_HEADS, MLA_NOPE + MLA_V)
    wk = jnp.pad(wkv[:, :, :MLA_NOPE], ((0, 0), (0, 0), (0, LANES - MLA_NOPE))).reshape(MLA_KV_RANK, MLA_HEADS * LANES)
    wv = wkv[:, :, MLA_NOPE:].reshape(MLA_KV_RANK, MLA_HEADS * MLA_V)
    return dict(win=win, wva=wva, qn=mla_q_norm.reshape(1, -1), wuq=wq.astype(BF16),
                kvn=mla_kv_norm.reshape(1, -1), wk=wk.astype(BF16), wv=wv.T.astype(BF16))


def _rec_weights(rec_w_in, bound_logits, dt_bias):
    d = rec_w_in.shape[0]
    w = rec_w_in[:, :4096].astype(BF16)
    wdt = rec_w_in[:, 4096:4112]
    pad = jnp.zeros((d, LANES - SSD_HEADS), F32)
    wdt_rows = jnp.concatenate([wdt[:, :SSD_HEADS], pad, wdt[:, SSD_HEADS:], pad], axis=1).astype(BF16)
    zb = jnp.zeros((LANES - SSD_HEADS,), F32)
    dtb = jnp.concatenate([dt_bias[0], zb, dt_bias[1], zb]).reshape(1, 2 * LANES)
    return dict(w=w, wdt=wdt_rows, bl=bound_logits, dtb=dtb)


def kernel(x, c, ctx, c_ctx, w_mod, b_mod, norm_mix, norm_mlp, w_mlp_in, w_mlp_out, att_w_in, att_lambda, att_subnorm, mla_q_norm, mla_w_uq, mla_kv_norm, mla_w_ukv, att_w_out, rec_w_in, hgrn_bound_logits, hgrn_out_norm, ssd_conv_w, ssd_conv_b, ssd_a_log, ssd_dt_bias, ssd_skip, ssd_norm, rec_w_out, final_norm):
    b_, length, d = x.shape
    n_ctx = ctx.shape[1]
    assert w_mod.shape[0] == 2 and d == D_MODEL

    cc = jnp.zeros((8, d), F32).at[:b_].set(c).at[b_].set(c_ctx)
    mods = _mod_vectors(cc, w_mod, b_mod)
    mods = jnp.pad(mods.reshape(2, 8, 6, d), ((0, 0), (0, 0), (0, 2), (0, 0)))
    mods_lat = [mods[l, :b_] for l in range(2)]
    mods_ctx = [jnp.broadcast_to(mods[l, b_], (b_, 8, d)) for l in range(2)]
    row = lambda v: v.reshape(1, -1)

    tr = min(512, length)
    trc = min(256, n_ctx)
    w0 = _att_weights(att_w_in[0], mla_q_norm[0], mla_w_uq[0], mla_kv_norm[0], mla_w_ukv[0])
    tables = _rope_tables(length)
    qa, ka, va, qm, km, vm = _proj0(x, mods_lat[0], row(norm_mix[0]), w0, tables, tr)
    qa_c, ka_c, va_c, qm_c, km_c, vm_c = _proj0(ctx, mods_ctx[0], row(norm_mix[0]), w0, None, trc)
    lam_init = 0.8 - 0.6 * math.exp(-0.3 * 0)
    da_extra = (att_lambda[0], row(att_subnorm[0]))
    tq, tk = min(1024, length), min(1024, length)
    ya = _attention("da", qa, ka_c, va_c, ka, va, da_extra, tq, tk, lam_init)
    ym = _attention("mla", qm, km_c, vm_c, km, vm, None, tq, tk, lam_init)
    ya_c = _attention("da", qa_c, ka_c, va_c, None, None, da_extra, trc, tk, lam_init)
    ym_c = _attention("mla", qm_c, km_c, vm_c, None, None, None, trc, tk, lam_init)
    wout = att_w_out[0].astype(BF16)
    w1 = w_mlp_in.astype(BF16)
    w2 = w_mlp_out.astype(BF16)
    trm0, th0, trm1, th1 = min(1024, length), 1024, min(512, length), 2048
    wo0 = (wout[:512], wout[512:])
    h_lat = _mix_mlp(_merge_attention, (ya, ym), wo0, x, mods_lat[0], row(norm_mlp[0]), w1[0], w2[0], None,
                     trm0, th0)
    h_ctx = _mix_mlp(_merge_attention, (ya_c, ym_c), wo0, ctx, mods_ctx[0], row(norm_mlp[0]), w1[0], w2[0], None,
                     trc, th0)

    w1r = _rec_weights(rec_w_in[0], hgrn_bound_logits, ssd_dt_bias[0])
    cw = ssd_conv_w[0]
    cb = row(ssd_conv_b[0])
    pc = _proj1(h_ctx, mods_ctx[1], row(norm_mix[1]), w1r, trc)
    pl_ = _proj1(h_lat, mods_lat[1], row(norm_mix[1]), w1r, tr)
    xs_c, bm_c, cm_c = _conv(pc[6], cw, cb, trc)
    xs_l, bm_l, cm_l = _conv(pl_[6], cw, cb, tr)
    rb = 256
    zs = jnp.zeros((b_, SSD_STATE, SSD_INNER), F32)
    zh = jnp.zeros((b_, HGRN_HEADS, LANES, HGRN_DK), F32)
    _, hs_f, _, hs_b = _ssd_scan(xs_c, bm_c, cm_c, pc[7], pc[8], ssd_a_log[0], zs, zs, rb)
    y_f, _, y_b, _ = _ssd_scan(xs_l, bm_l, cm_l, pl_[7], pl_[8], ssd_a_log[0], hs_f, hs_b, rb)
    _, ss_f, _, ss_b = _hgrn_scan(pc[0], pc[1], pc[2], pc[3], zh, zh, rb)
    o_f, _, o_b, _ = _hgrn_scan(pl_[0], pl_[1], pl_[2], pl_[3], ss_f, ss_b, rb)
    skip = row(jnp.repeat(ssd_skip[0], SSD_HEAD_DIM))
    wro = rec_w_out[0].astype(BF16)
    return _mix_mlp(_merge_recurrent, (o_f, o_b, y_f, y_b, pl_[4], pl_[5], xs_l),
                    (row(hgrn_out_norm[0]), skip, row(ssd_norm[0]), wro[:512], wro[512:]),
                    h_lat, mods_lat[1], row(norm_mlp[1]), w1[1], w2[1], row(final_norm), trm1, th1)
```

```python
import functools
import math

import numpy as np
import jax
import jax.numpy as jnp
from jax import lax
from jax.experimental import pallas as pl
from jax.experimental.pallas import tpu as pltpu

F32 = jnp.float32
BF16 = jnp.bfloat16

D_MODEL = 1024
GRID_W = 64
DA_HEADS = 4
DA_HEAD_DIM = 64
MLA_HEADS = 8
MLA_NOPE = 64
MLA_ROPE = 32
MLA_V = 64
MLA_Q_RANK = 384
MLA_KV_RANK = 256
HGRN_HEADS = 4
HGRN_DK = 128
SSD_HEADS = 8
SSD_HEAD_DIM = 64
SSD_GROUPS = 2
SSD_STATE = 128
SSD_CONV_W = 5
SSD_INNER = SSD_HEADS * SSD_HEAD_DIM
MLP_HIDDEN = 4 * D_MODEL
ROPE_BASE = 10000.0
NORM_EPS = 1e-6
CHUNK = 64

LANES = 128
LOG2E = 1.4426950408889634
VMEM_LIMIT = 56 * 1024 * 1024


def _cparams(sem):
    return pltpu.CompilerParams(dimension_semantics=sem, vmem_limit_bytes=VMEM_LIMIT)


def _rms(x):
    return x * lax.rsqrt(jnp.mean(x * x, axis=-1, keepdims=True) + NORM_EPS)


def _modulate(x, g, shift, scale):
    return (_rms(x) * g) * (1.0 + scale) + shift


def _silu(x):
    return x * jax.nn.sigmoid(x)


def _bdot(a, b):
    return jnp.dot(a.astype(BF16), b.astype(BF16), preferred_element_type=F32)


def _dot_nt(a, b):
    return lax.dot_general(a, b, (((1,), (1,)), ((), ())), preferred_element_type=F32)


def _dot_tn(a, b):
    return lax.dot_general(a, b, (((0,), (0,)), ((), ())), preferred_element_type=F32)


def _split2(x):
    hi = x.astype(BF16)
    lo = (x - hi.astype(F32)).astype(BF16)
    return hi, lo


def _exact_left(m01, x):
    hi, lo = _split2(x)
    return (jnp.dot(m01, hi, preferred_element_type=F32) + jnp.dot(m01, lo, preferred_element_type=F32))


def _exact_right(x, m01):
    hi, lo = _split2(x)
    return (jnp.dot(hi, m01, preferred_element_type=F32) + jnp.dot(lo, m01, preferred_element_type=F32))


def _mod_kernel(c_ref, w_ref, b_ref, o_ref):
    a = _silu(c_ref[...]).astype(BF16)
    o_ref[...] = jnp.dot(a, w_ref[...].astype(BF16), preferred_element_type=F32) + b_ref[...]


def _mod_vectors(cc, w_mod, b_mod):
    depth, d, n = w_mod.shape
    tn = 1024
    return pl.pallas_call(
        _mod_kernel,
        grid=(depth, n // tn),
        in_specs=[
            pl.BlockSpec((8, d), lambda l, j: (0, 0)),
            pl.BlockSpec((None, d, tn), lambda l, j: (l, 0, j)),
            pl.BlockSpec((None, 1, tn), lambda l, j: (l, 0, j)),
        ],
        out_specs=pl.BlockSpec((None, 8, tn), lambda l, j: (l, 0, j)),
        out_shape=jax.ShapeDtypeStruct((depth, 8, n), F32),
        compiler_params=_cparams(("arbitrary", "arbitrary")),
        name="mod_vectors",
    )(cc, w_mod, b_mod.reshape(depth, 1, n))


def _rope_partner(x, half):
    n = x.shape[-1]
    lane = lax.broadcasted_iota(jnp.int32, x.shape, x.ndim - 1)
    up = pltpu.roll(x, n - half, x.ndim - 1)
    dn = pltpu.roll(x, half, x.ndim - 1)
    return jnp.where((lane % (2 * half)) < half, up, dn)


def _tile_lanes(x, n):
    return jnp.concatenate([x] * n, axis=-1)


def _proj0_kernel(use_rope, *refs):
    if use_rope:
        (h_ref, mod_ref, g_ref, win_ref, wva_ref, qn_ref, wuq_ref, kvn_ref, wk_ref, wv_ref,
         ca_ref, sa_ref, cq_ref, sq_ref,
         qa_ref, ka_ref, va_ref, qm_ref, km_ref, vm_ref) = refs
    else:
        (h_ref, mod_ref, g_ref, win_ref, wva_ref, qn_ref, wuq_ref, kvn_ref, wk_ref, wv_ref,
         qa_ref, ka_ref, va_ref, qm_ref, km_ref, vm_ref) = refs
    mod = mod_ref[...]
    ub = _modulate(h_ref[...], g_ref[...], mod[0:1], mod[1:2]).astype(BF16)

    def proj(lo, hi):
        return jnp.dot(ub, win_ref[:, lo:hi], preferred_element_type=F32)

    qa = proj(0, 512) * (DA_HEAD_DIM ** -0.5 * LOG2E)
    ka = proj(512, 1024)
    va_ref[...] = _dot_nt(wva_ref[...], ub).astype(BF16)
    cq = proj(1024, 1408)
    ckv = proj(1408, 1664)
    kr = proj(1664, 1792)
    cqn = (_rms(cq) * qn_ref[...]).astype(BF16)
    qm = jnp.dot(cqn, wuq_ref[...], preferred_element_type=F32) * ((MLA_NOPE + MLA_ROPE) ** -0.5 * LOG2E)
    ckvn = (_rms(ckv) * kvn_ref[...]).astype(BF16)
    kn = jnp.dot(ckvn, wk_ref[...], preferred_element_type=F32)
    vm_ref[...] = _dot_nt(wv_ref[...], ckvn).astype(BF16)
    if use_rope:
        ca = _tile_lanes(ca_ref[...], DA_HEADS)
        sa = _tile_lanes(sa_ref[...], DA_HEADS)
        qa = qa * ca + _rope_partner(qa, DA_HEAD_DIM // 2) * sa
        ka = ka * ca + _rope_partner(ka, DA_HEAD_DIM // 2) * sa
        cq1 = cq_ref[...]
        sq1 = sq_ref[...]
        qm = qm * _tile_lanes(cq1, MLA_HEADS) + _rope_partner(qm, MLA_ROPE // 2) * _tile_lanes(sq1, MLA_HEADS)
        kr = kr * cq1 + _rope_partner(kr, MLA_ROPE // 2) * sq1
    qa_ref[...] = qa.astype(BF16)
    ka_ref[...] = ka.astype(BF16)
    qm_ref[...] = qm.astype(BF16)
    km_ref[...] = (kn + _tile_lanes(kr, MLA_HEADS)).astype(BF16)


def _proj0(h, mods, gain, w, tables, tr):
    g_, r_, d = h.shape
    use_rope = tables is not None
    row = lambda c: pl.BlockSpec((None, tr, c), lambda b, t: (b, t, 0))
    full = lambda a: pl.BlockSpec(a.shape, lambda b, t: (0,) * a.ndim)
    in_specs = [row(d), pl.BlockSpec((None, 8, d), lambda b, t: (b, 0, 0)), full(gain),
                full(w["win"]), full(w["wva"]), full(w["qn"]), full(w["wuq"]), full(w["kvn"]), full(w["wk"]),
                full(w["wv"])]
    args = [h, mods, gain, w["win"], w["wva"], w["qn"], w["wuq"], w["kvn"], w["wk"], w["wv"]]
    if use_rope:
        in_specs += [pl.BlockSpec((tr, LANES), lambda b, t: (t, 0))] * 4
        args += list(tables)
    col = pl.BlockSpec((None, 512, tr), lambda b, t: (b, 0, t))
    widths = (512, 512, None, 1024, 1024, None)
    return pl.pallas_call(
        functools.partial(_proj0_kernel, use_rope),
        grid=(g_, r_ // tr),
        in_specs=in_specs,
        out_specs=[col if c is None else row(c) for c in widths],
        out_shape=[jax.ShapeDtypeStruct((g_, 512, r_) if c is None else (g_, r_, c), BF16) for c in widths],
        compiler_params=_cparams(("arbitrary", "arbitrary")),
        name="proj0_rope" if use_rope else "proj0_ctx",
    )(*args)


def _attn_kernel(mode, has_lat, tk, nk, lam_init, *refs):
    refs = list(refs)
    q_ref, kc_ref, vc_ref = refs[:3]
    refs = refs[3:]
    if has_lat:
        kl_ref, vl_ref = refs[:2]
        refs = refs[2:]
    if mode == "da":
        lam_ref, sub_ref = refs[:2]
        refs = refs[2:]
    o_ref, ma_ref, la_ref, acca_ref, mb_ref, lb_ref, accb_ref = refs[:7]
    if has_lat:
        s_ref, mx_ref = refs[7:]

    q = q_ref[...]
    if mode == "da":
        lane = lax.broadcasted_iota(jnp.int32, q.shape, 1)
        zero = jnp.zeros_like(q)
        qs = (jnp.where(lane < DA_HEAD_DIM, q, zero), jnp.where(lane >= DA_HEAD_DIM, q, zero))
    else:
        qs = (q[:, :LANES], q[:, LANES:])
    stats = ((ma_ref, la_ref, acca_ref), (mb_ref, lb_ref, accb_ref))
    for m_ref, l_ref, acc_ref in stats:
        m_ref[...] = jnp.full(m_ref.shape, -jnp.inf, F32)
        l_ref[...] = jnp.zeros(l_ref.shape, F32)
        acc_ref[...] = jnp.zeros(acc_ref.shape, F32)

    def scores(k):
        ks = (k, k) if mode == "da" else (k[:, :LANES], k[:, LANES:])
        return [_dot_nt(ks[i], qs[i]) for i in range(2)]

    def softmax(i, s):
        m_ref, l_ref, _ = stats[i]
        m_prev = m_ref[...]
        m_new = jnp.maximum(m_prev, jnp.max(s, axis=0, keepdims=True))
        alpha = jnp.exp2(m_prev - m_new)
        p = jnp.exp2(s - m_new)
        l_ref[...] = alpha * l_ref[...] + jnp.sum(p, axis=0, keepdims=True)
        m_ref[...] = m_new
        return p.astype(BF16), alpha

    def accumulate(i, vt, p, alpha):
        acc_ref = stats[i][2]
        acc_ref[...] = alpha * acc_ref[...] + jnp.dot(vt, p, preferred_element_type=F32)

    if not has_lat:
        ss = scores(kc_ref[...])
        pa = [softmax(i, ss[i]) for i in range(2)]
        for i in range(2):
            accumulate(i, vc_ref[...], *pa[i])
    else:
        nc = kc_ref.shape[0]

        def chunk(c):
            return pl.ds(c * tk if isinstance(c, int) else pl.multiple_of(c * tk, tk), tk)

        def keys(st):
            return kc_ref[...] if isinstance(st, int) and st == 0 else kl_ref[chunk(st - 1), :]

        def vals(st):
            return vc_ref[...] if isinstance(st, int) and st == 0 else vl_ref[:, chunk(st - 1)]

        def lookahead(st, slot, n):
            ss = scores(keys(st))
            for i in range(2):
                s_ref[slot, i, 0:n, :] = ss[i]
                mx_ref[1, i] = jnp.maximum(mx_ref[0, i], jnp.max(ss[i], axis=0, keepdims=True))

        def consume(st, slot, n):
            vt = vals(st)
            for i in range(2):
                m_ref, l_ref, acc_ref = stats[i]
                m_cur = mx_ref[0, i]
                alpha = jnp.exp2(m_ref[...] - m_cur)
                p = jnp.exp2(s_ref[slot, i, 0:n, :] - m_cur)
                l_ref[...] = alpha * l_ref[...] + jnp.sum(p, axis=0, keepdims=True)
                acc_ref[...] = alpha * acc_ref[...] + jnp.dot(vt, p.astype(BF16), preferred_element_type=F32)
                m_ref[...] = m_cur

        def advance():
            for i in range(2):
                mx_ref[0, i] = mx_ref[1, i]

        def run(st, slot, has_next):
            n = nc if isinstance(st, int) and st == 0 else tk
            if has_next:
                lookahead(st + 1, 1 - slot, tk)
            consume(st, slot, n)
            if has_next:
                advance()

        for i in range(2):
            mx_ref[0, i] = jnp.full((1, mx_ref.shape[-1]), -jnp.inf, F32)
        lookahead(0, 0, nc)
        advance()
        run(0, 0, True)
        npair = (nk - 1) // 2

        def body(j, carry):
            run(2 * j + 1, 1, True)
            run(2 * j + 2, 0, True)
            return carry

        lax.fori_loop(0, npair, body, 0)
        for st in range(2 * npair + 1, nk + 1):
            run(st, st % 2, st < nk)

    oa = acca_ref[...] / la_ref[...]
    ob = accb_ref[...] / lb_ref[...]
    if mode == "da":
        lp = lam_ref[...]
        lam = (jnp.exp(jnp.sum(lp[0:1] * lp[1:2], axis=1, keepdims=True))
               - jnp.exp(jnp.sum(lp[2:3] * lp[3:4], axis=1, keepdims=True)) + lam_init)
        dlt = (oa - lam * ob).T
        o_ref[...] = ((_rms(dlt) * sub_ref[...]) * (1.0 - lam_init)).astype(o_ref.dtype)
    else:
        chan = lax.broadcasted_iota(jnp.int32, oa.shape, 0)
        o_ref[...] = jnp.where(chan < MLA_V, oa, ob).T.astype(o_ref.dtype)


def _attention(mode, q, kc, vc, kl, vl, extra, tq, tk, lam_init):
    b_, sq, _ = q.shape
    has_lat = kl is not None
    w = LANES if mode == "da" else 2 * LANES
    nh = 4
    kblk = lambda a: pl.BlockSpec((None, a.shape[1], w), lambda b, h, t: (b, 0, h))
    vblk = lambda a: pl.BlockSpec((None, LANES, a.shape[2]), lambda b, h, t: (b, h, 0))
    in_specs = [pl.BlockSpec((None, tq, w), lambda b, h, t: (b, t, h)), kblk(kc), vblk(vc)]
    args = [q, kc, vc]
    nk = 0
    if has_lat:
        in_specs += [kblk(kl), vblk(vl)]
        args += [kl, vl]
        nk = kl.shape[1] // tk
    if mode == "da":
        lam_p, subnorm = extra
        in_specs += [pl.BlockSpec(lam_p.shape, lambda b, h, t: (0, 0)),
                     pl.BlockSpec(subnorm.shape, lambda b, h, t: (0, 0))]
        args += [lam_p, subnorm]
    scratch = []
    for _ in range(2):
        scratch += [pltpu.VMEM((1, tq), F32), pltpu.VMEM((1, tq), F32), pltpu.VMEM((LANES, tq), F32)]
    if has_lat:
        scratch += [pltpu.VMEM((2, 2, tk, tq), F32), pltpu.VMEM((2, 2, 1, tq), F32)]
    return pl.pallas_call(
        functools.partial(_attn_kernel, mode, has_lat, tk, nk, lam_init),
        grid=(b_, nh, sq // tq),
        in_specs=in_specs,
        out_specs=pl.BlockSpec((None, tq, LANES), lambda b, h, t: (b, t, h)),
        out_shape=jax.ShapeDtypeStruct((b_, sq, nh * LANES), BF16),
        scratch_shapes=scratch,
        compiler_params=_cparams(("arbitrary", "arbitrary", "arbitrary")),
        name=f"attn_{mode}_{'lat' if has_lat else 'ctx'}",
    )(*args)


def _merge_attention(ya_ref, ym_ref, wa_ref, wm_ref):
    return (jnp.dot(ya_ref[...], wa_ref[...], preferred_element_type=F32)
            + jnp.dot(ym_ref[...], wm_ref[...], preferred_element_type=F32))


def _merge_recurrent(of_ref, ob_ref, yf_ref, yb_ref, sg_ref, sz_ref, xs_ref, on_ref, sk_ref, sn_ref, wo_ref, wy_ref):
    o = of_ref[...].astype(F32) + ob_ref[...].astype(F32)
    on = on_ref[...]
    parts = []
    for hh in range(HGRN_HEADS):
        ls = slice(hh * LANES, (hh + 1) * LANES)
        parts.append(_rms(o[:, ls]) * on[:, ls])
    o = jnp.concatenate(parts, axis=1) * sg_ref[...].astype(F32)
    y = ((yf_ref[...].astype(F32) + yb_ref[...].astype(F32) + sk_ref[...] * xs_ref[...])
         * sz_ref[...].astype(F32))
    sn = sn_ref[...]
    gw = SSD_INNER // SSD_GROUPS
    parts = []
    for gg in range(SSD_GROUPS):
        ls = slice(gg * gw, (gg + 1) * gw)
        parts.append(_rms(y[:, ls]) * sn[:, ls])
    y = jnp.concatenate(parts, axis=1)
    return (jnp.dot(o.astype(BF16), wo_ref[...], preferred_element_type=F32)
            + jnp.dot(y.astype(BF16), wy_ref[...], preferred_element_type=F32))


def _mix_mlp_kernel(merge, n_mix, final, *refs):
    mix_refs = refs[:n_mix]
    if final:
        h_ref, mod_ref, g_ref, w1_ref, w2_ref, fg_ref, o_ref, u_ref, acc_ref = refs[n_mix:]
    else:
        h_ref, mod_ref, g_ref, w1_ref, w2_ref, o_ref, u_ref, acc_ref = refs[n_mix:]
    j = pl.program_id(2)

    @pl.when(j == 0)
    def _():
        h1 = h_ref[...] + mod_ref[2:3, :] * merge(*mix_refs)
        o_ref[...] = h1
        u_ref[...] = _modulate(h1, g_ref[...], mod_ref[3:4, :], mod_ref[4:5, :]).astype(BF16)
        acc_ref[...] = jnp.zeros(acc_ref.shape, F32)

    a = jnp.dot(u_ref[...], w1_ref[...], preferred_element_type=F32)
    a = jnp.square(jnp.maximum(a, 0.0)).astype(BF16)
    acc_ref[...] += jnp.dot(a, w2_ref[...], preferred_element_type=F32)

    @pl.when(j == pl.num_programs(2) - 1)
    def _():
        out = o_ref[...] + mod_ref[5:6, :] * acc_ref[...]
        if final:
            out = _rms(out) * fg_ref[...]
        o_ref[...] = out


def _mix_mlp(merge, mix_rows, mix_full, h, mods, gain, w1, w2, final_gain, tr, th):
    g_, r_, d = h.shape
    hid = w1.shape[1]
    final = final_gain is not None
    row = lambda c: pl.BlockSpec((None, tr, c), lambda b, t, j: (b, t, 0))
    full = lambda a: pl.BlockSpec(a.shape, lambda b, t, j: (0,) * a.ndim)
    in_specs = ([row(a.shape[-1]) for a in mix_rows] + [full(a) for a in mix_full]
                + [row(d), pl.BlockSpec((None, 8, d), lambda b, t, j: (b, 0, 0)), full(gain),
                   pl.BlockSpec((d, th), lambda b, t, j: (0, j)),
                   pl.BlockSpec((th, d), lambda b, t, j: (j, 0))])
    args = list(mix_rows) + list(mix_full) + [h, mods, gain, w1, w2]
    if final:
        in_specs.append(full(final_gain))
        args.append(final_gain)
    return pl.pallas_call(
        functools.partial(_mix_mlp_kernel, merge, len(mix_rows) + len(mix_full), final),
        grid=(g_, r_ // tr, hid // th),
        in_specs=in_specs,
        out_specs=row(d),
        out_shape=jax.ShapeDtypeStruct(h.shape, F32),
        scratch_shapes=[pltpu.VMEM((tr, d), BF16), pltpu.VMEM((tr, d), F32)],
        compiler_params=_cparams(("arbitrary", "arbitrary", "arbitrary")),
        name="mix_mlp_final" if final else "mix_mlp",
    )(*args)


def _softplus(x):
    return jnp.maximum(x, 0.0) + jnp.log1p(jnp.exp(-jnp.abs(x)))


def _proj1_kernel(h_ref, mod_ref, g_ref, w_ref, wdt_ref, bl_ref, dtb_ref,
                  q_ref, gf_ref, gb_ref, i_ref, sg_ref, sz_ref, xbc_ref, dt_ref, dtt_ref):
    mod = mod_ref[...]
    ub = _modulate(h_ref[...], g_ref[...], mod[0:1], mod[1:2]).astype(BF16)

    def proj(lo, hi):
        return jnp.dot(ub, w_ref[:, lo:hi], preferred_element_type=F32)

    bl = bl_ref[...]
    e = jnp.exp(bl - jnp.max(bl, axis=0, keepdims=True))
    gamma = e / jnp.sum(e, axis=0, keepdims=True)
    lb = (gamma[0:1] + gamma[1:2]) - gamma[0:1]
    q_ref[...] = _silu(proj(0, 512)).astype(BF16)
    for k, out in ((0, gf_ref), (1, gb_ref)):
        lbk = lb[:, 512 * k:512 * (k + 1)]
        f = lbk + (1.0 - lbk) * jax.nn.sigmoid(proj(512 * (k + 1), 512 * (k + 2)))
        out[...] = jnp.log(f)
    i_ref[...] = proj(1536, 2048).astype(BF16)
    sg_ref[...] = _silu(proj(2048, 2560)).astype(BF16)
    sz_ref[...] = _silu(proj(2560, 3072)).astype(BF16)
    xbc_ref[...] = proj(3072, 4096)
    dt = _softplus(jnp.dot(ub, wdt_ref[...], preferred_element_type=F32) + dtb_ref[...])
    dt_ref[...] = dt
    dtt_ref[...] = jnp.concatenate([dt[:, :LANES].T[:SSD_HEADS], dt[:, LANES:].T[:SSD_HEADS]], axis=0)


def _proj1(h, mods, gain, w, tr):
    g_, r_, d = h.shape
    row = lambda c: pl.BlockSpec((None, tr, c), lambda b, t: (b, t, 0))
    full = lambda a: pl.BlockSpec(a.shape, lambda b, t: (0,) * a.ndim)
    names = ("w", "wdt", "bl", "dtb")
    widths = (512, 512, 512, 512, 512, 512, 1024, 2 * LANES)
    dts = (BF16, F32, F32, BF16, BF16, BF16, F32, F32)
    return pl.pallas_call(
        _proj1_kernel,
        grid=(g_, r_ // tr),
        in_specs=[row(d), pl.BlockSpec((None, 8, d), lambda b, t: (b, 0, 0)), full(gain)]
        + [full(w[n]) for n in names],
        out_specs=[row(c) for c in widths] + [pl.BlockSpec((None, 16, tr), lambda b, t: (b, 0, t))],
        out_shape=[jax.ShapeDtypeStruct((g_, r_, c), t_) for c, t_ in zip(widths, dts)]
        + [jax.ShapeDtypeStruct((g_, 16, r_), F32)],
        compiler_params=_cparams(("arbitrary", "arbitrary")),
        name="proj1",
    )(h, mods, gain, *[w[n] for n in names])


def _conv_kernel(nt, x_ref, p_ref, n_ref, w_ref, b_ref, xs_ref, bm_ref, cm_ref, pad_ref):
    t = pl.program_id(1)
    tr = x_ref.shape[0]
    pad = SSD_CONV_W // 2
    pad_ref[0:8, :] = jnp.where(t > 0, p_ref[...], 0.0)
    pad_ref[8:8 + tr, :] = x_ref[...]
    pad_ref[8 + tr:16 + tr, :] = jnp.where(t < nt - 1, n_ref[...], 0.0)
    y = jnp.zeros(x_ref.shape, F32) + b_ref[...]
    for j in range(SSD_CONV_W):
        y = y + pad_ref[8 - pad + j:8 - pad + j + tr, :] * w_ref[j:j + 1, :]
    y = _silu(y)
    xs_ref[...] = y[:, :SSD_INNER]
    bm_ref[...] = y[:, SSD_INNER:SSD_INNER + 256].astype(BF16)
    cm_ref[...] = y[:, SSD_INNER + 256:].astype(BF16)


def _conv(xbc, w, b, tr):
    g_, r_, c = xbc.shape
    nt = r_ // tr
    nb = tr // 8
    row = lambda cc: pl.BlockSpec((None, tr, cc), lambda g, t: (g, t, 0))
    return pl.pallas_call(
        functools.partial(_conv_kernel, nt),
        grid=(g_, nt),
        in_specs=[row(c),
                  pl.BlockSpec((None, 8, c), lambda g, t: (g, jnp.maximum(t * nb - 1, 0), 0)),
                  pl.BlockSpec((None, 8, c), lambda g, t: (g, jnp.minimum((t + 1) * nb, nt * nb - 1), 0)),
                  pl.BlockSpec(w.shape, lambda g, t: (0, 0)),
                  pl.BlockSpec(b.shape, lambda g, t: (0, 0))],
        out_specs=[row(512), row(256), row(256)],
        out_shape=[jax.ShapeDtypeStruct((g_, r_, 512), F32),
                   jax.ShapeDtypeStruct((g_, r_, 256), BF16),
                   jax.ShapeDtypeStruct((g_, r_, 256), BF16)],
        scratch_shapes=[pltpu.VMEM((tr + 16, c), F32)],
        compiler_params=_cparams(("arbitrary", "arbitrary")),
        name="ssd_conv",
    )(xbc, xbc, xbc, w, b)


def _tri(rev, strict=False):
    i = np.arange(CHUNK)
    if rev:
        m = i[None, :] > i[:, None] if strict else i[None, :] >= i[:, None]
    else:
        m = i[None, :] < i[:, None] if strict else i[None, :] <= i[:, None]
    return m


def _ssd_consts(rev, nchunk):
    tri = np.kron(np.eye(nchunk), _tri(rev).astype(np.float32))
    expand = np.zeros((LANES, SSD_INNER), np.float32)
    for h in range(SSD_HEADS):
        expand[h, h * SSD_HEAD_DIM:(h + 1) * SSD_HEAD_DIM] = 1.0
    return jnp.asarray(tri, BF16), jnp.asarray(tri.T.copy(), BF16), jnp.asarray(expand, BF16)


def _ssd_block_decays(rev, dt_ref, dtt_ref, an, ant, tri, trit):
    dir_ = 1 if rev else 0
    dt = dt_ref[:, dir_ * LANES:(dir_ + 1) * LANES]
    dtt = dtt_ref[8 * dir_:8 * dir_ + 8, :]
    acol = _exact_left(tri, dt * an)
    arow = _exact_right(dtt * ant, trit)
    return dt, acol, arow


def _expand_heads(vals, ex):
    rows = vals[0].shape[0]
    parts = [p for v in vals for p in _split2(v)]
    big = jnp.dot(jnp.concatenate(parts, axis=0), ex, preferred_element_type=F32)
    return [big[2 * i * rows:(2 * i + 1) * rows] + big[(2 * i + 1) * rows:(2 * i + 2) * rows]
            for i in range(len(vals))]


def _ssd_chunk(rev, r0, x_ref, bm_ref, cm_ref, decays, h_ref, y_ref):
    ti = lax.broadcasted_iota(jnp.int32, (CHUNK, CHUNK), 0)
    si = lax.broadcasted_iota(jnp.int32, (CHUNK, CHUNK), 1)
    mask = (si >= ti) if rev else (si <= ti)
    lane = lax.broadcasted_iota(jnp.int32, (CHUNK, LANES), 1)
    last = 0 if rev else CHUNK - 1
    x = x_ref[r0:r0 + CHUNK, :]
    bm = bm_ref[r0:r0 + CHUNK, :]
    cm = cm_ref[r0:r0 + CHUNK, :]
    acol = decays[0][r0:r0 + CHUNK, :]
    arow = decays[1][:, r0:r0 + CHUNK]
    dte = decays[2][r0:r0 + CHUNK, :]
    ae = decays[3][r0:r0 + CHUNK, :]
    ae_last = ae[last:last + 1, :]
    xdt = x * dte
    xw = (xdt * jnp.exp(ae_last - ae)).astype(BF16)
    xdtb = xdt.astype(BF16)
    ys = []
    for g in range(SSD_GROUPS):
        bg = bm[:, g * SSD_STATE:(g + 1) * SSD_STATE]
        cg = cm[:, g * SSD_STATE:(g + 1) * SSD_STATE]
        cb = _dot_nt(cg, bg)
        hs = h_ref[:, g * 256:(g + 1) * 256]
        yoff = jnp.dot(cg, hs.astype(BF16), preferred_element_type=F32) * jnp.exp(ae[:, g * 256:(g + 1) * 256])
        for pr in range(2):
            xp = xdtb[:, g * 256 + pr * LANES:g * 256 + (pr + 1) * LANES]
            res = []
            for k in range(2):
                hd = g * 4 + pr * 2 + k
                dmat = acol[:, hd:hd + 1] - arow[hd:hd + 1, :]
                lmat = jnp.exp(jnp.where(mask, dmat, -jnp.inf))
                res.append(jnp.dot((cb * lmat).astype(BF16), xp, preferred_element_type=F32))
            ys.append(jnp.where(lane < SSD_HEAD_DIM, res[0], res[1]) + yoff[:, pr * LANES:(pr + 1) * LANES])
        h_ref[:, g * 256:(g + 1) * 256] = (hs * jnp.exp(ae_last[:, g * 256:(g + 1) * 256])
                                           + _dot_tn(bg, xw[:, g * 256:(g + 1) * 256]))
    y_ref[r0:r0 + CHUNK, :] = jnp.concatenate(ys, axis=1).astype(y_ref.dtype)


def _ssd_kernel(nchunk, xf_ref, bmf_ref, cmf_ref, dtf_ref, dttf_ref, xb_ref, bmb_ref, cmb_ref, dtb_ref, dttb_ref,
                an_ref, ant_ref, tri_ref, trit_ref, ex_ref, h0f_ref, h0b_ref,
                yf_ref, hTf_ref, yb_ref, hTb_ref, hf_ref, hb_ref):
    s = pl.program_id(1)

    @pl.when(s == 0)
    def _():
        hf_ref[...] = h0f_ref[...]
        hb_ref[...] = h0b_ref[...]

    ex = ex_ref[...]
    dt_f, acol_f, arow_f = _ssd_block_decays(False, dtf_ref, dttf_ref, an_ref[0], ant_ref[0], tri_ref[0], trit_ref[0])
    dt_b, acol_b, arow_b = _ssd_block_decays(True, dtb_ref, dttb_ref, an_ref[1], ant_ref[1], tri_ref[1], trit_ref[1])
    dte_f, ae_f, dte_b, ae_b = _expand_heads([dt_f, acol_f, dt_b, acol_b], ex)
    dec_f = (acol_f, arow_f, dte_f, ae_f)
    dec_b = (acol_b, arow_b, dte_b, ae_b)
    for j in range(nchunk):
        _ssd_chunk(False, j * CHUNK, xf_ref, bmf_ref, cmf_ref, dec_f, hf_ref, yf_ref)
        _ssd_chunk(True, (nchunk - 1 - j) * CHUNK, xb_ref, bmb_ref, cmb_ref, dec_b, hb_ref, yb_ref)

    @pl.when(s == pl.num_programs(1) - 1)
    def _():
        hTf_ref[...] = hf_ref[...]
        hTb_ref[...] = hb_ref[...]


def _ssd_scan(xs, bm, cm, dt, dtt, a_log, h0f, h0b, rb):
    b_, r_, _ = xs.shape
    nblk = r_ // rb
    trif, tritf, ex = _ssd_consts(False, rb // CHUNK)
    trib, tritb, _ = _ssd_consts(True, rb // CHUNK)
    tri = jnp.stack([trif, trib])
    trit = jnp.stack([tritf, tritb])
    a_neg = -jnp.exp(a_log.astype(F32))
    an = jnp.zeros((2, 1, LANES), F32).at[:, 0, :SSD_HEADS].set(a_neg)
    ant = a_neg.reshape(2, SSD_HEADS, 1)
    rowf = lambda c: pl.BlockSpec((None, rb, c), lambda b, s: (b, s, 0))
    rowb = lambda c: pl.BlockSpec((None, rb, c), lambda b, s: (b, nblk - 1 - s, 0))
    full = lambda a: pl.BlockSpec(a.shape, lambda b, s: (0,) * a.ndim)
    st = pl.BlockSpec((None, SSD_STATE, SSD_INNER), lambda b, s: (b, 0, 0))
    y_shape = jax.ShapeDtypeStruct((b_, r_, SSD_INNER), BF16)
    h_shape = jax.ShapeDtypeStruct((b_, SSD_STATE, SSD_INNER), F32)
    return pl.pallas_call(
        functools.partial(_ssd_kernel, rb // CHUNK),
        grid=(b_, nblk),
        in_specs=[rowf(512), rowf(256), rowf(256), rowf(2 * LANES),
                  pl.BlockSpec((None, 16, rb), lambda b, s: (b, 0, s)),
                  rowb(512), rowb(256), rowb(256), rowb(2 * LANES),
                  pl.BlockSpec((None, 16, rb), lambda b, s: (b, 0, nblk - 1 - s)),
                  full(an), full(ant), full(tri), full(trit), full(ex), st, st],
        out_specs=[rowf(512), st, rowb(512), st],
        out_shape=[y_shape, h_shape, y_shape, h_shape],
        scratch_shapes=[pltpu.VMEM((SSD_STATE, SSD_INNER), F32)] * 2,
        compiler_params=_cparams(("arbitrary", "arbitrary")),
        name="ssd_scan",
    )(xs, bm, cm, dt, dtt, xs, bm, cm, dt, dtt, an, ant, tri, trit, ex, h0f, h0b)


_HGRN_LEVELS = (64, 32, 16, 8, 4, 2, 1)


def _hgrn_consts(rev):
    i = np.arange(CHUNK)
    before = _tri(rev)
    after_strict = ~before
    mats, masks = [], []
    for c in _HGRN_LEVELS:
        same = (i[:, None] // c) == (i[None, :] // c)
        q_side = same & before
        k_side = same & after_strict
        if c == CHUNK:
            mats += [q_side, k_side]
        else:
            blk = i // c
            first = (blk % 2 == 1) if rev else (blk % 2 == 0)
            if c > 1:
                mats.append(np.where(first[:, None], k_side, q_side))
            pair = first[None, :] & (~first)[:, None] & ((blk[:, None] // 2) == (blk[None, :] // 2))
            masks.append(pair)
    masks.append(np.eye(CHUNK, dtype=bool))
    m = np.concatenate(mats, axis=0).astype(np.float32)
    m = np.concatenate([m, m], axis=1)
    mk = np.stack(masks, axis=0).astype(np.float32)
    return jnp.asarray(m, BF16), jnp.asarray(mk, F32)


def _hgrn_chunk(rev, r0, q_ref, g_ref, v_ref, mall, mk_ref, s_ref, o_ref):
    nl = len(_HGRN_LEVELS)
    last = 0 if rev else CHUNK - 1
    g = g_ref[r0:r0 + CHUNK, :]
    g2 = jnp.concatenate(_split2(g), axis=0)
    eall = jnp.exp(jnp.dot(mall, g2, preferred_element_type=F32))
    ff = jnp.exp(g)
    kk = 1.0 - ff
    qq = q_ref[r0:r0 + CHUNK, :].astype(F32)
    vv = v_ref[r0:r0 + CHUNK, :]
    outs = []
    for h in range(HGRN_HEADS):
        ls = slice(h * LANES, (h + 1) * LANES)
        qh, kh, vh = qq[:, ls], kk[:, ls], vv[:, ls]
        khb = kh.astype(BF16)

        def fac(idx, ls=ls):
            return eall[idx * CHUNK:(idx + 1) * CHUNK, ls]

        q01 = jnp.concatenate([qh, qh * ff[:, ls]], axis=0).astype(BF16)
        a01 = _dot_nt(q01, khb)
        att = a01[:CHUNK] * mk_ref[nl - 1] + a01[CHUNK:] * mk_ref[nl - 2]
        for l in range(1, nl - 1):
            e = fac(l + 1)
            att = att + _dot_nt((qh * e).astype(BF16), (kh * e).astype(BF16)) * mk_ref[l - 1]
        st = s_ref[h]
        qs_ = (qh * fac(0)).astype(BF16)
        o = _dot_nt(qs_, st.astype(BF16)) + jnp.dot(att.astype(BF16), vh, preferred_element_type=F32)
        outs.append(o)
        ks_ = (kh * fac(1)).astype(BF16)
        dec = fac(0)[last:last + 1, :]
        s_ref[h] = st * dec + _dot_tn(vh, ks_)
    o_ref[r0:r0 + CHUNK, :] = jnp.concatenate(outs, axis=1).astype(o_ref.dtype)


def _hgrn_kernel(nchunk, qf_ref, gf_ref, vf_ref, qb_ref, gb_ref, vb_ref, mf_ref, mkf_ref, mb_ref, mkb_ref,
                 s0f_ref, s0b_ref, of_ref, sTf_ref, ob_ref, sTb_ref, sf_ref, sb_ref):
    s = pl.program_id(1)

    @pl.when(s == 0)
    def _():
        sf_ref[...] = s0f_ref[...]
        sb_ref[...] = s0b_ref[...]

    mallf = mf_ref[...]
    mallb = mb_ref[...]
    for j in range(nchunk):
        _hgrn_chunk(False, j * CHUNK, qf_ref, gf_ref, vf_ref, mallf, mkf_ref, sf_ref, of_ref)
        _hgrn_chunk(True, (nchunk - 1 - j) * CHUNK, qb_ref, gb_ref, vb_ref, mallb, mkb_ref, sb_ref, ob_ref)

    @pl.when(s == pl.num_programs(1) - 1)
    def _():
        sTf_ref[...] = sf_ref[...]
        sTb_ref[...] = sb_ref[...]


def _hgrn_scan(q, gf, gb, v, s0f, s0b, rb):
    b_, r_, _ = q.shape
    nblk = r_ // rb
    mallf, mkf = _hgrn_consts(False)
    mallb, mkb = _hgrn_consts(True)
    rowf = pl.BlockSpec((None, rb, 512), lambda b, s: (b, s, 0))
    rowb = pl.BlockSpec((None, rb, 512), lambda b, s: (b, nblk - 1 - s, 0))
    full = lambda a: pl.BlockSpec(a.shape, lambda b, s: (0,) * a.ndim)
    st = pl.BlockSpec((None, HGRN_HEADS, LANES, HGRN_DK), lambda b, s: (b, 0, 0, 0))
    o_shape = jax.ShapeDtypeStruct((b_, r_, 512), BF16)
    s_shape = jax.ShapeDtypeStruct((b_, HGRN_HEADS, LANES, HGRN_DK), F32)
    return pl.pallas_call(
        functools.partial(_hgrn_kernel, rb // CHUNK),
        grid=(b_, nblk),
        in_specs=[rowf, rowf, rowf, rowb, rowb, rowb, full(mallf), full(mkf), full(mallb), full(mkb), st, st],
        out_specs=[rowf, st, rowb, st],
        out_shape=[o_shape, s_shape, o_shape, s_shape],
        scratch_shapes=[pltpu.VMEM((HGRN_HEADS, LANES, HGRN_DK), F32)] * 2,
        compiler_params=_cparams(("arbitrary", "arbitrary")),
        name="hgrn_scan",
    )(q, gf, v, q, gb, v, mallf, mkf, mallb, mkb, s0f, s0b)


def _rope_tables(length):
    rows = length // GRID_W
    row = jnp.repeat(jnp.arange(rows, dtype=F32), GRID_W)
    col = jnp.tile(jnp.arange(GRID_W, dtype=F32), rows)

    def cs(rot_dim):
        n_freq = rot_dim // 4
        inv_freq = ROPE_BASE ** (-jnp.arange(n_freq, dtype=F32) / n_freq)
        ang = jnp.concatenate([row[:, None] * inv_freq, col[:, None] * inv_freq], axis=-1)
        c, s = jnp.cos(ang), jnp.sin(ang)
        return jnp.concatenate([c, c], axis=-1), jnp.concatenate([-s, s], axis=-1)

    ca, sa = cs(DA_HEAD_DIM)
    cq, sq = cs(MLA_ROPE)
    ones = jnp.ones((length, MLA_NOPE), F32)
    pad1 = jnp.ones((length, LANES - MLA_NOPE - MLA_ROPE), F32)
    cqt = jnp.concatenate([ones, cq, pad1], axis=-1)
    sqt = jnp.concatenate([0.0 * ones, sq, 0.0 * pad1], axis=-1)
    return jnp.tile(ca, (1, 2)), jnp.tile(sa, (1, 2)), cqt, sqt


def _att_weights(att_w_in, mla_q_norm, mla_w_uq, mla_kv_norm, mla_w_ukv):
    d = att_w_in.shape[0]
    kr = att_w_in[:, 2176:2208]
    kr_blk = jnp.concatenate([jnp.zeros((d, MLA_NOPE), F32), kr,
                              jnp.zeros((d, LANES - MLA_NOPE - MLA_ROPE), F32)], axis=1)
    win = jnp.concatenate([att_w_in[:, :1024], att_w_in[:, 1536:2176], kr_blk], axis=1).astype(BF16)
    wva = att_w_in[:, 1024:1536].T.astype(BF16)
    wq = mla_w_uq.reshape(MLA_Q_RANK, MLA_HEADS, MLA_NOPE + MLA_ROPE)
    wq = jnp.pad(wq, ((0, 0), (0, 0), (0, LANES - MLA_NOPE - MLA_ROPE))).reshape(MLA_Q_RANK, MLA_HEADS * LANES)
    wkv = mla_w_ukv.reshape(MLA_KV_RANK, MLA_HEADS, MLA_NOPE + MLA_V)
    wk = jnp.pad(wkv[:, :, :MLA_NOPE], ((0, 0), (0, 0), (0, LANES - MLA_NOPE))).reshape(MLA_KV_RANK, MLA_HEADS * LANES)
    wv = wkv[:, :, MLA_NOPE:].reshape(MLA_KV_RANK, MLA_HEADS * MLA_V)
    return dict(win=win, wva=wva, qn=mla_q_norm.reshape(1, -1), wuq=wq.astype(BF16),
                kvn=mla_kv_norm.reshape(1, -1), wk=wk.astype(BF16), wv=wv.T.astype(BF16))


def _rec_weights(rec_w_in, bound_logits, dt_bias):
    d = rec_w_in.shape[0]
    w = rec_w_in[:, :4096].astype(BF16)
    wdt = rec_w_in[:, 4096:4112]
    pad = jnp.zeros((d, LANES - SSD_HEADS), F32)
    wdt_rows = jnp.concatenate([wdt[:, :SSD_HEADS], pad, wdt[:, SSD_HEADS:], pad], axis=1).astype(BF16)
    zb = jnp.zeros((LANES - SSD_HEADS,), F32)
    dtb = jnp.concatenate([dt_bias[0], zb, dt_bias[1], zb]).reshape(1, 2 * LANES)
    return dict(w=w, wdt=wdt_rows, bl=bound_logits, dtb=dtb)


def kernel(x, c, ctx, c_ctx, w_mod, b_mod, norm_mix, norm_mlp, w_mlp_in, w_mlp_out, att_w_in, att_lambda, att_subnorm, mla_q_norm, mla_w_uq, mla_kv_norm, mla_w_ukv, att_w_out, rec_w_in, hgrn_bound_logits, hgrn_out_norm, ssd_conv_w, ssd_conv_b, ssd_a_log, ssd_dt_bias, ssd_skip, ssd_norm, rec_w_out, final_norm):
    b_, length, d = x.shape
    n_ctx = ctx.shape[1]
    assert w_mod.shape[0] == 2 and d == D_MODEL

    cc = jnp.zeros((8, d), F32).at[:b_].set(c).at[b_].set(c_ctx)
    mods = _mod_vectors(cc, w_mod, b_mod)
    mods = jnp.pad(mods.reshape(2, 8, 6, d), ((0, 0), (0, 0), (0, 2), (0, 0)))
    mods_lat = [mods[l, :b_] for l in range(2)]
    mods_ctx = [jnp.broadcast_to(mods[l, b_], (b_, 8, d)) for l in range(2)]
    row = lambda v: v.reshape(1, -1)

    tr = min(512, length)
    trc = min(256, n_ctx)
    w0 = _att_weights(att_w_in[0], mla_q_norm[0], mla_w_uq[0], mla_kv_norm[0], mla_w_ukv[0])
    tables = _rope_tables(length)
    qa, ka, va, qm, km, vm = _proj0(x, mods_lat[0], row(norm_mix[0]), w0, tables, tr)
    qa_c, ka_c, va_c, qm_c, km_c, vm_c = _proj0(ctx, mods_ctx[0], row(norm_mix[0]), w0, None, trc)
    lam_init = 0.8 - 0.6 * math.exp(-0.3 * 0)
    da_extra = (att_lambda[0], row(att_subnorm[0]))
    tq, tk = min(1024, length), min(1024, length)
    ya = _attention("da", qa, ka_c, va_c, ka, va, da_extra, tq, tk, lam_init)
    ym = _attention("mla", qm, km_c, vm_c, km, vm, None, tq, tk, lam_init)
    ya_c = _attention("da", qa_c, ka_c, va_c, None, None, da_extra, trc, tk, lam_init)
    ym_c = _attention("mla", qm_c, km_c, vm_c, None, None, None, trc, tk, lam_init)
    wout = att_w_out[0].astype(BF16)
    w1 = w_mlp_in.astype(BF16)
    w2 = w_mlp_out.astype(BF16)
    trm0, th0, trm1, th1 = min(1024, length), 1024, min(512, length), 2048
    wo0 = (wout[:512], wout[512:])
    h_lat = _mix_mlp(_merge_attention, (ya, ym), wo0, x, mods_lat[0], row(norm_mlp[0]), w1[0], w2[0], None,
                     trm0, th0)
    h_ctx = _mix_mlp(_merge_attention, (ya_c, ym_c), wo0, ctx, mods_ctx[0], row(norm_mlp[0]), w1[0], w2[0], None,
                     trc, th0)

    w1r = _rec_weights(rec_w_in[0], hgrn_bound_logits, ssd_dt_bias[0])
    cw = ssd_conv_w[0]
    cb = row(ssd_conv_b[0])
    pc = _proj1(h_ctx, mods_ctx[1], row(norm_mix[1]), w1r, trc)
    pl_ = _proj1(h_lat, mods_lat[1], row(norm_mix[1]), w1r, tr)
    xs_c, bm_c, cm_c = _conv(pc[6], cw, cb, trc)
    xs_l, bm_l, cm_l = _conv(pl_[6], cw, cb, tr)
    rbc, rbl = min(256, n_ctx), min(512, length)
    zs = jnp.zeros((b_, SSD_STATE, SSD_INNER), F32)
    zh = jnp.zeros((b_, HGRN_HEADS, LANES, HGRN_DK), F32)
    _, hs_f, _, hs_b = _ssd_scan(xs_c, bm_c, cm_c, pc[7], pc[8], ssd_a_log[0], zs, zs, rbc)
    y_f, _, y_b, _ = _ssd_scan(xs_l, bm_l, cm_l, pl_[7], pl_[8], ssd_a_log[0], hs_f, hs_b, rbl)
    _, ss_f, _, ss_b = _hgrn_scan(pc[0], pc[1], pc[2], pc[3], zh, zh, rbc)
    o_f, _, o_b, _ = _hgrn_scan(pl_[0], pl_[1], pl_[2], pl_[3], ss_f, ss_b, rbl)
    skip = row(jnp.repeat(ssd_skip[0], SSD_HEAD_DIM))
    wro = rec_w_out[0].astype(BF16)
    return _mix_mlp(_merge_recurrent, (o_f, o_b, y_f, y_b, pl_[4], pl_[5], xs_l),
                    (row(hgrn_out_norm[0]), skip, row(ssd_norm[0]), wro[:512], wro[512:]),
                    h_lat, mods_lat[1], row(norm_mlp[1]), w1[1], w2[1], row(final_norm), trm1, th1)
```

```python
import functools
import math
from typing import NamedTuple

import numpy as np
import jax
import jax.numpy as jnp
from jax import lax
from jax.experimental import pallas as pl
from jax.experimental.pallas import tpu as pltpu

F32 = jnp.float32
BF16 = jnp.bfloat16

D_MODEL = 1024
GRID_W = 64
DA_HEADS = 4
DA_HEAD_DIM = 64
MLA_HEADS = 8
MLA_NOPE = 64
MLA_ROPE = 32
MLA_V = 64
MLA_Q_RANK = 384
MLA_KV_RANK = 256
HGRN_HEADS = 4
HGRN_DK = 128
SSD_HEADS = 8
SSD_HEAD_DIM = 64
SSD_GROUPS = 2
SSD_STATE = 128
SSD_CONV_W = 5
SSD_INNER = SSD_HEADS * SSD_HEAD_DIM
SSD_CONV_CH = SSD_INNER + 2 * SSD_GROUPS * SSD_STATE
MLP_HIDDEN = 4 * D_MODEL
ROPE_BASE = 10000.0
NORM_EPS = 1e-6
CHUNK = 64

LANES = 128
LOG2E = 1.4426950408889634
VMEM_LIMIT = 56 * 1024 * 1024


def _cparams(sem):
    return pltpu.CompilerParams(dimension_semantics=sem, vmem_limit_bytes=VMEM_LIMIT)


def _rms(x):
    return x * lax.rsqrt(jnp.mean(x * x, axis=-1, keepdims=True) + NORM_EPS)


def _modulate(x, g, shift, scale):
    return (_rms(x) * g) * (1.0 + scale) + shift


def _silu(x):
    return x * jax.nn.sigmoid(x)


def _dot_nt(a, b):
    return lax.dot_general(a, b, (((1,), (1,)), ((), ())), preferred_element_type=F32)


def _dot_tn(a, b):
    return lax.dot_general(a, b, (((0,), (0,)), ((), ())), preferred_element_type=F32)


def _split2(x):
    hi = x.astype(BF16)
    lo = (x - hi.astype(F32)).astype(BF16)
    return hi, lo


def _exact_left(m01, x):
    hi, lo = _split2(x)
    return (jnp.dot(m01, hi, preferred_element_type=F32) + jnp.dot(m01, lo, preferred_element_type=F32))


def _exact_right(x, m01):
    hi, lo = _split2(x)
    return (jnp.dot(hi, m01, preferred_element_type=F32) + jnp.dot(lo, m01, preferred_element_type=F32))


def _mod_kernel(c_ref, w_ref, b_ref, o_ref):
    a = _silu(c_ref[...]).astype(BF16)
    o_ref[...] = jnp.dot(a, w_ref[...].astype(BF16), preferred_element_type=F32) + b_ref[...]


def _mod_vectors(cc, w_mod, b_mod):
    depth, d, n = w_mod.shape
    tn = 1024
    return pl.pallas_call(
        _mod_kernel,
        grid=(depth, n // tn),
        in_specs=[
            pl.BlockSpec((8, d), lambda l, j: (0, 0)),
            pl.BlockSpec((None, d, tn), lambda l, j: (l, 0, j)),
            pl.BlockSpec((None, 1, tn), lambda l, j: (l, 0, j)),
        ],
        out_specs=pl.BlockSpec((None, 8, tn), lambda l, j: (l, 0, j)),
        out_shape=jax.ShapeDtypeStruct((depth, 8, n), F32),
        compiler_params=_cparams(("arbitrary", "arbitrary")),
        name="mod_vectors",
    )(cc, w_mod, b_mod.reshape(depth, 1, n))


def _rope_partner(x, half):
    n = x.shape[-1]
    lane = lax.broadcasted_iota(jnp.int32, x.shape, x.ndim - 1)
    up = pltpu.roll(x, n - half, x.ndim - 1)
    dn = pltpu.roll(x, half, x.ndim - 1)
    return jnp.where((lane % (2 * half)) < half, up, dn)


def _tile_lanes(x, n):
    return jnp.concatenate([x] * n, axis=-1)


def _proj0_kernel(use_rope, *refs):
    if use_rope:
        (h_ref, mod_ref, g_ref, win_ref, wva_ref, qn_ref, wuq_ref, kvn_ref, wk_ref, wv_ref,
         ca_ref, sa_ref, cq_ref, sq_ref,
         qa_ref, ka_ref, va_ref, qm_ref, km_ref, vm_ref) = refs
    else:
        (h_ref, mod_ref, g_ref, win_ref, wva_ref, qn_ref, wuq_ref, kvn_ref, wk_ref, wv_ref,
         qa_ref, ka_ref, va_ref, qm_ref, km_ref, vm_ref) = refs
    mod = mod_ref[...]
    ub = _modulate(h_ref[...], g_ref[...], mod[0:1], mod[1:2]).astype(BF16)

    def proj(lo, hi):
        return jnp.dot(ub, win_ref[:, lo:hi], preferred_element_type=F32)

    qa = proj(0, 512) * (DA_HEAD_DIM ** -0.5 * LOG2E)
    ka = proj(512, 1024)
    va_ref[...] = _dot_nt(wva_ref[...], ub).astype(BF16)
    cq = proj(1024, 1408)
    ckv = proj(1408, 1664)
    kr = proj(1664, 1792)
    cqn = (_rms(cq) * qn_ref[...]).astype(BF16)
    qm = jnp.dot(cqn, wuq_ref[...], preferred_element_type=F32) * ((MLA_NOPE + MLA_ROPE) ** -0.5 * LOG2E)
    ckvn = (_rms(ckv) * kvn_ref[...]).astype(BF16)
    kn = jnp.dot(ckvn, wk_ref[...], preferred_element_type=F32)
    vm_ref[...] = _dot_nt(wv_ref[...], ckvn).astype(BF16)
    if use_rope:
        ca = _tile_lanes(ca_ref[...], DA_HEADS)
        sa = _tile_lanes(sa_ref[...], DA_HEADS)
        qa = qa * ca + _rope_partner(qa, DA_HEAD_DIM // 2) * sa
        ka = ka * ca + _rope_partner(ka, DA_HEAD_DIM // 2) * sa
        cq1 = cq_ref[...]
        sq1 = sq_ref[...]
        qm = qm * _tile_lanes(cq1, MLA_HEADS) + _rope_partner(qm, MLA_ROPE // 2) * _tile_lanes(sq1, MLA_HEADS)
        kr = kr * cq1 + _rope_partner(kr, MLA_ROPE // 2) * sq1
    qa_ref[...] = qa.astype(BF16)
    ka_ref[...] = ka.astype(BF16)
    qm_ref[...] = qm.astype(BF16)
    km_ref[...] = (kn + _tile_lanes(kr, MLA_HEADS)).astype(BF16)


def _proj0(h, mods, gain, w, tables, tr):
    g_, r_, d = h.shape
    use_rope = tables is not None
    row = lambda c: pl.BlockSpec((None, tr, c), lambda b, t: (b, t, 0))
    full = lambda a: pl.BlockSpec(a.shape, lambda b, t: (0,) * a.ndim)
    in_specs = [row(d), pl.BlockSpec((None, 8, d), lambda b, t: (b, 0, 0)), full(gain),
                full(w["win"]), full(w["wva"]), full(w["qn"]), full(w["wuq"]), full(w["kvn"]), full(w["wk"]),
                full(w["wv"])]
    args = [h, mods, gain, w["win"], w["wva"], w["qn"], w["wuq"], w["kvn"], w["wk"], w["wv"]]
    if use_rope:
        in_specs += [pl.BlockSpec((tr, LANES), lambda b, t: (t, 0))] * 4
        args += list(tables)
    col = pl.BlockSpec((None, 512, tr), lambda b, t: (b, 0, t))
    widths = (512, 512, None, 1024, 1024, None)
    return pl.pallas_call(
        functools.partial(_proj0_kernel, use_rope),
        grid=(g_, r_ // tr),
        in_specs=in_specs,
        out_specs=[col if c is None else row(c) for c in widths],
        out_shape=[jax.ShapeDtypeStruct((g_, 512, r_) if c is None else (g_, r_, c), BF16) for c in widths],
        compiler_params=_cparams(("arbitrary", "arbitrary")),
        name="proj0_rope" if use_rope else "proj0_ctx",
    )(*args)


def _attn_kernel(mode, has_lat, tk, nk, lam_init, *refs):
    refs = list(refs)
    q_ref, kc_ref, vc_ref = refs[:3]
    refs = refs[3:]
    if has_lat:
        kl_ref, vl_ref = refs[:2]
        refs = refs[2:]
    if mode == "da":
        lam_ref, sub_ref = refs[:2]
        refs = refs[2:]
    o_ref, ma_ref, la_ref, acca_ref, mb_ref, lb_ref, accb_ref = refs[:7]
    if has_lat:
        s_ref, mx_ref = refs[7:]

    q = q_ref[...]
    if mode == "da":
        lane = lax.broadcasted_iota(jnp.int32, q.shape, 1)
        zero = jnp.zeros_like(q)
        qs = (jnp.where(lane < DA_HEAD_DIM, q, zero), jnp.where(lane >= DA_HEAD_DIM, q, zero))
    else:
        qs = (q[:, :LANES], q[:, LANES:])
    stats = ((ma_ref, la_ref, acca_ref), (mb_ref, lb_ref, accb_ref))
    for m_ref, l_ref, acc_ref in stats:
        m_ref[...] = jnp.full(m_ref.shape, -jnp.inf, F32)
        l_ref[...] = jnp.zeros(l_ref.shape, F32)
        acc_ref[...] = jnp.zeros(acc_ref.shape, F32)

    def scores(k):
        ks = (k, k) if mode == "da" else (k[:, :LANES], k[:, LANES:])
        return [_dot_nt(ks[i], qs[i]) for i in range(2)]

    def softmax(i, s):
        m_ref, l_ref, _ = stats[i]
        m_prev = m_ref[...]
        m_new = jnp.maximum(m_prev, jnp.max(s, axis=0, keepdims=True))
        alpha = jnp.exp2(m_prev - m_new)
        p = jnp.exp2(s - m_new)
        l_ref[...] = alpha * l_ref[...] + jnp.sum(p, axis=0, keepdims=True)
        m_ref[...] = m_new
        return p.astype(BF16), alpha

    def accumulate(i, vt, p, alpha):
        acc_ref = stats[i][2]
        acc_ref[...] = alpha * acc_ref[...] + jnp.dot(vt, p, preferred_element_type=F32)

    if not has_lat:
        ss = scores(kc_ref[...])
        pa = [softmax(i, ss[i]) for i in range(2)]
        for i in range(2):
            accumulate(i, vc_ref[...], *pa[i])
    else:
        nc = kc_ref.shape[0]

        def chunk(c):
            return pl.ds(c * tk if isinstance(c, int) else pl.multiple_of(c * tk, tk), tk)

        def keys(st):
            return kc_ref[...] if isinstance(st, int) and st == 0 else kl_ref[chunk(st - 1), :]

        def vals(st):
            return vc_ref[...] if isinstance(st, int) and st == 0 else vl_ref[:, chunk(st - 1)]

        def lookahead(st, slot, n):
            ss = scores(keys(st))
            for i in range(2):
                s_ref[slot, i, 0:n, :] = ss[i]
                mx_ref[1, i] = jnp.maximum(mx_ref[0, i], jnp.max(ss[i], axis=0, keepdims=True))

        def consume(st, slot, n):
            vt = vals(st)
            for i in range(2):
                m_ref, l_ref, acc_ref = stats[i]
                m_cur = mx_ref[0, i]
                alpha = jnp.exp2(m_ref[...] - m_cur)
                p = jnp.exp2(s_ref[slot, i, 0:n, :] - m_cur)
                l_ref[...] = alpha * l_ref[...] + jnp.sum(p, axis=0, keepdims=True)
                acc_ref[...] = alpha * acc_ref[...] + jnp.dot(vt, p.astype(BF16), preferred_element_type=F32)
                m_ref[...] = m_cur

        def advance():
            for i in range(2):
                mx_ref[0, i] = mx_ref[1, i]

        def run(st, slot, has_next):
            n = nc if isinstance(st, int) and st == 0 else tk
            if has_next:
                lookahead(st + 1, 1 - slot, tk)
            consume(st, slot, n)
            if has_next:
                advance()

        for i in range(2):
            mx_ref[0, i] = jnp.full((1, mx_ref.shape[-1]), -jnp.inf, F32)
        lookahead(0, 0, nc)
        advance()
        run(0, 0, True)
        npair = (nk - 1) // 2

        def body(j, carry):
            run(2 * j + 1, 1, True)
            run(2 * j + 2, 0, True)
            return carry

        lax.fori_loop(0, npair, body, 0)
        for st in range(2 * npair + 1, nk + 1):
            run(st, st % 2, st < nk)

    oa = acca_ref[...] / la_ref[...]
    ob = accb_ref[...] / lb_ref[...]
    if mode == "da":
        lp = lam_ref[...]
        lam = (jnp.exp(jnp.sum(lp[0:1] * lp[1:2], axis=1, keepdims=True))
               - jnp.exp(jnp.sum(lp[2:3] * lp[3:4], axis=1, keepdims=True)) + lam_init)
        dlt = (oa - lam * ob).T
        o_ref[...] = ((_rms(dlt) * sub_ref[...]) * (1.0 - lam_init)).astype(o_ref.dtype)
    else:
        chan = lax.broadcasted_iota(jnp.int32, oa.shape, 0)
        o_ref[...] = jnp.where(chan < MLA_V, oa, ob).T.astype(o_ref.dtype)


def _attention(mode, q, kc, vc, kl, vl, extra, tq, tk, lam_init):
    b_, sq, _ = q.shape
    has_lat = kl is not None
    w = LANES if mode == "da" else 2 * LANES
    nh = 4
    kblk = lambda a: pl.BlockSpec((None, a.shape[1], w), lambda b, h, t: (b, 0, h))
    vblk = lambda a: pl.BlockSpec((None, LANES, a.shape[2]), lambda b, h, t: (b, h, 0))
    in_specs = [pl.BlockSpec((None, tq, w), lambda b, h, t: (b, t, h)), kblk(kc), vblk(vc)]
    args = [q, kc, vc]
    nk = 0
    if has_lat:
        in_specs += [kblk(kl), vblk(vl)]
        args += [kl, vl]
        nk = kl.shape[1] // tk
    if mode == "da":
        lam_p, subnorm = extra
        in_specs += [pl.BlockSpec(lam_p.shape, lambda b, h, t: (0, 0)),
                     pl.BlockSpec(subnorm.shape, lambda b, h, t: (0, 0))]
        args += [lam_p, subnorm]
    scratch = []
    for _ in range(2):
        scratch += [pltpu.VMEM((1, tq), F32), pltpu.VMEM((1, tq), F32), pltpu.VMEM((LANES, tq), F32)]
    if has_lat:
        scratch += [pltpu.VMEM((2, 2, tk, tq), F32), pltpu.VMEM((2, 2, 1, tq), F32)]
    return pl.pallas_call(
        functools.partial(_attn_kernel, mode, has_lat, tk, nk, lam_init),
        grid=(b_, nh, sq // tq),
        in_specs=in_specs,
        out_specs=pl.BlockSpec((None, tq, LANES), lambda b, h, t: (b, t, h)),
        out_shape=jax.ShapeDtypeStruct((b_, sq, nh * LANES), BF16),
        scratch_shapes=scratch,
        compiler_params=_cparams(("arbitrary", "arbitrary", "arbitrary")),
        name=f"attn_{mode}_{'lat' if has_lat else 'ctx'}",
    )(*args)


def _merge_attention(ya_ref, ym_ref, w_ref):
    return (jnp.dot(ya_ref[...], w_ref[0], preferred_element_type=F32)
            + jnp.dot(ym_ref[...], w_ref[1], preferred_element_type=F32))


def _merge_recurrent(of_ref, ob_ref, yf_ref, yb_ref, sg_ref, sz_ref, xs_ref, on_ref, sk_ref, sn_ref, w_ref):
    o = of_ref[...].astype(F32) + ob_ref[...].astype(F32)
    on = on_ref[...]
    parts = []
    for hh in range(HGRN_HEADS):
        ls = slice(hh * LANES, (hh + 1) * LANES)
        parts.append(_rms(o[:, ls]) * on[:, ls])
    o = jnp.concatenate(parts, axis=1) * sg_ref[...].astype(F32)
    y = ((yf_ref[...].astype(F32) + yb_ref[...].astype(F32) + sk_ref[...] * xs_ref[...])
         * sz_ref[...].astype(F32))
    sn = sn_ref[...]
    gw = SSD_INNER // SSD_GROUPS
    parts = []
    for gg in range(SSD_GROUPS):
        ls = slice(gg * gw, (gg + 1) * gw)
        parts.append(_rms(y[:, ls]) * sn[:, ls])
    y = jnp.concatenate(parts, axis=1)
    return (jnp.dot(o.astype(BF16), w_ref[0], preferred_element_type=F32)
            + jnp.dot(y.astype(BF16), w_ref[1], preferred_element_type=F32))


def _mix_mlp_kernel(merge, n_mix, final, *refs):
    mix_refs = refs[:n_mix]
    if final:
        h_ref, mod_ref, g_ref, w1_ref, w2_ref, fg_ref, o_ref, u_ref, acc_ref = refs[n_mix:]
    else:
        h_ref, mod_ref, g_ref, w1_ref, w2_ref, o_ref, u_ref, acc_ref = refs[n_mix:]
    j = pl.program_id(2)

    @pl.when(j == 0)
    def _():
        h1 = h_ref[...] + mod_ref[2:3, :] * merge(*mix_refs)
        o_ref[...] = h1
        u_ref[...] = _modulate(h1, g_ref[...], mod_ref[3:4, :], mod_ref[4:5, :]).astype(BF16)
        acc_ref[...] = jnp.zeros(acc_ref.shape, F32)

    a = jnp.dot(u_ref[...], w1_ref[...], preferred_element_type=F32)
    a = jnp.square(jnp.maximum(a, 0.0)).astype(BF16)
    acc_ref[...] += jnp.dot(a, w2_ref[...], preferred_element_type=F32)

    @pl.when(j == pl.num_programs(2) - 1)
    def _():
        out = o_ref[...] + mod_ref[5:6, :] * acc_ref[...]
        if final:
            out = _rms(out) * fg_ref[...]
        o_ref[...] = out


def _mix_mlp(merge, mix_rows, mix_full, h, mods, gain, w1, w2, final_gain, tr, th):
    g_, r_, d = h.shape
    hid = w1.shape[1]
    final = final_gain is not None
    row = lambda c: pl.BlockSpec((None, tr, c), lambda b, t, j: (b, t, 0))
    full = lambda a: pl.BlockSpec(a.shape, lambda b, t, j: (0,) * a.ndim)
    in_specs = ([row(a.shape[-1]) for a in mix_rows] + [full(a) for a in mix_full]
                + [row(d), pl.BlockSpec((None, 8, d), lambda b, t, j: (b, 0, 0)), full(gain),
                   pl.BlockSpec((d, th), lambda b, t, j: (0, j)),
                   pl.BlockSpec((th, d), lambda b, t, j: (j, 0))])
    args = list(mix_rows) + list(mix_full) + [h, mods, gain, w1, w2]
    if final:
        in_specs.append(full(final_gain))
        args.append(final_gain)
    return pl.pallas_call(
        functools.partial(_mix_mlp_kernel, merge, len(mix_rows) + len(mix_full), final),
        grid=(g_, r_ // tr, hid // th),
        in_specs=in_specs,
        out_specs=row(d),
        out_shape=jax.ShapeDtypeStruct(h.shape, F32),
        scratch_shapes=[pltpu.VMEM((tr, d), BF16), pltpu.VMEM((tr, d), F32)],
        compiler_params=_cparams(("arbitrary", "arbitrary", "arbitrary")),
        name="mix_mlp_final" if final else "mix_mlp",
    )(*args)


def _softplus(x):
    return jnp.maximum(x, 0.0) + jnp.log1p(jnp.exp(-jnp.abs(x)))


def _proj1_kernel(nt, h_ref, hp_ref, hn_ref, mod_ref, g_ref, w_ref, wdt_ref, bl_ref, dtb_ref, cw_ref, cb_ref,
                  q_ref, gf_ref, gb_ref, i_ref, sg_ref, sz_ref, xs_ref, bm_ref, cm_ref, dt_ref, dtt_ref, pad_ref):
    t = pl.program_id(1)
    tr = h_ref.shape[0]
    mod = mod_ref[...]
    gain = g_ref[...]
    u = _modulate(h_ref[...], gain, mod[0:1], mod[1:2])
    ub = u.astype(BF16)
    u_ext = jnp.concatenate([_modulate(hp_ref[...], gain, mod[0:1], mod[1:2]), u,
                             _modulate(hn_ref[...], gain, mod[0:1], mod[1:2])], axis=0).astype(BF16)
    xbc = jnp.dot(u_ext, w_ref[:, 3072:4096], preferred_element_type=F32)
    ri = lax.broadcasted_iota(jnp.int32, (tr + 16, 1), 0)
    inside = ((ri >= 8) | (t > 0)) & ((ri < tr + 8) | (t < nt - 1))
    pad_ref[...] = jnp.where(inside, xbc, 0.0)
    half = SSD_CONV_W // 2
    y = jnp.zeros((tr, xbc.shape[1]), F32) + cb_ref[...]
    for j in range(SSD_CONV_W):
        y = y + pad_ref[8 - half + j:8 - half + j + tr, :] * cw_ref[j:j + 1, :]
    y = _silu(y)
    xs_ref[...] = y[:, :SSD_INNER]
    bm_ref[...] = y[:, SSD_INNER:SSD_INNER + 256].astype(BF16)
    cm_ref[...] = y[:, SSD_INNER + 256:].astype(BF16)

    def proj(lo, hi):
        return jnp.dot(ub, w_ref[:, lo:hi], preferred_element_type=F32)

    bl = bl_ref[...]
    e = jnp.exp(bl - jnp.max(bl, axis=0, keepdims=True))
    gamma = e / jnp.sum(e, axis=0, keepdims=True)
    lb = (gamma[0:1] + gamma[1:2]) - gamma[0:1]
    q_ref[...] = _silu(proj(0, 512)).astype(BF16)
    for k, out in ((0, gf_ref), (1, gb_ref)):
        lbk = lb[:, 512 * k:512 * (k + 1)]
        f = lbk + (1.0 - lbk) * jax.nn.sigmoid(proj(512 * (k + 1), 512 * (k + 2)))
        out[...] = jnp.log(f)
    i_ref[...] = proj(1536, 2048).astype(BF16)
    sg_ref[...] = _silu(proj(2048, 2560)).astype(BF16)
    sz_ref[...] = _silu(proj(2560, 3072)).astype(BF16)
    dt =_softplus(jnp.dot(ub, wdt_ref[...], preferred_element_type=F32) + dtb_ref[...])
    dt_ref[...] = dt
    dtt_ref[...] = jnp.concatenate([dt[:, :LANES].T[:SSD_HEADS], dt[:, LANES:].T[:SSD_HEADS]], axis=0)


def _proj1(h, mods, gain, w, conv_w, conv_b, tr):
    g_, r_, d = h.shape
    nt = r_ // tr
    nb = tr // 8
    row = lambda c: pl.BlockSpec((None, tr, c), lambda b, t: (b, t, 0))
    full = lambda a: pl.BlockSpec(a.shape, lambda b, t: (0,) * a.ndim)
    names = ("w", "wdt", "bl", "dtb")
    widths = (512, 512, 512, 512, 512, 512, 512, 256, 256, 2 * LANES)
    dts = (BF16, F32, F32, BF16, BF16, BF16, F32, BF16, BF16, F32)
    return pl.pallas_call(
        functools.partial(_proj1_kernel, nt),
        grid=(g_, nt),
        in_specs=[row(d),
                  pl.BlockSpec((None, 8, d), lambda b, t: (b, jnp.maximum(t * nb - 1, 0), 0)),
                  pl.BlockSpec((None, 8, d), lambda b, t: (b, jnp.minimum((t + 1) * nb, nt * nb - 1), 0)),
                  pl.BlockSpec((None, 8, d), lambda b, t: (b, 0, 0)), full(gain)]
        + [full(w[n]) for n in names] + [full(conv_w), full(conv_b)],
        out_specs=[row(c) for c in widths] + [pl.BlockSpec((None, 16, tr), lambda b, t: (b, 0, t))],
        out_shape=[jax.ShapeDtypeStruct((g_, r_, c), t_) for c, t_ in zip(widths, dts)]
        + [jax.ShapeDtypeStruct((g_, 16, r_), F32)],
        scratch_shapes=[pltpu.VMEM((tr + 16, SSD_CONV_CH), F32)],
        compiler_params=_cparams(("arbitrary", "arbitrary")),
        name="proj1",
    )(h, h, h, mods, gain, *[w[n] for n in names], conv_w, conv_b)


def _tri(rev, strict=False):
    i = np.arange(CHUNK)
    if rev:
        m = i[None, :] > i[:, None] if strict else i[None, :] >= i[:, None]
    else:
        m = i[None, :] < i[:, None] if strict else i[None, :] <= i[:, None]
    return m


def _ssd_consts(rev, nchunk):
    tri = np.kron(np.eye(nchunk), _tri(rev).astype(np.float32))
    expand = np.zeros((LANES, SSD_INNER), np.float32)
    for h in range(SSD_HEADS):
        expand[h, h * SSD_HEAD_DIM:(h + 1) * SSD_HEAD_DIM] = 1.0
    return jnp.asarray(tri, BF16), jnp.asarray(tri.T.copy(), BF16), jnp.asarray(expand, BF16)


def _ssd_block_decays(rev, dt_ref, dtt_ref, an, ant, tri, trit):
    dir_ = 1 if rev else 0
    dt = dt_ref[:, dir_ * LANES:(dir_ + 1) * LANES]
    dtt = dtt_ref[8 * dir_:8 * dir_ + 8, :]
    acol = _exact_left(tri, dt * an)
    arow = _exact_right(dtt * ant, trit)
    return dt, acol, arow


def _expand_heads(vals, ex):
    rows = vals[0].shape[0]
    parts = [p for v in vals for p in _split2(v)]
    big = jnp.dot(jnp.concatenate(parts, axis=0), ex, preferred_element_type=F32)
    return [big[2 * i * rows:(2 * i + 1) * rows] + big[(2 * i + 1) * rows:(2 * i + 2) * rows]
            for i in range(len(vals))]


def _ssd_chunk(rev, r0, x_ref, bm_ref, cm_ref, decays, h_ref, y_ref):
    ti = lax.broadcasted_iota(jnp.int32, (CHUNK, CHUNK), 0)
    si = lax.broadcasted_iota(jnp.int32, (CHUNK, CHUNK), 1)
    mask = (si >= ti) if rev else (si <= ti)
    lane = lax.broadcasted_iota(jnp.int32, (CHUNK, LANES), 1)
    last = 0 if rev else CHUNK - 1
    x = x_ref[r0:r0 + CHUNK, :]
    bm = bm_ref[r0:r0 + CHUNK, :]
    cm = cm_ref[r0:r0 + CHUNK, :]
    acol = decays[0][r0:r0 + CHUNK, :]
    arow = decays[1][:, r0:r0 + CHUNK]
    dte = decays[2][r0:r0 + CHUNK, :]
    ae = decays[3][r0:r0 + CHUNK, :]
    ae_last = ae[last:last + 1, :]
    xdt = x * dte
    xw = (xdt * jnp.exp(ae_last - ae)).astype(BF16)
    xdtb = xdt.astype(BF16)
    ys = []
    for g in range(SSD_GROUPS):
        bg = bm[:, g * SSD_STATE:(g + 1) * SSD_STATE]
        cg = cm[:, g * SSD_STATE:(g + 1) * SSD_STATE]
        cb = _dot_nt(cg, bg)
        hs = h_ref[:, g * 256:(g + 1) * 256]
        yoff = jnp.dot(cg, hs.astype(BF16), preferred_element_type=F32) * jnp.exp(ae[:, g * 256:(g + 1) * 256])
        for pr in range(2):
            xp = xdtb[:, g * 256 + pr * LANES:g * 256 + (pr + 1) * LANES]
            res = []
            for k in range(2):
                hd = g * 4 + pr * 2 + k
                dmat = acol[:, hd:hd + 1] - arow[hd:hd + 1, :]
                lmat = jnp.exp(jnp.where(mask, dmat, -jnp.inf))
                res.append(jnp.dot((cb * lmat).astype(BF16), xp, preferred_element_type=F32))
            ys.append(jnp.where(lane < SSD_HEAD_DIM, res[0], res[1]) + yoff[:, pr * LANES:(pr + 1) * LANES])
        h_ref[:, g * 256:(g + 1) * 256] = (hs * jnp.exp(ae_last[:, g * 256:(g + 1) * 256])
                                           + _dot_tn(bg, xw[:, g * 256:(g + 1) * 256]))
    y_ref[r0:r0 + CHUNK, :] = jnp.concatenate(ys, axis=1).astype(y_ref.dtype)


def _ssd_kernel(nchunk, xf_ref, bmf_ref, cmf_ref, dtf_ref, dttf_ref, xb_ref, bmb_ref, cmb_ref, dtb_ref, dttb_ref,
                an_ref, ant_ref, tri_ref, trit_ref, ex_ref, h0f_ref, h0b_ref,
                yf_ref, hTf_ref, yb_ref, hTb_ref, hf_ref, hb_ref):
    s = pl.program_id(1)

    @pl.when(s == 0)
    def _():
        hf_ref[...] = h0f_ref[...]
        hb_ref[...] = h0b_ref[...]

    ex = ex_ref[...]
    dt_f, acol_f, arow_f = _ssd_block_decays(False, dtf_ref, dttf_ref, an_ref[0], ant_ref[0], tri_ref[0], trit_ref[0])
    dt_b, acol_b, arow_b = _ssd_block_decays(True, dtb_ref, dttb_ref, an_ref[1], ant_ref[1], tri_ref[1], trit_ref[1])
    dte_f, ae_f, dte_b, ae_b = _expand_heads([dt_f, acol_f, dt_b, acol_b], ex)
    dec_f = (acol_f, arow_f, dte_f, ae_f)
    dec_b = (acol_b, arow_b, dte_b, ae_b)
    for j in range(nchunk):
        _ssd_chunk(False, j * CHUNK, xf_ref, bmf_ref, cmf_ref, dec_f, hf_ref, yf_ref)
        _ssd_chunk(True, (nchunk - 1 - j) * CHUNK, xb_ref, bmb_ref, cmb_ref, dec_b, hb_ref, yb_ref)

    @pl.when(s == pl.num_programs(1) - 1)
    def _():
        hTf_ref[...] = hf_ref[...]
        hTb_ref[...] = hb_ref[...]


def _ssd_scan(xs, bm, cm, dt, dtt, a_log, h0f, h0b, rb):
    b_, r_, _ = xs.shape
    nblk = r_ // rb
    trif, tritf, ex = _ssd_consts(False, rb // CHUNK)
    trib, tritb, _ = _ssd_consts(True, rb // CHUNK)
    tri = jnp.stack([trif, trib])
    trit = jnp.stack([tritf, tritb])
    a_neg = -jnp.exp(a_log.astype(F32))
    an = jnp.zeros((2, 1, LANES), F32).at[:, 0, :SSD_HEADS].set(a_neg)
    ant = a_neg.reshape(2, SSD_HEADS, 1)
    rowf = lambda c: pl.BlockSpec((None, rb, c), lambda b, s: (b, s, 0))
    rowb = lambda c: pl.BlockSpec((None, rb, c), lambda b, s: (b, nblk - 1 - s, 0))
    full = lambda a: pl.BlockSpec(a.shape, lambda b, s: (0,) * a.ndim)
    st = pl.BlockSpec((None, SSD_STATE, SSD_INNER), lambda b, s: (b, 0, 0))
    y_shape = jax.ShapeDtypeStruct((b_, r_, SSD_INNER), BF16)
    h_shape = jax.ShapeDtypeStruct((b_, SSD_STATE, SSD_INNER), F32)
    return pl.pallas_call(
        functools.partial(_ssd_kernel, rb // CHUNK),
        grid=(b_, nblk),
        in_specs=[rowf(512), rowf(256), rowf(256), rowf(2 * LANES),
                  pl.BlockSpec((None, 16, rb), lambda b, s: (b, 0, s)),
                  rowb(512), rowb(256), rowb(256), rowb(2 * LANES),
                  pl.BlockSpec((None, 16, rb), lambda b, s: (b, 0, nblk - 1 - s)),
                  full(an), full(ant), full(tri), full(trit), full(ex), st, st],
        out_specs=[rowf(512), st, rowb(512), st],
        out_shape=[y_shape, h_shape, y_shape, h_shape],
        scratch_shapes=[pltpu.VMEM((SSD_STATE, SSD_INNER), F32)] * 2,
        compiler_params=_cparams(("arbitrary", "arbitrary")),
        name="ssd_scan",
    )(xs, bm, cm, dt, dtt, xs, bm, cm, dt, dtt, an, ant, tri, trit, ex, h0f, h0b)


_HGRN_LEVELS = (64, 32, 16, 8, 4, 2, 1)


def _hgrn_consts(rev):
    i = np.arange(CHUNK)
    before = _tri(rev)
    after_strict = ~before
    mats, masks = [], []
    for c in _HGRN_LEVELS:
        same = (i[:, None] // c) == (i[None, :] // c)
        q_side = same & before
        k_side = same & after_strict
        if c == CHUNK:
            mats += [q_side, k_side]
        else:
            blk = i // c
            first = (blk % 2 == 1) if rev else (blk % 2 == 0)
            if c > 1:
                mats.append(np.where(first[:, None], k_side, q_side))
            pair = first[None, :] & (~first)[:, None] & ((blk[:, None] // 2) == (blk[None, :] // 2))
            masks.append(pair)
    masks.append(np.eye(CHUNK, dtype=bool))
    m = np.concatenate(mats, axis=0).astype(np.float32)
    m = np.concatenate([m, m], axis=1)
    mk = np.stack(masks, axis=0).astype(np.float32)
    return jnp.asarray(m, BF16), jnp.asarray(mk, F32)


def _hgrn_chunk(rev, r0, q_ref, g_ref, v_ref, mall, mk_ref, s_ref, o_ref):
    nl = len(_HGRN_LEVELS)
    last = 0 if rev else CHUNK - 1
    g = g_ref[r0:r0 + CHUNK, :]
    g2 = jnp.concatenate(_split2(g), axis=0)
    eall = jnp.exp(jnp.dot(mall, g2, preferred_element_type=F32))
    ff = jnp.exp(g)
    kk = 1.0 - ff
    qq = q_ref[r0:r0 + CHUNK, :].astype(F32)
    vv = v_ref[r0:r0 + CHUNK, :]
    outs = []
    for h in range(HGRN_HEADS):
        ls = slice(h * LANES, (h + 1) * LANES)
        qh, kh, vh = qq[:, ls], kk[:, ls], vv[:, ls]
        khb = kh.astype(BF16)

        def fac(idx, ls=ls):
            return eall[idx * CHUNK:(idx + 1) * CHUNK, ls]

        q01 = jnp.concatenate([qh, qh * ff[:, ls]], axis=0).astype(BF16)
        a01 = _dot_nt(q01, khb)
        att = a01[:CHUNK] * mk_ref[nl - 1] + a01[CHUNK:] * mk_ref[nl - 2]
        for l in range(1, nl - 1):
            e = fac(l + 1)
            att = att + _dot_nt((qh * e).astype(BF16), (kh * e).astype(BF16)) * mk_ref[l - 1]
        st = s_ref[h]
        qs_ = (qh * fac(0)).astype(BF16)
        o = _dot_nt(qs_, st.astype(BF16)) + jnp.dot(att.astype(BF16), vh, preferred_element_type=F32)
        outs.append(o)
        ks_ = (kh * fac(1)).astype(BF16)
        dec = fac(0)[last:last + 1, :]
        s_ref[h] = st * dec + _dot_tn(vh, ks_)
    o_ref[r0:r0 + CHUNK, :] = jnp.concatenate(outs, axis=1).astype(o_ref.dtype)


def _hgrn_kernel(nchunk, qf_ref, gf_ref, vf_ref, qb_ref, gb_ref, vb_ref, mf_ref, mkf_ref, mb_ref, mkb_ref,
                 s0f_ref, s0b_ref, of_ref, sTf_ref, ob_ref, sTb_ref, sf_ref, sb_ref):
    s = pl.program_id(1)

    @pl.when(s == 0)
    def _():
        sf_ref[...] = s0f_ref[...]
        sb_ref[...] = s0b_ref[...]

    mallf = mf_ref[...]
    mallb = mb_ref[...]
    for j in range(nchunk):
        _hgrn_chunk(False, j * CHUNK, qf_ref, gf_ref, vf_ref, mallf, mkf_ref, sf_ref, of_ref)
        _hgrn_chunk(True, (nchunk - 1 - j) * CHUNK, qb_ref, gb_ref, vb_ref, mallb, mkb_ref, sb_ref, ob_ref)

    @pl.when(s == pl.num_programs(1) - 1)
    def _():
        sTf_ref[...] = sf_ref[...]
        sTb_ref[...] = sb_ref[...]


def _hgrn_scan(q, gf, gb, v, s0f, s0b, rb):
    b_, r_, _ = q.shape
    nblk = r_ // rb
    mallf, mkf = _hgrn_consts(False)
    mallb, mkb = _hgrn_consts(True)
    rowf = pl.BlockSpec((None, rb, 512), lambda b, s: (b, s, 0))
    rowb = pl.BlockSpec((None, rb, 512), lambda b, s: (b, nblk - 1 - s, 0))
    full = lambda a: pl.BlockSpec(a.shape, lambda b, s: (0,) * a.ndim)
    st = pl.BlockSpec((None, HGRN_HEADS, LANES, HGRN_DK), lambda b, s: (b, 0, 0, 0))
    o_shape = jax.ShapeDtypeStruct((b_, r_, 512), BF16)
    s_shape = jax.ShapeDtypeStruct((b_, HGRN_HEADS, LANES, HGRN_DK), F32)
    return pl.pallas_call(
        functools.partial(_hgrn_kernel, rb // CHUNK),
        grid=(b_, nblk),
        in_specs=[rowf, rowf, rowf, rowb, rowb, rowb, full(mallf), full(mkf), full(mallb), full(mkb), st, st],
        out_specs=[rowf, st, rowb, st],
        out_shape=[o_shape, s_shape, o_shape, s_shape],
        scratch_shapes=[pltpu.VMEM((HGRN_HEADS, LANES, HGRN_DK), F32)] * 2,
        compiler_params=_cparams(("arbitrary", "arbitrary")),
        name="hgrn_scan",
    )(q, gf, v, q, gb, v, mallf, mkf, mallb, mkb, s0f, s0b)


def _rope_tables(length):
    rows = length // GRID_W
    row = jnp.repeat(jnp.arange(rows, dtype=F32), GRID_W)
    col = jnp.tile(jnp.arange(GRID_W, dtype=F32), rows)

    def cs(rot_dim):
        n_freq = rot_dim // 4
        inv_freq = ROPE_BASE ** (-jnp.arange(n_freq, dtype=F32) / n_freq)
        ang = jnp.concatenate([row[:, None] * inv_freq, col[:, None] * inv_freq], axis=-1)
        c, s = jnp.cos(ang), jnp.sin(ang)
        return jnp.concatenate([c, c], axis=-1), jnp.concatenate([-s, s], axis=-1)

    ca, sa = cs(DA_HEAD_DIM)
    cq, sq = cs(MLA_ROPE)
    ones = jnp.ones((length, MLA_NOPE), F32)
    pad1 = jnp.ones((length, LANES - MLA_NOPE - MLA_ROPE), F32)
    cqt = jnp.concatenate([ones, cq, pad1], axis=-1)
    sqt = jnp.concatenate([0.0 * ones, sq, 0.0 * pad1], axis=-1)
    return jnp.tile(ca, (1, 2)), jnp.tile(sa, (1, 2)), cqt, sqt


def _att_weights(att_w_in, mla_q_norm, mla_w_uq, mla_kv_norm, mla_w_ukv):
    d = att_w_in.shape[0]
    kr = att_w_in[:, 2176:2208]
    kr_blk = jnp.concatenate([jnp.zeros((d, MLA_NOPE), F32), kr,
                              jnp.zeros((d, LANES - MLA_NOPE - MLA_ROPE), F32)], axis=1)
    win = jnp.concatenate([att_w_in[:, :1024], att_w_in[:, 1536:2176], kr_blk], axis=1).astype(BF16)
    wva = att_w_in[:, 1024:1536].T.astype(BF16)
    wq = mla_w_uq.reshape(MLA_Q_RANK, MLA_HEADS, MLA_NOPE + MLA_ROPE)
    wq = jnp.pad(wq, ((0, 0), (0, 0), (0, LANES - MLA_NOPE - MLA_ROPE))).reshape(MLA_Q_RANK, MLA_HEADS * LANES)
    wkv = mla_w_ukv.reshape(MLA_KV_RANK, MLA_HEADS, MLA_NOPE + MLA_V)
    wk = jnp.pad(wkv[:, :, :MLA_NOPE], ((0, 0), (0, 0), (0, LANES - MLA_NOPE))).reshape(MLA_KV_RANK, MLA_HEADS * LANES)
    wv = wkv[:, :, MLA_NOPE:].reshape(MLA_KV_RANK, MLA_HEADS * MLA_V)
    return dict(win=win, wva=wva, qn=mla_q_norm.reshape(1, -1), wuq=wq.astype(BF16),
                kvn=mla_kv_norm.reshape(1, -1), wk=wk.astype(BF16), wv=wv.T.astype(BF16))


def _rec_weights(rec_w_in, bound_logits, dt_bias):
    d = rec_w_in.shape[0]
    w = rec_w_in[:, :4096].astype(BF16)
    wdt = rec_w_in[:, 4096:4112]
    pad = jnp.zeros((d, LANES - SSD_HEADS), F32)
    wdt_rows = jnp.concatenate([wdt[:, :SSD_HEADS], pad, wdt[:, SSD_HEADS:], pad], axis=1).astype(BF16)
    zb = jnp.zeros((LANES - SSD_HEADS,), F32)
    dtb = jnp.concatenate([dt_bias[0], zb, dt_bias[1], zb]).reshape(1, 2 * LANES)
    return dict(w=w, wdt=wdt_rows, bl=bound_logits, dtb=dtb)


class _Tiles(NamedTuple):
    proj: int
    ctx: int
    att_q: int
    att_k: int
    mlp0: int
    hid0: int
    mlp1: int
    hid1: int
    scan: int


def _tiles(length, n_ctx):
    cap = lambda n, full: min(n, full)
    return _Tiles(proj=cap(512, length), ctx=cap(256, n_ctx), att_q=cap(1024, length), att_k=cap(1024, length),
                  mlp0=cap(1024, length), hid0=1024, mlp1=cap(512, length), hid1=2048, scan=cap(512, length))


def kernel(x, c, ctx, c_ctx, w_mod, b_mod, norm_mix, norm_mlp, w_mlp_in, w_mlp_out, att_w_in, att_lambda, att_subnorm, mla_q_norm, mla_w_uq, mla_kv_norm, mla_w_ukv, att_w_out, rec_w_in, hgrn_bound_logits, hgrn_out_norm, ssd_conv_w, ssd_conv_b, ssd_a_log, ssd_dt_bias, ssd_skip, ssd_norm, rec_w_out, final_norm):
    b_, length, d = x.shape
    n_ctx = ctx.shape[1]
    assert w_mod.shape[0] == 2 and d == D_MODEL

    cc = jnp.zeros((8, d), F32).at[:b_].set(c).at[b_].set(c_ctx)
    mods = _mod_vectors(cc, w_mod, b_mod)
    mods = jnp.pad(mods.reshape(2, 8, 6, d), ((0, 0), (0, 0), (0, 2), (0, 0)))
    mods_lat = [mods[l, :b_] for l in range(2)]
    mods_ctx = [jnp.broadcast_to(mods[l, b_], (b_, 8, d)) for l in range(2)]
    row = lambda v: v.reshape(1, -1)

    t = _tiles(length, n_ctx)
    tr, trc = t.proj, t.ctx
    w0 = _att_weights(att_w_in[0], mla_q_norm[0], mla_w_uq[0], mla_kv_norm[0], mla_w_ukv[0])
    tables = _rope_tables(length)
    qa, ka, va, qm, km, vm = _proj0(x, mods_lat[0], row(norm_mix[0]), w0, tables, tr)
    qa_c, ka_c, va_c, qm_c, km_c, vm_c = _proj0(ctx, mods_ctx[0], row(norm_mix[0]), w0, None, trc)
    lam_init = 0.8 - 0.6 * math.exp(-0.3 * 0)
    da_extra = (att_lambda[0], row(att_subnorm[0]))
    ya = _attention("da", qa, ka_c, va_c, ka, va, da_extra, t.att_q, t.att_k, lam_init)
    ym = _attention("mla", qm, km_c, vm_c, km, vm, None, t.att_q, t.att_k, lam_init)
    ya_c = _attention("da", qa_c, ka_c, va_c, None, None, da_extra, trc, t.att_k, lam_init)
    ym_c = _attention("mla", qm_c, km_c, vm_c, None, None, None, trc, t.att_k, lam_init)
    wo0 = (att_w_out[0].astype(BF16).reshape(2, 512, d),)
    w1 = w_mlp_in.astype(BF16)
    w2 = w_mlp_out.astype(BF16)
    h_lat = _mix_mlp(_merge_attention, (ya, ym), wo0, x, mods_lat[0], row(norm_mlp[0]), w1[0], w2[0], None,
                     t.mlp0, t.hid0)
    h_ctx = _mix_mlp(_merge_attention, (ya_c, ym_c), wo0, ctx, mods_ctx[0], row(norm_mlp[0]), w1[0], w2[0], None,
                     trc, t.hid0)

    w1r = _rec_weights(rec_w_in[0], hgrn_bound_logits, ssd_dt_bias[0])
    cw = ssd_conv_w[0]
    cb = row(ssd_conv_b[0])
    q_c, gf_c, gb_c, i_c, _, _, xs_c, bm_c, cm_c, dt_c, dtt_c = _proj1(h_ctx, mods_ctx[1], row(norm_mix[1]), w1r,
                                                                        cw, cb, trc)
    q_l, gf_l, gb_l, i_l, sg_l, sz_l, xs_l, bm_l, cm_l, dt_l, dtt_l = _proj1(h_lat, mods_lat[1], row(norm_mix[1]),
                                                                             w1r, cw, cb, tr)
    rbc, rbl = t.ctx, t.scan
    zs = jnp.zeros((b_, SSD_STATE, SSD_INNER), F32)
    zh = jnp.zeros((b_, HGRN_HEADS, LANES, HGRN_DK), F32)
    _, hs_f, _, hs_b = _ssd_scan(xs_c, bm_c, cm_c, dt_c, dtt_c, ssd_a_log[0], zs, zs, rbc)
    y_f, _, y_b, _ = _ssd_scan(xs_l, bm_l, cm_l, dt_l, dtt_l, ssd_a_log[0], hs_f, hs_b, rbl)
    _, ss_f, _, ss_b = _hgrn_scan(q_c, gf_c, gb_c, i_c, zh, zh, rbc)
    o_f, _, o_b, _ = _hgrn_scan(q_l, gf_l, gb_l, i_l, ss_f, ss_b, rbl)
    skip = row(jnp.repeat(ssd_skip[0], SSD_HEAD_DIM))
    wro = rec_w_out[0].astype(BF16).reshape(2, 512, d)
    return _mix_mlp(_merge_recurrent, (o_f, o_b, y_f, y_b, sg_l, sz_l, xs_l),
                    (row(hgrn_out_norm[0]), skip, row(ssd_norm[0]), wro),
                    h_lat, mods_lat[1], row(norm_mlp[1]), w1[1], w2[1], row(final_norm), t.mlp1, t.hid1)
```

```python
import functools
import math
from typing import NamedTuple

import numpy as np
import jax
import jax.numpy as jnp
from jax import lax
from jax.experimental import pallas as pl
from jax.experimental.pallas import tpu as pltpu

F32 = jnp.float32
BF16 = jnp.bfloat16

D_MODEL = 1024
GRID_W = 64
DA_HEADS = 4
DA_HEAD_DIM = 64
MLA_HEADS = 8
MLA_NOPE = 64
MLA_ROPE = 32
MLA_V = 64
MLA_Q_RANK = 384
MLA_KV_RANK = 256
HGRN_HEADS = 4
HGRN_DK = 128
SSD_HEADS = 8
SSD_HEAD_DIM = 64
SSD_GROUPS = 2
SSD_STATE = 128
SSD_CONV_W = 5
SSD_INNER = SSD_HEADS * SSD_HEAD_DIM
SSD_CONV_CH = SSD_INNER + 2 * SSD_GROUPS * SSD_STATE
MLP_HIDDEN = 4 * D_MODEL
ROPE_BASE = 10000.0
NORM_EPS = 1e-6
CHUNK = 64

LANES = 128
LOG2E = 1.4426950408889634
VMEM_LIMIT = 56 * 1024 * 1024


def _cparams(sem):
    return pltpu.CompilerParams(dimension_semantics=sem, vmem_limit_bytes=VMEM_LIMIT)


def _rms(x):
    return x * lax.rsqrt(jnp.mean(x * x, axis=-1, keepdims=True) + NORM_EPS)


def _modulate(x, g, shift, scale):
    return (_rms(x) * g) * (1.0 + scale) + shift


def _silu(x):
    return x * jax.nn.sigmoid(x)


def _dot_nt(a, b):
    return lax.dot_general(a, b, (((1,), (1,)), ((), ())), preferred_element_type=F32)


def _dot_tn(a, b):
    return lax.dot_general(a, b, (((0,), (0,)), ((), ())), preferred_element_type=F32)


def _split2(x):
    hi = x.astype(BF16)
    lo = (x - hi.astype(F32)).astype(BF16)
    return hi, lo


def _exact_left(m01, x):
    hi, lo = _split2(x)
    return (jnp.dot(m01, hi, preferred_element_type=F32) + jnp.dot(m01, lo, preferred_element_type=F32))


def _exact_right(x, m01):
    hi, lo = _split2(x)
    return (jnp.dot(hi, m01, preferred_element_type=F32) + jnp.dot(lo, m01, preferred_element_type=F32))


def _mod_kernel(c_ref, w_ref, b_ref, o_ref):
    a = _silu(c_ref[...]).astype(BF16)
    o_ref[...] = jnp.dot(a, w_ref[...].astype(BF16), preferred_element_type=F32) + b_ref[...]


def _mod_vectors(cc, w_mod, b_mod):
    depth, d, n = w_mod.shape
    tn = 1024
    return pl.pallas_call(
        _mod_kernel,
        grid=(depth, n // tn),
        in_specs=[
            pl.BlockSpec((8, d), lambda l, j: (0, 0)),
            pl.BlockSpec((None, d, tn), lambda l, j: (l, 0, j)),
            pl.BlockSpec((None, 1, tn), lambda l, j: (l, 0, j)),
        ],
        out_specs=pl.BlockSpec((None, 8, tn), lambda l, j: (l, 0, j)),
        out_shape=jax.ShapeDtypeStruct((depth, 8, n), F32),
        compiler_params=_cparams(("arbitrary", "arbitrary")),
        name="mod_vectors",
    )(cc, w_mod, b_mod.reshape(depth, 1, n))


def _rope_partner(x, half):
    n = x.shape[-1]
    lane = lax.broadcasted_iota(jnp.int32, x.shape, x.ndim - 1)
    up = pltpu.roll(x, n - half, x.ndim - 1)
    dn = pltpu.roll(x, half, x.ndim - 1)
    return jnp.where((lane % (2 * half)) < half, up, dn)


def _tile_lanes(x, n):
    return jnp.concatenate([x] * n, axis=-1)


def _proj0_kernel(use_rope, *refs):
    if use_rope:
        (h_ref, mod_ref, g_ref, win_ref, wva_ref, qn_ref, wuq_ref, kvn_ref, wk_ref, wv_ref,
         ca_ref, sa_ref, cq_ref, sq_ref,
         qa_ref, ka_ref, va_ref, qm_ref, km_ref, vm_ref) = refs
    else:
        (h_ref, mod_ref, g_ref, win_ref, wva_ref, qn_ref, wuq_ref, kvn_ref, wk_ref, wv_ref,
         qa_ref, ka_ref, va_ref, qm_ref, km_ref, vm_ref) = refs
    mod = mod_ref[...]
    ub = _modulate(h_ref[...], g_ref[...], mod[0:1], mod[1:2]).astype(BF16)

    def proj(lo, hi):
        return jnp.dot(ub, win_ref[:, lo:hi], preferred_element_type=F32)

    qa = proj(0, 512) * (DA_HEAD_DIM ** -0.5 * LOG2E)
    ka = proj(512, 1024)
    va_ref[...] = _dot_nt(wva_ref[...], ub).astype(BF16)
    cq = proj(1024, 1408)
    ckv = proj(1408, 1664)
    kr = proj(1664, 1792)
    cqn = (_rms(cq) * qn_ref[...]).astype(BF16)
    qm = jnp.dot(cqn, wuq_ref[...], preferred_element_type=F32) * ((MLA_NOPE + MLA_ROPE) ** -0.5 * LOG2E)
    ckvn = (_rms(ckv) * kvn_ref[...]).astype(BF16)
    kn = jnp.dot(ckvn, wk_ref[...], preferred_element_type=F32)
    vm_ref[...] = _dot_nt(wv_ref[...], ckvn).astype(BF16)
    if use_rope:
        ca = _tile_lanes(ca_ref[...], DA_HEADS)
        sa = _tile_lanes(sa_ref[...], DA_HEADS)
        qa = qa * ca + _rope_partner(qa, DA_HEAD_DIM // 2) * sa
        ka = ka * ca + _rope_partner(ka, DA_HEAD_DIM // 2) * sa
        cq1 = cq_ref[...]
        sq1 = sq_ref[...]
        qm = qm * _tile_lanes(cq1, MLA_HEADS) + _rope_partner(qm, MLA_ROPE // 2) * _tile_lanes(sq1, MLA_HEADS)
        kr = kr * cq1 + _rope_partner(kr, MLA_ROPE // 2) * sq1
    qa_ref[...] = qa.astype(BF16)
    ka_ref[...] = ka.astype(BF16)
    qm_ref[...] = qm.astype(BF16)
    km_ref[...] = (kn + _tile_lanes(kr, MLA_HEADS)).astype(BF16)


def _proj0(h, mods, gain, w, tables, tr):
    g_, r_, d = h.shape
    use_rope = tables is not None
    row = lambda c: pl.BlockSpec((None, tr, c), lambda b, t: (b, t, 0))
    full = lambda a: pl.BlockSpec(a.shape, lambda b, t: (0,) * a.ndim)
    in_specs = [row(d), pl.BlockSpec((None, 8, d), lambda b, t: (b, 0, 0)), full(gain),
                full(w["win"]), full(w["wva"]), full(w["qn"]), full(w["wuq"]), full(w["kvn"]), full(w["wk"]),
                full(w["wv"])]
    args = [h, mods, gain, w["win"], w["wva"], w["qn"], w["wuq"], w["kvn"], w["wk"], w["wv"]]
    if use_rope:
        in_specs += [pl.BlockSpec((tr, LANES), lambda b, t: (t, 0))] * 4
        args += list(tables)
    col = pl.BlockSpec((None, 512, tr), lambda b, t: (b, 0, t))
    widths = (512, 512, None, 1024, 1024, None)
    return pl.pallas_call(
        functools.partial(_proj0_kernel, use_rope),
        grid=(g_, r_ // tr),
        in_specs=in_specs,
        out_specs=[col if c is None else row(c) for c in widths],
        out_shape=[jax.ShapeDtypeStruct((g_, 512, r_) if c is None else (g_, r_, c), BF16) for c in widths],
        compiler_params=_cparams(("arbitrary", "arbitrary")),
        name="proj0_rope" if use_rope else "proj0_ctx",
    )(*args)


def _attn_kernel(mode, has_lat, tk, nk, lam_init, *refs):
    refs = list(refs)
    q_ref, kc_ref, vc_ref = refs[:3]
    refs = refs[3:]
    if has_lat:
        kl_ref, vl_ref = refs[:2]
        refs = refs[2:]
    if mode == "da":
        lam_ref, sub_ref = refs[:2]
        refs = refs[2:]
    o_ref, ma_ref, la_ref, acca_ref, mb_ref, lb_ref, accb_ref = refs[:7]
    if has_lat:
        s_ref, mx_ref = refs[7:]

    q = q_ref[...]
    if mode == "da":
        lane = lax.broadcasted_iota(jnp.int32, q.shape, 1)
        zero = jnp.zeros_like(q)
        qs = (jnp.where(lane < DA_HEAD_DIM, q, zero), jnp.where(lane >= DA_HEAD_DIM, q, zero))
    else:
        qs = (q[:, :LANES], q[:, LANES:])
    stats = ((ma_ref, la_ref, acca_ref), (mb_ref, lb_ref, accb_ref))
    for m_ref, l_ref, acc_ref in stats:
        m_ref[...] = jnp.full(m_ref.shape, -jnp.inf, F32)
        l_ref[...] = jnp.zeros(l_ref.shape, F32)
        acc_ref[...] = jnp.zeros(acc_ref.shape, F32)

    def scores(k):
        ks = (k, k) if mode == "da" else (k[:, :LANES], k[:, LANES:])
        return [_dot_nt(ks[i], qs[i]) for i in range(2)]

    def softmax(i, s):
        m_ref, l_ref, _ = stats[i]
        m_prev = m_ref[...]
        m_new = jnp.maximum(m_prev, jnp.max(s, axis=0, keepdims=True))
        alpha = jnp.exp2(m_prev - m_new)
        p = jnp.exp2(s - m_new)
        l_ref[...] = alpha * l_ref[...] + jnp.sum(p, axis=0, keepdims=True)
        m_ref[...] = m_new
        return p.astype(BF16), alpha

    def accumulate(i, vt, p, alpha):
        acc_ref = stats[i][2]
        acc_ref[...] = alpha * acc_ref[...] + jnp.dot(vt, p, preferred_element_type=F32)

    if not has_lat:
        ss = scores(kc_ref[...])
        pa = [softmax(i, ss[i]) for i in range(2)]
        for i in range(2):
            accumulate(i, vc_ref[...], *pa[i])
    else:
        nc = kc_ref.shape[0]

        def chunk(c):
            return pl.ds(c * tk if isinstance(c, int) else pl.multiple_of(c * tk, tk), tk)

        def keys(st):
            return kc_ref[...] if isinstance(st, int) and st == 0 else kl_ref[chunk(st - 1), :]

        def vals(st):
            return vc_ref[...] if isinstance(st, int) and st == 0 else vl_ref[:, chunk(st - 1)]

        def lookahead(st, slot, n):
            ss = scores(keys(st))
            for i in range(2):
                s_ref[slot, i, 0:n, :] = ss[i]
                mx_ref[1, i] = jnp.maximum(mx_ref[0, i], jnp.max(ss[i], axis=0, keepdims=True))

        def consume(st, slot, n):
            vt = vals(st)
            for i in range(2):
                m_ref, l_ref, acc_ref = stats[i]
                m_cur = mx_ref[0, i]
                alpha = jnp.exp2(m_ref[...] - m_cur)
                p = jnp.exp2(s_ref[slot, i, 0:n, :] - m_cur)
                l_ref[...] = alpha * l_ref[...] + jnp.sum(p, axis=0, keepdims=True)
                acc_ref[...] = alpha * acc_ref[...] + jnp.dot(vt, p.astype(BF16), preferred_element_type=F32)
                m_ref[...] = m_cur

        def advance():
            for i in range(2):
                mx_ref[0, i] = mx_ref[1, i]

        def run(st, slot, has_next):
            n = nc if isinstance(st, int) and st == 0 else tk
            if has_next:
                lookahead(st + 1, 1 - slot, tk)
            consume(st, slot, n)
            if has_next:
                advance()

        for i in range(2):
            mx_ref[0, i] = jnp.full((1, mx_ref.shape[-1]), -jnp.inf, F32)
        lookahead(0, 0, nc)
        advance()
        run(0, 0, True)
        npair = (nk - 1) // 2

        def body(j, carry):
            run(2 * j + 1, 1, True)
            run(2 * j + 2, 0, True)
            return carry

        lax.fori_loop(0, npair, body, 0)
        for st in range(2 * npair + 1, nk + 1):
            run(st, st % 2, st < nk)

    oa = acca_ref[...] / la_ref[...]
    ob = accb_ref[...] / lb_ref[...]
    if mode == "da":
        lp = lam_ref[...]
        lam = (jnp.exp(jnp.sum(lp[0:1] * lp[1:2], axis=1, keepdims=True))
               - jnp.exp(jnp.sum(lp[2:3] * lp[3:4], axis=1, keepdims=True)) + lam_init)
        dlt = (oa - lam * ob).T
        o_ref[...] = ((_rms(dlt) * sub_ref[...]) * (1.0 - lam_init)).astype(o_ref.dtype)
    else:
        chan = lax.broadcasted_iota(jnp.int32, oa.shape, 0)
        o_ref[...] = jnp.where(chan < MLA_V, oa, ob).T.astype(o_ref.dtype)


def _attention(mode, q, kc, vc, kl, vl, extra, tq, tk, lam_init):
    b_, sq, _ = q.shape
    has_lat = kl is not None
    w = LANES if mode == "da" else 2 * LANES
    nh = 4
    kblk = lambda a: pl.BlockSpec((None, a.shape[1], w), lambda b, h, t: (b, 0, h))
    vblk = lambda a: pl.BlockSpec((None, LANES, a.shape[2]), lambda b, h, t: (b, h, 0))
    in_specs = [pl.BlockSpec((None, tq, w), lambda b, h, t: (b, t, h)), kblk(kc), vblk(vc)]
    args = [q, kc, vc]
    nk = 0
    if has_lat:
        in_specs += [kblk(kl), vblk(vl)]
        args += [kl, vl]
        nk = kl.shape[1] // tk
    if mode == "da":
        lam_p, subnorm = extra
        in_specs += [pl.BlockSpec(lam_p.shape, lambda b, h, t: (0, 0)),
                     pl.BlockSpec(subnorm.shape, lambda b, h, t: (0, 0))]
        args += [lam_p, subnorm]
    scratch = []
    for _ in range(2):
        scratch += [pltpu.VMEM((1, tq), F32), pltpu.VMEM((1, tq), F32), pltpu.VMEM((LANES, tq), F32)]
    if has_lat:
        scratch += [pltpu.VMEM((2, 2, tk, tq), F32), pltpu.VMEM((2, 2, 1, tq), F32)]
    return pl.pallas_call(
        functools.partial(_attn_kernel, mode, has_lat, tk, nk, lam_init),
        grid=(b_, nh, sq // tq),
        in_specs=in_specs,
        out_specs=pl.BlockSpec((None, tq, LANES), lambda b, h, t: (b, t, h)),
        out_shape=jax.ShapeDtypeStruct((b_, sq, nh * LANES), BF16),
        scratch_shapes=scratch,
        compiler_params=_cparams(("arbitrary", "arbitrary", "arbitrary")),
        name=f"attn_{mode}_{'lat' if has_lat else 'ctx'}",
    )(*args)


def _merge_attention(ya_ref, ym_ref, w_ref):
    return (jnp.dot(ya_ref[...], w_ref[0], preferred_element_type=F32)
            + jnp.dot(ym_ref[...], w_ref[1], preferred_element_type=F32))


def _merge_recurrent(of_ref, ob_ref, yf_ref, yb_ref, sg_ref, sz_ref, xs_ref, on_ref, sk_ref, sn_ref, w_ref):
    o = of_ref[...].astype(F32) + ob_ref[...].astype(F32)
    on = on_ref[...]
    parts = []
    for hh in range(HGRN_HEADS):
        ls = slice(hh * LANES, (hh + 1) * LANES)
        parts.append(_rms(o[:, ls]) * on[:, ls])
    o = jnp.concatenate(parts, axis=1) * sg_ref[...].astype(F32)
    y = ((yf_ref[...].astype(F32) + yb_ref[...].astype(F32) + sk_ref[...] * xs_ref[...])
         * sz_ref[...].astype(F32))
    sn = sn_ref[...]
    gw = SSD_INNER // SSD_GROUPS
    parts = []
    for gg in range(SSD_GROUPS):
        ls = slice(gg * gw, (gg + 1) * gw)
        parts.append(_rms(y[:, ls]) * sn[:, ls])
    y = jnp.concatenate(parts, axis=1)
    return (jnp.dot(o.astype(BF16), w_ref[0], preferred_element_type=F32)
            + jnp.dot(y.astype(BF16), w_ref[1], preferred_element_type=F32))


def _mix_mlp_kernel(merge, n_mix, final, *refs):
    mix_refs = refs[:n_mix]
    if final:
        h_ref, mod_ref, g_ref, w1_ref, w2_ref, fg_ref, o_ref = refs[n_mix:]
    else:
        h_ref, mod_ref, g_ref, w1_ref, w2_ref, o_ref = refs[n_mix:]
    h1 = h_ref[...] + mod_ref[2:3, :] * merge(*mix_refs)
    u = _modulate(h1, g_ref[...], mod_ref[3:4, :], mod_ref[4:5, :]).astype(BF16)
    a = jnp.dot(u, w1_ref[...], preferred_element_type=F32)
    a = jnp.square(jnp.maximum(a, 0.0)).astype(BF16)
    out = h1 + mod_ref[5:6, :] * jnp.dot(a, w2_ref[...], preferred_element_type=F32)
    if final:
        out = _rms(out) * fg_ref[...]
    o_ref[...] = out


def _mix_mlp(merge, mix_rows, mix_full, h, mods, gain, w1, w2, final_gain, tr):
    g_, r_, d = h.shape
    final = final_gain is not None
    row = lambda c: pl.BlockSpec((None, tr, c), lambda b, t: (b, t, 0))
    full = lambda a: pl.BlockSpec(a.shape, lambda b, t: (0,) * a.ndim)
    resident = lambda a: pl.BlockSpec(a.shape, lambda b, t: (0,) * a.ndim, pipeline_mode=pl.Buffered(1))
    in_specs = ([row(a.shape[-1]) for a in mix_rows] + [full(a) for a in mix_full]
                + [row(d), pl.BlockSpec((None, 8, d), lambda b, t: (b, 0, 0)), full(gain),
                   resident(w1), resident(w2)])
    args = list(mix_rows) + list(mix_full) + [h, mods, gain, w1, w2]
    if final:
        in_specs.append(full(final_gain))
        args.append(final_gain)
    return pl.pallas_call(
        functools.partial(_mix_mlp_kernel, merge, len(mix_rows) + len(mix_full), final),
        grid=(g_, r_ // tr),
        in_specs=in_specs,
        out_specs=row(d),
        out_shape=jax.ShapeDtypeStruct(h.shape, F32),
        compiler_params=_cparams(("arbitrary", "arbitrary")),
        name="mix_mlp_final" if final else "mix_mlp",
    )(*args)


def _softplus(x):
    return jnp.maximum(x, 0.0) + jnp.log1p(jnp.exp(-jnp.abs(x)))


def _proj1_kernel(nt, h_ref, hp_ref, hn_ref, mod_ref, g_ref, w_ref, wdt_ref, bl_ref, dtb_ref, cw_ref, cb_ref,
                  q_ref, gf_ref, gb_ref, i_ref, sg_ref, sz_ref, xs_ref, bm_ref, cm_ref, dt_ref, dtt_ref, pad_ref):
    t = pl.program_id(1)
    tr = h_ref.shape[0]
    mod = mod_ref[...]
    gain = g_ref[...]
    u = _modulate(h_ref[...], gain, mod[0:1], mod[1:2])
    ub = u.astype(BF16)
    u_ext = jnp.concatenate([_modulate(hp_ref[...], gain, mod[0:1], mod[1:2]), u,
                             _modulate(hn_ref[...], gain, mod[0:1], mod[1:2])], axis=0).astype(BF16)
    xbc = jnp.dot(u_ext, w_ref[:, 3072:4096], preferred_element_type=F32)
    ri = lax.broadcasted_iota(jnp.int32, (tr + 16, 1), 0)
    inside = ((ri >= 8) | (t > 0)) & ((ri < tr + 8) | (t < nt - 1))
    pad_ref[...] = jnp.where(inside, xbc, 0.0)
    half = SSD_CONV_W // 2
    y = jnp.zeros((tr, xbc.shape[1]), F32) + cb_ref[...]
    for j in range(SSD_CONV_W):
        y = y + pad_ref[8 - half + j:8 - half + j + tr, :] * cw_ref[j:j + 1, :]
    y = _silu(y)
    xs_ref[...] = y[:, :SSD_INNER]
    bm_ref[...] = y[:, SSD_INNER:SSD_INNER + 256].astype(BF16)
    cm_ref[...] = y[:, SSD_INNER + 256:].astype(BF16)

    def proj(lo, hi):
        return jnp.dot(ub, w_ref[:, lo:hi], preferred_element_type=F32)

    bl = bl_ref[...]
    e = jnp.exp(bl - jnp.max(bl, axis=0, keepdims=True))
    gamma = e / jnp.sum(e, axis=0, keepdims=True)
    lb = (gamma[0:1] + gamma[1:2]) - gamma[0:1]
    q_ref[...] = _silu(proj(0, 512)).astype(BF16)
    for k, out in ((0, gf_ref), (1, gb_ref)):
        lbk = lb[:, 512 * k:512 * (k + 1)]
        f = lbk + (1.0 - lbk) * jax.nn.sigmoid(proj(512 * (k + 1), 512 * (k + 2)))
        out[...] = jnp.log(f)
    i_ref[...] = proj(1536, 2048).astype(BF16)
    sg_ref[...] = _silu(proj(2048, 2560)).astype(BF16)
    sz_ref[...] = _silu(proj(2560, 3072)).astype(BF16)
    dt =_softplus(jnp.dot(ub, wdt_ref[...], preferred_element_type=F32) + dtb_ref[...])
    dt_ref[...] = dt
    dtt_ref[...] = jnp.concatenate([dt[:, :LANES].T[:SSD_HEADS], dt[:, LANES:].T[:SSD_HEADS]], axis=0)


def _proj1(h, mods, gain, w, conv_w, conv_b, tr):
    g_, r_, d = h.shape
    nt = r_ // tr
    nb = tr // 8
    row = lambda c: pl.BlockSpec((None, tr, c), lambda b, t: (b, t, 0))
    full = lambda a: pl.BlockSpec(a.shape, lambda b, t: (0,) * a.ndim)
    names = ("w", "wdt", "bl", "dtb")
    widths = (512, 512, 512, 512, 512, 512, 512, 256, 256, 2 * LANES)
    dts = (BF16, F32, F32, BF16, BF16, BF16, F32, BF16, BF16, F32)
    return pl.pallas_call(
        functools.partial(_proj1_kernel, nt),
        grid=(g_, nt),
        in_specs=[row(d),
                  pl.BlockSpec((None, 8, d), lambda b, t: (b, jnp.maximum(t * nb - 1, 0), 0)),
                  pl.BlockSpec((None, 8, d), lambda b, t: (b, jnp.minimum((t + 1) * nb, nt * nb - 1), 0)),
                  pl.BlockSpec((None, 8, d), lambda b, t: (b, 0, 0)), full(gain)]
        + [full(w[n]) for n in names] + [full(conv_w), full(conv_b)],
        out_specs=[row(c) for c in widths] + [pl.BlockSpec((None, 16, tr), lambda b, t: (b, 0, t))],
        out_shape=[jax.ShapeDtypeStruct((g_, r_, c), t_) for c, t_ in zip(widths, dts)]
        + [jax.ShapeDtypeStruct((g_, 16, r_), F32)],
        scratch_shapes=[pltpu.VMEM((tr + 16, SSD_CONV_CH), F32)],
        compiler_params=_cparams(("arbitrary", "arbitrary")),
        name="proj1",
    )(h, h, h, mods, gain, *[w[n] for n in names], conv_w, conv_b)


def _tri(rev, strict=False):
    i = np.arange(CHUNK)
    if rev:
        m = i[None, :] > i[:, None] if strict else i[None, :] >= i[:, None]
    else:
        m = i[None, :] < i[:, None] if strict else i[None, :] <= i[:, None]
    return m


def _ssd_consts(rev, nchunk):
    tri = np.kron(np.eye(nchunk), _tri(rev).astype(np.float32))
    expand = np.zeros((LANES, SSD_INNER), np.float32)
    for h in range(SSD_HEADS):
        expand[h, h * SSD_HEAD_DIM:(h + 1) * SSD_HEAD_DIM] = 1.0
    return jnp.asarray(tri, BF16), jnp.asarray(tri.T.copy(), BF16), jnp.asarray(expand, BF16)


def _ssd_block_decays(rev, dt_ref, dtt_ref, an, ant, tri, trit):
    dir_ = 1 if rev else 0
    dt = dt_ref[:, dir_ * LANES:(dir_ + 1) * LANES]
    dtt = dtt_ref[8 * dir_:8 * dir_ + 8, :]
    acol = _exact_left(tri, dt * an)
    arow = _exact_right(dtt * ant, trit)
    return dt, acol, arow


def _expand_heads(vals, ex):
    rows = vals[0].shape[0]
    parts = [p for v in vals for p in _split2(v)]
    big = jnp.dot(jnp.concatenate(parts, axis=0), ex, preferred_element_type=F32)
    return [big[2 * i * rows:(2 * i + 1) * rows] + big[(2 * i + 1) * rows:(2 * i + 2) * rows]
            for i in range(len(vals))]


def _ssd_chunk(rev, r0, x_ref, bm_ref, cm_ref, decays, h_ref, y_ref):
    ti = lax.broadcasted_iota(jnp.int32, (CHUNK, CHUNK), 0)
    si = lax.broadcasted_iota(jnp.int32, (CHUNK, CHUNK), 1)
    mask = (si >= ti) if rev else (si <= ti)
    lane = lax.broadcasted_iota(jnp.int32, (CHUNK, LANES), 1)
    last = 0 if rev else CHUNK - 1
    x = x_ref[r0:r0 + CHUNK, :]
    bm = bm_ref[r0:r0 + CHUNK, :]
    cm = cm_ref[r0:r0 + CHUNK, :]
    acol = decays[0][r0:r0 + CHUNK, :]
    arow = decays[1][:, r0:r0 + CHUNK]
    dte = decays[2][r0:r0 + CHUNK, :]
    ae = decays[3][r0:r0 + CHUNK, :]
    ae_last = ae[last:last + 1, :]
    xdt = x * dte
    xw = (xdt * jnp.exp(ae_last - ae)).astype(BF16)
    xdtb = xdt.astype(BF16)
    ys = []
    for g in range(SSD_GROUPS):
        bg = bm[:, g * SSD_STATE:(g + 1) * SSD_STATE]
        cg = cm[:, g * SSD_STATE:(g + 1) * SSD_STATE]
        cb = _dot_nt(cg, bg)
        hs = h_ref[:, g * 256:(g + 1) * 256]
        yoff = jnp.dot(cg, hs.astype(BF16), preferred_element_type=F32) * jnp.exp(ae[:, g * 256:(g + 1) * 256])
        for pr in range(2):
            xp = xdtb[:, g * 256 + pr * LANES:g * 256 + (pr + 1) * LANES]
            res = []
            for k in range(2):
                hd = g * 4 + pr * 2 + k
                dmat = acol[:, hd:hd + 1] - arow[hd:hd + 1, :]
                lmat = jnp.exp(jnp.where(mask, dmat, -jnp.inf))
                res.append(jnp.dot((cb * lmat).astype(BF16), xp, preferred_element_type=F32))
            ys.append(jnp.where(lane < SSD_HEAD_DIM, res[0], res[1]) + yoff[:, pr * LANES:(pr + 1) * LANES])
        h_ref[:, g * 256:(g + 1) * 256] = (hs * jnp.exp(ae_last[:, g * 256:(g + 1) * 256])
                                           + _dot_tn(bg, xw[:, g * 256:(g + 1) * 256]))
    y_ref[r0:r0 + CHUNK, :] = jnp.concatenate(ys, axis=1).astype(y_ref.dtype)


def _ssd_kernel(nchunk, xf_ref, bmf_ref, cmf_ref, dtf_ref, dttf_ref, xb_ref, bmb_ref, cmb_ref, dtb_ref, dttb_ref,
                an_ref, ant_ref, tri_ref, trit_ref, ex_ref, h0f_ref, h0b_ref,
                yf_ref, hTf_ref, yb_ref, hTb_ref, hf_ref, hb_ref):
    s = pl.program_id(1)

    @pl.when(s == 0)
    def _():
        hf_ref[...] = h0f_ref[...]
        hb_ref[...] = h0b_ref[...]

    ex = ex_ref[...]
    dt_f, acol_f, arow_f = _ssd_block_decays(False, dtf_ref, dttf_ref, an_ref[0], ant_ref[0], tri_ref[0], trit_ref[0])
    dt_b, acol_b, arow_b = _ssd_block_decays(True, dtb_ref, dttb_ref, an_ref[1], ant_ref[1], tri_ref[1], trit_ref[1])
    dte_f, ae_f, dte_b, ae_b = _expand_heads([dt_f, acol_f, dt_b, acol_b], ex)
    dec_f = (acol_f, arow_f, dte_f, ae_f)
    dec_b = (acol_b, arow_b, dte_b, ae_b)
    for j in range(nchunk):
        _ssd_chunk(False, j * CHUNK, xf_ref, bmf_ref, cmf_ref, dec_f, hf_ref, yf_ref)
        _ssd_chunk(True, (nchunk - 1 - j) * CHUNK, xb_ref, bmb_ref, cmb_ref, dec_b, hb_ref, yb_ref)

    @pl.when(s == pl.num_programs(1) - 1)
    def _():
        hTf_ref[...] = hf_ref[...]
        hTb_ref[...] = hb_ref[...]


def _ssd_scan(xs, bm, cm, dt, dtt, a_log, h0f, h0b, rb):
    b_, r_, _ = xs.shape
    nblk = r_ // rb
    trif, tritf, ex = _ssd_consts(False, rb // CHUNK)
    trib, tritb, _ = _ssd_consts(True, rb // CHUNK)
    tri = jnp.stack([trif, trib])
    trit = jnp.stack([tritf, tritb])
    a_neg = -jnp.exp(a_log.astype(F32))
    an = jnp.zeros((2, 1, LANES), F32).at[:, 0, :SSD_HEADS].set(a_neg)
    ant = a_neg.reshape(2, SSD_HEADS, 1)
    rowf = lambda c: pl.BlockSpec((None, rb, c), lambda b, s: (b, s, 0))
    rowb = lambda c: pl.BlockSpec((None, rb, c), lambda b, s: (b, nblk - 1 - s, 0))
    full = lambda a: pl.BlockSpec(a.shape, lambda b, s: (0,) * a.ndim)
    st = pl.BlockSpec((None, SSD_STATE, SSD_INNER), lambda b, s: (b, 0, 0))
    y_shape = jax.ShapeDtypeStruct((b_, r_, SSD_INNER), BF16)
    h_shape = jax.ShapeDtypeStruct((b_, SSD_STATE, SSD_INNER), F32)
    return pl.pallas_call(
        functools.partial(_ssd_kernel, rb // CHUNK),
        grid=(b_, nblk),
        in_specs=[rowf(512), rowf(256), rowf(256), rowf(2 * LANES),
                  pl.BlockSpec((None, 16, rb), lambda b, s: (b, 0, s)),
                  rowb(512), rowb(256), rowb(256), rowb(2 * LANES),
                  pl.BlockSpec((None, 16, rb), lambda b, s: (b, 0, nblk - 1 - s)),
                  full(an), full(ant), full(tri), full(trit), full(ex), st, st],
        out_specs=[rowf(512), st, rowb(512), st],
        out_shape=[y_shape, h_shape, y_shape, h_shape],
        scratch_shapes=[pltpu.VMEM((SSD_STATE, SSD_INNER), F32)] * 2,
        compiler_params=_cparams(("arbitrary", "arbitrary")),
        name="ssd_scan",
    )(xs, bm, cm, dt, dtt, xs, bm, cm, dt, dtt, an, ant, tri, trit, ex, h0f, h0b)


_HGRN_LEVELS = (64, 32, 16, 8, 4, 2, 1)


def _hgrn_consts(rev):
    i = np.arange(CHUNK)
    before = _tri(rev)
    after_strict = ~before
    mats, masks = [], []
    for c in _HGRN_LEVELS:
        same = (i[:, None] // c) == (i[None, :] // c)
        q_side = same & before
        k_side = same & after_strict
        if c == CHUNK:
            mats += [q_side, k_side]
        else:
            blk = i // c
            first = (blk % 2 == 1) if rev else (blk % 2 == 0)
            if c > 1:
                mats.append(np.where(first[:, None], k_side, q_side))
            pair = first[None, :] & (~first)[:, None] & ((blk[:, None] // 2) == (blk[None, :] // 2))
            masks.append(pair)
    masks.append(np.eye(CHUNK, dtype=bool))
    m = np.concatenate(mats, axis=0).astype(np.float32)
    m = np.concatenate([m, m], axis=1)
    mk = np.stack(masks, axis=0).astype(np.float32)
    return jnp.asarray(m, BF16), jnp.asarray(mk, F32)


def _hgrn_chunk(rev, r0, q_ref, g_ref, v_ref, mall, mk_ref, s_ref, o_ref):
    nl = len(_HGRN_LEVELS)
    last = 0 if rev else CHUNK - 1
    g = g_ref[r0:r0 + CHUNK, :]
    g2 = jnp.concatenate(_split2(g), axis=0)
    eall = jnp.exp(jnp.dot(mall, g2, preferred_element_type=F32))
    ff = jnp.exp(g)
    kk = 1.0 - ff
    qq = q_ref[r0:r0 + CHUNK, :].astype(F32)
    vv = v_ref[r0:r0 + CHUNK, :]
    outs = []
    for h in range(HGRN_HEADS):
        ls = slice(h * LANES, (h + 1) * LANES)
        qh, kh, vh = qq[:, ls], kk[:, ls], vv[:, ls]
        khb = kh.astype(BF16)

        def fac(idx, ls=ls):
            return eall[idx * CHUNK:(idx + 1) * CHUNK, ls]

        q01 = jnp.concatenate([qh, qh * ff[:, ls]], axis=0).astype(BF16)
        a01 = _dot_nt(q01, khb)
        att = a01[:CHUNK] * mk_ref[nl - 1] + a01[CHUNK:] * mk_ref[nl - 2]
        for l in range(1, nl - 1):
            e = fac(l + 1)
            att = att + _dot_nt((qh * e).astype(BF16), (kh * e).astype(BF16)) * mk_ref[l - 1]
        st = s_ref[h]
        qs_ = (qh * fac(0)).astype(BF16)
        o = _dot_nt(qs_, st.astype(BF16)) + jnp.dot(att.astype(BF16), vh, preferred_element_type=F32)
        outs.append(o)
        ks_ = (kh * fac(1)).astype(BF16)
        dec = fac(0)[last:last + 1, :]
        s_ref[h] = st * dec + _dot_tn(vh, ks_)
    o_ref[r0:r0 + CHUNK, :] = jnp.concatenate(outs, axis=1).astype(o_ref.dtype)


def _hgrn_kernel(nchunk, qf_ref, gf_ref, vf_ref, qb_ref, gb_ref, vb_ref, mf_ref, mkf_ref, mb_ref, mkb_ref,
                 s0f_ref, s0b_ref, of_ref, sTf_ref, ob_ref, sTb_ref, sf_ref, sb_ref):
    s = pl.program_id(1)

    @pl.when(s == 0)
    def _():
        sf_ref[...] = s0f_ref[...]
        sb_ref[...] = s0b_ref[...]

    mallf = mf_ref[...]
    mallb = mb_ref[...]
    for j in range(nchunk):
        _hgrn_chunk(False, j * CHUNK, qf_ref, gf_ref, vf_ref, mallf, mkf_ref, sf_ref, of_ref)
        _hgrn_chunk(True, (nchunk - 1 - j) * CHUNK, qb_ref, gb_ref, vb_ref, mallb, mkb_ref, sb_ref, ob_ref)

    @pl.when(s == pl.num_programs(1) - 1)
    def _():
        sTf_ref[...] = sf_ref[...]
        sTb_ref[...] = sb_ref[...]


def _hgrn_scan(q, gf, gb, v, s0f, s0b, rb):
    b_, r_, _ = q.shape
    nblk = r_ // rb
    mallf, mkf = _hgrn_consts(False)
    mallb, mkb = _hgrn_consts(True)
    rowf = pl.BlockSpec((None, rb, 512), lambda b, s: (b, s, 0))
    rowb = pl.BlockSpec((None, rb, 512), lambda b, s: (b, nblk - 1 - s, 0))
    full = lambda a: pl.BlockSpec(a.shape, lambda b, s: (0,) * a.ndim)
    st = pl.BlockSpec((None, HGRN_HEADS, LANES, HGRN_DK), lambda b, s: (b, 0, 0, 0))
    o_shape = jax.ShapeDtypeStruct((b_, r_, 512), BF16)
    s_shape = jax.ShapeDtypeStruct((b_, HGRN_HEADS, LANES, HGRN_DK), F32)
    return pl.pallas_call(
        functools.partial(_hgrn_kernel, rb // CHUNK),
        grid=(b_, nblk),
        in_specs=[rowf, rowf, rowf, rowb, rowb, rowb, full(mallf), full(mkf), full(mallb), full(mkb), st, st],
        out_specs=[rowf, st, rowb, st],
        out_shape=[o_shape, s_shape, o_shape, s_shape],
        scratch_shapes=[pltpu.VMEM((HGRN_HEADS, LANES, HGRN_DK), F32)] * 2,
        compiler_params=_cparams(("arbitrary", "arbitrary")),
        name="hgrn_scan",
    )(q, gf, v, q, gb, v, mallf, mkf, mallb, mkb, s0f, s0b)


def _rope_tables(length):
    rows = length // GRID_W
    row = jnp.repeat(jnp.arange(rows, dtype=F32), GRID_W)
    col = jnp.tile(jnp.arange(GRID_W, dtype=F32), rows)

    def cs(rot_dim):
        n_freq = rot_dim // 4
        inv_freq = ROPE_BASE ** (-jnp.arange(n_freq, dtype=F32) / n_freq)
        ang = jnp.concatenate([row[:, None] * inv_freq, col[:, None] * inv_freq], axis=-1)
        c, s = jnp.cos(ang), jnp.sin(ang)
        return jnp.concatenate([c, c], axis=-1), jnp.concatenate([-s, s], axis=-1)

    ca, sa = cs(DA_HEAD_DIM)
    cq, sq = cs(MLA_ROPE)
    ones = jnp.ones((length, MLA_NOPE), F32)
    pad1 = jnp.ones((length, LANES - MLA_NOPE - MLA_ROPE), F32)
    cqt = jnp.concatenate([ones, cq, pad1], axis=-1)
    sqt = jnp.concatenate([0.0 * ones, sq, 0.0 * pad1], axis=-1)
    return jnp.tile(ca, (1, 2)), jnp.tile(sa, (1, 2)), cqt, sqt


def _att_weights(att_w_in, mla_q_norm, mla_w_uq, mla_kv_norm, mla_w_ukv):
    d = att_w_in.shape[0]
    kr = att_w_in[:, 2176:2208]
    kr_blk = jnp.concatenate([jnp.zeros((d, MLA_NOPE), F32), kr,
                              jnp.zeros((d, LANES - MLA_NOPE - MLA_ROPE), F32)], axis=1)
    win = jnp.concatenate([att_w_in[:, :1024], att_w_in[:, 1536:2176], kr_blk], axis=1).astype(BF16)
    wva = att_w_in[:, 1024:1536].T.astype(BF16)
    wq = mla_w_uq.reshape(MLA_Q_RANK, MLA_HEADS, MLA_NOPE + MLA_ROPE)
    wq = jnp.pad(wq, ((0, 0), (0, 0), (0, LANES - MLA_NOPE - MLA_ROPE))).reshape(MLA_Q_RANK, MLA_HEADS * LANES)
    wkv = mla_w_ukv.reshape(MLA_KV_RANK, MLA_HEADS, MLA_NOPE + MLA_V)
    wk = jnp.pad(wkv[:, :, :MLA_NOPE], ((0, 0), (0, 0), (0, LANES - MLA_NOPE))).reshape(MLA_KV_RANK, MLA_HEADS * LANES)
    wv = wkv[:, :, MLA_NOPE:].reshape(MLA_KV_RANK, MLA_HEADS * MLA_V)
    return dict(win=win, wva=wva, qn=mla_q_norm.reshape(1, -1), wuq=wq.astype(BF16),
                kvn=mla_kv_norm.reshape(1, -1), wk=wk.astype(BF16), wv=wv.T.astype(BF16))


def _rec_weights(rec_w_in, bound_logits, dt_bias):
    d = rec_w_in.shape[0]
    w = rec_w_in[:, :4096].astype(BF16)
    wdt = rec_w_in[:, 4096:4112]
    pad = jnp.zeros((d, LANES - SSD_HEADS), F32)
    wdt_rows = jnp.concatenate([wdt[:, :SSD_HEADS], pad, wdt[:, SSD_HEADS:], pad], axis=1).astype(BF16)
    zb = jnp.zeros((LANES - SSD_HEADS,), F32)
    dtb = jnp.concatenate([dt_bias[0], zb, dt_bias[1], zb]).reshape(1, 2 * LANES)
    return dict(w=w, wdt=wdt_rows, bl=bound_logits, dtb=dtb)


class _Tiles(NamedTuple):
    proj: int
    ctx: int
    att_q: int
    att_k: int
    mlp0: int
    mlp1: int
    scan: int


def _tiles(length, n_ctx):
    cap = lambda n, full: min(n, full)
    return _Tiles(proj=cap(512, length), ctx=cap(256, n_ctx), att_q=cap(1024, length), att_k=cap(1024, length),
                  mlp0=cap(1024, length), mlp1=cap(512, length), scan=cap(512, length))


def kernel(x, c, ctx, c_ctx, w_mod, b_mod, norm_mix, norm_mlp, w_mlp_in, w_mlp_out, att_w_in, att_lambda, att_subnorm, mla_q_norm, mla_w_uq, mla_kv_norm, mla_w_ukv, att_w_out, rec_w_in, hgrn_bound_logits, hgrn_out_norm, ssd_conv_w, ssd_conv_b, ssd_a_log, ssd_dt_bias, ssd_skip, ssd_norm, rec_w_out, final_norm):
    b_, length, d = x.shape
    n_ctx = ctx.shape[1]
    assert w_mod.shape[0] == 2 and d == D_MODEL

    cc = jnp.zeros((8, d), F32).at[:b_].set(c).at[b_].set(c_ctx)
    mods = _mod_vectors(cc, w_mod, b_mod)
    mods = jnp.pad(mods.reshape(2, 8, 6, d), ((0, 0), (0, 0), (0, 2), (0, 0)))
    mods_lat = [mods[l, :b_] for l in range(2)]
    mods_ctx = [jnp.broadcast_to(mods[l, b_], (b_, 8, d)) for l in range(2)]
    row = lambda v: v.reshape(1, -1)

    t = _tiles(length, n_ctx)
    tr, trc = t.proj, t.ctx
    w0 = _att_weights(att_w_in[0], mla_q_norm[0], mla_w_uq[0], mla_kv_norm[0], mla_w_ukv[0])
    tables = _rope_tables(length)
    qa, ka, va, qm, km, vm = _proj0(x, mods_lat[0], row(norm_mix[0]), w0, tables, tr)
    qa_c, ka_c, va_c, qm_c, km_c, vm_c = _proj0(ctx, mods_ctx[0], row(norm_mix[0]), w0, None, trc)
    lam_init = 0.8 - 0.6 * math.exp(-0.3 * 0)
    da_extra = (att_lambda[0], row(att_subnorm[0]))
    ya = _attention("da", qa, ka_c, va_c, ka, va, da_extra, t.att_q, t.att_k, lam_init)
    ym = _attention("mla", qm, km_c, vm_c, km, vm, None, t.att_q, t.att_k, lam_init)
    ya_c = _attention("da", qa_c, ka_c, va_c, None, None, da_extra, trc, t.att_k, lam_init)
    ym_c = _attention("mla", qm_c, km_c, vm_c, None, None, None, trc, t.att_k, lam_init)
    wo0 = (att_w_out[0].astype(BF16).reshape(2, 512, d),)
    w1 = w_mlp_in.astype(BF16)
    w2 = w_mlp_out.astype(BF16)
    h_lat = _mix_mlp(_merge_attention, (ya, ym), wo0, x, mods_lat[0], row(norm_mlp[0]), w1[0], w2[0], None, t.mlp0)
    h_ctx = _mix_mlp(_merge_attention, (ya_c, ym_c), wo0, ctx, mods_ctx[0], row(norm_mlp[0]), w1[0], w2[0], None,
                     trc)

    w1r = _rec_weights(rec_w_in[0], hgrn_bound_logits, ssd_dt_bias[0])
    cw = ssd_conv_w[0]
    cb = row(ssd_conv_b[0])
    q_c, gf_c, gb_c, i_c, _, _, xs_c, bm_c, cm_c, dt_c, dtt_c = _proj1(h_ctx, mods_ctx[1], row(norm_mix[1]), w1r,
                                                                        cw, cb, trc)
    q_l, gf_l, gb_l, i_l, sg_l, sz_l, xs_l, bm_l, cm_l, dt_l, dtt_l = _proj1(h_lat, mods_lat[1], row(norm_mix[1]),
                                                                             w1r, cw, cb, tr)
    rbc, rbl = t.ctx, t.scan
    zs = jnp.zeros((b_, SSD_STATE, SSD_INNER), F32)
    zh = jnp.zeros((b_, HGRN_HEADS, LANES, HGRN_DK), F32)
    _, hs_f, _, hs_b = _ssd_scan(xs_c, bm_c, cm_c, dt_c, dtt_c, ssd_a_log[0], zs, zs, rbc)
    y_f, _, y_b, _ = _ssd_scan(xs_l, bm_l, cm_l, dt_l, dtt_l, ssd_a_log[0], hs_f, hs_b, rbl)
    _, ss_f, _, ss_b = _hgrn_scan(q_c, gf_c, gb_c, i_c, zh, zh, rbc)
    o_f, _, o_b, _ = _hgrn_scan(q_l, gf_l, gb_l, i_l, ss_f, ss_b, rbl)
    skip = row(jnp.repeat(ssd_skip[0], SSD_HEAD_DIM))
    wro = rec_w_out[0].astype(BF16).reshape(2, 512, d)
    return _mix_mlp(_merge_recurrent, (o_f, o_b, y_f, y_b, sg_l, sz_l, xs_l),
                    (row(hgrn_out_norm[0]), skip, row(ssd_norm[0]), wro),
                    h_lat, mods_lat[1], row(norm_mlp[1]), w1[1], w2[1], row(final_norm), t.mlp1)
```

```python
import functools
import math
from typing import NamedTuple

import numpy as np
import jax
import jax.numpy as jnp
from jax import lax
from jax.experimental import pallas as pl
from jax.experimental.pallas import tpu as pltpu

F32 = jnp.float32
BF16 = jnp.bfloat16

D_MODEL = 1024
GRID_W = 64
DA_HEADS = 4
DA_HEAD_DIM = 64
MLA_HEADS = 8
MLA_NOPE = 64
MLA_ROPE = 32
MLA_V = 64
MLA_Q_RANK = 384
MLA_KV_RANK = 256
HGRN_HEADS = 4
HGRN_DK = 128
SSD_HEADS = 8
SSD_HEAD_DIM = 64
SSD_GROUPS = 2
SSD_STATE = 128
SSD_CONV_W = 5
SSD_INNER = SSD_HEADS * SSD_HEAD_DIM
SSD_CONV_CH = SSD_INNER + 2 * SSD_GROUPS * SSD_STATE
MLP_HIDDEN = 4 * D_MODEL
ROPE_BASE = 10000.0
NORM_EPS = 1e-6
CHUNK = 64
SSD_CHUNK = 256

LANES = 128
HALO = 16
LOG2E = 1.4426950408889634
VMEM_LIMIT = 56 * 1024 * 1024


def _cparams(sem):
    return pltpu.CompilerParams(dimension_semantics=sem, vmem_limit_bytes=VMEM_LIMIT)


def _rms(x):
    return x * lax.rsqrt(jnp.mean(x * x, axis=-1, keepdims=True) + NORM_EPS)


def _modulate(x, g, shift, scale):
    return (_rms(x) * g) * (1.0 + scale) + shift


def _silu(x):
    return x * jax.nn.sigmoid(x)


def _dot_nt(a, b):
    return lax.dot_general(a, b, (((1,), (1,)), ((), ())), preferred_element_type=F32)


def _dot_tn(a, b):
    return lax.dot_general(a, b, (((0,), (0,)), ((), ())), preferred_element_type=F32)


def _split2(x):
    hi = x.astype(BF16)
    lo = (x - hi.astype(F32)).astype(BF16)
    return hi, lo


def _exact_left(m01, x):
    hi, lo = _split2(x)
    return (jnp.dot(m01, hi, preferred_element_type=F32) + jnp.dot(m01, lo, preferred_element_type=F32))


def _exact_right(x, m01):
    hi, lo = _split2(x)
    return (jnp.dot(hi, m01, preferred_element_type=F32) + jnp.dot(lo, m01, preferred_element_type=F32))


def _mod_kernel(c_ref, w_ref, b_ref, o_ref):
    a = _silu(c_ref[...]).astype(BF16)
    o_ref[...] = jnp.dot(a, w_ref[...].astype(BF16), preferred_element_type=F32) + b_ref[...]


def _mod_vectors(cc, w_mod, b_mod):
    depth, d, n = w_mod.shape
    tn = 1024
    return pl.pallas_call(
        _mod_kernel,
        grid=(depth, n // tn),
        in_specs=[
            pl.BlockSpec((8, d), lambda l, j: (0, 0)),
            pl.BlockSpec((None, d, tn), lambda l, j: (l, 0, j)),
            pl.BlockSpec((None, 1, tn), lambda l, j: (l, 0, j)),
        ],
        out_specs=pl.BlockSpec((None, 8, tn), lambda l, j: (l, 0, j)),
        out_shape=jax.ShapeDtypeStruct((depth, 8, n), F32),
        compiler_params=_cparams(("arbitrary", "arbitrary")),
        name="mod_vectors",
    )(cc, w_mod, b_mod.reshape(depth, 1, n))


def _rope_partner(x, half):
    n = x.shape[-1]
    lane = lax.broadcasted_iota(jnp.int32, x.shape, x.ndim - 1)
    up = pltpu.roll(x, n - half, x.ndim - 1)
    dn = pltpu.roll(x, half, x.ndim - 1)
    return jnp.where((lane % (2 * half)) < half, up, dn)


def _tile_lanes(x, n):
    return jnp.concatenate([x] * n, axis=-1)


def _proj0_kernel(use_rope, *refs):
    if use_rope:
        (h_ref, mod_ref, g_ref, win_ref, wva_ref, qn_ref, wuq_ref, kvn_ref, wk_ref, wv_ref,
         ca_ref, sa_ref, cq_ref, sq_ref,
         qa_ref, ka_ref, va_ref, qm_ref, km_ref, vm_ref) = refs
    else:
        (h_ref, mod_ref, g_ref, win_ref, wva_ref, qn_ref, wuq_ref, kvn_ref, wk_ref, wv_ref,
         qa_ref, ka_ref, va_ref, qm_ref, km_ref, vm_ref) = refs
    mod = mod_ref[...]
    ub = _modulate(h_ref[...], g_ref[...], mod[0:1], mod[1:2]).astype(BF16)

    def proj(lo, hi):
        return jnp.dot(ub, win_ref[:, lo:hi], preferred_element_type=F32)

    qa = proj(0, 512) * (DA_HEAD_DIM ** -0.5 * LOG2E)
    ka = proj(512, 1024)
    va_ref[...] = _dot_nt(wva_ref[...], ub).astype(BF16)
    cq = proj(1024, 1408)
    ckv = proj(1408, 1664)
    kr = proj(1664, 1792)
    cqn = (_rms(cq) * qn_ref[...]).astype(BF16)
    qm = jnp.dot(cqn, wuq_ref[...], preferred_element_type=F32) * ((MLA_NOPE + MLA_ROPE) ** -0.5 * LOG2E)
    ckvn = (_rms(ckv) * kvn_ref[...]).astype(BF16)
    kn = jnp.dot(ckvn, wk_ref[...], preferred_element_type=F32)
    vm_ref[...] = _dot_nt(wv_ref[...], ckvn).astype(BF16)
    if use_rope:
        ca = _tile_lanes(ca_ref[...], DA_HEADS)
        sa = _tile_lanes(sa_ref[...], DA_HEADS)
        qa = qa * ca + _rope_partner(qa, DA_HEAD_DIM // 2) * sa
        ka = ka * ca + _rope_partner(ka, DA_HEAD_DIM // 2) * sa
        cq1 = cq_ref[...]
        sq1 = sq_ref[...]
        qm = qm * _tile_lanes(cq1, MLA_HEADS) + _rope_partner(qm, MLA_ROPE // 2) * _tile_lanes(sq1, MLA_HEADS)
        kr = kr * cq1 + _rope_partner(kr, MLA_ROPE // 2) * sq1
    qa_ref[...] = qa.astype(BF16)
    ka_ref[...] = ka.astype(BF16)
    qm_ref[...] = qm.astype(BF16)
    km_ref[...] = (kn + _tile_lanes(kr, MLA_HEADS)).astype(BF16)


def _proj0(h, mods, gain, w, tables, tr):
    g_, r_, d = h.shape
    use_rope = tables is not None
    row = lambda c: pl.BlockSpec((None, tr, c), lambda b, t: (b, t, 0))
    full = lambda a: pl.BlockSpec(a.shape, lambda b, t: (0,) * a.ndim)
    in_specs = [row(d), pl.BlockSpec((None, 8, d), lambda b, t: (b, 0, 0)), full(gain),
                full(w["win"]), full(w["wva"]), full(w["qn"]), full(w["wuq"]), full(w["kvn"]), full(w["wk"]),
                full(w["wv"])]
    args = [h, mods, gain, w["win"], w["wva"], w["qn"], w["wuq"], w["kvn"], w["wk"], w["wv"]]
    if use_rope:
        in_specs += [pl.BlockSpec((tr, LANES), lambda b, t: (t, 0))] * 4
        args += list(tables)
    col = pl.BlockSpec((None, 512, tr), lambda b, t: (b, 0, t))
    widths = (512, 512, None, 1024, 1024, None)
    return pl.pallas_call(
        functools.partial(_proj0_kernel, use_rope),
        grid=(g_, r_ // tr),
        in_specs=in_specs,
        out_specs=[col if c is None else row(c) for c in widths],
        out_shape=[jax.ShapeDtypeStruct((g_, 512, r_) if c is None else (g_, r_, c), BF16) for c in widths],
        compiler_params=_cparams(("arbitrary", "arbitrary")),
        name="proj0_rope" if use_rope else "proj0_ctx",
    )(*args)


def _attn_kernel(mode, has_lat, tk, nk, lam_init, *refs):
    refs = list(refs)
    q_ref, kc_ref, vc_ref = refs[:3]
    refs = refs[3:]
    if has_lat:
        kl_ref, vl_ref = refs[:2]
        refs = refs[2:]
    if mode == "da":
        lam_ref, sub_ref = refs[:2]
        refs = refs[2:]
    o_ref, ma_ref, la_ref, acca_ref, mb_ref, lb_ref, accb_ref = refs[:7]
    if has_lat:
        s_ref, mx_ref = refs[7:]

    q = q_ref[...]
    if mode == "da":
        lane = lax.broadcasted_iota(jnp.int32, q.shape, 1)
        zero = jnp.zeros_like(q)
        qs = (jnp.where(lane < DA_HEAD_DIM, q, zero), jnp.where(lane >= DA_HEAD_DIM, q, zero))
    else:
        qs = (q[:, :LANES], q[:, LANES:])
    stats = ((ma_ref, la_ref, acca_ref), (mb_ref, lb_ref, accb_ref))
    for m_ref, l_ref, acc_ref in stats:
        m_ref[...] = jnp.full(m_ref.shape, -jnp.inf, F32)
        l_ref[...] = jnp.zeros(l_ref.shape, F32)
        acc_ref[...] = jnp.zeros(acc_ref.shape, F32)

    def scores(k):
        ks = (k, k) if mode == "da" else (k[:, :LANES], k[:, LANES:])
        return [_dot_nt(ks[i], qs[i]) for i in range(2)]

    def softmax(i, s):
        m_ref, l_ref, _ = stats[i]
        m_prev = m_ref[...]
        m_new = jnp.maximum(m_prev, jnp.max(s, axis=0, keepdims=True))
        alpha = jnp.exp2(m_prev - m_new)
        p = jnp.exp2(s - m_new)
        l_ref[...] = alpha * l_ref[...] + jnp.sum(p, axis=0, keepdims=True)
        m_ref[...] = m_new
        return p.astype(BF16), alpha

    def accumulate(i, vt, p, alpha):
        acc_ref = stats[i][2]
        acc_ref[...] = alpha * acc_ref[...] + jnp.dot(vt, p, preferred_element_type=F32)

    if not has_lat:
        ss = scores(kc_ref[...])
        pa = [softmax(i, ss[i]) for i in range(2)]
        for i in range(2):
            accumulate(i, vc_ref[...], *pa[i])
    else:
        nc = kc_ref.shape[0]

        def chunk(c):
            return pl.ds(c * tk if isinstance(c, int) else pl.multiple_of(c * tk, tk), tk)

        def keys(st):
            return kc_ref[...] if isinstance(st, int) and st == 0 else kl_ref[chunk(st - 1), :]

        def vals(st):
            return vc_ref[...] if isinstance(st, int) and st == 0 else vl_ref[:, chunk(st - 1)]

        def lookahead(st, slot, n):
            ss = scores(keys(st))
            for i in range(2):
                s_ref[slot, i, 0:n, :] = ss[i]
                mx_ref[1, i] = jnp.maximum(mx_ref[0, i], jnp.max(ss[i], axis=0, keepdims=True))

        def consume(st, slot, n):
            vt = vals(st)
            for i in range(2):
                m_ref, l_ref, acc_ref = stats[i]
                m_cur = mx_ref[0, i]
                alpha = jnp.exp2(m_ref[...] - m_cur)
                p = jnp.exp2(s_ref[slot, i, 0:n, :] - m_cur)
                l_ref[...] = alpha * l_ref[...] + jnp.sum(p, axis=0, keepdims=True)
                acc_ref[...] = alpha * acc_ref[...] + jnp.dot(vt, p.astype(BF16), preferred_element_type=F32)
                m_ref[...] = m_cur

        def advance():
            for i in range(2):
                mx_ref[0, i] = mx_ref[1, i]

        def run(st, slot, has_next):
            n = nc if isinstance(st, int) and st == 0 else tk
            if has_next:
                lookahead(st + 1, 1 - slot, tk)
            consume(st, slot, n)
            if has_next:
                advance()

        for i in range(2):
            mx_ref[0, i] = jnp.full((1, mx_ref.shape[-1]), -jnp.inf, F32)
        lookahead(0, 0, nc)
        advance()
        run(0, 0, True)
        npair = (nk - 1) // 2

        def body(j, carry):
            run(2 * j + 1, 1, True)
            run(2 * j + 2, 0, True)
            return carry

        lax.fori_loop(0, npair, body, 0)
        for st in range(2 * npair + 1, nk + 1):
            run(st, st % 2, st < nk)

    oa = acca_ref[...] / la_ref[...]
    ob = accb_ref[...] / lb_ref[...]
    if mode == "da":
        lp = lam_ref[...]
        lam = (jnp.exp(jnp.sum(lp[0:1] * lp[1:2], axis=1, keepdims=True))
               - jnp.exp(jnp.sum(lp[2:3] * lp[3:4], axis=1, keepdims=True)) + lam_init)
        dlt = (oa - lam * ob).T
        o_ref[...] = ((_rms(dlt) * sub_ref[...]) * (1.0 - lam_init)).astype(o_ref.dtype)
    else:
        chan = lax.broadcasted_iota(jnp.int32, oa.shape, 0)
        o_ref[...] = jnp.where(chan < MLA_V, oa, ob).T.astype(o_ref.dtype)


def _attention(mode, q, kc, vc, kl, vl, extra, tq, tk, lam_init):
    b_, sq, _ = q.shape
    has_lat = kl is not None
    w = LANES if mode == "da" else 2 * LANES
    nh = 4
    kblk = lambda a: pl.BlockSpec((None, a.shape[1], w), lambda b, h, t: (b, 0, h))
    vblk = lambda a: pl.BlockSpec((None, LANES, a.shape[2]), lambda b, h, t: (b, h, 0))
    in_specs = [pl.BlockSpec((None, tq, w), lambda b, h, t: (b, t, h)), kblk(kc), vblk(vc)]
    args = [q, kc, vc]
    nk = 0
    if has_lat:
        in_specs += [kblk(kl), vblk(vl)]
        args += [kl, vl]
        nk = kl.shape[1] // tk
    if mode == "da":
        lam_p, subnorm = extra
        in_specs += [pl.BlockSpec(lam_p.shape, lambda b, h, t: (0, 0)),
                     pl.BlockSpec(subnorm.shape, lambda b, h, t: (0, 0))]
        args += [lam_p, subnorm]
    scratch = []
    for _ in range(2):
        scratch += [pltpu.VMEM((1, tq), F32), pltpu.VMEM((1, tq), F32), pltpu.VMEM((LANES, tq), F32)]
    if has_lat:
        scratch += [pltpu.VMEM((2, 2, tk, tq), F32), pltpu.VMEM((2, 2, 1, tq), F32)]
    return pl.pallas_call(
        functools.partial(_attn_kernel, mode, has_lat, tk, nk, lam_init),
        grid=(b_, nh, sq // tq),
        in_specs=in_specs,
        out_specs=pl.BlockSpec((None, tq, LANES), lambda b, h, t: (b, t, h)),
        out_shape=jax.ShapeDtypeStruct((b_, sq, nh * LANES), BF16),
        scratch_shapes=scratch,
        compiler_params=_cparams(("arbitrary", "arbitrary", "arbitrary")),
        name=f"attn_{mode}_{'lat' if has_lat else 'ctx'}",
    )(*args)


def _merge_attention(ya_ref, ym_ref, w_ref):
    return (jnp.dot(ya_ref[...], w_ref[0], preferred_element_type=F32)
            + jnp.dot(ym_ref[...], w_ref[1], preferred_element_type=F32))


def _merge_recurrent(of_ref, ob_ref, yf_ref, yb_ref, sg_ref, sz_ref, xs_ref, on_ref, sk_ref, sn_ref, w_ref):
    o = of_ref[...].astype(F32) + ob_ref[...].astype(F32)
    on = on_ref[...]
    parts = []
    for hh in range(HGRN_HEADS):
        ls = slice(hh * LANES, (hh + 1) * LANES)
        parts.append(_rms(o[:, ls]) * on[:, ls])
    o = jnp.concatenate(parts, axis=1) * sg_ref[...].astype(F32)
    y = ((yf_ref[...].astype(F32) + yb_ref[...].astype(F32) + sk_ref[...] * xs_ref[...])
         * sz_ref[...].astype(F32))
    sn = sn_ref[...]
    gw = SSD_INNER // SSD_GROUPS
    parts = []
    for gg in range(SSD_GROUPS):
        ls = slice(gg * gw, (gg + 1) * gw)
        parts.append(_rms(y[:, ls]) * sn[:, ls])
    y = jnp.concatenate(parts, axis=1)
    return (jnp.dot(o.astype(BF16), w_ref[0], preferred_element_type=F32)
            + jnp.dot(y.astype(BF16), w_ref[1], preferred_element_type=F32))


def _mix_mlp_kernel(merge, n_mix, final, *refs):
    mix_refs = refs[:n_mix]
    if final:
        h_ref, mod_ref, g_ref, w1_ref, w2_ref, fg_ref, o_ref = refs[n_mix:]
    else:
        h_ref, mod_ref, g_ref, w1_ref, w2_ref, o_ref = refs[n_mix:]
    h1 = h_ref[...] + mod_ref[2:3, :] * merge(*mix_refs)
    u = _modulate(h1, g_ref[...], mod_ref[3:4, :], mod_ref[4:5, :]).astype(BF16)
    a = jnp.dot(u, w1_ref[...], preferred_element_type=F32)
    a = jnp.square(jnp.maximum(a, 0.0)).astype(BF16)
    out = h1 + mod_ref[5:6, :] * jnp.dot(a, w2_ref[...], preferred_element_type=F32)
    if final:
        out = _rms(out) * fg_ref[...]
    o_ref[...] = out


def _mix_mlp(merge, mix_rows, mix_full, h, mods, gain, w1, w2, final_gain, tr):
    g_, r_, d = h.shape
    final = final_gain is not None
    row = lambda c: pl.BlockSpec((None, tr, c), lambda b, t: (b, t, 0))
    full = lambda a: pl.BlockSpec(a.shape, lambda b, t: (0,) * a.ndim)
    resident = lambda a: pl.BlockSpec(a.shape, lambda b, t: (0,) * a.ndim, pipeline_mode=pl.Buffered(1))
    in_specs = ([row(a.shape[-1]) for a in mix_rows] + [full(a) for a in mix_full]
                + [row(d), pl.BlockSpec((None, 8, d), lambda b, t: (b, 0, 0)), full(gain),
                   resident(w1), resident(w2)])
    args = list(mix_rows) + list(mix_full) + [h, mods, gain, w1, w2]
    if final:
        in_specs.append(full(final_gain))
        args.append(final_gain)
    return pl.pallas_call(
        functools.partial(_mix_mlp_kernel, merge, len(mix_rows) + len(mix_full), final),
        grid=(g_, r_ // tr),
        in_specs=in_specs,
        out_specs=row(d),
        out_shape=jax.ShapeDtypeStruct(h.shape, F32),
        compiler_params=_cparams(("arbitrary", "arbitrary")),
        name="mix_mlp_final" if final else "mix_mlp",
    )(*args)


def _softplus(x):
    return jnp.maximum(x, 0.0) + jnp.log1p(jnp.exp(-jnp.abs(x)))


def _proj1_kernel(nt, h_ref, hp_ref, hn_ref, mod_ref, g_ref, w_ref, wdt_ref, bl_ref, dtb_ref, cw_ref, cb_ref,
                  q_ref, gf_ref, gb_ref, i_ref, sg_ref, sz_ref, xs_ref, bm_ref, cm_ref, dt_ref, dtt_ref, pad_ref):
    t = pl.program_id(1)
    tr = h_ref.shape[0]
    mod = mod_ref[...]
    gain = g_ref[...]
    ub = _modulate(h_ref[...], gain, mod[0:1], mod[1:2]).astype(BF16)
    u_ext = jnp.concatenate([_modulate(hp_ref[...], gain, mod[0:1], mod[1:2]).astype(BF16), ub,
                             _modulate(hn_ref[...], gain, mod[0:1], mod[1:2]).astype(BF16)], axis=0)
    xbc = jnp.dot(u_ext, w_ref[:, 3072:4096], preferred_element_type=F32)
    ri = lax.broadcasted_iota(jnp.int32, (tr + 2 * HALO, 1), 0)
    inside = ((ri >= HALO) | (t > 0)) & ((ri < tr + HALO) | (t < nt - 1))
    pad_ref[...] = jnp.where(inside, xbc, 0.0)
    half = SSD_CONV_W // 2
    y = jnp.zeros((tr, xbc.shape[1]), F32) + cb_ref[...]
    for j in range(SSD_CONV_W):
        y = y + pad_ref[HALO - half + j:HALO - half + j + tr, :] * cw_ref[j:j + 1, :]
    y = _silu(y)
    xs_ref[...] = y[:, :SSD_INNER]
    bm_ref[...] = y[:, SSD_INNER:SSD_INNER + 256].astype(BF16)
    cm_ref[...] = y[:, SSD_INNER + 256:].astype(BF16)

    def proj(lo, hi):
        return jnp.dot(ub, w_ref[:, lo:hi], preferred_element_type=F32)

    bl = bl_ref[...]
    e = jnp.exp(bl - jnp.max(bl, axis=0, keepdims=True))
    gamma = e / jnp.sum(e, axis=0, keepdims=True)
    lb = (gamma[0:1] + gamma[1:2]) - gamma[0:1]
    q_ref[...] = _silu(proj(0, 512)).astype(BF16)
    for k, out in ((0, gf_ref), (1, gb_ref)):
        lbk = lb[:, 512 * k:512 * (k + 1)]
        f = lbk + (1.0 - lbk) * jax.nn.sigmoid(proj(512 * (k + 1), 512 * (k + 2)))
        out[...] = jnp.log(f)
    i_ref[...] = proj(1536, 2048).astype(BF16)
    sg_ref[...] = _silu(proj(2048, 2560)).astype(BF16)
    sz_ref[...] = _silu(proj(2560, 3072)).astype(BF16)
    dt =_softplus(jnp.dot(ub, wdt_ref[...], preferred_element_type=F32) + dtb_ref[...])
    dt_ref[...] = dt
    dtt_ref[...] = jnp.concatenate([dt[:, :LANES].T[:SSD_HEADS], dt[:, LANES:].T[:SSD_HEADS]], axis=0)


def _proj1(h, mods, gain, w, conv_w, conv_b, tr):
    g_, r_, d = h.shape
    nt = r_ // tr
    nb = tr // HALO
    row = lambda c: pl.BlockSpec((None, tr, c), lambda b, t: (b, t, 0))
    full = lambda a: pl.BlockSpec(a.shape, lambda b, t: (0,) * a.ndim)
    names = ("w", "wdt", "bl", "dtb")
    widths = (512, 512, 512, 512, 512, 512, 512, 256, 256, 2 * LANES)
    dts = (BF16, F32, F32, BF16, BF16, BF16, F32, BF16, BF16, F32)
    return pl.pallas_call(
        functools.partial(_proj1_kernel, nt),
        grid=(g_, nt),
        in_specs=[row(d),
                  pl.BlockSpec((None, HALO, d), lambda b, t: (b, jnp.maximum(t * nb - 1, 0), 0)),
                  pl.BlockSpec((None, HALO, d), lambda b, t: (b, jnp.minimum((t + 1) * nb, nt * nb - 1), 0)),
                  pl.BlockSpec((None, 8, d), lambda b, t: (b, 0, 0)), full(gain)]
        + [full(w[n]) for n in names] + [full(conv_w), full(conv_b)],
        out_specs=[row(c) for c in widths] + [pl.BlockSpec((None, 16, tr), lambda b, t: (b, 0, t))],
        out_shape=[jax.ShapeDtypeStruct((g_, r_, c), t_) for c, t_ in zip(widths, dts)]
        + [jax.ShapeDtypeStruct((g_, 16, r_), F32)],
        scratch_shapes=[pltpu.VMEM((tr + 2 * HALO, SSD_CONV_CH), F32)],
        compiler_params=_cparams(("arbitrary", "arbitrary")),
        name="proj1",
    )(h, h, h, mods, gain, *[w[n] for n in names], conv_w, conv_b)


def _tri(rev, n):
    i = np.arange(n)
    m = i[None, :] >= i[:, None] if rev else i[None, :] <= i[:, None]
    return m


def _ssd_consts(rev, nchunk):
    tri = np.kron(np.eye(nchunk), _tri(rev, SSD_CHUNK).astype(np.float32))
    expand = np.zeros((LANES, SSD_INNER), np.float32)
    for h in range(SSD_HEADS):
        expand[h, h * SSD_HEAD_DIM:(h + 1) * SSD_HEAD_DIM] = 1.0
    return jnp.asarray(tri, BF16), jnp.asarray(tri.T.copy(), BF16), jnp.asarray(expand, BF16)


def _ssd_block_decays(rev, dt_ref, dtt_ref, an, ant, tri, trit):
    dir_ = 1 if rev else 0
    dt = dt_ref[:, dir_ * LANES:(dir_ + 1) * LANES]
    dtt = dtt_ref[8 * dir_:8 * dir_ + 8, :]
    acol = _exact_left(tri, dt * an)
    arow = _exact_right(dtt * ant, trit)
    return dt, acol, arow


def _expand_heads(vals, ex):
    rows = vals[0].shape[0]
    parts = [p for v in vals for p in _split2(v)]
    big = jnp.dot(jnp.concatenate(parts, axis=0), ex, preferred_element_type=F32)
    return [big[2 * i * rows:(2 * i + 1) * rows] + big[(2 * i + 1) * rows:(2 * i + 2) * rows]
            for i in range(len(vals))]


def _ssd_chunk(rev, r0, x_ref, bm_ref, cm_ref, decays, h_ref, y_ref):
    n = SSD_CHUNK
    ti = lax.broadcasted_iota(jnp.int32, (n, n), 0)
    si = lax.broadcasted_iota(jnp.int32, (n, n), 1)
    mask = (si >= ti) if rev else (si <= ti)
    lane = lax.broadcasted_iota(jnp.int32, (n, LANES), 1)
    last = 0 if rev else n - 1
    x = x_ref[r0:r0 + n, :]
    bm = bm_ref[r0:r0 + n, :]
    cm = cm_ref[r0:r0 + n, :]
    acol = decays[0][r0:r0 + n, :]
    arow = decays[1][:, r0:r0 + n]
    dte = decays[2][r0:r0 + n, :]
    ae = decays[3][r0:r0 + n, :]
    ae_last = ae[last:last + 1, :]
    xdt = x * dte
    xw = (xdt * jnp.exp(ae_last - ae)).astype(BF16)
    xdtb = xdt.astype(BF16)
    ys = []
    for g in range(SSD_GROUPS):
        bg = bm[:, g * SSD_STATE:(g + 1) * SSD_STATE]
        cg = cm[:, g * SSD_STATE:(g + 1) * SSD_STATE]
        cb = _dot_nt(cg, bg)
        hs = h_ref[:, g * 256:(g + 1) * 256]
        yoff = jnp.dot(cg, hs.astype(BF16), preferred_element_type=F32) * jnp.exp(ae[:, g * 256:(g + 1) * 256])
        for pr in range(2):
            xp = xdtb[:, g * 256 + pr * LANES:g * 256 + (pr + 1) * LANES]
            res = []
            for k in range(2):
                hd = g * 4 + pr * 2 + k
                dmat = acol[:, hd:hd + 1] - arow[hd:hd + 1, :]
                lmat = jnp.exp(jnp.where(mask, dmat, -jnp.inf))
                res.append(jnp.dot((cb * lmat).astype(BF16), xp, preferred_element_type=F32))
            ys.append(jnp.where(lane < SSD_HEAD_DIM, res[0], res[1]) + yoff[:, pr * LANES:(pr + 1) * LANES])
        h_ref[:, g * 256:(g + 1) * 256] = (hs * jnp.exp(ae_last[:, g * 256:(g + 1) * 256])
                                           + _dot_tn(bg, xw[:, g * 256:(g + 1) * 256]))
    y_ref[r0:r0 + n, :] = jnp.concatenate(ys, axis=1).astype(y_ref.dtype)


def _ssd_kernel(nchunk, xf_ref, bmf_ref, cmf_ref, dtf_ref, dttf_ref, xb_ref, bmb_ref, cmb_ref, dtb_ref, dttb_ref,
                an_ref, ant_ref, tri_ref, trit_ref, ex_ref, h0f_ref, h0b_ref,
                yf_ref, hTf_ref, yb_ref, hTb_ref, hf_ref, hb_ref):
    s = pl.program_id(1)

    @pl.when(s == 0)
    def _():
        hf_ref[...] = h0f_ref[...]
        hb_ref[...] = h0b_ref[...]

    ex = ex_ref[...]
    dt_f, acol_f, arow_f = _ssd_block_decays(False, dtf_ref, dttf_ref, an_ref[0], ant_ref[0], tri_ref[0], trit_ref[0])
    dt_b, acol_b, arow_b = _ssd_block_decays(True, dtb_ref, dttb_ref, an_ref[1], ant_ref[1], tri_ref[1], trit_ref[1])
    dte_f, ae_f, dte_b, ae_b = _expand_heads([dt_f, acol_f, dt_b, acol_b], ex)
    dec_f = (acol_f, arow_f, dte_f, ae_f)
    dec_b = (acol_b, arow_b, dte_b, ae_b)
    for j in range(nchunk):
        _ssd_chunk(False, j * SSD_CHUNK, xf_ref, bmf_ref, cmf_ref, dec_f, hf_ref, yf_ref)
        _ssd_chunk(True, (nchunk - 1 - j) * SSD_CHUNK, xb_ref, bmb_ref, cmb_ref, dec_b, hb_ref, yb_ref)

    @pl.when(s == pl.num_programs(1) - 1)
    def _():
        hTf_ref[...] = hf_ref[...]
        hTb_ref[...] = hb_ref[...]


def _ssd_scan(xs, bm, cm, dt, dtt, a_log, h0f, h0b, rb):
    b_, r_, _ = xs.shape
    nblk = r_ // rb
    trif, tritf, ex = _ssd_consts(False, rb // SSD_CHUNK)
    trib, tritb, _ = _ssd_consts(True, rb // SSD_CHUNK)
    tri = jnp.stack([trif, trib])
    trit = jnp.stack([tritf, tritb])
    a_neg = -jnp.exp(a_log.astype(F32))
    an = jnp.zeros((2, 1, LANES), F32).at[:, 0, :SSD_HEADS].set(a_neg)
    ant = a_neg.reshape(2, SSD_HEADS, 1)
    rowf = lambda c: pl.BlockSpec((None, rb, c), lambda b, s: (b, s, 0))
    rowb = lambda c: pl.BlockSpec((None, rb, c), lambda b, s: (b, nblk - 1 - s, 0))
    full = lambda a: pl.BlockSpec(a.shape, lambda b, s: (0,) * a.ndim)
    st = pl.BlockSpec((None, SSD_STATE, SSD_INNER), lambda b, s: (b, 0, 0))
    y_shape = jax.ShapeDtypeStruct((b_, r_, SSD_INNER), BF16)
    h_shape = jax.ShapeDtypeStruct((b_, SSD_STATE, SSD_INNER), F32)
    return pl.pallas_call(
        functools.partial(_ssd_kernel, rb // SSD_CHUNK),
        grid=(b_, nblk),
        in_specs=[rowf(512), rowf(256), rowf(256), rowf(2 * LANES),
                  pl.BlockSpec((None, 16, rb), lambda b, s: (b, 0, s)),
                  rowb(512), rowb(256), rowb(256), rowb(2 * LANES),
                  pl.BlockSpec((None, 16, rb), lambda b, s: (b, 0, nblk - 1 - s)),
                  full(an), full(ant), full(tri), full(trit), full(ex), st, st],
        out_specs=[rowf(512), st, rowb(512), st],
        out_shape=[y_shape, h_shape, y_shape, h_shape],
        scratch_shapes=[pltpu.VMEM((SSD_STATE, SSD_INNER), F32)] * 2,
        compiler_params=_cparams(("arbitrary", "arbitrary")),
        name="ssd_scan",
    )(xs, bm, cm, dt, dtt, xs, bm, cm, dt, dtt, an, ant, tri, trit, ex, h0f, h0b)


_HGRN_LEVELS = (64, 32, 16, 8, 4, 2, 1)


def _hgrn_consts(rev):
    i = np.arange(CHUNK)
    before = _tri(rev, CHUNK)
    after_strict = ~before
    mats, masks = [], []
    for c in _HGRN_LEVELS:
        same = (i[:, None] // c) == (i[None, :] // c)
        q_side = same & before
        k_side = same & after_strict
        if c == CHUNK:
            mats += [q_side, k_side]
        else:
            blk = i // c
            first = (blk % 2 == 1) if rev else (blk % 2 == 0)
            if c > 1:
                mats.append(np.where(first[:, None], k_side, q_side))
            pair = first[None, :] & (~first)[:, None] & ((blk[:, None] // 2) == (blk[None, :] // 2))
            masks.append(pair)
    masks.append(np.eye(CHUNK, dtype=bool))
    m = np.concatenate(mats, axis=0).astype(np.float32)
    m = np.concatenate([m, m], axis=1)
    mk = np.stack(masks, axis=0).astype(np.float32)
    return jnp.asarray(m, BF16), jnp.asarray(mk, F32)


def _hgrn_chunk(rev, r0, q_ref, g_ref, v_ref, mall, mk_ref, s_ref, o_ref):
    nl = len(_HGRN_LEVELS)
    last = 0 if rev else CHUNK - 1
    g = g_ref[r0:r0 + CHUNK, :]
    g2 = jnp.concatenate(_split2(g), axis=0)
    eall = jnp.exp(jnp.dot(mall, g2, preferred_element_type=F32))
    ff = jnp.exp(g)
    kk = 1.0 - ff
    qq = q_ref[r0:r0 + CHUNK, :].astype(F32)
    vv = v_ref[r0:r0 + CHUNK, :]
    outs = []
    for h in range(HGRN_HEADS):
        ls = slice(h * LANES, (h + 1) * LANES)
        qh, kh, vh = qq[:, ls], kk[:, ls], vv[:, ls]
        khb = kh.astype(BF16)

        def fac(idx, ls=ls):
            return eall[idx * CHUNK:(idx + 1) * CHUNK, ls]

        q01 = jnp.concatenate([qh, qh * ff[:, ls]], axis=0).astype(BF16)
        a01 = _dot_nt(q01, khb)
        att = a01[:CHUNK] * mk_ref[nl - 1] + a01[CHUNK:] * mk_ref[nl - 2]
        for l in range(1, nl - 1):
            e = fac(l + 1)
            att = att + _dot_nt((qh * e).astype(BF16), (kh * e).astype(BF16)) * mk_ref[l - 1]
        st = s_ref[h]
        qs_ = (qh * fac(0)).astype(BF16)
        o = _dot_nt(qs_, st.astype(BF16)) + jnp.dot(att.astype(BF16), vh, preferred_element_type=F32)
        outs.append(o)
        ks_ = (kh * fac(1)).astype(BF16)
        dec = fac(0)[last:last + 1, :]
        s_ref[h] = st * dec + _dot_tn(vh, ks_)
    o_ref[r0:r0 + CHUNK, :] = jnp.concatenate(outs, axis=1).astype(o_ref.dtype)


def _hgrn_kernel(nchunk, qf_ref, gf_ref, vf_ref, qb_ref, gb_ref, vb_ref, mf_ref, mkf_ref, mb_ref, mkb_ref,
                 s0f_ref, s0b_ref, of_ref, sTf_ref, ob_ref, sTb_ref, sf_ref, sb_ref):
    s = pl.program_id(1)

    @pl.when(s == 0)
    def _():
        sf_ref[...] = s0f_ref[...]
        sb_ref[...] = s0b_ref[...]

    mallf = mf_ref[...]
    mallb = mb_ref[...]
    for j in range(nchunk):
        _hgrn_chunk(False, j * CHUNK, qf_ref, gf_ref, vf_ref, mallf, mkf_ref, sf_ref, of_ref)
        _hgrn_chunk(True, (nchunk - 1 - j) * CHUNK, qb_ref, gb_ref, vb_ref, mallb, mkb_ref, sb_ref, ob_ref)

    @pl.when(s == pl.num_programs(1) - 1)
    def _():
        sTf_ref[...] = sf_ref[...]
        sTb_ref[...] = sb_ref[...]


def _hgrn_scan(q, gf, gb, v, s0f, s0b, rb):
    b_, r_, _ = q.shape
    nblk = r_ // rb
    mallf, mkf = _hgrn_consts(False)
    mallb, mkb = _hgrn_consts(True)
    rowf = pl.BlockSpec((None, rb, 512), lambda b, s: (b, s, 0))
    rowb = pl.BlockSpec((None, rb, 512), lambda b, s: (b, nblk - 1 - s, 0))
    full = lambda a: pl.BlockSpec(a.shape, lambda b, s: (0,) * a.ndim)
    st = pl.BlockSpec((None, HGRN_HEADS, LANES, HGRN_DK), lambda b, s: (b, 0, 0, 0))
    o_shape = jax.ShapeDtypeStruct((b_, r_, 512), BF16)
    s_shape = jax.ShapeDtypeStruct((b_, HGRN_HEADS, LANES, HGRN_DK), F32)
    return pl.pallas_call(
        functools.partial(_hgrn_kernel, rb // CHUNK),
        grid=(b_, nblk),
        in_specs=[rowf, rowf, rowf, rowb, rowb, rowb, full(mallf), full(mkf), full(mallb), full(mkb), st, st],
        out_specs=[rowf, st, rowb, st],
        out_shape=[o_shape, s_shape, o_shape, s_shape],
        scratch_shapes=[pltpu.VMEM((HGRN_HEADS, LANES, HGRN_DK), F32)] * 2,
        compiler_params=_cparams(("arbitrary", "arbitrary")),
        name="hgrn_scan",
    )(q, gf, v, q, gb, v, mallf, mkf, mallb, mkb, s0f, s0b)


def _rope_tables(length):
    rows = length // GRID_W
    row = jnp.repeat(jnp.arange(rows, dtype=F32), GRID_W)
    col = jnp.tile(jnp.arange(GRID_W, dtype=F32), rows)

    def cs(rot_dim):
        n_freq = rot_dim // 4
        inv_freq = ROPE_BASE ** (-jnp.arange(n_freq, dtype=F32) / n_freq)
        ang = jnp.concatenate([row[:, None] * inv_freq, col[:, None] * inv_freq], axis=-1)
        c, s = jnp.cos(ang), jnp.sin(ang)
        return jnp.concatenate([c, c], axis=-1), jnp.concatenate([-s, s], axis=-1)

    ca, sa = cs(DA_HEAD_DIM)
    cq, sq = cs(MLA_ROPE)
    ones = jnp.ones((length, MLA_NOPE), F32)
    pad1 = jnp.ones((length, LANES - MLA_NOPE - MLA_ROPE), F32)
    cqt = jnp.concatenate([ones, cq, pad1], axis=-1)
    sqt = jnp.concatenate([0.0 * ones, sq, 0.0 * pad1], axis=-1)
    return jnp.tile(ca, (1, 2)), jnp.tile(sa, (1, 2)), cqt, sqt


def _att_weights(att_w_in, mla_q_norm, mla_w_uq, mla_kv_norm, mla_w_ukv):
    d = att_w_in.shape[0]
    kr = att_w_in[:, 2176:2208]
    kr_blk = jnp.concatenate([jnp.zeros((d, MLA_NOPE), F32), kr,
                              jnp.zeros((d, LANES - MLA_NOPE - MLA_ROPE), F32)], axis=1)
    win = jnp.concatenate([att_w_in[:, :1024], att_w_in[:, 1536:2176], kr_blk], axis=1).astype(BF16)
    wva = att_w_in[:, 1024:1536].T.astype(BF16)
    wq = mla_w_uq.reshape(MLA_Q_RANK, MLA_HEADS, MLA_NOPE + MLA_ROPE)
    wq = jnp.pad(wq, ((0, 0), (0, 0), (0, LANES - MLA_NOPE - MLA_ROPE))).reshape(MLA_Q_RANK, MLA_HEADS * LANES)
    wkv = mla_w_ukv.reshape(MLA_KV_RANK, MLA_HEADS, MLA_NOPE + MLA_V)
    wk = jnp.pad(wkv[:, :, :MLA_NOPE], ((0, 0), (0, 0), (0, LANES - MLA_NOPE))).reshape(MLA_KV_RANK, MLA_HEADS * LANES)
    wv = wkv[:, :, MLA_NOPE:].reshape(MLA_KV_RANK, MLA_HEADS * MLA_V)
    return dict(win=win, wva=wva, qn=mla_q_norm.reshape(1, -1), wuq=wq.astype(BF16),
                kvn=mla_kv_norm.reshape(1, -1), wk=wk.astype(BF16), wv=wv.T.astype(BF16))


def _rec_weights(rec_w_in, bound_logits, dt_bias):
    d = rec_w_in.shape[0]
    w = rec_w_in[:, :4096].astype(BF16)
    wdt = rec_w_in[:, 4096:4112]
    pad = jnp.zeros((d, LANES - SSD_HEADS), F32)
    wdt_rows = jnp.concatenate([wdt[:, :SSD_HEADS], pad, wdt[:, SSD_HEADS:], pad], axis=1).astype(BF16)
    zb = jnp.zeros((LANES - SSD_HEADS,), F32)
    dtb = jnp.concatenate([dt_bias[0], zb, dt_bias[1], zb]).reshape(1, 2 * LANES)
    return dict(w=w, wdt=wdt_rows, bl=bound_logits, dtb=dtb)


class _Tiles(NamedTuple):
    proj: int
    ctx: int
    att_q: int
    att_k: int
    mlp0: int
    mlp1: int
    scan: int


def _tiles(length, n_ctx):
    cap = lambda n, full: min(n, full)
    return _Tiles(proj=cap(512, length), ctx=cap(256, n_ctx), att_q=cap(1024, length), att_k=cap(1024, length),
                  mlp0=cap(1024, length), mlp1=cap(512, length), scan=cap(512, length))


def kernel(x, c, ctx, c_ctx, w_mod, b_mod, norm_mix, norm_mlp, w_mlp_in, w_mlp_out, att_w_in, att_lambda, att_subnorm, mla_q_norm, mla_w_uq, mla_kv_norm, mla_w_ukv, att_w_out, rec_w_in, hgrn_bound_logits, hgrn_out_norm, ssd_conv_w, ssd_conv_b, ssd_a_log, ssd_dt_bias, ssd_skip, ssd_norm, rec_w_out, final_norm):
    b_, length, d = x.shape
    n_ctx = ctx.shape[1]
    assert w_mod.shape[0] == 2 and d == D_MODEL

    cc = jnp.zeros((8, d), F32).at[:b_].set(c).at[b_].set(c_ctx)
    mods = _mod_vectors(cc, w_mod, b_mod)
    mods = jnp.pad(mods.reshape(2, 8, 6, d), ((0, 0), (0, 0), (0, 2), (0, 0)))
    mods_lat = [mods[l, :b_] for l in range(2)]
    mods_ctx = [jnp.broadcast_to(mods[l, b_], (b_, 8, d)) for l in range(2)]
    row = lambda v: v.reshape(1, -1)

    t = _tiles(length, n_ctx)
    tr, trc = t.proj, t.ctx
    w0 = _att_weights(att_w_in[0], mla_q_norm[0], mla_w_uq[0], mla_kv_norm[0], mla_w_ukv[0])
    tables = _rope_tables(length)
    qa, ka, va, qm, km, vm = _proj0(x, mods_lat[0], row(norm_mix[0]), w0, tables, tr)
    qa_c, ka_c, va_c, qm_c, km_c, vm_c = _proj0(ctx, mods_ctx[0], row(norm_mix[0]), w0, None, trc)
    lam_init = 0.8 - 0.6 * math.exp(-0.3 * 0)
    da_extra = (att_lambda[0], row(att_subnorm[0]))
    ya = _attention("da", qa, ka_c, va_c, ka, va, da_extra, t.att_q, t.att_k, lam_init)
    ym = _attention("mla", qm, km_c, vm_c, km, vm, None, t.att_q, t.att_k, lam_init)
    ya_c = _attention("da", qa_c, ka_c, va_c, None, None, da_extra, trc, t.att_k, lam_init)
    ym_c = _attention("mla", qm_c, km_c, vm_c, None, None, None, trc, t.att_k, lam_init)
    wo0 = (att_w_out[0].astype(BF16).reshape(2, 512, d),)
    w1 = w_mlp_in.astype(BF16)
    w2 = w_mlp_out.astype(BF16)
    h_lat = _mix_mlp(_merge_attention, (ya, ym), wo0, x, mods_lat[0], row(norm_mlp[0]), w1[0], w2[0], None, t.mlp0)
    h_ctx = _mix_mlp(_merge_attention, (ya_c, ym_c), wo0, ctx, mods_ctx[0], row(norm_mlp[0]), w1[0], w2[0], None,
                     trc)

    w1r = _rec_weights(rec_w_in[0], hgrn_bound_logits, ssd_dt_bias[0])
    cw = ssd_conv_w[0]
    cb = row(ssd_conv_b[0])
    q_c, gf_c, gb_c, i_c, _, _, xs_c, bm_c, cm_c, dt_c, dtt_c = _proj1(h_ctx, mods_ctx[1], row(norm_mix[1]), w1r,
                                                                        cw, cb, trc)
    q_l, gf_l, gb_l, i_l, sg_l, sz_l, xs_l, bm_l, cm_l, dt_l, dtt_l = _proj1(h_lat, mods_lat[1], row(norm_mix[1]),
                                                                             w1r, cw, cb, tr)
    rbc, rbl = t.ctx, t.scan
    zs = jnp.zeros((b_, SSD_STATE, SSD_INNER), F32)
    zh = jnp.zeros((b_, HGRN_HEADS, LANES, HGRN_DK), F32)
    _, hs_f, _, hs_b = _ssd_scan(xs_c, bm_c, cm_c, dt_c, dtt_c, ssd_a_log[0], zs, zs, rbc)
    y_f, _, y_b, _ = _ssd_scan(xs_l, bm_l, cm_l, dt_l, dtt_l, ssd_a_log[0], hs_f, hs_b, rbl)
    _, ss_f, _, ss_b = _hgrn_scan(q_c, gf_c, gb_c, i_c, zh, zh, rbc)
    o_f, _, o_b, _ = _hgrn_scan(q_l, gf_l, gb_l, i_l, ss_f, ss_b, rbl)
    skip = row(jnp.repeat(ssd_skip[0], SSD_HEAD_DIM))
    wro = rec_w_out[0].astype(BF16).reshape(2, 512, d)
    return _mix_mlp(_merge_recurrent, (o_f, o_b, y_f, y_b, sg_l, sz_l, xs_l),
                    (row(hgrn_out_norm[0]), skip, row(ssd_norm[0]), wro),
                    h_lat, mods_lat[1], row(norm_mlp[1]), w1[1], w2[1], row(final_norm), t.mlp1)
```

```python
import functools
import math
from typing import NamedTuple

import numpy as np
import jax
import jax.numpy as jnp
from jax import lax
from jax.experimental import pallas as pl
from jax.experimental.pallas import tpu as pltpu

F32 = jnp.float32
BF16 = jnp.bfloat16

D_MODEL = 1024
GRID_W = 64
DA_HEADS = 4
DA_HEAD_DIM = 64
MLA_HEADS = 8
MLA_NOPE = 64
MLA_ROPE = 32
MLA_V = 64
MLA_Q_RANK = 384
MLA_KV_RANK = 256
HGRN_HEADS = 4
HGRN_DK = 128
SSD_HEADS = 8
SSD_HEAD_DIM = 64
SSD_GROUPS = 2
SSD_STATE = 128
SSD_CONV_W = 5
SSD_INNER = SSD_HEADS * SSD_HEAD_DIM
SSD_CONV_CH = SSD_INNER + 2 * SSD_GROUPS * SSD_STATE
MLP_HIDDEN = 4 * D_MODEL
ROPE_BASE = 10000.0
NORM_EPS = 1e-6
CHUNK = 64
SSD_CHUNK = 256

LANES = 128
HALO = 16
LOG2E = 1.4426950408889634
VMEM_LIMIT = 56 * 1024 * 1024


def _cparams(sem):
    return pltpu.CompilerParams(dimension_semantics=sem, vmem_limit_bytes=VMEM_LIMIT)


def _rms(x):
    return x * lax.rsqrt(jnp.mean(x * x, axis=-1, keepdims=True) + NORM_EPS)


def _modulate(x, g, shift, scale):
    return (_rms(x) * g) * (1.0 + scale) + shift


def _silu(x):
    return x * jax.nn.sigmoid(x)


def _dot_nt(a, b):
    return lax.dot_general(a, b, (((1,), (1,)), ((), ())), preferred_element_type=F32)


def _dot_tn(a, b):
    return lax.dot_general(a, b, (((0,), (0,)), ((), ())), preferred_element_type=F32)


def _split2(x):
    hi = x.astype(BF16)
    lo = (x - hi.astype(F32)).astype(BF16)
    return hi, lo


def _exact_left(m01, x):
    hi, lo = _split2(x)
    return (jnp.dot(m01, hi, preferred_element_type=F32) + jnp.dot(m01, lo, preferred_element_type=F32))


def _exact_right(x, m01):
    hi, lo = _split2(x)
    return (jnp.dot(hi, m01, preferred_element_type=F32) + jnp.dot(lo, m01, preferred_element_type=F32))


def _mod_kernel(c_ref, w_ref, b_ref, o_ref):
    a = _silu(c_ref[...]).astype(BF16)
    o_ref[...] = jnp.dot(a, w_ref[...].astype(BF16), preferred_element_type=F32) + b_ref[...]


def _mod_vectors(cc, w_mod, b_mod):
    depth, d, n = w_mod.shape
    tn = 1024
    return pl.pallas_call(
        _mod_kernel,
        grid=(depth, n // tn),
        in_specs=[
            pl.BlockSpec((8, d), lambda l, j: (0, 0)),
            pl.BlockSpec((None, d, tn), lambda l, j: (l, 0, j)),
            pl.BlockSpec((None, 1, tn), lambda l, j: (l, 0, j)),
        ],
        out_specs=pl.BlockSpec((None, 8, tn), lambda l, j: (l, 0, j)),
        out_shape=jax.ShapeDtypeStruct((depth, 8, n), F32),
        compiler_params=_cparams(("arbitrary", "arbitrary")),
        name="mod_vectors",
    )(cc, w_mod, b_mod.reshape(depth, 1, n))


def _rope_partner(x, half):
    n = x.shape[-1]
    lane = lax.broadcasted_iota(jnp.int32, x.shape, x.ndim - 1)
    up = pltpu.roll(x, n - half, x.ndim - 1)
    dn = pltpu.roll(x, half, x.ndim - 1)
    return jnp.where((lane % (2 * half)) < half, up, dn)


def _tile_lanes(x, n):
    return jnp.concatenate([x] * n, axis=-1)


def _proj0_kernel(use_rope, *refs):
    if use_rope:
        (h_ref, mod_ref, g_ref, win_ref, wva_ref, qn_ref, wuq_ref, kvn_ref, wk_ref, wv_ref,
         ca_ref, sa_ref, cq_ref, sq_ref,
         qa_ref, ka_ref, va_ref, qm_ref, km_ref, vm_ref) = refs
    else:
        (h_ref, mod_ref, g_ref, win_ref, wva_ref, qn_ref, wuq_ref, kvn_ref, wk_ref, wv_ref,
         qa_ref, ka_ref, va_ref, qm_ref, km_ref, vm_ref) = refs
    mod = mod_ref[...]
    ub = _modulate(h_ref[...], g_ref[...], mod[0:1], mod[1:2]).astype(BF16)

    def proj(lo, hi):
        return jnp.dot(ub, win_ref[:, lo:hi], preferred_element_type=F32)

    qa = proj(0, 512) * (DA_HEAD_DIM ** -0.5 * LOG2E)
    ka = proj(512, 1024)
    va_ref[...] = _dot_nt(wva_ref[...], ub).astype(BF16)
    cq = proj(1024, 1408)
    ckv = proj(1408, 1664)
    kr = proj(1664, 1792)
    cqn = (_rms(cq) * qn_ref[...]).astype(BF16)
    qm = jnp.dot(cqn, wuq_ref[...], preferred_element_type=F32) * ((MLA_NOPE + MLA_ROPE) ** -0.5 * LOG2E)
    ckvn = (_rms(ckv) * kvn_ref[...]).astype(BF16)
    kn = jnp.dot(ckvn, wk_ref[...], preferred_element_type=F32)
    vm_ref[...] = _dot_nt(wv_ref[...], ckvn).astype(BF16)
    if use_rope:
        ca = _tile_lanes(ca_ref[...], DA_HEADS)
        sa = _tile_lanes(sa_ref[...], DA_HEADS)
        qa = qa * ca + _rope_partner(qa, DA_HEAD_DIM // 2) * sa
        ka = ka * ca + _rope_partner(ka, DA_HEAD_DIM // 2) * sa
        cq1 = cq_ref[...]
        sq1 = sq_ref[...]
        qm = qm * _tile_lanes(cq1, MLA_HEADS) + _rope_partner(qm, MLA_ROPE // 2) * _tile_lanes(sq1, MLA_HEADS)
        kr = kr * cq1 + _rope_partner(kr, MLA_ROPE // 2) * sq1
    qa_ref[...] = qa.astype(BF16)
    ka_ref[...] = ka.astype(BF16)
    qm_ref[...] = qm.astype(BF16)
    km_ref[...] = (kn + _tile_lanes(kr, MLA_HEADS)).astype(BF16)


def _proj0(h, mods, gain, w, tables, tr):
    g_, r_, d = h.shape
    use_rope = tables is not None
    row = lambda c: pl.BlockSpec((None, tr, c), lambda b, t: (b, t, 0))
    full = lambda a: pl.BlockSpec(a.shape, lambda b, t: (0,) * a.ndim)
    in_specs = [row(d), pl.BlockSpec((None, 8, d), lambda b, t: (b, 0, 0)), full(gain),
                full(w["win"]), full(w["wva"]), full(w["qn"]), full(w["wuq"]), full(w["kvn"]), full(w["wk"]),
                full(w["wv"])]
    args = [h, mods, gain, w["win"], w["wva"], w["qn"], w["wuq"], w["kvn"], w["wk"], w["wv"]]
    if use_rope:
        in_specs += [pl.BlockSpec((tr, LANES), lambda b, t: (t, 0))] * 4
        args += list(tables)
    col = pl.BlockSpec((None, 512, tr), lambda b, t: (b, 0, t))
    widths = (512, 512, None, 1024, 1024, None)
    return pl.pallas_call(
        functools.partial(_proj0_kernel, use_rope),
        grid=(g_, r_ // tr),
        in_specs=in_specs,
        out_specs=[col if c is None else row(c) for c in widths],
        out_shape=[jax.ShapeDtypeStruct((g_, 512, r_) if c is None else (g_, r_, c), BF16) for c in widths],
        compiler_params=_cparams(("arbitrary", "arbitrary")),
        name="proj0_rope" if use_rope else "proj0_ctx",
    )(*args)


def _attn_kernel(mode, has_lat, tk, nk, lam_init, *refs):
    refs = list(refs)
    q_ref, kc_ref, vc_ref = refs[:3]
    refs = refs[3:]
    if has_lat:
        kl_ref, vl_ref = refs[:2]
        refs = refs[2:]
    if mode == "da":
        lam_ref, sub_ref = refs[:2]
        refs = refs[2:]
    o_ref, ma_ref, la_ref, acca_ref, mb_ref, lb_ref, accb_ref = refs[:7]
    if has_lat:
        s_ref, mx_ref = refs[7:]

    q = q_ref[...]
    if mode == "da":
        lane = lax.broadcasted_iota(jnp.int32, q.shape, 1)
        zero = jnp.zeros_like(q)
        qs = (jnp.where(lane < DA_HEAD_DIM, q, zero), jnp.where(lane >= DA_HEAD_DIM, q, zero))
    else:
        qs = (q[:, :LANES], q[:, LANES:])
    stats = ((ma_ref, la_ref, acca_ref), (mb_ref, lb_ref, accb_ref))
    for m_ref, l_ref, acc_ref in stats:
        m_ref[...] = jnp.full(m_ref.shape, -jnp.inf, F32)
        l_ref[...] = jnp.zeros(l_ref.shape, F32)
        acc_ref[...] = jnp.zeros(acc_ref.shape, F32)

    def scores(k):
        ks = (k, k) if mode == "da" else (k[:, :LANES], k[:, LANES:])
        return [_dot_nt(ks[i], qs[i]) for i in range(2)]

    def softmax(i, s):
        m_ref, l_ref, _ = stats[i]
        m_prev = m_ref[...]
        m_new = jnp.maximum(m_prev, jnp.max(s, axis=0, keepdims=True))
        alpha = jnp.exp2(m_prev - m_new)
        p = jnp.exp2(s - m_new)
        l_ref[...] = alpha * l_ref[...] + jnp.sum(p, axis=0, keepdims=True)
        m_ref[...] = m_new
        return p.astype(BF16), alpha

    def accumulate(i, vt, p, alpha):
        acc_ref = stats[i][2]
        acc_ref[...] = alpha * acc_ref[...] + jnp.dot(vt, p, preferred_element_type=F32)

    if not has_lat:
        ss = scores(kc_ref[...])
        pa = [softmax(i, ss[i]) for i in range(2)]
        for i in range(2):
            accumulate(i, vc_ref[...], *pa[i])
    else:
        nc = kc_ref.shape[0]

        def chunk(c):
            return pl.ds(c * tk if isinstance(c, int) else pl.multiple_of(c * tk, tk), tk)

        def keys(st):
            return kc_ref[...] if isinstance(st, int) and st == 0 else kl_ref[chunk(st - 1), :]

        def vals(st):
            return vc_ref[...] if isinstance(st, int) and st == 0 else vl_ref[:, chunk(st - 1)]

        def lookahead(st, slot, n):
            ss = scores(keys(st))
            for i in range(2):
                s_ref[slot, i, 0:n, :] = ss[i]
                mx_ref[1, i] = jnp.maximum(mx_ref[0, i], jnp.max(ss[i], axis=0, keepdims=True))

        def consume(st, slot, n):
            vt = vals(st)
            for i in range(2):
                m_ref, l_ref, acc_ref = stats[i]
                m_cur = mx_ref[0, i]
                alpha = jnp.exp2(m_ref[...] - m_cur)
                p = jnp.exp2(s_ref[slot, i, 0:n, :] - m_cur)
                l_ref[...] = alpha * l_ref[...] + jnp.sum(p, axis=0, keepdims=True)
                acc_ref[...] = alpha * acc_ref[...] + jnp.dot(vt, p.astype(BF16), preferred_element_type=F32)
                m_ref[...] = m_cur

        def advance():
            for i in range(2):
                mx_ref[0, i] = mx_ref[1, i]

        def run(st, slot, has_next):
            n = nc if isinstance(st, int) and st == 0 else tk
            if has_next:
                lookahead(st + 1, 1 - slot, tk)
            consume(st, slot, n)
            if has_next:
                advance()

        for i in range(2):
            mx_ref[0, i] = jnp.full((1, mx_ref.shape[-1]), -jnp.inf, F32)
        lookahead(0, 0, nc)
        advance()
        run(0, 0, True)
        npair = (nk - 1) // 2

        def body(j, carry):
            run(2 * j + 1, 1, True)
            run(2 * j + 2, 0, True)
            return carry

        lax.fori_loop(0, npair, body, 0)
        for st in range(2 * npair + 1, nk + 1):
            run(st, st % 2, st < nk)

    oa = acca_ref[...] / la_ref[...]
    ob = accb_ref[...] / lb_ref[...]
    if mode == "da":
        lp = lam_ref[...]
        lam = (jnp.exp(jnp.sum(lp[0:1] * lp[1:2], axis=1, keepdims=True))
               - jnp.exp(jnp.sum(lp[2:3] * lp[3:4], axis=1, keepdims=True)) + lam_init)
        dlt = (oa - lam * ob).T
        o_ref[...] = ((_rms(dlt) * sub_ref[...]) * (1.0 - lam_init)).astype(o_ref.dtype)
    else:
        chan = lax.broadcasted_iota(jnp.int32, oa.shape, 0)
        o_ref[...] = jnp.where(chan < MLA_V, oa, ob).T.astype(o_ref.dtype)


def _attention(mode, q, kc, vc, kl, vl, extra, tq, tk, lam_init):
    b_, sq, _ = q.shape
    has_lat = kl is not None
    w = LANES if mode == "da" else 2 * LANES
    nh = 4
    kblk = lambda a: pl.BlockSpec((None, a.shape[1], w), lambda b, h, t: (b, 0, h))
    vblk = lambda a: pl.BlockSpec((None, LANES, a.shape[2]), lambda b, h, t: (b, h, 0))
    in_specs = [pl.BlockSpec((None, tq, w), lambda b, h, t: (b, t, h)), kblk(kc), vblk(vc)]
    args = [q, kc, vc]
    nk = 0
    if has_lat:
        in_specs += [kblk(kl), vblk(vl)]
        args += [kl, vl]
        nk = kl.shape[1] // tk
    if mode == "da":
        lam_p, subnorm = extra
        in_specs += [pl.BlockSpec(lam_p.shape, lambda b, h, t: (0, 0)),
                     pl.BlockSpec(subnorm.shape, lambda b, h, t: (0, 0))]
        args += [lam_p, subnorm]
    scratch = []
    for _ in range(2):
        scratch += [pltpu.VMEM((1, tq), F32), pltpu.VMEM((1, tq), F32), pltpu.VMEM((LANES, tq), F32)]
    if has_lat:
        scratch += [pltpu.VMEM((2, 2, tk, tq), F32), pltpu.VMEM((2, 2, 1, tq), F32)]
    return pl.pallas_call(
        functools.partial(_attn_kernel, mode, has_lat, tk, nk, lam_init),
        grid=(b_, nh, sq // tq),
        in_specs=in_specs,
        out_specs=pl.BlockSpec((None, tq, LANES), lambda b, h, t: (b, t, h)),
        out_shape=jax.ShapeDtypeStruct((b_, sq, nh * LANES), BF16),
        scratch_shapes=scratch,
        compiler_params=_cparams(("arbitrary", "arbitrary", "arbitrary")),
        name=f"attn_{mode}_{'lat' if has_lat else 'ctx'}",
    )(*args)


def _merge_attention(ya_ref, ym_ref, w_ref):
    return (jnp.dot(ya_ref[...], w_ref[0], preferred_element_type=F32)
            + jnp.dot(ym_ref[...], w_ref[1], preferred_element_type=F32))


def _merge_recurrent(of_ref, ob_ref, yf_ref, yb_ref, sg_ref, sz_ref, xs_ref, on_ref, sk_ref, sn_ref, w_ref):
    o = of_ref[...].astype(F32) + ob_ref[...].astype(F32)
    on = on_ref[...]
    parts = []
    for hh in range(HGRN_HEADS):
        ls = slice(hh * LANES, (hh + 1) * LANES)
        parts.append(_rms(o[:, ls]) * on[:, ls])
    o = jnp.concatenate(parts, axis=1) * sg_ref[...].astype(F32)
    y = ((yf_ref[...].astype(F32) + yb_ref[...].astype(F32) + sk_ref[...] * xs_ref[...])
         * sz_ref[...].astype(F32))
    sn = sn_ref[...]
    gw = SSD_INNER // SSD_GROUPS
    parts = []
    for gg in range(SSD_GROUPS):
        ls = slice(gg * gw, (gg + 1) * gw)
        parts.append(_rms(y[:, ls]) * sn[:, ls])
    y = jnp.concatenate(parts, axis=1)
    return (jnp.dot(o.astype(BF16), w_ref[0], preferred_element_type=F32)
            + jnp.dot(y.astype(BF16), w_ref[1], preferred_element_type=F32))


def _mix_mlp_kernel(merge, n_mix, final, *refs):
    mix_refs = refs[:n_mix]
    if final:
        h_ref, mod_ref, g_ref, w1_ref, w2_ref, fg_ref, o_ref = refs[n_mix:]
    else:
        h_ref, mod_ref, g_ref, w1_ref, w2_ref, o_ref = refs[n_mix:]
    h1 = h_ref[...] + mod_ref[2:3, :] * merge(*mix_refs)
    u = _modulate(h1, g_ref[...], mod_ref[3:4, :], mod_ref[4:5, :]).astype(BF16)
    a = jnp.dot(u, w1_ref[...], preferred_element_type=F32)
    a = jnp.square(jnp.maximum(a, 0.0)).astype(BF16)
    out = h1 + mod_ref[5:6, :] * jnp.dot(a, w2_ref[...], preferred_element_type=F32)
    if final:
        out = _rms(out) * fg_ref[...]
    o_ref[...] = out


def _mix_mlp(merge, mix_rows, mix_full, h, mods, gain, w1, w2, final_gain, tr):
    g_, r_, d = h.shape
    final = final_gain is not None
    row = lambda c: pl.BlockSpec((None, tr, c), lambda b, t: (b, t, 0))
    full = lambda a: pl.BlockSpec(a.shape, lambda b, t: (0,) * a.ndim)
    resident = lambda a: pl.BlockSpec(a.shape, lambda b, t: (0,) * a.ndim, pipeline_mode=pl.Buffered(1))
    in_specs = ([row(a.shape[-1]) for a in mix_rows] + [full(a) for a in mix_full]
                + [row(d), pl.BlockSpec((None, 8, d), lambda b, t: (b, 0, 0)), full(gain),
                   resident(w1), resident(w2)])
    args = list(mix_rows) + list(mix_full) + [h, mods, gain, w1, w2]
    if final:
        in_specs.append(full(final_gain))
        args.append(final_gain)
    return pl.pallas_call(
        functools.partial(_mix_mlp_kernel, merge, len(mix_rows) + len(mix_full), final),
        grid=(g_, r_ // tr),
        in_specs=in_specs,
        out_specs=row(d),
        out_shape=jax.ShapeDtypeStruct(h.shape, F32),
        compiler_params=_cparams(("arbitrary", "arbitrary")),
        name="mix_mlp_final" if final else "mix_mlp",
    )(*args)


def _softplus(x):
    return jnp.maximum(x, 0.0) + jnp.log1p(jnp.exp(-jnp.abs(x)))


def _proj1_kernel(nt, h_ref, hp_ref, hn_ref, mod_ref, g_ref, w_ref, wdt_ref, bl_ref, dtb_ref, cw_ref, cb_ref,
                  q_ref, gf_ref, gb_ref, i_ref, sg_ref, sz_ref, xs_ref, bm_ref, cm_ref, dt_ref, dtt_ref, pad_ref):
    t = pl.program_id(1)
    tr = h_ref.shape[0]
    mod = mod_ref[...]
    gain = g_ref[...]
    ub = _modulate(h_ref[...], gain, mod[0:1], mod[1:2]).astype(BF16)
    u_ext = jnp.concatenate([_modulate(hp_ref[...], gain, mod[0:1], mod[1:2]).astype(BF16), ub,
                             _modulate(hn_ref[...], gain, mod[0:1], mod[1:2]).astype(BF16)], axis=0)
    xbc = jnp.dot(u_ext, w_ref[:, 3072:4096], preferred_element_type=F32)
    ri = lax.broadcasted_iota(jnp.int32, (tr + 2 * HALO, 1), 0)
    inside = ((ri >= HALO) | (t > 0)) & ((ri < tr + HALO) | (t < nt - 1))
    pad_ref[...] = jnp.where(inside, xbc, 0.0)
    half = SSD_CONV_W // 2
    y = jnp.zeros((tr, xbc.shape[1]), F32) + cb_ref[...]
    for j in range(SSD_CONV_W):
        y = y + pad_ref[HALO - half + j:HALO - half + j + tr, :] * cw_ref[j:j + 1, :]
    y = _silu(y)
    xs_ref[...] = y[:, :SSD_INNER]
    bm_ref[...] = y[:, SSD_INNER:SSD_INNER + 256].astype(BF16)
    cm_ref[...] = y[:, SSD_INNER + 256:].astype(BF16)

    def proj(lo, hi):
        return jnp.dot(ub, w_ref[:, lo:hi], preferred_element_type=F32)

    bl = bl_ref[...]
    e = jnp.exp(bl - jnp.max(bl, axis=0, keepdims=True))
    gamma = e / jnp.sum(e, axis=0, keepdims=True)
    lb = (gamma[0:1] + gamma[1:2]) - gamma[0:1]
    q_ref[...] = _silu(proj(0, 512)).astype(BF16)
    for k, out in ((0, gf_ref), (1, gb_ref)):
        lbk = lb[:, 512 * k:512 * (k + 1)]
        f = lbk + (1.0 - lbk) * jax.nn.sigmoid(proj(512 * (k + 1), 512 * (k + 2)))
        out[...] = jnp.log(f)
    i_ref[...] = proj(1536, 2048).astype(BF16)
    sg_ref[...] = _silu(proj(2048, 2560)).astype(BF16)
    sz_ref[...] = _silu(proj(2560, 3072)).astype(BF16)
    dt =_softplus(jnp.dot(ub, wdt_ref[...], preferred_element_type=F32) + dtb_ref[...])
    dt_ref[...] = dt
    dtt_ref[...] = jnp.concatenate([dt[:, :LANES].T[:SSD_HEADS], dt[:, LANES:].T[:SSD_HEADS]], axis=0)


def _proj1(h, mods, gain, w, conv_w, conv_b, tr):
    g_, r_, d = h.shape
    nt = r_ // tr
    nb = tr // HALO
    row = lambda c: pl.BlockSpec((None, tr, c), lambda b, t: (b, t, 0))
    full = lambda a: pl.BlockSpec(a.shape, lambda b, t: (0,) * a.ndim)
    names = ("w", "wdt", "bl", "dtb")
    widths = (512, 512, 512, 512, 512, 512, 512, 256, 256, 2 * LANES)
    dts = (BF16, F32, F32, BF16, BF16, BF16, F32, BF16, BF16, F32)
    return pl.pallas_call(
        functools.partial(_proj1_kernel, nt),
        grid=(g_, nt),
        in_specs=[row(d),
                  pl.BlockSpec((None, HALO, d), lambda b, t: (b, jnp.maximum(t * nb - 1, 0), 0)),
                  pl.BlockSpec((None, HALO, d), lambda b, t: (b, jnp.minimum((t + 1) * nb, nt * nb - 1), 0)),
                  pl.BlockSpec((None, 8, d), lambda b, t: (b, 0, 0)), full(gain)]
        + [full(w[n]) for n in names] + [full(conv_w), full(conv_b)],
        out_specs=[row(c) for c in widths] + [pl.BlockSpec((None, 16, tr), lambda b, t: (b, 0, t))],
        out_shape=[jax.ShapeDtypeStruct((g_, r_, c), t_) for c, t_ in zip(widths, dts)]
        + [jax.ShapeDtypeStruct((g_, 16, r_), F32)],
        scratch_shapes=[pltpu.VMEM((tr + 2 * HALO, SSD_CONV_CH), F32)],
        compiler_params=_cparams(("arbitrary", "arbitrary")),
        name="proj1",
    )(h, h, h, mods, gain, *[w[n] for n in names], conv_w, conv_b)


def _tri(rev, n):
    i = np.arange(n)
    m = i[None, :] >= i[:, None] if rev else i[None, :] <= i[:, None]
    return m


def _ssd_consts(rev, nchunk):
    tri = np.kron(np.eye(nchunk), _tri(rev, SSD_CHUNK).astype(np.float32))
    expand = np.zeros((LANES, SSD_INNER), np.float32)
    for h in range(SSD_HEADS):
        expand[h, h * SSD_HEAD_DIM:(h + 1) * SSD_HEAD_DIM] = 1.0
    return jnp.asarray(tri, BF16), jnp.asarray(tri.T.copy(), BF16), jnp.asarray(expand, BF16)


def _ssd_block_decays(rev, dt_ref, dtt_ref, an, ant, tri, trit):
    dir_ = 1 if rev else 0
    dt = dt_ref[:, dir_ * LANES:(dir_ + 1) * LANES]
    dtt = dtt_ref[8 * dir_:8 * dir_ + 8, :]
    acol = _exact_left(tri, dt * an)
    arow = _exact_right(dtt * ant, trit)
    return dt, acol, arow


def _expand_heads(vals, ex):
    rows = vals[0].shape[0]
    parts = [p for v in vals for p in _split2(v)]
    big = jnp.dot(jnp.concatenate(parts, axis=0), ex, preferred_element_type=F32)
    return [big[2 * i * rows:(2 * i + 1) * rows] + big[(2 * i + 1) * rows:(2 * i + 2) * rows]
            for i in range(len(vals))]


def _ssd_chunk(rev, r0, x_ref, bm_ref, cm_ref, decays, h_ref, y_ref):
    n = SSD_CHUNK
    ti = lax.broadcasted_iota(jnp.int32, (n, n), 0)
    si = lax.broadcasted_iota(jnp.int32, (n, n), 1)
    mask = (si >= ti) if rev else (si <= ti)
    lane = lax.broadcasted_iota(jnp.int32, (n, LANES), 1)
    last = 0 if rev else n - 1
    x = x_ref[r0:r0 + n, :]
    bm = bm_ref[r0:r0 + n, :]
    cm = cm_ref[r0:r0 + n, :]
    acol = decays[0][r0:r0 + n, :]
    arow = decays[1][:, r0:r0 + n]
    dte = decays[2][r0:r0 + n, :]
    ae = decays[3][r0:r0 + n, :]
    ae_last = ae[last:last + 1, :]
    xdt = x * dte
    xw = (xdt * jnp.exp(ae_last - ae)).astype(BF16)
    xdtb = xdt.astype(BF16)
    ys = []
    for g in range(SSD_GROUPS):
        bg = bm[:, g * SSD_STATE:(g + 1) * SSD_STATE]
        cg = cm[:, g * SSD_STATE:(g + 1) * SSD_STATE]
        cb = _dot_nt(cg, bg)
        hs = h_ref[:, g * 256:(g + 1) * 256]
        yoff = jnp.dot(cg, hs.astype(BF16), preferred_element_type=F32) * jnp.exp(ae[:, g * 256:(g + 1) * 256])
        for pr in range(2):
            xp = xdtb[:, g * 256 + pr * LANES:g * 256 + (pr + 1) * LANES]
            res = []
            for k in range(2):
                hd = g * 4 + pr * 2 + k
                dmat = acol[:, hd:hd + 1] - arow[hd:hd + 1, :]
                lmat = jnp.exp(jnp.where(mask, dmat, -jnp.inf))
                res.append(jnp.dot((cb * lmat).astype(BF16), xp, preferred_element_type=F32))
            ys.append(jnp.where(lane < SSD_HEAD_DIM, res[0], res[1]) + yoff[:, pr * LANES:(pr + 1) * LANES])
        h_ref[:, g * 256:(g + 1) * 256] = (hs * jnp.exp(ae_last[:, g * 256:(g + 1) * 256])
                                           + _dot_tn(bg, xw[:, g * 256:(g + 1) * 256]))
    y_ref[r0:r0 + n, :] = jnp.concatenate(ys, axis=1).astype(y_ref.dtype)


def _ssd_kernel(nchunk, xf_ref, bmf_ref, cmf_ref, dtf_ref, dttf_ref, xb_ref, bmb_ref, cmb_ref, dtb_ref, dttb_ref,
                an_ref, ant_ref, tri_ref, trit_ref, ex_ref, h0f_ref, h0b_ref,
                yf_ref, hTf_ref, yb_ref, hTb_ref, hf_ref, hb_ref):
    s = pl.program_id(1)

    @pl.when(s == 0)
    def _():
        hf_ref[...] = h0f_ref[...]
        hb_ref[...] = h0b_ref[...]

    ex = ex_ref[...]
    dt_f, acol_f, arow_f = _ssd_block_decays(False, dtf_ref, dttf_ref, an_ref[0], ant_ref[0], tri_ref[0], trit_ref[0])
    dt_b, acol_b, arow_b = _ssd_block_decays(True, dtb_ref, dttb_ref, an_ref[1], ant_ref[1], tri_ref[1], trit_ref[1])
    dte_f, ae_f, dte_b, ae_b = _expand_heads([dt_f, acol_f, dt_b, acol_b], ex)
    dec_f = (acol_f, arow_f, dte_f, ae_f)
    dec_b = (acol_b, arow_b, dte_b, ae_b)
    for j in range(nchunk):
        _ssd_chunk(False, j * SSD_CHUNK, xf_ref, bmf_ref, cmf_ref, dec_f, hf_ref, yf_ref)
        _ssd_chunk(True, (nchunk - 1 - j) * SSD_CHUNK, xb_ref, bmb_ref, cmb_ref, dec_b, hb_ref, yb_ref)

    @pl.when(s == pl.num_programs(1) - 1)
    def _():
        hTf_ref[...] = hf_ref[...]
        hTb_ref[...] = hb_ref[...]


def _ssd_scan(xs, bm, cm, dt, dtt, a_log, h0f, h0b, rb):
    b_, r_, _ = xs.shape
    nblk = r_ // rb
    trif, tritf, ex = _ssd_consts(False, rb // SSD_CHUNK)
    trib, tritb, _ = _ssd_consts(True, rb // SSD_CHUNK)
    tri = jnp.stack([trif, trib])
    trit = jnp.stack([tritf, tritb])
    a_neg = -jnp.exp(a_log.astype(F32))
    an = jnp.zeros((2, 1, LANES), F32).at[:, 0, :SSD_HEADS].set(a_neg)
    ant = a_neg.reshape(2, SSD_HEADS, 1)
    rowf = lambda c: pl.BlockSpec((None, rb, c), lambda b, s: (b, s, 0))
    rowb = lambda c: pl.BlockSpec((None, rb, c), lambda b, s: (b, nblk - 1 - s, 0))
    full = lambda a: pl.BlockSpec(a.shape, lambda b, s: (0,) * a.ndim)
    st = pl.BlockSpec((None, SSD_STATE, SSD_INNER), lambda b, s: (b, 0, 0))
    y_shape = jax.ShapeDtypeStruct((b_, r_, SSD_INNER), BF16)
    h_shape = jax.ShapeDtypeStruct((b_, SSD_STATE, SSD_INNER), F32)
    return pl.pallas_call(
        functools.partial(_ssd_kernel, rb // SSD_CHUNK),
        grid=(b_, nblk),
        in_specs=[rowf(512), rowf(256), rowf(256), rowf(2 * LANES),
                  pl.BlockSpec((None, 16, rb), lambda b, s: (b, 0, s)),
                  rowb(512), rowb(256), rowb(256), rowb(2 * LANES),
                  pl.BlockSpec((None, 16, rb), lambda b, s: (b, 0, nblk - 1 - s)),
                  full(an), full(ant), full(tri), full(trit), full(ex), st, st],
        out_specs=[rowf(512), st, rowb(512), st],
        out_shape=[y_shape, h_shape, y_shape, h_shape],
        scratch_shapes=[pltpu.VMEM((SSD_STATE, SSD_INNER), F32)] * 2,
        compiler_params=_cparams(("arbitrary", "arbitrary")),
        name="ssd_scan",
    )(xs, bm, cm, dt, dtt, xs, bm, cm, dt, dtt, an, ant, tri, trit, ex, h0f, h0b)


_HGRN_LEVELS = (64, 32, 16, 8, 4, 2, 1)


def _hgrn_consts(rev):
    i = np.arange(CHUNK)
    before = _tri(rev, CHUNK)
    after_strict = ~before
    mats, masks = [], []
    for c in _HGRN_LEVELS:
        same = (i[:, None] // c) == (i[None, :] // c)
        q_side = same & before
        k_side = same & after_strict
        if c == CHUNK:
            mats += [q_side, k_side]
        else:
            blk = i // c
            first = (blk % 2 == 1) if rev else (blk % 2 == 0)
            if c > 1:
                mats.append(np.where(first[:, None], k_side, q_side))
            pair = first[None, :] & (~first)[:, None] & ((blk[:, None] // 2) == (blk[None, :] // 2))
            masks.append(pair)
    masks.append(np.eye(CHUNK, dtype=bool))
    m = np.concatenate(mats, axis=0).astype(np.float32)
    m = np.concatenate([m, m], axis=1)
    mk = np.stack(masks, axis=0).astype(np.float32)
    return jnp.asarray(m, BF16), jnp.asarray(mk, F32)


def _hgrn_chunk(rev, r0, q_ref, g_ref, v_ref, mall, mk_ref, s_ref, o_ref):
    nl = len(_HGRN_LEVELS)
    last = 0 if rev else CHUNK - 1
    g = g_ref[r0:r0 + CHUNK, :]
    g2 = jnp.concatenate(_split2(g), axis=0)
    eall = jnp.exp(jnp.dot(mall, g2, preferred_element_type=F32))
    ff = jnp.exp(g)
    kk = 1.0 - ff
    qq = q_ref[r0:r0 + CHUNK, :].astype(F32)
    vv = v_ref[r0:r0 + CHUNK, :]
    outs = []
    for h in range(HGRN_HEADS):
        ls = slice(h * LANES, (h + 1) * LANES)
        qh, kh, vh = qq[:, ls], kk[:, ls], vv[:, ls]
        khb = kh.astype(BF16)

        def fac(idx, ls=ls):
            return eall[idx * CHUNK:(idx + 1) * CHUNK, ls]

        q01 = jnp.concatenate([qh, qh * ff[:, ls]], axis=0).astype(BF16)
        a01 = _dot_nt(q01, khb)
        att = a01[:CHUNK] * mk_ref[nl - 1] + a01[CHUNK:] * mk_ref[nl - 2]
        for l in range(1, nl - 1):
            e = fac(l + 1)
            att = att + _dot_nt((qh * e).astype(BF16), (kh * e).astype(BF16)) * mk_ref[l - 1]
        st = s_ref[h]
        qs_ = (qh * fac(0)).astype(BF16)
        o = _dot_nt(qs_, st.astype(BF16)) + jnp.dot(att.astype(BF16), vh, preferred_element_type=F32)
        outs.append(o)
        ks_ = (kh * fac(1)).astype(BF16)
        dec = fac(0)[last:last + 1, :]
        s_ref[h] = st * dec + _dot_tn(vh, ks_)
    o_ref[r0:r0 + CHUNK, :] = jnp.concatenate(outs, axis=1).astype(o_ref.dtype)


def _hgrn_kernel(nchunk, qf_ref, gf_ref, vf_ref, qb_ref, gb_ref, vb_ref, mf_ref, mkf_ref, mb_ref, mkb_ref,
                 s0f_ref, s0b_ref, of_ref, sTf_ref, ob_ref, sTb_ref, sf_ref, sb_ref):
    s = pl.program_id(1)

    @pl.when(s == 0)
    def _():
        sf_ref[...] = s0f_ref[...]
        sb_ref[...] = s0b_ref[...]

    mallf = mf_ref[...]
    mallb = mb_ref[...]
    for j in range(nchunk):
        _hgrn_chunk(False, j * CHUNK, qf_ref, gf_ref, vf_ref, mallf, mkf_ref, sf_ref, of_ref)
        _hgrn_chunk(True, (nchunk - 1 - j) * CHUNK, qb_ref, gb_ref, vb_ref, mallb, mkb_ref, sb_ref, ob_ref)

    @pl.when(s == pl.num_programs(1) - 1)
    def _():
        sTf_ref[...] = sf_ref[...]
        sTb_ref[...] = sb_ref[...]


def _hgrn_scan(q, gf, gb, v, s0f, s0b, rb):
    b_, r_, _ = q.shape
    nblk = r_ // rb
    mallf, mkf = _hgrn_consts(False)
    mallb, mkb = _hgrn_consts(True)
    rowf = pl.BlockSpec((None, rb, 512), lambda b, s: (b, s, 0))
    rowb = pl.BlockSpec((None, rb, 512), lambda b, s: (b, nblk - 1 - s, 0))
    full = lambda a: pl.BlockSpec(a.shape, lambda b, s: (0,) * a.ndim)
    st = pl.BlockSpec((None, HGRN_HEADS, LANES, HGRN_DK), lambda b, s: (b, 0, 0, 0))
    o_shape = jax.ShapeDtypeStruct((b_, r_, 512), BF16)
    s_shape = jax.ShapeDtypeStruct((b_, HGRN_HEADS, LANES, HGRN_DK), F32)
    return pl.pallas_call(
        functools.partial(_hgrn_kernel, rb // CHUNK),
        grid=(b_, nblk),
        in_specs=[rowf, rowf, rowf, rowb, rowb, rowb, full(mallf), full(mkf), full(mallb), full(mkb), st, st],
        out_specs=[rowf, st, rowb, st],
        out_shape=[o_shape, s_shape, o_shape, s_shape],
        scratch_shapes=[pltpu.VMEM((HGRN_HEADS, LANES, HGRN_DK), F32)] * 2,
        compiler_params=_cparams(("arbitrary", "arbitrary")),
        name="hgrn_scan",
    )(q, gf, v, q, gb, v, mallf, mkf, mallb, mkb, s0f, s0b)


def _rope_tables(length):
    rows = length // GRID_W
    row = jnp.repeat(jnp.arange(rows, dtype=F32), GRID_W)
    col = jnp.tile(jnp.arange(GRID_W, dtype=F32), rows)

    def cs(rot_dim):
        n_freq = rot_dim // 4
        inv_freq = ROPE_BASE ** (-jnp.arange(n_freq, dtype=F32) / n_freq)
        ang = jnp.concatenate([row[:, None] * inv_freq, col[:, None] * inv_freq], axis=-1)
        c, s = jnp.cos(ang), jnp.sin(ang)
        return jnp.concatenate([c, c], axis=-1), jnp.concatenate([-s, s], axis=-1)

    ca, sa = cs(DA_HEAD_DIM)
    cq, sq = cs(MLA_ROPE)
    ones = jnp.ones((length, MLA_NOPE), F32)
    pad1 = jnp.ones((length, LANES - MLA_NOPE - MLA_ROPE), F32)
    cqt = jnp.concatenate([ones, cq, pad1], axis=-1)
    sqt = jnp.concatenate([0.0 * ones, sq, 0.0 * pad1], axis=-1)
    return jnp.tile(ca, (1, 2)), jnp.tile(sa, (1, 2)), cqt, sqt


def _att_weights(att_w_in, mla_q_norm, mla_w_uq, mla_kv_norm, mla_w_ukv):
    d = att_w_in.shape[0]
    kr = att_w_in[:, 2176:2208]
    kr_blk = jnp.concatenate([jnp.zeros((d, MLA_NOPE), F32), kr,
                              jnp.zeros((d, LANES - MLA_NOPE - MLA_ROPE), F32)], axis=1)
    win = jnp.concatenate([att_w_in[:, :1024], att_w_in[:, 1536:2176], kr_blk], axis=1).astype(BF16)
    wva = att_w_in[:, 1024:1536].T.astype(BF16)
    wq = mla_w_uq.reshape(MLA_Q_RANK, MLA_HEADS, MLA_NOPE + MLA_ROPE)
    wq = jnp.pad(wq, ((0, 0), (0, 0), (0, LANES - MLA_NOPE - MLA_ROPE))).reshape(MLA_Q_RANK, MLA_HEADS * LANES)
    wkv = mla_w_ukv.reshape(MLA_KV_RANK, MLA_HEADS, MLA_NOPE + MLA_V)
    wk = jnp.pad(wkv[:, :, :MLA_NOPE], ((0, 0), (0, 0), (0, LANES - MLA_NOPE))).reshape(MLA_KV_RANK, MLA_HEADS * LANES)
    wv = wkv[:, :, MLA_NOPE:].reshape(MLA_KV_RANK, MLA_HEADS * MLA_V)
    return dict(win=win, wva=wva, qn=mla_q_norm.reshape(1, -1), wuq=wq.astype(BF16),
                kvn=mla_kv_norm.reshape(1, -1), wk=wk.astype(BF16), wv=wv.T.astype(BF16))


def _rec_weights(rec_w_in, bound_logits, dt_bias):
    d = rec_w_in.shape[0]
    w = rec_w_in[:, :4096].astype(BF16)
    wdt = rec_w_in[:, 4096:4112]
    pad = jnp.zeros((d, LANES - SSD_HEADS), F32)
    wdt_rows = jnp.concatenate([wdt[:, :SSD_HEADS], pad, wdt[:, SSD_HEADS:], pad], axis=1).astype(BF16)
    zb = jnp.zeros((LANES - SSD_HEADS,), F32)
    dtb = jnp.concatenate([dt_bias[0], zb, dt_bias[1], zb]).reshape(1, 2 * LANES)
    return dict(w=w, wdt=wdt_rows, bl=bound_logits, dtb=dtb)


class _Tiles(NamedTuple):
    proj: int
    ctx: int
    att_q: int
    att_k: int
    mlp0: int
    mlp1: int
    scan: int


def _tiles(length, n_ctx):
    cap = lambda n, full: min(n, full)
    return _Tiles(proj=cap(512, length), ctx=cap(256, n_ctx), att_q=cap(2048, length), att_k=cap(512, length),
                  mlp0=cap(1024, length), mlp1=cap(512, length), scan=cap(512, length))


def kernel(x, c, ctx, c_ctx, w_mod, b_mod, norm_mix, norm_mlp, w_mlp_in, w_mlp_out, att_w_in, att_lambda, att_subnorm, mla_q_norm, mla_w_uq, mla_kv_norm, mla_w_ukv, att_w_out, rec_w_in, hgrn_bound_logits, hgrn_out_norm, ssd_conv_w, ssd_conv_b, ssd_a_log, ssd_dt_bias, ssd_skip, ssd_norm, rec_w_out, final_norm):
    b_, length, d = x.shape
    n_ctx = ctx.shape[1]
    assert w_mod.shape[0] == 2 and d == D_MODEL

    cc = jnp.zeros((8, d), F32).at[:b_].set(c).at[b_].set(c_ctx)
    mods = _mod_vectors(cc, w_mod, b_mod)
    mods = jnp.pad(mods.reshape(2, 8, 6, d), ((0, 0), (0, 0), (0, 2), (0, 0)))
    mods_lat = [mods[l, :b_] for l in range(2)]
    mods_ctx = [jnp.broadcast_to(mods[l, b_], (b_, 8, d)) for l in range(2)]
    row = lambda v: v.reshape(1, -1)

    t = _tiles(length, n_ctx)
    tr, trc = t.proj, t.ctx
    w0 = _att_weights(att_w_in[0], mla_q_norm[0], mla_w_uq[0], mla_kv_norm[0], mla_w_ukv[0])
    tables = _rope_tables(length)
    qa, ka, va, qm, km, vm = _proj0(x, mods_lat[0], row(norm_mix[0]), w0, tables, tr)
    qa_c, ka_c, va_c, qm_c, km_c, vm_c = _proj0(ctx, mods_ctx[0], row(norm_mix[0]), w0, None, trc)
    lam_init = 0.8 - 0.6 * math.exp(-0.3 * 0)
    da_extra = (att_lambda[0], row(att_subnorm[0]))
    ya = _attention("da", qa, ka_c, va_c, ka, va, da_extra, t.att_q, t.att_k, lam_init)
    ym = _attention("mla", qm, km_c, vm_c, km, vm, None, t.att_q, t.att_k, lam_init)
    ya_c = _attention("da", qa_c, ka_c, va_c, None, None, da_extra, trc, t.att_k, lam_init)
    ym_c = _attention("mla", qm_c, km_c, vm_c, None, None, None, trc, t.att_k, lam_init)
    wo0 = (att_w_out[0].astype(BF16).reshape(2, 512, d),)
    w1 = w_mlp_in.astype(BF16)
    w2 = w_mlp_out.astype(BF16)
    h_lat = _mix_mlp(_merge_attention, (ya, ym), wo0, x, mods_lat[0], row(norm_mlp[0]), w1[0], w2[0], None, t.mlp0)
    h_ctx = _mix_mlp(_merge_attention, (ya_c, ym_c), wo0, ctx, mods_ctx[0], row(norm_mlp[0]), w1[0], w2[0], None,
                     trc)

    w1r = _rec_weights(rec_w_in[0], hgrn_bound_logits, ssd_dt_bias[0])
    cw = ssd_conv_w[0]
    cb = row(ssd_conv_b[0])
    q_c, gf_c, gb_c, i_c, _, _, xs_c, bm_c, cm_c, dt_c, dtt_c = _proj1(h_ctx, mods_ctx[1], row(norm_mix[1]), w1r,
                                                                        cw, cb, trc)
    q_l, gf_l, gb_l, i_l, sg_l, sz_l, xs_l, bm_l, cm_l, dt_l, dtt_l = _proj1(h_lat, mods_lat[1], row(norm_mix[1]),
                                                                             w1r, cw, cb, tr)
    rbc, rbl = t.ctx, t.scan
    zs = jnp.zeros((b_, SSD_STATE, SSD_INNER), F32)
    zh = jnp.zeros((b_, HGRN_HEADS, LANES, HGRN_DK), F32)
    _, hs_f, _, hs_b = _ssd_scan(xs_c, bm_c, cm_c, dt_c, dtt_c, ssd_a_log[0], zs, zs, rbc)
    y_f, _, y_b, _ = _ssd_scan(xs_l, bm_l, cm_l, dt_l, dtt_l, ssd_a_log[0], hs_f, hs_b, rbl)
    _, ss_f, _, ss_b = _hgrn_scan(q_c, gf_c, gb_c, i_c, zh, zh, rbc)
    o_f, _, o_b, _ = _hgrn_scan(q_l, gf_l, gb_l, i_l, ss_f, ss_b, rbl)
    skip = row(jnp.repeat(ssd_skip[0], SSD_HEAD_DIM))
    wro = rec_w_out[0].astype(BF16).reshape(2, 512, d)
    return _mix_mlp(_merge_recurrent, (o_f, o_b, y_f, y_b, sg_l, sz_l, xs_l),
                    (row(hgrn_out_norm[0]), skip, row(ssd_norm[0]), wro),
                    h_lat, mods_lat[1], row(norm_mlp[1]), w1[1], w2[1], row(final_norm), t.mlp1)
```

```python
import functools
import math
from typing import NamedTuple

import numpy as np
import jax
import jax.numpy as jnp
from jax import lax
from jax.experimental import pallas as pl
from jax.experimental.pallas import tpu as pltpu

F32 = jnp.float32
BF16 = jnp.bfloat16

D_MODEL = 1024
GRID_W = 64
DA_HEADS = 4
DA_HEAD_DIM = 64
MLA_HEADS = 8
MLA_NOPE = 64
MLA_ROPE = 32
MLA_V = 64
MLA_Q_RANK = 384
MLA_KV_RANK = 256
HGRN_HEADS = 4
HGRN_DK = 128
SSD_HEADS = 8
SSD_HEAD_DIM = 64
SSD_GROUPS = 2
SSD_STATE = 128
SSD_CONV_W = 5
SSD_INNER = SSD_HEADS * SSD_HEAD_DIM
SSD_CONV_CH = SSD_INNER + 2 * SSD_GROUPS * SSD_STATE
MLP_HIDDEN = 4 * D_MODEL
ROPE_BASE = 10000.0
NORM_EPS = 1e-6
CHUNK = 64
SSD_CHUNK = 256

LANES = 128
HALO = 16
LOG2E = 1.4426950408889634
VMEM_LIMIT = 56 * 1024 * 1024


def _cparams(sem):
    return pltpu.CompilerParams(dimension_semantics=sem, vmem_limit_bytes=VMEM_LIMIT)


def _rms(x):
    return x * lax.rsqrt(jnp.mean(x * x, axis=-1, keepdims=True) + NORM_EPS)


def _modulate(x, g, shift, scale):
    return (_rms(x) * g) * (1.0 + scale) + shift


def _silu(x):
    return x * jax.nn.sigmoid(x)


def _dot_nt(a, b):
    return lax.dot_general(a, b, (((1,), (1,)), ((), ())), preferred_element_type=F32)


def _dot_tn(a, b):
    return lax.dot_general(a, b, (((0,), (0,)), ((), ())), preferred_element_type=F32)


def _split2(x):
    hi = x.astype(BF16)
    lo = (x - hi.astype(F32)).astype(BF16)
    return hi, lo


def _exact_left(m01, x):
    hi, lo = _split2(x)
    return (jnp.dot(m01, hi, preferred_element_type=F32) + jnp.dot(m01, lo, preferred_element_type=F32))


def _exact_right(x, m01):
    hi, lo = _split2(x)
    return (jnp.dot(hi, m01, preferred_element_type=F32) + jnp.dot(lo, m01, preferred_element_type=F32))


def _mod_kernel(c_ref, w_ref, b_ref, o_ref):
    a = _silu(c_ref[...]).astype(BF16)
    o_ref[...] = jnp.dot(a, w_ref[...].astype(BF16), preferred_element_type=F32) + b_ref[...]


def _mod_vectors(cc, w_mod, b_mod):
    depth, d, n = w_mod.shape
    tn = 1024
    return pl.pallas_call(
        _mod_kernel,
        grid=(depth, n // tn),
        in_specs=[
            pl.BlockSpec((8, d), lambda l, j: (0, 0)),
            pl.BlockSpec((None, d, tn), lambda l, j: (l, 0, j)),
            pl.BlockSpec((None, 1, tn), lambda l, j: (l, 0, j)),
        ],
        out_specs=pl.BlockSpec((None, 8, tn), lambda l, j: (l, 0, j)),
        out_shape=jax.ShapeDtypeStruct((depth, 8, n), F32),
        compiler_params=_cparams(("arbitrary", "arbitrary")),
        name="mod_vectors",
    )(cc, w_mod, b_mod.reshape(depth, 1, n))


def _rope_partner(x, half):
    n = x.shape[-1]
    lane = lax.broadcasted_iota(jnp.int32, x.shape, x.ndim - 1)
    up = pltpu.roll(x, n - half, x.ndim - 1)
    dn = pltpu.roll(x, half, x.ndim - 1)
    return jnp.where((lane % (2 * half)) < half, up, dn)


def _tile_lanes(x, n):
    return jnp.concatenate([x] * n, axis=-1)


def _proj0_kernel(use_rope, *refs):
    if use_rope:
        (h_ref, mod_ref, g_ref, win_ref, wva_ref, qn_ref, wuq_ref, kvn_ref, wk_ref, wv_ref,
         ca_ref, sa_ref, cq_ref, sq_ref,
         qa_ref, ka_ref, va_ref, qm_ref, km_ref, vm_ref) = refs
    else:
        (h_ref, mod_ref, g_ref, win_ref, wva_ref, qn_ref, wuq_ref, kvn_ref, wk_ref, wv_ref,
         qa_ref, ka_ref, va_ref, qm_ref, km_ref, vm_ref) = refs
    mod = mod_ref[...]
    ub = _modulate(h_ref[...], g_ref[...], mod[0:1], mod[1:2]).astype(BF16)

    def proj(lo, hi):
        return jnp.dot(ub, win_ref[:, lo:hi], preferred_element_type=F32)

    qa = proj(0, 512) * (DA_HEAD_DIM ** -0.5 * LOG2E)
    ka = proj(512, 1024)
    va_ref[...] = _dot_nt(wva_ref[...], ub).astype(BF16)
    cq = proj(1024, 1408)
    ckv = proj(1408, 1664)
    kr = proj(1664, 1792)
    cqn = (_rms(cq) * qn_ref[...]).astype(BF16)
    qm = jnp.dot(cqn, wuq_ref[...], preferred_element_type=F32) * ((MLA_NOPE + MLA_ROPE) ** -0.5 * LOG2E)
    ckvn = (_rms(ckv) * kvn_ref[...]).astype(BF16)
    kn = jnp.dot(ckvn, wk_ref[...], preferred_element_type=F32)
    vm_ref[...] = _dot_nt(wv_ref[...], ckvn).astype(BF16)
    if use_rope:
        ca = _tile_lanes(ca_ref[...], DA_HEADS)
        sa = _tile_lanes(sa_ref[...], DA_HEADS)
        qa = qa * ca + _rope_partner(qa, DA_HEAD_DIM // 2) * sa
        ka = ka * ca + _rope_partner(ka, DA_HEAD_DIM // 2) * sa
        cq1 = cq_ref[...]
        sq1 = sq_ref[...]
        qm = qm * _tile_lanes(cq1, MLA_HEADS) + _rope_partner(qm, MLA_ROPE // 2) * _tile_lanes(sq1, MLA_HEADS)
        kr = kr * cq1 + _rope_partner(kr, MLA_ROPE // 2) * sq1
    qa_ref[...] = qa.astype(BF16)
    ka_ref[...] = ka.astype(BF16)
    qm_ref[...] = qm.astype(BF16)
    km_ref[...] = (kn + _tile_lanes(kr, MLA_HEADS)).astype(BF16)


def _proj0(h, mods, gain, w, tables, tr):
    g_, r_, d = h.shape
    use_rope = tables is not None
    row = lambda c: pl.BlockSpec((None, tr, c), lambda b, t: (b, t, 0))
    full = lambda a: pl.BlockSpec(a.shape, lambda b, t: (0,) * a.ndim)
    in_specs = [row(d), pl.BlockSpec((None, 8, d), lambda b, t: (b, 0, 0)), full(gain),
                full(w["win"]), full(w["wva"]), full(w["qn"]), full(w["wuq"]), full(w["kvn"]), full(w["wk"]),
                full(w["wv"])]
    args = [h, mods, gain, w["win"], w["wva"], w["qn"], w["wuq"], w["kvn"], w["wk"], w["wv"]]
    if use_rope:
        in_specs += [pl.BlockSpec((tr, LANES), lambda b, t: (t, 0))] * 4
        args += list(tables)
    col = pl.BlockSpec((None, 512, tr), lambda b, t: (b, 0, t))
    widths = (512, 512, None, 1024, 1024, None)
    return pl.pallas_call(
        functools.partial(_proj0_kernel, use_rope),
        grid=(g_, r_ // tr),
        in_specs=in_specs,
        out_specs=[col if c is None else row(c) for c in widths],
        out_shape=[jax.ShapeDtypeStruct((g_, 512, r_) if c is None else (g_, r_, c), BF16) for c in widths],
        compiler_params=_cparams(("arbitrary", "arbitrary")),
        name="proj0_rope" if use_rope else "proj0_ctx",
    )(*args)


def _attn_kernel(mode, has_lat, tk, nk, lam_init, *refs):
    refs = list(refs)
    q_ref, kc_ref, vc_ref = refs[:3]
    refs = refs[3:]
    if has_lat:
        kl_ref, vl_ref = refs[:2]
        refs = refs[2:]
    if mode == "da":
        lam_ref, sub_ref = refs[:2]
        refs = refs[2:]
    o_ref, ma_ref, la_ref, acca_ref, mb_ref, lb_ref, accb_ref = refs[:7]
    if has_lat:
        s_ref, mx_ref = refs[7:]

    q = q_ref[...]
    if mode == "da":
        lane = lax.broadcasted_iota(jnp.int32, q.shape, 1)
        zero = jnp.zeros_like(q)
        qs = (jnp.where(lane < DA_HEAD_DIM, q, zero), jnp.where(lane >= DA_HEAD_DIM, q, zero))
    else:
        qs = (q[:, :LANES], q[:, LANES:])
    stats = ((ma_ref, la_ref, acca_ref), (mb_ref, lb_ref, accb_ref))
    for m_ref, l_ref, acc_ref in stats:
        m_ref[...] = jnp.full(m_ref.shape, -jnp.inf, F32)
        l_ref[...] = jnp.zeros(l_ref.shape, F32)
        acc_ref[...] = jnp.zeros(acc_ref.shape, F32)

    def scores(k):
        ks = (k, k) if mode == "da" else (k[:, :LANES], k[:, LANES:])
        return [_dot_nt(ks[i], qs[i]) for i in range(2)]

    def softmax(i, s):
        m_ref, l_ref, _ = stats[i]
        m_prev = m_ref[...]
        m_new = jnp.maximum(m_prev, jnp.max(s, axis=0, keepdims=True))
        alpha = jnp.exp2(m_prev - m_new)
        p = jnp.exp2(s - m_new)
        l_ref[...] = alpha * l_ref[...] + jnp.sum(p, axis=0, keepdims=True)
        m_ref[...] = m_new
        return p.astype(BF16), alpha

    def accumulate(i, vt, p, alpha):
        acc_ref = stats[i][2]
        acc_ref[...] = alpha * acc_ref[...] + jnp.dot(vt, p, preferred_element_type=F32)

    if not has_lat:
        ss = scores(kc_ref[...])
        pa = [softmax(i, ss[i]) for i in range(2)]
        for i in range(2):
            accumulate(i, vc_ref[...], *pa[i])
    else:
        nc = kc_ref.shape[0]

        def chunk(c):
            return pl.ds(c * tk if isinstance(c, int) else pl.multiple_of(c * tk, tk), tk)

        def keys(st):
            return kc_ref[...] if isinstance(st, int) and st == 0 else kl_ref[chunk(st - 1), :]

        def vals(st):
            return vc_ref[...] if isinstance(st, int) and st == 0 else vl_ref[:, chunk(st - 1)]

        def lookahead(st, slot, n):
            ss = scores(keys(st))
            for i in range(2):
                s_ref[slot, i, 0:n, :] = ss[i]
                mx_ref[1, i] = jnp.maximum(mx_ref[0, i], jnp.max(ss[i], axis=0, keepdims=True))

        def consume(st, slot, n):
            vt = vals(st)
            for i in range(2):
                m_ref, l_ref, acc_ref = stats[i]
                m_cur = mx_ref[0, i]
                alpha = jnp.exp2(m_ref[...] - m_cur)
                p = jnp.exp2(s_ref[slot, i, 0:n, :] - m_cur)
                l_ref[...] = alpha * l_ref[...] + jnp.sum(p, axis=0, keepdims=True)
                acc_ref[...] = alpha * acc_ref[...] + jnp.dot(vt, p.astype(BF16), preferred_element_type=F32)
                m_ref[...] = m_cur

        def advance():
            for i in range(2):
                mx_ref[0, i] = mx_ref[1, i]

        def run(st, slot, has_next):
            n = nc if isinstance(st, int) and st == 0 else tk
            if has_next:
                lookahead(st + 1, 1 - slot, tk)
            consume(st, slot, n)
            if has_next:
                advance()

        for i in range(2):
            mx_ref[0, i] = jnp.full((1, mx_ref.shape[-1]), -jnp.inf, F32)
        lookahead(0, 0, nc)
        advance()
        run(0, 0, True)
        npair = (nk - 1) // 2

        def body(j, carry):
            run(2 * j + 1, 1, True)
            run(2 * j + 2, 0, True)
            return carry

        lax.fori_loop(0, npair, body, 0)
        for st in range(2 * npair + 1, nk + 1):
            run(st, st % 2, st < nk)

    oa = acca_ref[...] / la_ref[...]
    ob = accb_ref[...] / lb_ref[...]
    if mode == "da":
        lp = lam_ref[...]
        lam = (jnp.exp(jnp.sum(lp[0:1] * lp[1:2], axis=1, keepdims=True))
               - jnp.exp(jnp.sum(lp[2:3] * lp[3:4], axis=1, keepdims=True)) + lam_init)
        dlt = (oa - lam * ob).T
        o_ref[...] = ((_rms(dlt) * sub_ref[...]) * (1.0 - lam_init)).astype(o_ref.dtype)
    else:
        chan = lax.broadcasted_iota(jnp.int32, oa.shape, 0)
        o_ref[...] = jnp.where(chan < MLA_V, oa, ob).T.astype(o_ref.dtype)


def _attention(mode, q, kc, vc, kl, vl, extra, tq, tk, lam_init):
    b_, sq, _ = q.shape
    has_lat = kl is not None
    w = LANES if mode == "da" else 2 * LANES
    nh = 4
    kblk = lambda a: pl.BlockSpec((None, a.shape[1], w), lambda b, h, t: (b, 0, h))
    vblk = lambda a: pl.BlockSpec((None, LANES, a.shape[2]), lambda b, h, t: (b, h, 0))
    in_specs = [pl.BlockSpec((None, tq, w), lambda b, h, t: (b, t, h)), kblk(kc), vblk(vc)]
    args = [q, kc, vc]
    nk = 0
    if has_lat:
        in_specs += [kblk(kl), vblk(vl)]
        args += [kl, vl]
        nk = kl.shape[1] // tk
    if mode == "da":
        lam_p, subnorm = extra
        in_specs += [pl.BlockSpec(lam_p.shape, lambda b, h, t: (0, 0)),
                     pl.BlockSpec(subnorm.shape, lambda b, h, t: (0, 0))]
        args += [lam_p, subnorm]
    scratch = []
    for _ in range(2):
        scratch += [pltpu.VMEM((1, tq), F32), pltpu.VMEM((1, tq), F32), pltpu.VMEM((LANES, tq), F32)]
    if has_lat:
        scratch += [pltpu.VMEM((2, 2, tk, tq), F32), pltpu.VMEM((2, 2, 1, tq), F32)]
    return pl.pallas_call(
        functools.partial(_attn_kernel, mode, has_lat, tk, nk, lam_init),
        grid=(b_, nh, sq // tq),
        in_specs=in_specs,
        out_specs=pl.BlockSpec((None, tq, LANES), lambda b, h, t: (b, t, h)),
        out_shape=jax.ShapeDtypeStruct((b_, sq, nh * LANES), BF16),
        scratch_shapes=scratch,
        compiler_params=_cparams(("arbitrary", "arbitrary", "arbitrary")),
        name=f"attn_{mode}_{'lat' if has_lat else 'ctx'}",
    )(*args)


def _merge_attention(ya_ref, ym_ref, w_ref):
    return (jnp.dot(ya_ref[...], w_ref[0], preferred_element_type=F32)
            + jnp.dot(ym_ref[...], w_ref[1], preferred_element_type=F32))


def _merge_recurrent(of_ref, ob_ref, yf_ref, yb_ref, sg_ref, sz_ref, xs_ref, on_ref, sk_ref, sn_ref, w_ref):
    o = of_ref[...].astype(F32) + ob_ref[...].astype(F32)
    on = on_ref[...]
    parts = []
    for hh in range(HGRN_HEADS):
        ls = slice(hh * LANES, (hh + 1) * LANES)
        parts.append(_rms(o[:, ls]) * on[:, ls])
    o = jnp.concatenate(parts, axis=1) * sg_ref[...].astype(F32)
    y = ((yf_ref[...].astype(F32) + yb_ref[...].astype(F32) + sk_ref[...] * xs_ref[...])
         * sz_ref[...].astype(F32))
    sn = sn_ref[...]
    gw = SSD_INNER // SSD_GROUPS
    parts = []
    for gg in range(SSD_GROUPS):
        ls = slice(gg * gw, (gg + 1) * gw)
        parts.append(_rms(y[:, ls]) * sn[:, ls])
    y = jnp.concatenate(parts, axis=1)
    return (jnp.dot(o.astype(BF16), w_ref[0], preferred_element_type=F32)
            + jnp.dot(y.astype(BF16), w_ref[1], preferred_element_type=F32))


def _mix_mlp_kernel(merge, n_mix, final, *refs):
    mix_refs = refs[:n_mix]
    if final:
        h_ref, mod_ref, g_ref, w1_ref, w2_ref, fg_ref, o_ref = refs[n_mix:]
    else:
        h_ref, mod_ref, g_ref, w1_ref, w2_ref, o_ref = refs[n_mix:]
    h1 = h_ref[...] + mod_ref[2:3, :] * merge(*mix_refs)
    u = _modulate(h1, g_ref[...], mod_ref[3:4, :], mod_ref[4:5, :]).astype(BF16)
    a = jnp.dot(u, w1_ref[...], preferred_element_type=F32)
    a = jnp.square(jnp.maximum(a, 0.0)).astype(BF16)
    out = h1 + mod_ref[5:6, :] * jnp.dot(a, w2_ref[...], preferred_element_type=F32)
    if final:
        out = _rms(out) * fg_ref[...]
    o_ref[...] = out


def _mix_mlp(merge, mix_rows, mix_full, h, mods, gain, w1, w2, final_gain, tr):
    g_, r_, d = h.shape
    final = final_gain is not None
    row = lambda c: pl.BlockSpec((None, tr, c), lambda b, t: (b, t, 0))
    full = lambda a: pl.BlockSpec(a.shape, lambda b, t: (0,) * a.ndim)
    resident = lambda a: pl.BlockSpec(a.shape, lambda b, t: (0,) * a.ndim, pipeline_mode=pl.Buffered(1))
    in_specs = ([row(a.shape[-1]) for a in mix_rows] + [full(a) for a in mix_full]
                + [row(d), pl.BlockSpec((None, 8, d), lambda b, t: (b, 0, 0)), full(gain),
                   resident(w1), resident(w2)])
    args = list(mix_rows) + list(mix_full) + [h, mods, gain, w1, w2]
    if final:
        in_specs.append(full(final_gain))
        args.append(final_gain)
    return pl.pallas_call(
        functools.partial(_mix_mlp_kernel, merge, len(mix_rows) + len(mix_full), final),
        grid=(g_, r_ // tr),
        in_specs=in_specs,
        out_specs=row(d),
        out_shape=jax.ShapeDtypeStruct(h.shape, F32),
        compiler_params=_cparams(("arbitrary", "arbitrary")),
        name="mix_mlp_final" if final else "mix_mlp",
    )(*args)


def _softplus(x):
    return jnp.maximum(x, 0.0) + jnp.log1p(jnp.exp(-jnp.abs(x)))


def _proj1_kernel(nt, h_ref, hp_ref, hn_ref, mod_ref, g_ref, w_ref, wdt_ref, bl_ref, dtb_ref, cw_ref, cb_ref,
                  q_ref, gf_ref, gb_ref, i_ref, sg_ref, sz_ref, xs_ref, bm_ref, cm_ref, dt_ref, dtt_ref, pad_ref):
    t = pl.program_id(1)
    tr = h_ref.shape[0]
    mod = mod_ref[...]
    gain = g_ref[...]
    ub = _modulate(h_ref[...], gain, mod[0:1], mod[1:2]).astype(BF16)
    u_ext = jnp.concatenate([_modulate(hp_ref[...], gain, mod[0:1], mod[1:2]).astype(BF16), ub,
                             _modulate(hn_ref[...], gain, mod[0:1], mod[1:2]).astype(BF16)], axis=0)
    xbc = jnp.dot(u_ext, w_ref[:, 3072:4096], preferred_element_type=F32)
    ri = lax.broadcasted_iota(jnp.int32, (tr + 2 * HALO, 1), 0)
    inside = ((ri >= HALO) | (t > 0)) & ((ri < tr + HALO) | (t < nt - 1))
    pad_ref[...] = jnp.where(inside, xbc, 0.0)
    half = SSD_CONV_W // 2
    y = jnp.zeros((tr, xbc.shape[1]), F32) + cb_ref[...]
    for j in range(SSD_CONV_W):
        y = y + pad_ref[HALO - half + j:HALO - half + j + tr, :] * cw_ref[j:j + 1, :]
    y = _silu(y)
    xs_ref[...] = y[:, :SSD_INNER]
    bm_ref[...] = y[:, SSD_INNER:SSD_INNER + 256].astype(BF16)
    cm_ref[...] = y[:, SSD_INNER + 256:].astype(BF16)

    def proj(lo, hi):
        return jnp.dot(ub, w_ref[:, lo:hi], preferred_element_type=F32)

    bl = bl_ref[...]
    e = jnp.exp(bl - jnp.max(bl, axis=0, keepdims=True))
    gamma = e / jnp.sum(e, axis=0, keepdims=True)
    lb = (gamma[0:1] + gamma[1:2]) - gamma[0:1]
    q_ref[...] = _silu(proj(0, 512)).astype(BF16)
    for k, out in ((0, gf_ref), (1, gb_ref)):
        lbk = lb[:, 512 * k:512 * (k + 1)]
        f = lbk + (1.0 - lbk) * jax.nn.sigmoid(proj(512 * (k + 1), 512 * (k + 2)))
        out[...] = jnp.log(f)
    i_ref[...] = proj(1536, 2048).astype(BF16)
    sg_ref[...] = _silu(proj(2048, 2560)).astype(BF16)
    sz_ref[...] = _silu(proj(2560, 3072)).astype(BF16)
    dt =_softplus(jnp.dot(ub, wdt_ref[...], preferred_element_type=F32) + dtb_ref[...])
    dt_ref[...] = dt
    dtt_ref[...] = jnp.concatenate([dt[:, :LANES].T[:SSD_HEADS], dt[:, LANES:].T[:SSD_HEADS]], axis=0)


def _proj1(h, mods, gain, w, conv_w, conv_b, tr):
    g_, r_, d = h.shape
    nt = r_ // tr
    nb = tr // HALO
    row = lambda c: pl.BlockSpec((None, tr, c), lambda b, t: (b, t, 0))
    full = lambda a: pl.BlockSpec(a.shape, lambda b, t: (0,) * a.ndim)
    names = ("w", "wdt", "bl", "dtb")
    widths = (512, 512, 512, 512, 512, 512, 512, 256, 256, 2 * LANES)
    dts = (BF16, F32, F32, BF16, BF16, BF16, F32, BF16, BF16, F32)
    return pl.pallas_call(
        functools.partial(_proj1_kernel, nt),
        grid=(g_, nt),
        in_specs=[row(d),
                  pl.BlockSpec((None, HALO, d), lambda b, t: (b, jnp.maximum(t * nb - 1, 0), 0)),
                  pl.BlockSpec((None, HALO, d), lambda b, t: (b, jnp.minimum((t + 1) * nb, nt * nb - 1), 0)),
                  pl.BlockSpec((None, 8, d), lambda b, t: (b, 0, 0)), full(gain)]
        + [full(w[n]) for n in names] + [full(conv_w), full(conv_b)],
        out_specs=[row(c) for c in widths] + [pl.BlockSpec((None, 16, tr), lambda b, t: (b, 0, t))],
        out_shape=[jax.ShapeDtypeStruct((g_, r_, c), t_) for c, t_ in zip(widths, dts)]
        + [jax.ShapeDtypeStruct((g_, 16, r_), F32)],
        scratch_shapes=[pltpu.VMEM((tr + 2 * HALO, SSD_CONV_CH), F32)],
        compiler_params=_cparams(("arbitrary", "arbitrary")),
        name="proj1",
    )(h, h, h, mods, gain, *[w[n] for n in names], conv_w, conv_b)


def _tri(rev, n):
    i = np.arange(n)
    m = i[None, :] >= i[:, None] if rev else i[None, :] <= i[:, None]
    return m


def _ssd_consts(rev, nchunk):
    tri = np.kron(np.eye(nchunk), _tri(rev, SSD_CHUNK).astype(np.float32))
    expand = np.zeros((LANES, SSD_INNER), np.float32)
    for h in range(SSD_HEADS):
        expand[h, h * SSD_HEAD_DIM:(h + 1) * SSD_HEAD_DIM] = 1.0
    return jnp.asarray(tri, BF16), jnp.asarray(tri.T.copy(), BF16), jnp.asarray(expand, BF16)


def _ssd_block_decays(rev, dt_ref, dtt_ref, an, ant, tri, trit):
    dir_ = 1 if rev else 0
    dt = dt_ref[:, dir_ * LANES:(dir_ + 1) * LANES]
    dtt = dtt_ref[8 * dir_:8 * dir_ + 8, :]
    acol = _exact_left(tri, dt * an)
    arow = _exact_right(dtt * ant, trit)
    return dt, acol, arow


def _expand_heads(vals, ex):
    rows = vals[0].shape[0]
    parts = [p for v in vals for p in _split2(v)]
    big = jnp.dot(jnp.concatenate(parts, axis=0), ex, preferred_element_type=F32)
    return [big[2 * i * rows:(2 * i + 1) * rows] + big[(2 * i + 1) * rows:(2 * i + 2) * rows]
            for i in range(len(vals))]


def _ssd_chunk(rev, r0, x_ref, bm_ref, cm_ref, decays, h_ref, y_ref):
    n = SSD_CHUNK
    ti = lax.broadcasted_iota(jnp.int32, (n, n), 0)
    si = lax.broadcasted_iota(jnp.int32, (n, n), 1)
    mask = (si >= ti) if rev else (si <= ti)
    lane = lax.broadcasted_iota(jnp.int32, (n, LANES), 1)
    last = 0 if rev else n - 1
    x = x_ref[r0:r0 + n, :]
    bm = bm_ref[r0:r0 + n, :]
    cm = cm_ref[r0:r0 + n, :]
    acol = decays[0][r0:r0 + n, :]
    arow = decays[1][:, r0:r0 + n]
    dte = decays[2][r0:r0 + n, :]
    ae = decays[3][r0:r0 + n, :]
    ae_last = ae[last:last + 1, :]
    xdt = x * dte
    xw = (xdt * jnp.exp(ae_last - ae)).astype(BF16)
    xdtb = xdt.astype(BF16)
    ys = []
    for g in range(SSD_GROUPS):
        bg = bm[:, g * SSD_STATE:(g + 1) * SSD_STATE]
        cg = cm[:, g * SSD_STATE:(g + 1) * SSD_STATE]
        cb = _dot_nt(cg, bg)
        hs = h_ref[:, g * 256:(g + 1) * 256]
        yoff = jnp.dot(cg, hs.astype(BF16), preferred_element_type=F32) * jnp.exp(ae[:, g * 256:(g + 1) * 256])
        for pr in range(2):
            xp = xdtb[:, g * 256 + pr * LANES:g * 256 + (pr + 1) * LANES]
            res = []
            for k in range(2):
                hd = g * 4 + pr * 2 + k
                dmat = acol[:, hd:hd + 1] - arow[hd:hd + 1, :]
                lmat = jnp.exp(jnp.where(mask, dmat, -jnp.inf))
                res.append(jnp.dot((cb * lmat).astype(BF16), xp, preferred_element_type=F32))
            ys.append(jnp.where(lane < SSD_HEAD_DIM, res[0], res[1]) + yoff[:, pr * LANES:(pr + 1) * LANES])
        h_ref[:, g * 256:(g + 1) * 256] = (hs * jnp.exp(ae_last[:, g * 256:(g + 1) * 256])
                                           + _dot_tn(bg, xw[:, g * 256:(g + 1) * 256]))
    y_ref[r0:r0 + n, :] = jnp.concatenate(ys, axis=1).astype(y_ref.dtype)


def _ssd_kernel(nchunk, xf_ref, bmf_ref, cmf_ref, dtf_ref, dttf_ref, xb_ref, bmb_ref, cmb_ref, dtb_ref, dttb_ref,
                an_ref, ant_ref, tri_ref, trit_ref, ex_ref, h0f_ref, h0b_ref,
                yf_ref, hTf_ref, yb_ref, hTb_ref, hf_ref, hb_ref):
    s = pl.program_id(1)

    @pl.when(s == 0)
    def _():
        hf_ref[...] = h0f_ref[...]
        hb_ref[...] = h0b_ref[...]

    ex = ex_ref[...]
    dt_f, acol_f, arow_f = _ssd_block_decays(False, dtf_ref, dttf_ref, an_ref[0], ant_ref[0], tri_ref[0], trit_ref[0])
    dt_b, acol_b, arow_b = _ssd_block_decays(True, dtb_ref, dttb_ref, an_ref[1], ant_ref[1], tri_ref[1], trit_ref[1])
    dte_f, ae_f, dte_b, ae_b = _expand_heads([dt_f, acol_f, dt_b, acol_b], ex)
    dec_f = (acol_f, arow_f, dte_f, ae_f)
    dec_b = (acol_b, arow_b, dte_b, ae_b)
    for j in range(nchunk):
        _ssd_chunk(False, j * SSD_CHUNK, xf_ref, bmf_ref, cmf_ref, dec_f, hf_ref, yf_ref)
        _ssd_chunk(True, (nchunk - 1 - j) * SSD_CHUNK, xb_ref, bmb_ref, cmb_ref, dec_b, hb_ref, yb_ref)

    @pl.when(s == pl.num_programs(1) - 1)
    def _():
        hTf_ref[...] = hf_ref[...]
        hTb_ref[...] = hb_ref[...]


def _ssd_scan(xs, bm, cm, dt, dtt, a_log, h0f, h0b, rb):
    b_, r_, _ = xs.shape
    nblk = r_ // rb
    trif, tritf, ex = _ssd_consts(False, rb // SSD_CHUNK)
    trib, tritb, _ = _ssd_consts(True, rb // SSD_CHUNK)
    tri = jnp.stack([trif, trib])
    trit = jnp.stack([tritf, tritb])
    a_neg = -jnp.exp(a_log.astype(F32))
    an = jnp.zeros((2, 1, LANES), F32).at[:, 0, :SSD_HEADS].set(a_neg)
    ant = a_neg.reshape(2, SSD_HEADS, 1)
    rowf = lambda c: pl.BlockSpec((None, rb, c), lambda b, s: (b, s, 0))
    rowb = lambda c: pl.BlockSpec((None, rb, c), lambda b, s: (b, nblk - 1 - s, 0))
    full = lambda a: pl.BlockSpec(a.shape, lambda b, s: (0,) * a.ndim)
    st = pl.BlockSpec((None, SSD_STATE, SSD_INNER), lambda b, s: (b, 0, 0))
    y_shape = jax.ShapeDtypeStruct((b_, r_, SSD_INNER), BF16)
    h_shape = jax.ShapeDtypeStruct((b_, SSD_STATE, SSD_INNER), F32)
    return pl.pallas_call(
        functools.partial(_ssd_kernel, rb // SSD_CHUNK),
        grid=(b_, nblk),
        in_specs=[rowf(512), rowf(256), rowf(256), rowf(2 * LANES),
                  pl.BlockSpec((None, 16, rb), lambda b, s: (b, 0, s)),
                  rowb(512), rowb(256), rowb(256), rowb(2 * LANES),
                  pl.BlockSpec((None, 16, rb), lambda b, s: (b, 0, nblk - 1 - s)),
                  full(an), full(ant), full(tri), full(trit), full(ex), st, st],
        out_specs=[rowf(512), st, rowb(512), st],
        out_shape=[y_shape, h_shape, y_shape, h_shape],
        scratch_shapes=[pltpu.VMEM((SSD_STATE, SSD_INNER), F32)] * 2,
        compiler_params=_cparams(("arbitrary", "arbitrary")),
        name="ssd_scan",
    )(xs, bm, cm, dt, dtt, xs, bm, cm, dt, dtt, an, ant, tri, trit, ex, h0f, h0b)


_HGRN_LEVELS = (64, 32, 16, 8, 4, 2, 1)


def _hgrn_consts(rev):
    i = np.arange(CHUNK)
    before = _tri(rev, CHUNK)
    after_strict = ~before
    mats, masks = [], []
    for c in _HGRN_LEVELS:
        same = (i[:, None] // c) == (i[None, :] // c)
        q_side = same & before
        k_side = same & after_strict
        if c == CHUNK:
            mats += [q_side, k_side]
        else:
            blk = i // c
            first = (blk % 2 == 1) if rev else (blk % 2 == 0)
            if c > 1:
                mats.append(np.where(first[:, None], k_side, q_side))
            pair = first[None, :] & (~first)[:, None] & ((blk[:, None] // 2) == (blk[None, :] // 2))
            masks.append(pair)
    masks.append(np.eye(CHUNK, dtype=bool))
    m = np.concatenate(mats, axis=0).astype(np.float32)
    m = np.concatenate([m, m], axis=1)
    mk = np.stack(masks, axis=0).astype(np.float32)
    return jnp.asarray(m, BF16), jnp.asarray(mk, F32)


def _hgrn_chunk(rev, r0, q_ref, g_ref, v_ref, mall, mk_ref, s_ref, o_ref):
    nl = len(_HGRN_LEVELS)
    last = 0 if rev else CHUNK - 1
    g = g_ref[r0:r0 + CHUNK, :]
    g2 = jnp.concatenate(_split2(g), axis=0)
    eall = jnp.exp(jnp.dot(mall, g2, preferred_element_type=F32))
    ff = jnp.exp(g)
    kk = 1.0 - ff
    qq = q_ref[r0:r0 + CHUNK, :].astype(F32)
    vv = v_ref[r0:r0 + CHUNK, :]
    outs = []
    for h in range(HGRN_HEADS):
        ls = slice(h * LANES, (h + 1) * LANES)
        qh, kh, vh = qq[:, ls], kk[:, ls], vv[:, ls]
        khb = kh.astype(BF16)

        def fac(idx, ls=ls):
            return eall[idx * CHUNK:(idx + 1) * CHUNK, ls]

        q01 = jnp.concatenate([qh, qh * ff[:, ls]], axis=0).astype(BF16)
        a01 = _dot_nt(q01, khb)
        att = a01[:CHUNK] * mk_ref[nl - 1] + a01[CHUNK:] * mk_ref[nl - 2]
        for l in range(1, nl - 1):
            e = fac(l + 1)
            att = att + _dot_nt((qh * e).astype(BF16), (kh * e).astype(BF16)) * mk_ref[l - 1]
        st = s_ref[h]
        qs_ = (qh * fac(0)).astype(BF16)
        o = _dot_nt(qs_, st.astype(BF16)) + jnp.dot(att.astype(BF16), vh, preferred_element_type=F32)
        outs.append(o)
        ks_ = (kh * fac(1)).astype(BF16)
        dec = fac(0)[last:last + 1, :]
        s_ref[h] = st * dec + _dot_tn(vh, ks_)
    o_ref[r0:r0 + CHUNK, :] = jnp.concatenate(outs, axis=1).astype(o_ref.dtype)


def _hgrn_kernel(nchunk, qf_ref, gf_ref, vf_ref, qb_ref, gb_ref, vb_ref, mf_ref, mkf_ref, mb_ref, mkb_ref,
                 s0f_ref, s0b_ref, of_ref, sTf_ref, ob_ref, sTb_ref, sf_ref, sb_ref):
    s = pl.program_id(1)

    @pl.when(s == 0)
    def _():
        sf_ref[...] = s0f_ref[...]
        sb_ref[...] = s0b_ref[...]

    mallf = mf_ref[...]
    mallb = mb_ref[...]
    for j in range(nchunk):
        _hgrn_chunk(False, j * CHUNK, qf_ref, gf_ref, vf_ref, mallf, mkf_ref, sf_ref, of_ref)
        _hgrn_chunk(True, (nchunk - 1 - j) * CHUNK, qb_ref, gb_ref, vb_ref, mallb, mkb_ref, sb_ref, ob_ref)

    @pl.when(s == pl.num_programs(1) - 1)
    def _():
        sTf_ref[...] = sf_ref[...]
        sTb_ref[...] = sb_ref[...]


def _hgrn_scan(q, gf, gb, v, s0f, s0b, rb):
    b_, r_, _ = q.shape
    nblk = r_ // rb
    mallf, mkf = _hgrn_consts(False)
    mallb, mkb = _hgrn_consts(True)
    rowf = pl.BlockSpec((None, rb, 512), lambda b, s: (b, s, 0))
    rowb = pl.BlockSpec((None, rb, 512), lambda b, s: (b, nblk - 1 - s, 0))
    full = lambda a: pl.BlockSpec(a.shape, lambda b, s: (0,) * a.ndim)
    st = pl.BlockSpec((None, HGRN_HEADS, LANES, HGRN_DK), lambda b, s: (b, 0, 0, 0))
    o_shape = jax.ShapeDtypeStruct((b_, r_, 512), BF16)
    s_shape = jax.ShapeDtypeStruct((b_, HGRN_HEADS, LANES, HGRN_DK), F32)
    return pl.pallas_call(
        functools.partial(_hgrn_kernel, rb // CHUNK),
        grid=(b_, nblk),
        in_specs=[rowf, rowf, rowf, rowb, rowb, rowb, full(mallf), full(mkf), full(mallb), full(mkb), st, st],
        out_specs=[rowf, st, rowb, st],
        out_shape=[o_shape, s_shape, o_shape, s_shape],
        scratch_shapes=[pltpu.VMEM((HGRN_HEADS, LANES, HGRN_DK), F32)] * 2,
        compiler_params=_cparams(("arbitrary", "arbitrary")),
        name="hgrn_scan",
    )(q, gf, v, q, gb, v, mallf, mkf, mallb, mkb, s0f, s0b)


def _rope_tables(length):
    rows = length // GRID_W
    row = jnp.repeat(jnp.arange(rows, dtype=F32), GRID_W)
    col = jnp.tile(jnp.arange(GRID_W, dtype=F32), rows)

    def cs(rot_dim):
        n_freq = rot_dim // 4
        inv_freq = ROPE_BASE ** (-jnp.arange(n_freq, dtype=F32) / n_freq)
        ang = jnp.concatenate([row[:, None] * inv_freq, col[:, None] * inv_freq], axis=-1)
        c, s = jnp.cos(ang), jnp.sin(ang)
        return jnp.concatenate([c, c], axis=-1), jnp.concatenate([-s, s], axis=-1)

    ca, sa = cs(DA_HEAD_DIM)
    cq, sq = cs(MLA_ROPE)
    ones = jnp.ones((length, MLA_NOPE), F32)
    pad1 = jnp.ones((length, LANES - MLA_NOPE - MLA_ROPE), F32)
    cqt = jnp.concatenate([ones, cq, pad1], axis=-1)
    sqt = jnp.concatenate([0.0 * ones, sq, 0.0 * pad1], axis=-1)
    return jnp.tile(ca, (1, 2)), jnp.tile(sa, (1, 2)), cqt, sqt


def _att_weights(att_w_in, mla_q_norm, mla_w_uq, mla_kv_norm, mla_w_ukv):
    d = att_w_in.shape[0]
    kr = att_w_in[:, 2176:2208]
    kr_blk = jnp.concatenate([jnp.zeros((d, MLA_NOPE), F32), kr,
                              jnp.zeros((d, LANES - MLA_NOPE - MLA_ROPE), F32)], axis=1)
    win = jnp.concatenate([att_w_in[:, :1024], att_w_in[:, 1536:2176], kr_blk], axis=1).astype(BF16)
    wva = att_w_in[:, 1024:1536].T.astype(BF16)
    wq = mla_w_uq.reshape(MLA_Q_RANK, MLA_HEADS, MLA_NOPE + MLA_ROPE)
    wq = jnp.pad(wq, ((0, 0), (0, 0), (0, LANES - MLA_NOPE - MLA_ROPE))).reshape(MLA_Q_RANK, MLA_HEADS * LANES)
    wkv = mla_w_ukv.reshape(MLA_KV_RANK, MLA_HEADS, MLA_NOPE + MLA_V)
    wk = jnp.pad(wkv[:, :, :MLA_NOPE], ((0, 0), (0, 0), (0, LANES - MLA_NOPE))).reshape(MLA_KV_RANK, MLA_HEADS * LANES)
    wv = wkv[:, :, MLA_NOPE:].reshape(MLA_KV_RANK, MLA_HEADS * MLA_V)
    return dict(win=win, wva=wva, qn=mla_q_norm.reshape(1, -1), wuq=wq.astype(BF16),
                kvn=mla_kv_norm.reshape(1, -1), wk=wk.astype(BF16), wv=wv.T.astype(BF16))


def _rec_weights(rec_w_in, bound_logits, dt_bias):
    d = rec_w_in.shape[0]
    w = rec_w_in[:, :4096].astype(BF16)
    wdt = rec_w_in[:, 4096:4112]
    pad = jnp.zeros((d, LANES - SSD_HEADS), F32)
    wdt_rows = jnp.concatenate([wdt[:, :SSD_HEADS], pad, wdt[:, SSD_HEADS:], pad], axis=1).astype(BF16)
    zb = jnp.zeros((LANES - SSD_HEADS,), F32)
    dtb = jnp.concatenate([dt_bias[0], zb, dt_bias[1], zb]).reshape(1, 2 * LANES)
    return dict(w=w, wdt=wdt_rows, bl=bound_logits, dtb=dtb)


class _Tiles(NamedTuple):
    proj: int
    ctx: int
    att_q: int
    att_k: int
    mlp0: int
    mlp1: int
    scan: int


def _tiles(length, n_ctx):
    cap = lambda n, full: min(n, full)
    return _Tiles(proj=cap(512, length), ctx=cap(256, n_ctx), att_q=cap(2048, length), att_k=cap(512, length),
                  mlp0=cap(1024, length), mlp1=cap(512, length), scan=cap(512, length))


def kernel(x, c, ctx, c_ctx, w_mod, b_mod, norm_mix, norm_mlp, w_mlp_in, w_mlp_out, att_w_in, att_lambda, att_subnorm, mla_q_norm, mla_w_uq, mla_kv_norm, mla_w_ukv, att_w_out, rec_w_in, hgrn_bound_logits, hgrn_out_norm, ssd_conv_w, ssd_conv_b, ssd_a_log, ssd_dt_bias, ssd_skip, ssd_norm, rec_w_out, final_norm):
    b_, length, d = x.shape
    n_ctx = ctx.shape[1]
    assert w_mod.shape[0] == 2 and d == D_MODEL

    cc = jnp.zeros((8, d), F32).at[:b_].set(c).at[b_].set(c_ctx)
    mods = _mod_vectors(cc, w_mod, b_mod)
    mods = jnp.pad(mods.reshape(2, 8, 6, d), ((0, 0), (0, 0), (0, 2), (0, 0)))
    mods_lat = [mods[l, :b_] for l in range(2)]
    mods_ctx = [jnp.broadcast_to(mods[l, b_], (b_, 8, d)) for l in range(2)]
    row = lambda v: v.reshape(1, -1)

    t = _tiles(length, n_ctx)
    tr, trc = t.proj, t.ctx
    w0 = _att_weights(att_w_in[0], mla_q_norm[0], mla_w_uq[0], mla_kv_norm[0], mla_w_ukv[0])
    tables = _rope_tables(length)
    qa, ka, va, qm, km, vm = _proj0(x, mods_lat[0], row(norm_mix[0]), w0, tables, tr)
    qa_c, ka_c, va_c, qm_c, km_c, vm_c = _proj0(ctx, mods_ctx[0], row(norm_mix[0]), w0, None, trc)
    lam_init = 0.8 - 0.6 * math.exp(-0.3 * 0)
    da_extra = (att_lambda[0], row(att_subnorm[0]))
    ya = _attention("da", qa, ka_c, va_c, ka, va, da_extra, t.att_q, t.att_k, lam_init)
    ym = _attention("mla", qm, km_c, vm_c, km, vm, None, t.att_q, t.att_k, lam_init)
    ya_c = _attention("da", qa_c, ka_c, va_c, None, None, da_extra, trc, t.att_k, lam_init)
    ym_c = _attention("mla", qm_c, km_c, vm_c, None, None, None, trc, t.att_k, lam_init)
    wo0 = (att_w_out[0].astype(BF16).reshape(2, 512, d),)
    w1 = [w_mlp_in[l].astype(BF16) for l in range(2)]
    w2 = [w_mlp_out[l].astype(BF16) for l in range(2)]
    h_lat = _mix_mlp(_merge_attention, (ya, ym), wo0, x, mods_lat[0], row(norm_mlp[0]), w1[0], w2[0], None, t.mlp0)
    h_ctx = _mix_mlp(_merge_attention, (ya_c, ym_c), wo0, ctx, mods_ctx[0], row(norm_mlp[0]), w1[0], w2[0], None,
                     trc)

    w1r = _rec_weights(rec_w_in[0], hgrn_bound_logits, ssd_dt_bias[0])
    cw = ssd_conv_w[0]
    cb = row(ssd_conv_b[0])
    q_c, gf_c, gb_c, i_c, _, _, xs_c, bm_c, cm_c, dt_c, dtt_c = _proj1(h_ctx, mods_ctx[1], row(norm_mix[1]), w1r,
                                                                        cw, cb, trc)
    q_l, gf_l, gb_l, i_l, sg_l, sz_l, xs_l, bm_l, cm_l, dt_l, dtt_l = _proj1(h_lat, mods_lat[1], row(norm_mix[1]),
                                                                             w1r, cw, cb, tr)
    rbc, rbl = t.ctx, t.scan
    zs = jnp.zeros((b_, SSD_STATE, SSD_INNER), F32)
    zh = jnp.zeros((b_, HGRN_HEADS, LANES, HGRN_DK), F32)
    _, hs_f, _, hs_b = _ssd_scan(xs_c, bm_c, cm_c, dt_c, dtt_c, ssd_a_log[0], zs, zs, rbc)
    y_f, _, y_b, _ = _ssd_scan(xs_l, bm_l, cm_l, dt_l, dtt_l, ssd_a_log[0], hs_f, hs_b, rbl)
    _, ss_f, _, ss_b = _hgrn_scan(q_c, gf_c, gb_c, i_c, zh, zh, rbc)
    o_f, _, o_b, _ = _hgrn_scan(q_l, gf_l, gb_l, i_l, ss_f, ss_b, rbl)
    skip = row(jnp.repeat(ssd_skip[0], SSD_HEAD_DIM))
    wro = rec_w_out[0].astype(BF16).reshape(2, 512, d)
    return _mix_mlp(_merge_recurrent, (o_f, o_b, y_f, y_b, sg_l, sz_l, xs_l),
                    (row(hgrn_out_norm[0]), skip, row(ssd_norm[0]), wro),
                    h_lat, mods_lat[1], row(norm_mlp[1]), w1[1], w2[1], row(final_norm), t.mlp1)
```

```python
import functools
import math
from typing import NamedTuple

import numpy as np
import jax
import jax.numpy as jnp
from jax import lax
from jax.experimental import pallas as pl
from jax.experimental.pallas import tpu as pltpu

F32 = jnp.float32
BF16 = jnp.bfloat16

D_MODEL = 1024
GRID_W = 64
DA_HEADS = 4
DA_HEAD_DIM = 64
MLA_HEADS = 8
MLA_NOPE = 64
MLA_ROPE = 32
MLA_V = 64
MLA_Q_RANK = 384
MLA_KV_RANK = 256
HGRN_HEADS = 4
HGRN_DK = 128
SSD_HEADS = 8
SSD_HEAD_DIM = 64
SSD_GROUPS = 2
SSD_STATE = 128
SSD_CONV_W = 5
SSD_INNER = SSD_HEADS * SSD_HEAD_DIM
SSD_CONV_CH = SSD_INNER + 2 * SSD_GROUPS * SSD_STATE
MLP_HIDDEN = 4 * D_MODEL
ROPE_BASE = 10000.0
NORM_EPS = 1e-6
CHUNK = 64
SSD_CHUNK = 256

LANES = 128
HALO = 16
LOG2E = 1.4426950408889634
VMEM_LIMIT = 56 * 1024 * 1024


def _cparams(sem):
    return pltpu.CompilerParams(dimension_semantics=sem, vmem_limit_bytes=VMEM_LIMIT)


def _rms(x):
    return x * lax.rsqrt(jnp.mean(x * x, axis=-1, keepdims=True) + NORM_EPS)


def _modulate(x, g, shift, scale):
    return (_rms(x) * g) * (1.0 + scale) + shift


def _silu(x):
    return x * jax.nn.sigmoid(x)


def _dot_nt(a, b):
    return lax.dot_general(a, b, (((1,), (1,)), ((), ())), preferred_element_type=F32)


def _dot_tn(a, b):
    return lax.dot_general(a, b, (((0,), (0,)), ((), ())), preferred_element_type=F32)


def _split2(x):
    hi = x.astype(BF16)
    lo = (x - hi.astype(F32)).astype(BF16)
    return hi, lo


def _exact_left(m01, x):
    hi, lo = _split2(x)
    return (jnp.dot(m01, hi, preferred_element_type=F32) + jnp.dot(m01, lo, preferred_element_type=F32))


def _exact_right(x, m01):
    hi, lo = _split2(x)
    return (jnp.dot(hi, m01, preferred_element_type=F32) + jnp.dot(lo, m01, preferred_element_type=F32))


def _mod_kernel(c_ref, w_ref, b_ref, o_ref):
    a = _silu(c_ref[...]).astype(BF16)
    o_ref[...] = jnp.dot(a, w_ref[...].astype(BF16), preferred_element_type=F32) + b_ref[...]


def _mod_vectors(cc, w_mod, b_mod):
    depth, d, n = w_mod.shape
    tn = 1024
    return pl.pallas_call(
        _mod_kernel,
        grid=(depth, n // tn),
        in_specs=[
            pl.BlockSpec((8, d), lambda l, j: (0, 0)),
            pl.BlockSpec((None, d, tn), lambda l, j: (l, 0, j)),
            pl.BlockSpec((None, 1, tn), lambda l, j: (l, 0, j)),
        ],
        out_specs=pl.BlockSpec((None, 8, tn), lambda l, j: (l, 0, j)),
        out_shape=jax.ShapeDtypeStruct((depth, 8, n), F32),
        compiler_params=_cparams(("arbitrary", "arbitrary")),
        name="mod_vectors",
    )(cc, w_mod, b_mod.reshape(depth, 1, n))


def _rope_partner(x, half):
    n = x.shape[-1]
    lane = lax.broadcasted_iota(jnp.int32, x.shape, x.ndim - 1)
    up = pltpu.roll(x, n - half, x.ndim - 1)
    dn = pltpu.roll(x, half, x.ndim - 1)
    return jnp.where((lane % (2 * half)) < half, up, dn)


def _tile_lanes(x, n):
    return jnp.concatenate([x] * n, axis=-1)


def _proj0_kernel(use_rope, *refs):
    if use_rope:
        (h_ref, mod_ref, g_ref, win_ref, wva_ref, qn_ref, wuq_ref, kvn_ref, wk_ref, wv_ref,
         ca_ref, sa_ref, cq_ref, sq_ref,
         qa_ref, ka_ref, va_ref, qm_ref, km_ref, vm_ref) = refs
    else:
        (h_ref, mod_ref, g_ref, win_ref, wva_ref, qn_ref, wuq_ref, kvn_ref, wk_ref, wv_ref,
         qa_ref, ka_ref, va_ref, qm_ref, km_ref, vm_ref) = refs
    mod = mod_ref[...]
    ub = _modulate(h_ref[...], g_ref[...], mod[0:1], mod[1:2]).astype(BF16)

    def proj(lo, hi):
        return jnp.dot(ub, win_ref[:, lo:hi], preferred_element_type=F32)

    qa = proj(0, 512) * (DA_HEAD_DIM ** -0.5 * LOG2E)
    ka = proj(512, 1024)
    va_ref[...] = _dot_nt(wva_ref[...], ub).astype(BF16)
    cq = proj(1024, 1408)
    ckv = proj(1408, 1664)
    kr = proj(1664, 1792)
    cqn = (_rms(cq) * qn_ref[...]).astype(BF16)
    qm = jnp.dot(cqn, wuq_ref[...], preferred_element_type=F32) * ((MLA_NOPE + MLA_ROPE) ** -0.5 * LOG2E)
    ckvn = (_rms(ckv) * kvn_ref[...]).astype(BF16)
    kn = jnp.dot(ckvn, wk_ref[...], preferred_element_type=F32)
    vm_ref[...] = _dot_nt(wv_ref[...], ckvn).astype(BF16)
    if use_rope:
        ca = _tile_lanes(ca_ref[...], DA_HEADS)
        sa = _tile_lanes(sa_ref[...], DA_HEADS)
        qa = qa * ca + _rope_partner(qa, DA_HEAD_DIM // 2) * sa
        ka = ka * ca + _rope_partner(ka, DA_HEAD_DIM // 2) * sa
        cq1 = cq_ref[...]
        sq1 = sq_ref[...]
        qm = qm * _tile_lanes(cq1, MLA_HEADS) + _rope_partner(qm, MLA_ROPE // 2) * _tile_lanes(sq1, MLA_HEADS)
        kr = kr * cq1 + _rope_partner(kr, MLA_ROPE // 2) * sq1
    qa_ref[...] = qa.astype(BF16)
    ka_ref[...] = ka.astype(BF16)
    qm_ref[...] = qm.astype(BF16)
    km_ref[...] = (kn + _tile_lanes(kr, MLA_HEADS)).astype(BF16)


def _proj0(h, mods, gain, w, tables, tr):
    g_, r_, d = h.shape
    use_rope = tables is not None
    row = lambda c: pl.BlockSpec((None, tr, c), lambda b, t: (b, t, 0))
    full = lambda a: pl.BlockSpec(a.shape, lambda b, t: (0,) * a.ndim)
    in_specs = [row(d), pl.BlockSpec((None, 8, d), lambda b, t: (b, 0, 0)), full(gain),
                full(w["win"]), full(w["wva"]), full(w["qn"]), full(w["wuq"]), full(w["kvn"]), full(w["wk"]),
                full(w["wv"])]
    args = [h, mods, gain, w["win"], w["wva"], w["qn"], w["wuq"], w["kvn"], w["wk"], w["wv"]]
    if use_rope:
        in_specs += [pl.BlockSpec((tr, LANES), lambda b, t: (t, 0))] * 4
        args += list(tables)
    col = pl.BlockSpec((None, 512, tr), lambda b, t: (b, 0, t))
    widths = (512, 512, None, 1024, 1024, None)
    return pl.pallas_call(
        functools.partial(_proj0_kernel, use_rope),
        grid=(g_, r_ // tr),
        in_specs=in_specs,
        out_specs=[col if c is None else row(c) for c in widths],
        out_shape=[jax.ShapeDtypeStruct((g_, 512, r_) if c is None else (g_, r_, c), BF16) for c in widths],
        compiler_params=_cparams(("arbitrary", "arbitrary")),
        name="proj0_rope" if use_rope else "proj0_ctx",
    )(*args)


def _attn_kernel(mode, has_lat, tk, nk, lam_init, *refs):
    refs = list(refs)
    q_ref, kc_ref, vc_ref = refs[:3]
    refs = refs[3:]
    if has_lat:
        kl_ref, vl_ref = refs[:2]
        refs = refs[2:]
    if mode == "da":
        lam_ref, sub_ref = refs[:2]
        refs = refs[2:]
    o_ref, ma_ref, la_ref, acca_ref, mb_ref, lb_ref, accb_ref = refs[:7]
    if has_lat:
        s_ref, mx_ref = refs[7:]

    q = q_ref[...]
    if mode == "da":
        lane = lax.broadcasted_iota(jnp.int32, q.shape, 1)
        zero = jnp.zeros_like(q)
        qs = (jnp.where(lane < DA_HEAD_DIM, q, zero), jnp.where(lane >= DA_HEAD_DIM, q, zero))
    else:
        qs = (q[:, :LANES], q[:, LANES:])
    stats = ((ma_ref, la_ref, acca_ref), (mb_ref, lb_ref, accb_ref))
    for m_ref, l_ref, acc_ref in stats:
        m_ref[...] = jnp.full(m_ref.shape, -jnp.inf, F32)
        l_ref[...] = jnp.zeros(l_ref.shape, F32)
        acc_ref[...] = jnp.zeros(acc_ref.shape, F32)

    def scores(k):
        ks = (k, k) if mode == "da" else (k[:, :LANES], k[:, LANES:])
        return [_dot_nt(ks[i], qs[i]) for i in range(2)]

    def softmax(i, s):
        m_ref, l_ref, _ = stats[i]
        m_prev = m_ref[...]
        m_new = jnp.maximum(m_prev, jnp.max(s, axis=0, keepdims=True))
        alpha = jnp.exp2(m_prev - m_new)
        p = jnp.exp2(s - m_new)
        l_ref[...] = alpha * l_ref[...] + jnp.sum(p, axis=0, keepdims=True)
        m_ref[...] = m_new
        return p.astype(BF16), alpha

    def accumulate(i, vt, p, alpha):
        acc_ref = stats[i][2]
        acc_ref[...] = alpha * acc_ref[...] + jnp.dot(vt, p, preferred_element_type=F32)

    if not has_lat:
        ss = scores(kc_ref[...])
        pa = [softmax(i, ss[i]) for i in range(2)]
        for i in range(2):
            accumulate(i, vc_ref[...], *pa[i])
    else:
        nc = kc_ref.shape[0]

        def chunk(c):
            return pl.ds(c * tk if isinstance(c, int) else pl.multiple_of(c * tk, tk), tk)

        def keys(st):
            return kc_ref[...] if isinstance(st, int) and st == 0 else kl_ref[chunk(st - 1), :]

        def vals(st):
            return vc_ref[...] if isinstance(st, int) and st == 0 else vl_ref[:, chunk(st - 1)]

        def lookahead(st, slot, n):
            ss = scores(keys(st))
            for i in range(2):
                s_ref[slot, i, 0:n, :] = ss[i]
                mx_ref[1, i] = jnp.maximum(mx_ref[0, i], jnp.max(ss[i], axis=0, keepdims=True))

        def consume(st, slot, n):
            vt = vals(st)
            for i in range(2):
                m_ref, l_ref, acc_ref = stats[i]
                m_cur = mx_ref[0, i]
                alpha = jnp.exp2(m_ref[...] - m_cur)
                p = jnp.exp2(s_ref[slot, i, 0:n, :] - m_cur)
                l_ref[...] = alpha * l_ref[...] + jnp.sum(p, axis=0, keepdims=True)
                acc_ref[...] = alpha * acc_ref[...] + jnp.dot(vt, p.astype(BF16), preferred_element_type=F32)
                m_ref[...] = m_cur

        def advance():
            for i in range(2):
                mx_ref[0, i] = mx_ref[1, i]

        def run(st, slot, has_next):
            n = nc if isinstance(st, int) and st == 0 else tk
            if has_next:
                lookahead(st + 1, 1 - slot, tk)
            consume(st, slot, n)
            if has_next:
                advance()

        for i in range(2):
            mx_ref[0, i] = jnp.full((1, mx_ref.shape[-1]), -jnp.inf, F32)
        lookahead(0, 0, nc)
        advance()
        run(0, 0, True)
        npair = (nk - 1) // 2

        def body(j, carry):
            run(2 * j + 1, 1, True)
            run(2 * j + 2, 0, True)
            return carry

        lax.fori_loop(0, npair, body, 0)
        for st in range(2 * npair + 1, nk + 1):
            run(st, st % 2, st < nk)

    oa = acca_ref[...] / la_ref[...]
    ob = accb_ref[...] / lb_ref[...]
    if mode == "da":
        lp = lam_ref[...]
        lam = (jnp.exp(jnp.sum(lp[0:1] * lp[1:2], axis=1, keepdims=True))
               - jnp.exp(jnp.sum(lp[2:3] * lp[3:4], axis=1, keepdims=True)) + lam_init)
        dlt = (oa - lam * ob).T
        o_ref[...] = ((_rms(dlt) * sub_ref[...]) * (1.0 - lam_init)).astype(o_ref.dtype)
    else:
        chan = lax.broadcasted_iota(jnp.int32, oa.shape, 0)
        o_ref[...] = jnp.where(chan < MLA_V, oa, ob).T.astype(o_ref.dtype)


def _attention(mode, q, kc, vc, kl, vl, extra, tq, tk, lam_init):
    b_, sq, _ = q.shape
    has_lat = kl is not None
    w = LANES if mode == "da" else 2 * LANES
    nh = 4
    kblk = lambda a: pl.BlockSpec((None, a.shape[1], w), lambda b, h, t: (b, 0, h))
    vblk = lambda a: pl.BlockSpec((None, LANES, a.shape[2]), lambda b, h, t: (b, h, 0))
    in_specs = [pl.BlockSpec((None, tq, w), lambda b, h, t: (b, t, h)), kblk(kc), vblk(vc)]
    args = [q, kc, vc]
    nk = 0
    if has_lat:
        in_specs += [kblk(kl), vblk(vl)]
        args += [kl, vl]
        nk = kl.shape[1] // tk
    if mode == "da":
        lam_p, subnorm = extra
        in_specs += [pl.BlockSpec(lam_p.shape, lambda b, h, t: (0, 0)),
                     pl.BlockSpec(subnorm.shape, lambda b, h, t: (0, 0))]
        args += [lam_p, subnorm]
    scratch = []
    for _ in range(2):
        scratch += [pltpu.VMEM((1, tq), F32), pltpu.VMEM((1, tq), F32), pltpu.VMEM((LANES, tq), F32)]
    if has_lat:
        scratch += [pltpu.VMEM((2, 2, tk, tq), F32), pltpu.VMEM((2, 2, 1, tq), F32)]
    return pl.pallas_call(
        functools.partial(_attn_kernel, mode, has_lat, tk, nk, lam_init),
        grid=(b_, nh, sq // tq),
        in_specs=in_specs,
        out_specs=pl.BlockSpec((None, tq, LANES), lambda b, h, t: (b, t, h)),
        out_shape=jax.ShapeDtypeStruct((b_, sq, nh * LANES), BF16),
        scratch_shapes=scratch,
        compiler_params=_cparams(("arbitrary", "arbitrary", "arbitrary")),
        name=f"attn_{mode}_{'lat' if has_lat else 'ctx'}",
    )(*args)


def _merge_attention(ya_ref, ym_ref, w_ref):
    return (jnp.dot(ya_ref[...], w_ref[0], preferred_element_type=F32)
            + jnp.dot(ym_ref[...], w_ref[1], preferred_element_type=F32))


def _merge_recurrent(of_ref, ob_ref, yf_ref, yb_ref, sg_ref, sz_ref, xs_ref, on_ref, sk_ref, sn_ref, w_ref):
    o = of_ref[...].astype(F32) + ob_ref[...].astype(F32)
    on = on_ref[...]
    parts = []
    for hh in range(HGRN_HEADS):
        ls = slice(hh * LANES, (hh + 1) * LANES)
        parts.append(_rms(o[:, ls]) * on[:, ls])
    o = jnp.concatenate(parts, axis=1) * sg_ref[...].astype(F32)
    y = ((yf_ref[...].astype(F32) + yb_ref[...].astype(F32) + sk_ref[...] * xs_ref[...])
         * sz_ref[...].astype(F32))
    sn = sn_ref[...]
    gw = SSD_INNER // SSD_GROUPS
    parts = []
    for gg in range(SSD_GROUPS):
        ls = slice(gg * gw, (gg + 1) * gw)
        parts.append(_rms(y[:, ls]) * sn[:, ls])
    y = jnp.concatenate(parts, axis=1)
    return (jnp.dot(o.astype(BF16), w_ref[0], preferred_element_type=F32)
            + jnp.dot(y.astype(BF16), w_ref[1], preferred_element_type=F32))


def _mix_mlp_kernel(merge, n_mix, final, *refs):
    mix_refs = refs[:n_mix]
    if final:
        h_ref, mod_ref, g_ref, w1_ref, w2_ref, fg_ref, o_ref = refs[n_mix:]
    else:
        h_ref, mod_ref, g_ref, w1_ref, w2_ref, o_ref = refs[n_mix:]
    h1 = h_ref[...] + mod_ref[2:3, :] * merge(*mix_refs)
    u = _modulate(h1, g_ref[...], mod_ref[3:4, :], mod_ref[4:5, :]).astype(BF16)
    a = jnp.dot(u, w1_ref[...], preferred_element_type=F32)
    a = jnp.square(jnp.maximum(a, 0.0)).astype(BF16)
    out = h1 + mod_ref[5:6, :] * jnp.dot(a, w2_ref[...], preferred_element_type=F32)
    if final:
        out = _rms(out) * fg_ref[...]
    o_ref[...] = out


def _mix_mlp(merge, mix_rows, mix_full, h, mods, gain, layer, w1, w2, final_gain, tr):
    g_, r_, d = h.shape
    final = final_gain is not None
    row = lambda c: pl.BlockSpec((None, tr, c), lambda b, t: (b, t, 0))
    full = lambda a: pl.BlockSpec(a.shape, lambda b, t: (0,) * a.ndim)
    resident = lambda a: pl.BlockSpec((None,) + a.shape[1:], lambda b, t: (layer, 0, 0),
                                      pipeline_mode=pl.Buffered(1))
    in_specs = ([row(a.shape[-1]) for a in mix_rows] + [full(a) for a in mix_full]
                + [row(d), pl.BlockSpec((None, 8, d), lambda b, t: (b, 0, 0)), full(gain),
                   resident(w1), resident(w2)])
    args = list(mix_rows) + list(mix_full) + [h, mods, gain, w1, w2]
    if final:
        in_specs.append(full(final_gain))
        args.append(final_gain)
    return pl.pallas_call(
        functools.partial(_mix_mlp_kernel, merge, len(mix_rows) + len(mix_full), final),
        grid=(g_, r_ // tr),
        in_specs=in_specs,
        out_specs=row(d),
        out_shape=jax.ShapeDtypeStruct(h.shape, F32),
        compiler_params=_cparams(("arbitrary", "arbitrary")),
        name="mix_mlp_final" if final else "mix_mlp",
    )(*args)


def _softplus(x):
    return jnp.maximum(x, 0.0) + jnp.log1p(jnp.exp(-jnp.abs(x)))


def _proj1_kernel(nt, h_ref, hp_ref, hn_ref, mod_ref, g_ref, w_ref, wdt_ref, bl_ref, dtb_ref, cw_ref, cb_ref,
                  q_ref, gf_ref, gb_ref, i_ref, sg_ref, sz_ref, xs_ref, bm_ref, cm_ref, dt_ref, dtt_ref, pad_ref):
    t = pl.program_id(1)
    tr = h_ref.shape[0]
    mod = mod_ref[...]
    gain = g_ref[...]
    ub = _modulate(h_ref[...], gain, mod[0:1], mod[1:2]).astype(BF16)
    u_ext = jnp.concatenate([_modulate(hp_ref[...], gain, mod[0:1], mod[1:2]).astype(BF16), ub,
                             _modulate(hn_ref[...], gain, mod[0:1], mod[1:2]).astype(BF16)], axis=0)
    xbc = jnp.dot(u_ext, w_ref[:, 3072:4096], preferred_element_type=F32)
    ri = lax.broadcasted_iota(jnp.int32, (tr + 2 * HALO, 1), 0)
    inside = ((ri >= HALO) | (t > 0)) & ((ri < tr + HALO) | (t < nt - 1))
    pad_ref[...] = jnp.where(inside, xbc, 0.0)
    half = SSD_CONV_W // 2
    y = jnp.zeros((tr, xbc.shape[1]), F32) + cb_ref[...]
    for j in range(SSD_CONV_W):
        y = y + pad_ref[HALO - half + j:HALO - half + j + tr, :] * cw_ref[j:j + 1, :]
    y = _silu(y)
    xs_ref[...] = y[:, :SSD_INNER]
    bm_ref[...] = y[:, SSD_INNER:SSD_INNER + 256].astype(BF16)
    cm_ref[...] = y[:, SSD_INNER + 256:].astype(BF16)

    def proj(lo, hi):
        return jnp.dot(ub, w_ref[:, lo:hi], preferred_element_type=F32)

    bl = bl_ref[...]
    e = jnp.exp(bl - jnp.max(bl, axis=0, keepdims=True))
    gamma = e / jnp.sum(e, axis=0, keepdims=True)
    lb = (gamma[0:1] + gamma[1:2]) - gamma[0:1]
    q_ref[...] = _silu(proj(0, 512)).astype(BF16)
    for k, out in ((0, gf_ref), (1, gb_ref)):
        lbk = lb[:, 512 * k:512 * (k + 1)]
        f = lbk + (1.0 - lbk) * jax.nn.sigmoid(proj(512 * (k + 1), 512 * (k + 2)))
        out[...] = jnp.log(f)
    i_ref[...] = proj(1536, 2048).astype(BF16)
    sg_ref[...] = _silu(proj(2048, 2560)).astype(BF16)
    sz_ref[...] = _silu(proj(2560, 3072)).astype(BF16)
    dt =_softplus(jnp.dot(ub, wdt_ref[...], preferred_element_type=F32) + dtb_ref[...])
    dt_ref[...] = dt
    dtt_ref[...] = jnp.concatenate([dt[:, :LANES].T[:SSD_HEADS], dt[:, LANES:].T[:SSD_HEADS]], axis=0)


def _proj1(h, mods, gain, w, conv_w, conv_b, tr):
    g_, r_, d = h.shape
    nt = r_ // tr
    nb = tr // HALO
    row = lambda c: pl.BlockSpec((None, tr, c), lambda b, t: (b, t, 0))
    full = lambda a: pl.BlockSpec(a.shape, lambda b, t: (0,) * a.ndim)
    names = ("w", "wdt", "bl", "dtb")
    widths = (512, 512, 512, 512, 512, 512, 512, 256, 256, 2 * LANES)
    dts = (BF16, F32, F32, BF16, BF16, BF16, F32, BF16, BF16, F32)
    return pl.pallas_call(
        functools.partial(_proj1_kernel, nt),
        grid=(g_, nt),
        in_specs=[row(d),
                  pl.BlockSpec((None, HALO, d), lambda b, t: (b, jnp.maximum(t * nb - 1, 0), 0)),
                  pl.BlockSpec((None, HALO, d), lambda b, t: (b, jnp.minimum((t + 1) * nb, nt * nb - 1), 0)),
                  pl.BlockSpec((None, 8, d), lambda b, t: (b, 0, 0)), full(gain)]
        + [full(w[n]) for n in names] + [full(conv_w), full(conv_b)],
        out_specs=[row(c) for c in widths] + [pl.BlockSpec((None, 16, tr), lambda b, t: (b, 0, t))],
        out_shape=[jax.ShapeDtypeStruct((g_, r_, c), t_) for c, t_ in zip(widths, dts)]
        + [jax.ShapeDtypeStruct((g_, 16, r_), F32)],
        scratch_shapes=[pltpu.VMEM((tr + 2 * HALO, SSD_CONV_CH), F32)],
        compiler_params=_cparams(("arbitrary", "arbitrary")),
        name="proj1",
    )(h, h, h, mods, gain, *[w[n] for n in names], conv_w, conv_b)


def _tri(rev, n):
    i = np.arange(n)
    m = i[None, :] >= i[:, None] if rev else i[None, :] <= i[:, None]
    return m


def _ssd_consts(rev, nchunk):
    tri = np.kron(np.eye(nchunk), _tri(rev, SSD_CHUNK).astype(np.float32))
    expand = np.zeros((LANES, SSD_INNER), np.float32)
    for h in range(SSD_HEADS):
        expand[h, h * SSD_HEAD_DIM:(h + 1) * SSD_HEAD_DIM] = 1.0
    return jnp.asarray(tri, BF16), jnp.asarray(tri.T.copy(), BF16), jnp.asarray(expand, BF16)


def _ssd_block_decays(rev, dt_ref, dtt_ref, an, ant, tri, trit):
    dir_ = 1 if rev else 0
    dt = dt_ref[:, dir_ * LANES:(dir_ + 1) * LANES]
    dtt = dtt_ref[8 * dir_:8 * dir_ + 8, :]
    acol = _exact_left(tri, dt * an)
    arow = _exact_right(dtt * ant, trit)
    return dt, acol, arow


def _expand_heads(vals, ex):
    rows = vals[0].shape[0]
    parts = [p for v in vals for p in _split2(v)]
    big = jnp.dot(jnp.concatenate(parts, axis=0), ex, preferred_element_type=F32)
    return [big[2 * i * rows:(2 * i + 1) * rows] + big[(2 * i + 1) * rows:(2 * i + 2) * rows]
            for i in range(len(vals))]


def _ssd_chunk(rev, r0, x_ref, bm_ref, cm_ref, decays, h_ref, y_ref):
    n = SSD_CHUNK
    ti = lax.broadcasted_iota(jnp.int32, (n, n), 0)
    si = lax.broadcasted_iota(jnp.int32, (n, n), 1)
    mask = (si >= ti) if rev else (si <= ti)
    lane = lax.broadcasted_iota(jnp.int32, (n, LANES), 1)
    last = 0 if rev else n - 1
    x = x_ref[r0:r0 + n, :]
    bm = bm_ref[r0:r0 + n, :]
    cm = cm_ref[r0:r0 + n, :]
    acol = decays[0][r0:r0 + n, :]
    arow = decays[1][:, r0:r0 + n]
    dte = decays[2][r0:r0 + n, :]
    ae = decays[3][r0:r0 + n, :]
    ae_last = ae[last:last + 1, :]
    xdt = x * dte
    xw = (xdt * jnp.exp(ae_last - ae)).astype(BF16)
    xdtb = xdt.astype(BF16)
    ys = []
    for g in range(SSD_GROUPS):
        bg = bm[:, g * SSD_STATE:(g + 1) * SSD_STATE]
        cg = cm[:, g * SSD_STATE:(g + 1) * SSD_STATE]
        cb = _dot_nt(cg, bg)
        hs = h_ref[:, g * 256:(g + 1) * 256]
        yoff = jnp.dot(cg, hs.astype(BF16), preferred_element_type=F32) * jnp.exp(ae[:, g * 256:(g + 1) * 256])
        for pr in range(2):
            xp = xdtb[:, g * 256 + pr * LANES:g * 256 + (pr + 1) * LANES]
            res = []
            for k in range(2):
                hd = g * 4 + pr * 2 + k
                dmat = acol[:, hd:hd + 1] - arow[hd:hd + 1, :]
                lmat = jnp.exp(jnp.where(mask, dmat, -jnp.inf))
                res.append(jnp.dot((cb * lmat).astype(BF16), xp, preferred_element_type=F32))
            ys.append(jnp.where(lane < SSD_HEAD_DIM, res[0], res[1]) + yoff[:, pr * LANES:(pr + 1) * LANES])
        h_ref[:, g * 256:(g + 1) * 256] = (hs * jnp.exp(ae_last[:, g * 256:(g + 1) * 256])
                                           + _dot_tn(bg, xw[:, g * 256:(g + 1) * 256]))
    y_ref[r0:r0 + n, :] = jnp.concatenate(ys, axis=1).astype(y_ref.dtype)


def _ssd_kernel(nchunk, xf_ref, bmf_ref, cmf_ref, dtf_ref, dttf_ref, xb_ref, bmb_ref, cmb_ref, dtb_ref, dttb_ref,
                an_ref, ant_ref, tri_ref, trit_ref, ex_ref, h0f_ref, h0b_ref,
                yf_ref, hTf_ref, yb_ref, hTb_ref, hf_ref, hb_ref):
    s = pl.program_id(1)

    @pl.when(s == 0)
    def _():
        hf_ref[...] = h0f_ref[...]
        hb_ref[...] = h0b_ref[...]

    ex = ex_ref[...]
    dt_f, acol_f, arow_f = _ssd_block_decays(False, dtf_ref, dttf_ref, an_ref[0], ant_ref[0], tri_ref[0], trit_ref[0])
    dt_b, acol_b, arow_b = _ssd_block_decays(True, dtb_ref, dttb_ref, an_ref[1], ant_ref[1], tri_ref[1], trit_ref[1])
    dte_f, ae_f, dte_b, ae_b = _expand_heads([dt_f, acol_f, dt_b, acol_b], ex)
    dec_f = (acol_f, arow_f, dte_f, ae_f)
    dec_b = (acol_b, arow_b, dte_b, ae_b)
    for j in range(nchunk):
        _ssd_chunk(False, j * SSD_CHUNK, xf_ref, bmf_ref, cmf_ref, dec_f, hf_ref, yf_ref)
        _ssd_chunk(True, (nchunk - 1 - j) * SSD_CHUNK, xb_ref, bmb_ref, cmb_ref, dec_b, hb_ref, yb_ref)

    @pl.when(s == pl.num_programs(1) - 1)
    def _():
        hTf_ref[...] = hf_ref[...]
        hTb_ref[...] = hb_ref[...]


def _ssd_scan(xs, bm, cm, dt, dtt, a_log, h0f, h0b, rb):
    b_, r_, _ = xs.shape
    nblk = r_ // rb
    trif, tritf, ex = _ssd_consts(False, rb // SSD_CHUNK)
    trib, tritb, _ = _ssd_consts(True, rb // SSD_CHUNK)
    tri = jnp.stack([trif, trib])
    trit = jnp.stack([tritf, tritb])
    a_neg = -jnp.exp(a_log.astype(F32))
    an = jnp.zeros((2, 1, LANES), F32).at[:, 0, :SSD_HEADS].set(a_neg)
    ant = a_neg.reshape(2, SSD_HEADS, 1)
    rowf = lambda c: pl.BlockSpec((None, rb, c), lambda b, s: (b, s, 0))
    rowb = lambda c: pl.BlockSpec((None, rb, c), lambda b, s: (b, nblk - 1 - s, 0))
    full = lambda a: pl.BlockSpec(a.shape, lambda b, s: (0,) * a.ndim)
    st = pl.BlockSpec((None, SSD_STATE, SSD_INNER), lambda b, s: (b, 0, 0))
    y_shape = jax.ShapeDtypeStruct((b_, r_, SSD_INNER), BF16)
    h_shape = jax.ShapeDtypeStruct((b_, SSD_STATE, SSD_INNER), F32)
    return pl.pallas_call(
        functools.partial(_ssd_kernel, rb // SSD_CHUNK),
        grid=(b_, nblk),
        in_specs=[rowf(512), rowf(256), rowf(256), rowf(2 * LANES),
                  pl.BlockSpec((None, 16, rb), lambda b, s: (b, 0, s)),
                  rowb(512), rowb(256), rowb(256), rowb(2 * LANES),
                  pl.BlockSpec((None, 16, rb), lambda b, s: (b, 0, nblk - 1 - s)),
                  full(an), full(ant), full(tri), full(trit), full(ex), st, st],
        out_specs=[rowf(512), st, rowb(512), st],
        out_shape=[y_shape, h_shape, y_shape, h_shape],
        scratch_shapes=[pltpu.VMEM((SSD_STATE, SSD_INNER), F32)] * 2,
        compiler_params=_cparams(("arbitrary", "arbitrary")),
        name="ssd_scan",
    )(xs, bm, cm, dt, dtt, xs, bm, cm, dt, dtt, an, ant, tri, trit, ex, h0f, h0b)


_HGRN_LEVELS = (64, 32, 16, 8, 4, 2, 1)


def _hgrn_consts(rev):
    i = np.arange(CHUNK)
    before = _tri(rev, CHUNK)
    after_strict = ~before
    mats, masks = [], []
    for c in _HGRN_LEVELS:
        same = (i[:, None] // c) == (i[None, :] // c)
        q_side = same & before
        k_side = same & after_strict
        if c == CHUNK:
            mats += [q_side, k_side]
        else:
            blk = i // c
            first = (blk % 2 == 1) if rev else (blk % 2 == 0)
            if c > 1:
                mats.append(np.where(first[:, None], k_side, q_side))
            pair = first[None, :] & (~first)[:, None] & ((blk[:, None] // 2) == (blk[None, :] // 2))
            masks.append(pair)
    masks.append(np.eye(CHUNK, dtype=bool))
    m = np.concatenate(mats, axis=0).astype(np.float32)
    m = np.concatenate([m, m], axis=1)
    mk = np.stack(masks, axis=0).astype(np.float32)
    return jnp.asarray(m, BF16), jnp.asarray(mk, F32)


def _hgrn_chunk(rev, r0, q_ref, g_ref, v_ref, mall, mk_ref, s_ref, o_ref):
    nl = len(_HGRN_LEVELS)
    last = 0 if rev else CHUNK - 1
    g = g_ref[r0:r0 + CHUNK, :]
    g2 = jnp.concatenate(_split2(g), axis=0)
    eall = jnp.exp(jnp.dot(mall, g2, preferred_element_type=F32))
    ff = jnp.exp(g)
    kk = 1.0 - ff
    qq = q_ref[r0:r0 + CHUNK, :].astype(F32)
    vv = v_ref[r0:r0 + CHUNK, :]
    outs = []
    for h in range(HGRN_HEADS):
        ls = slice(h * LANES, (h + 1) * LANES)
        qh, kh, vh = qq[:, ls], kk[:, ls], vv[:, ls]
        khb = kh.astype(BF16)

        def fac(idx, ls=ls):
            return eall[idx * CHUNK:(idx + 1) * CHUNK, ls]

        q01 = jnp.concatenate([qh, qh * ff[:, ls]], axis=0).astype(BF16)
        a01 = _dot_nt(q01, khb)
        att = a01[:CHUNK] * mk_ref[nl - 1] + a01[CHUNK:] * mk_ref[nl - 2]
        for l in range(1, nl - 1):
            e = fac(l + 1)
            att = att + _dot_nt((qh * e).astype(BF16), (kh * e).astype(BF16)) * mk_ref[l - 1]
        st = s_ref[h]
        qs_ = (qh * fac(0)).astype(BF16)
        o = _dot_nt(qs_, st.astype(BF16)) + jnp.dot(att.astype(BF16), vh, preferred_element_type=F32)
        outs.append(o)
        ks_ = (kh * fac(1)).astype(BF16)
        dec = fac(0)[last:last + 1, :]
        s_ref[h] = st * dec + _dot_tn(vh, ks_)
    o_ref[r0:r0 + CHUNK, :] = jnp.concatenate(outs, axis=1).astype(o_ref.dtype)


def _hgrn_kernel(nchunk, qf_ref, gf_ref, vf_ref, qb_ref, gb_ref, vb_ref, mf_ref, mkf_ref, mb_ref, mkb_ref,
                 s0f_ref, s0b_ref, of_ref, sTf_ref, ob_ref, sTb_ref, sf_ref, sb_ref):
    s = pl.program_id(1)

    @pl.when(s == 0)
    def _():
        sf_ref[...] = s0f_ref[...]
        sb_ref[...] = s0b_ref[...]

    mallf = mf_ref[...]
    mallb = mb_ref[...]
    for j in range(nchunk):
        _hgrn_chunk(False, j * CHUNK, qf_ref, gf_ref, vf_ref, mallf, mkf_ref, sf_ref, of_ref)
        _hgrn_chunk(True, (nchunk - 1 - j) * CHUNK, qb_ref, gb_ref, vb_ref, mallb, mkb_ref, sb_ref, ob_ref)

    @pl.when(s == pl.num_programs(1) - 1)
    def _():
        sTf_ref[...] = sf_ref[...]
        sTb_ref[...] = sb_ref[...]


def _hgrn_scan(q, gf, gb, v, s0f, s0b, rb):
    b_, r_, _ = q.shape
    nblk = r_ // rb
    mallf, mkf = _hgrn_consts(False)
    mallb, mkb = _hgrn_consts(True)
    rowf = pl.BlockSpec((None, rb, 512), lambda b, s: (b, s, 0))
    rowb = pl.BlockSpec((None, rb, 512), lambda b, s: (b, nblk - 1 - s, 0))
    full = lambda a: pl.BlockSpec(a.shape, lambda b, s: (0,) * a.ndim)
    st = pl.BlockSpec((None, HGRN_HEADS, LANES, HGRN_DK), lambda b, s: (b, 0, 0, 0))
    o_shape = jax.ShapeDtypeStruct((b_, r_, 512), BF16)
    s_shape = jax.ShapeDtypeStruct((b_, HGRN_HEADS, LANES, HGRN_DK), F32)
    return pl.pallas_call(
        functools.partial(_hgrn_kernel, rb // CHUNK),
        grid=(b_, nblk),
        in_specs=[rowf, rowf, rowf, rowb, rowb, rowb, full(mallf), full(mkf), full(mallb), full(mkb), st, st],
        out_specs=[rowf, st, rowb, st],
        out_shape=[o_shape, s_shape, o_shape, s_shape],
        scratch_shapes=[pltpu.VMEM((HGRN_HEADS, LANES, HGRN_DK), F32)] * 2,
        compiler_params=_cparams(("arbitrary", "arbitrary")),
        name="hgrn_scan",
    )(q, gf, v, q, gb, v, mallf, mkf, mallb, mkb, s0f, s0b)


def _rope_tables(length):
    rows = length // GRID_W
    row = jnp.repeat(jnp.arange(rows, dtype=F32), GRID_W)
    col = jnp.tile(jnp.arange(GRID_W, dtype=F32), rows)

    def cs(rot_dim):
        n_freq = rot_dim // 4
        inv_freq = ROPE_BASE ** (-jnp.arange(n_freq, dtype=F32) / n_freq)
        ang = jnp.concatenate([row[:, None] * inv_freq, col[:, None] * inv_freq], axis=-1)
        c, s = jnp.cos(ang), jnp.sin(ang)
        return jnp.concatenate([c, c], axis=-1), jnp.concatenate([-s, s], axis=-1)

    ca, sa = cs(DA_HEAD_DIM)
    cq, sq = cs(MLA_ROPE)
    ones = jnp.ones((length, MLA_NOPE), F32)
    pad1 = jnp.ones((length, LANES - MLA_NOPE - MLA_ROPE), F32)
    cqt = jnp.concatenate([ones, cq, pad1], axis=-1)
    sqt = jnp.concatenate([0.0 * ones, sq, 0.0 * pad1], axis=-1)
    return jnp.tile(ca, (1, 2)), jnp.tile(sa, (1, 2)), cqt, sqt


def _att_weights(att_w_in, mla_q_norm, mla_w_uq, mla_kv_norm, mla_w_ukv):
    d = att_w_in.shape[0]
    kr = att_w_in[:, 2176:2208]
    kr_blk = jnp.concatenate([jnp.zeros((d, MLA_NOPE), F32), kr,
                              jnp.zeros((d, LANES - MLA_NOPE - MLA_ROPE), F32)], axis=1)
    win = jnp.concatenate([att_w_in[:, :1024], att_w_in[:, 1536:2176], kr_blk], axis=1).astype(BF16)
    wva = att_w_in[:, 1024:1536].T.astype(BF16)
    wq = mla_w_uq.reshape(MLA_Q_RANK, MLA_HEADS, MLA_NOPE + MLA_ROPE)
    wq = jnp.pad(wq, ((0, 0), (0, 0), (0, LANES - MLA_NOPE - MLA_ROPE))).reshape(MLA_Q_RANK, MLA_HEADS * LANES)
    wkv = mla_w_ukv.reshape(MLA_KV_RANK, MLA_HEADS, MLA_NOPE + MLA_V)
    wk = jnp.pad(wkv[:, :, :MLA_NOPE], ((0, 0), (0, 0), (0, LANES - MLA_NOPE))).reshape(MLA_KV_RANK, MLA_HEADS * LANES)
    wv = wkv[:, :, MLA_NOPE:].reshape(MLA_KV_RANK, MLA_HEADS * MLA_V)
    return dict(win=win, wva=wva, qn=mla_q_norm.reshape(1, -1), wuq=wq.astype(BF16),
                kvn=mla_kv_norm.reshape(1, -1), wk=wk.astype(BF16), wv=wv.T.astype(BF16))


def _rec_weights(rec_w_in, bound_logits, dt_bias):
    d = rec_w_in.shape[0]
    w = rec_w_in.astype(BF16)
    wdt = rec_w_in[:, 4096:4112]
    pad = jnp.zeros((d, LANES - SSD_HEADS), F32)
    wdt_rows = jnp.concatenate([wdt[:, :SSD_HEADS], pad, wdt[:, SSD_HEADS:], pad], axis=1).astype(BF16)
    zb = jnp.zeros((LANES - SSD_HEADS,), F32)
    dtb = jnp.concatenate([dt_bias[0], zb, dt_bias[1], zb]).reshape(1, 2 * LANES)
    return dict(w=w, wdt=wdt_rows, bl=bound_logits, dtb=dtb)


class _Tiles(NamedTuple):
    proj: int
    ctx: int
    att_q: int
    att_k: int
    mlp0: int
    mlp1: int
    scan: int


def _tiles(length, n_ctx):
    cap = lambda n, full: min(n, full)
    return _Tiles(proj=cap(512, length), ctx=cap(256, n_ctx), att_q=cap(2048, length), att_k=cap(512, length),
                  mlp0=cap(1024, length), mlp1=cap(512, length), scan=cap(512, length))


def kernel(x, c, ctx, c_ctx, w_mod, b_mod, norm_mix, norm_mlp, w_mlp_in, w_mlp_out, att_w_in, att_lambda, att_subnorm, mla_q_norm, mla_w_uq, mla_kv_norm, mla_w_ukv, att_w_out, rec_w_in, hgrn_bound_logits, hgrn_out_norm, ssd_conv_w, ssd_conv_b, ssd_a_log, ssd_dt_bias, ssd_skip, ssd_norm, rec_w_out, final_norm):
    b_, length, d = x.shape
    n_ctx = ctx.shape[1]
    assert w_mod.shape[0] == 2 and d == D_MODEL

    cc = jnp.zeros((8, d), F32).at[:b_].set(c).at[b_].set(c_ctx)
    mods = _mod_vectors(cc, w_mod, b_mod)
    mods = jnp.pad(mods.reshape(2, 8, 6, d), ((0, 0), (0, 0), (0, 2), (0, 0)))
    mods_lat = [mods[l, :b_] for l in range(2)]
    mods_ctx = [jnp.broadcast_to(mods[l, b_], (b_, 8, d)) for l in range(2)]
    row = lambda v: v.reshape(1, -1)

    t = _tiles(length, n_ctx)
    tr, trc = t.proj, t.ctx
    w0 = _att_weights(att_w_in[0], mla_q_norm[0], mla_w_uq[0], mla_kv_norm[0], mla_w_ukv[0])
    tables = _rope_tables(length)
    qa, ka, va, qm, km, vm = _proj0(x, mods_lat[0], row(norm_mix[0]), w0, tables, tr)
    qa_c, ka_c, va_c, qm_c, km_c, vm_c = _proj0(ctx, mods_ctx[0], row(norm_mix[0]), w0, None, trc)
    lam_init = 0.8 - 0.6 * math.exp(-0.3 * 0)
    da_extra = (att_lambda[0], row(att_subnorm[0]))
    ya = _attention("da", qa, ka_c, va_c, ka, va, da_extra, t.att_q, t.att_k, lam_init)
    ym = _attention("mla", qm, km_c, vm_c, km, vm, None, t.att_q, t.att_k, lam_init)
    ya_c = _attention("da", qa_c, ka_c, va_c, None, None, da_extra, trc, t.att_k, lam_init)
    ym_c = _attention("mla", qm_c, km_c, vm_c, None, None, None, trc, t.att_k, lam_init)
    wo0 = (att_w_out[0].astype(BF16).reshape(2, 512, d),)
    w1 = w_mlp_in.astype(BF16)
    w2 = w_mlp_out.astype(BF16)
    h_lat = _mix_mlp(_merge_attention, (ya, ym), wo0, x, mods_lat[0], row(norm_mlp[0]), 0, w1, w2, None, t.mlp0)
    h_ctx = _mix_mlp(_merge_attention, (ya_c, ym_c), wo0, ctx, mods_ctx[0], row(norm_mlp[0]), 0, w1, w2, None, trc)

    w1r = _rec_weights(rec_w_in[0], hgrn_bound_logits, ssd_dt_bias[0])
    cw = ssd_conv_w[0]
    cb = row(ssd_conv_b[0])
    q_c, gf_c, gb_c, i_c, _, _, xs_c, bm_c, cm_c, dt_c, dtt_c = _proj1(h_ctx, mods_ctx[1], row(norm_mix[1]), w1r,
                                                                        cw, cb, trc)
    q_l, gf_l, gb_l, i_l, sg_l, sz_l, xs_l, bm_l, cm_l, dt_l, dtt_l = _proj1(h_lat, mods_lat[1], row(norm_mix[1]),
                                                                             w1r, cw, cb, tr)
    rbc, rbl = t.ctx, t.scan
    zs = jnp.zeros((b_, SSD_STATE, SSD_INNER), F32)
    zh = jnp.zeros((b_, HGRN_HEADS, LANES, HGRN_DK), F32)
    _, hs_f, _, hs_b = _ssd_scan(xs_c, bm_c, cm_c, dt_c, dtt_c, ssd_a_log[0], zs, zs, rbc)
    y_f, _, y_b, _ = _ssd_scan(xs_l, bm_l, cm_l, dt_l, dtt_l, ssd_a_log[0], hs_f, hs_b, rbl)
    _, ss_f, _, ss_b = _hgrn_scan(q_c, gf_c, gb_c, i_c, zh, zh, rbc)
    o_f, _, o_b, _ = _hgrn_scan(q_l, gf_l, gb_l, i_l, ss_f, ss_b, rbl)
    skip = row(jnp.repeat(ssd_skip[0], SSD_HEAD_DIM))
    wro = rec_w_out[0].astype(BF16).reshape(2, 512, d)
    return _mix_mlp(_merge_recurrent, (o_f, o_b, y_f, y_b, sg_l, sz_l, xs_l),
                    (row(hgrn_out_norm[0]), skip, row(ssd_norm[0]), wro),
                    h_lat, mods_lat[1], row(norm_mlp[1]), 1, w1, w2, row(final_norm), t.mlp1)
```

```python
import functools
import math
from typing import NamedTuple

import numpy as np
import jax
import jax.numpy as jnp
from jax import lax
from jax.experimental import pallas as pl
from jax.experimental.pallas import tpu as pltpu

F32 = jnp.float32
BF16 = jnp.bfloat16

D_MODEL = 1024
GRID_W = 64
DA_HEADS = 4
DA_HEAD_DIM = 64
MLA_HEADS = 8
MLA_NOPE = 64
MLA_ROPE = 32
MLA_V = 64
MLA_Q_RANK = 384
MLA_KV_RANK = 256
HGRN_HEADS = 4
HGRN_DK = 128
SSD_HEADS = 8
SSD_HEAD_DIM = 64
SSD_GROUPS = 2
SSD_STATE = 128
SSD_CONV_W = 5
SSD_INNER = SSD_HEADS * SSD_HEAD_DIM
SSD_CONV_CH = SSD_INNER + 2 * SSD_GROUPS * SSD_STATE
MLP_HIDDEN = 4 * D_MODEL
ROPE_BASE = 10000.0
NORM_EPS = 1e-6
CHUNK = 64
SSD_CHUNK = 256

LANES = 128
HALO = 16
LOG2E = 1.4426950408889634
VMEM_LIMIT = 56 * 1024 * 1024


def _cparams(sem):
    return pltpu.CompilerParams(dimension_semantics=sem, vmem_limit_bytes=VMEM_LIMIT)


def _rms(x):
    return x * lax.rsqrt(jnp.mean(x * x, axis=-1, keepdims=True) + NORM_EPS)


def _modulate(x, g, shift, scale):
    return (_rms(x) * g) * (1.0 + scale) + shift


def _silu(x):
    return x * jax.nn.sigmoid(x)


def _dot_nt(a, b):
    return lax.dot_general(a, b, (((1,), (1,)), ((), ())), preferred_element_type=F32)


def _dot_tn(a, b):
    return lax.dot_general(a, b, (((0,), (0,)), ((), ())), preferred_element_type=F32)


def _split2(x):
    hi = x.astype(BF16)
    lo = (x - hi.astype(F32)).astype(BF16)
    return hi, lo


def _exact_left(m01, x):
    hi, lo = _split2(x)
    return (jnp.dot(m01, hi, preferred_element_type=F32) + jnp.dot(m01, lo, preferred_element_type=F32))


def _exact_right(x, m01):
    hi, lo = _split2(x)
    return (jnp.dot(hi, m01, preferred_element_type=F32) + jnp.dot(lo, m01, preferred_element_type=F32))


def _mod_kernel(c_ref, w_ref, b_ref, o_ref):
    a = _silu(c_ref[...]).astype(BF16)
    o_ref[...] = jnp.dot(a, w_ref[...].astype(BF16), preferred_element_type=F32) + b_ref[...]


def _mod_vectors(cc, w_mod, b_mod):
    depth, d, n = w_mod.shape
    tn = 1024
    return pl.pallas_call(
        _mod_kernel,
        grid=(depth, n // tn),
        in_specs=[
            pl.BlockSpec((8, d), lambda l, j: (0, 0)),
            pl.BlockSpec((None, d, tn), lambda l, j: (l, 0, j)),
            pl.BlockSpec((None, 1, tn), lambda l, j: (l, 0, j)),
        ],
        out_specs=pl.BlockSpec((None, 8, tn), lambda l, j: (l, 0, j)),
        out_shape=jax.ShapeDtypeStruct((depth, 8, n), F32),
        compiler_params=_cparams(("arbitrary", "arbitrary")),
        name="mod_vectors",
    )(cc, w_mod, b_mod.reshape(depth, 1, n))


def _rope_partner(x):
    blocks = [pltpu.roll(x[:, i * LANES:(i + 1) * LANES], LANES // 2, 1) for i in range(x.shape[-1] // LANES)]
    return blocks[0] if len(blocks) == 1 else jnp.concatenate(blocks, axis=1)


def _tile_lanes(x, n):
    return jnp.concatenate([x] * n, axis=-1)


def _proj0_kernel(use_rope, *refs):
    if use_rope:
        (h_ref, mod_ref, g_ref, win_ref, wva_ref, qn_ref, wuq_ref, kvn_ref, wk_ref, wv_ref,
         ca_ref, sa_ref, cq_ref, sq_ref,
         qa_ref, ka_ref, va_ref, qm_ref, km_ref, vm_ref) = refs
    else:
        (h_ref, mod_ref, g_ref, win_ref, wva_ref, qn_ref, wuq_ref, kvn_ref, wk_ref, wv_ref,
         qa_ref, ka_ref, va_ref, qm_ref, km_ref, vm_ref) = refs
    mod = mod_ref[...]
    ub = _modulate(h_ref[...], g_ref[...], mod[0:1], mod[1:2]).astype(BF16)

    def proj(lo, hi):
        return jnp.dot(ub, win_ref[:, lo:hi], preferred_element_type=F32)

    qa = proj(0, 512) * (DA_HEAD_DIM ** -0.5 * LOG2E)
    ka = proj(512, 1024)
    va_ref[...] = _dot_nt(wva_ref[...], ub).astype(BF16)
    cq = proj(1024, 1408)
    ckv = proj(1408, 1664)
    kr = proj(1664, 1792)
    cqn = (_rms(cq) * qn_ref[...]).astype(BF16)
    qm = jnp.dot(cqn, wuq_ref[...], preferred_element_type=F32) * ((MLA_NOPE + MLA_ROPE) ** -0.5 * LOG2E)
    ckvn = (_rms(ckv) * kvn_ref[...]).astype(BF16)
    kn = jnp.dot(ckvn, wk_ref[...], preferred_element_type=F32)
    vm_ref[...] = _dot_nt(wv_ref[...], ckvn).astype(BF16)
    if use_rope:
        ca = _tile_lanes(ca_ref[...], DA_HEADS)
        sa = _tile_lanes(sa_ref[...], DA_HEADS)
        qa = qa * ca + _rope_partner(qa) * sa
        ka = ka * ca + _rope_partner(ka) * sa
        cq1 = cq_ref[...]
        sq1 = sq_ref[...]
        qm = qm * _tile_lanes(cq1, MLA_HEADS) + _rope_partner(qm) * _tile_lanes(sq1, MLA_HEADS)
        kr = kr * cq1 + _rope_partner(kr) * sq1
    qa_ref[...] = qa.astype(BF16)
    ka_ref[...] = ka.astype(BF16)
    qm_ref[...] = qm.astype(BF16)
    km_ref[...] = (kn + _tile_lanes(kr, MLA_HEADS)).astype(BF16)


def _proj0(h, mods, gain, w, tables, tr):
    g_, r_, d = h.shape
    use_rope = tables is not None
    row = lambda c: pl.BlockSpec((None, tr, c), lambda b, t: (b, t, 0))
    full = lambda a: pl.BlockSpec(a.shape, lambda b, t: (0,) * a.ndim)
    in_specs = [row(d), pl.BlockSpec((None, 8, d), lambda b, t: (b, 0, 0)), full(gain),
                full(w["win"]), full(w["wva"]), full(w["qn"]), full(w["wuq"]), full(w["kvn"]), full(w["wk"]),
                full(w["wv"])]
    args = [h, mods, gain, w["win"], w["wva"], w["qn"], w["wuq"], w["kvn"], w["wk"], w["wv"]]
    if use_rope:
        in_specs += [pl.BlockSpec((tr, LANES), lambda b, t: (t, 0))] * 4
        args += list(tables)
    col = pl.BlockSpec((None, 512, tr), lambda b, t: (b, 0, t))
    widths = (512, 512, None, 1024, 1024, None)
    return pl.pallas_call(
        functools.partial(_proj0_kernel, use_rope),
        grid=(g_, r_ // tr),
        in_specs=in_specs,
        out_specs=[col if c is None else row(c) for c in widths],
        out_shape=[jax.ShapeDtypeStruct((g_, 512, r_) if c is None else (g_, r_, c), BF16) for c in widths],
        compiler_params=_cparams(("arbitrary", "arbitrary")),
        name="proj0_rope" if use_rope else "proj0_ctx",
    )(*args)


def _attn_kernel(mode, has_lat, tk, nk, lam_init, *refs):
    refs = list(refs)
    q_ref, kc_ref, vc_ref = refs[:3]
    refs = refs[3:]
    if has_lat:
        kl_ref, vl_ref = refs[:2]
        refs = refs[2:]
    if mode == "da":
        lam_ref, sub_ref = refs[:2]
        refs = refs[2:]
    o_ref, ma_ref, la_ref, acca_ref, mb_ref, lb_ref, accb_ref = refs[:7]
    if has_lat:
        s_ref, mx_ref = refs[7:]

    q = q_ref[...]
    if mode == "da":
        lane = lax.broadcasted_iota(jnp.int32, q.shape, 1)
        zero = jnp.zeros_like(q)
        first = (lane % DA_HEAD_DIM) < DA_HEAD_DIM // 2
        qs = (jnp.where(first, q, zero), jnp.where(first, zero, q))
    else:
        qs = (q[:, :LANES], q[:, LANES:])
    stats = ((ma_ref, la_ref, acca_ref), (mb_ref, lb_ref, accb_ref))
    for m_ref, l_ref, acc_ref in stats:
        m_ref[...] = jnp.full(m_ref.shape, -jnp.inf, F32)
        l_ref[...] = jnp.zeros(l_ref.shape, F32)
        acc_ref[...] = jnp.zeros(acc_ref.shape, F32)

    def scores(k):
        ks = (k, k) if mode == "da" else (k[:, :LANES], k[:, LANES:])
        return [_dot_nt(ks[i], qs[i]) for i in range(2)]

    def softmax(i, s):
        m_ref, l_ref, _ = stats[i]
        m_prev = m_ref[...]
        m_new = jnp.maximum(m_prev, jnp.max(s, axis=0, keepdims=True))
        alpha = jnp.exp2(m_prev - m_new)
        p = jnp.exp2(s - m_new)
        l_ref[...] = alpha * l_ref[...] + jnp.sum(p, axis=0, keepdims=True)
        m_ref[...] = m_new
        return p.astype(BF16), alpha

    def accumulate(i, vt, p, alpha):
        acc_ref = stats[i][2]
        acc_ref[...] = alpha * acc_ref[...] + jnp.dot(vt, p, preferred_element_type=F32)

    if not has_lat:
        ss = scores(kc_ref[...])
        pa = [softmax(i, ss[i]) for i in range(2)]
        for i in range(2):
            accumulate(i, vc_ref[...], *pa[i])
    else:
        nc = kc_ref.shape[0]

        def chunk(c):
            return pl.ds(c * tk if isinstance(c, int) else pl.multiple_of(c * tk, tk), tk)

        def keys(st):
            return kc_ref[...] if isinstance(st, int) and st == 0 else kl_ref[chunk(st - 1), :]

        def vals(st):
            return vc_ref[...] if isinstance(st, int) and st == 0 else vl_ref[:, chunk(st - 1)]

        def lookahead(st, slot, n):
            ss = scores(keys(st))
            for i in range(2):
                s_ref[slot, i, 0:n, :] = ss[i]
                mx_ref[1, i] = jnp.maximum(mx_ref[0, i], jnp.max(ss[i], axis=0, keepdims=True))

        def consume(st, slot, n):
            vt = vals(st)
            for i in range(2):
                m_ref, l_ref, acc_ref = stats[i]
                m_cur = mx_ref[0, i]
                alpha = jnp.exp2(m_ref[...] - m_cur)
                p = jnp.exp2(s_ref[slot, i, 0:n, :] - m_cur)
                l_ref[...] = alpha * l_ref[...] + jnp.sum(p, axis=0, keepdims=True)
                acc_ref[...] = alpha * acc_ref[...] + jnp.dot(vt, p.astype(BF16), preferred_element_type=F32)
                m_ref[...] = m_cur

        def advance():
            for i in range(2):
                mx_ref[0, i] = mx_ref[1, i]

        def run(st, slot, has_next):
            n = nc if isinstance(st, int) and st == 0 else tk
            if has_next:
                lookahead(st + 1, 1 - slot, tk)
            consume(st, slot, n)
            if has_next:
                advance()

        for i in range(2):
            mx_ref[0, i] = jnp.full((1, mx_ref.shape[-1]), -jnp.inf, F32)
        lookahead(0, 0, nc)
        advance()
        run(0, 0, True)
        npair = (nk - 1) // 2

        def body(j, carry):
            run(2 * j + 1, 1, True)
            run(2 * j + 2, 0, True)
            return carry

        lax.fori_loop(0, npair, body, 0)
        for st in range(2 * npair + 1, nk + 1):
            run(st, st % 2, st < nk)

    oa = acca_ref[...] / la_ref[...]
    ob = accb_ref[...] / lb_ref[...]
    if mode == "da":
        lp = lam_ref[...]
        lam = (jnp.exp(jnp.sum(lp[0:1] * lp[1:2], axis=1, keepdims=True))
               - jnp.exp(jnp.sum(lp[2:3] * lp[3:4], axis=1, keepdims=True)) + lam_init)
        dlt = (oa - lam * ob).T
        o_ref[...] = ((_rms(dlt) * sub_ref[...]) * (1.0 - lam_init)).astype(o_ref.dtype)
    else:
        chan = lax.broadcasted_iota(jnp.int32, oa.shape, 0)
        o_ref[...] = jnp.where(chan < MLA_V, oa, ob).T.astype(o_ref.dtype)


def _attention(mode, q, kc, vc, kl, vl, extra, tq, tk, lam_init):
    b_, sq, _ = q.shape
    has_lat = kl is not None
    w = LANES if mode == "da" else 2 * LANES
    nh = 4
    kblk = lambda a: pl.BlockSpec((None, a.shape[1], w), lambda b, h, t: (b, 0, h))
    vblk = lambda a: pl.BlockSpec((None, LANES, a.shape[2]), lambda b, h, t: (b, h, 0))
    in_specs = [pl.BlockSpec((None, tq, w), lambda b, h, t: (b, t, h)), kblk(kc), vblk(vc)]
    args = [q, kc, vc]
    nk = 0
    if has_lat:
        in_specs += [kblk(kl), vblk(vl)]
        args += [kl, vl]
        nk = kl.shape[1] // tk
    if mode == "da":
        lam_p, subnorm = extra
        in_specs += [pl.BlockSpec(lam_p.shape, lambda b, h, t: (0, 0)),
                     pl.BlockSpec(subnorm.shape, lambda b, h, t: (0, 0))]
        args += [lam_p, subnorm]
    scratch = []
    for _ in range(2):
        scratch += [pltpu.VMEM((1, tq), F32), pltpu.VMEM((1, tq), F32), pltpu.VMEM((LANES, tq), F32)]
    if has_lat:
        scratch += [pltpu.VMEM((2, 2, tk, tq), F32), pltpu.VMEM((2, 2, 1, tq), F32)]
    return pl.pallas_call(
        functools.partial(_attn_kernel, mode, has_lat, tk, nk, lam_init),
        grid=(b_, nh, sq // tq),
        in_specs=in_specs,
        out_specs=pl.BlockSpec((None, tq, LANES), lambda b, h, t: (b, t, h)),
        out_shape=jax.ShapeDtypeStruct((b_, sq, nh * LANES), BF16),
        scratch_shapes=scratch,
        compiler_params=_cparams(("arbitrary", "arbitrary", "arbitrary")),
        name=f"attn_{mode}_{'lat' if has_lat else 'ctx'}",
    )(*args)


def _merge_attention(ya_ref, ym_ref, w_ref):
    return (jnp.dot(ya_ref[...], w_ref[0], preferred_element_type=F32)
            + jnp.dot(ym_ref[...], w_ref[1], preferred_element_type=F32))


def _merge_recurrent(of_ref, ob_ref, yf_ref, yb_ref, sg_ref, sz_ref, xs_ref, on_ref, sk_ref, sn_ref, w_ref):
    o = of_ref[...].astype(F32) + ob_ref[...].astype(F32)
    on = on_ref[...]
    parts = []
    for hh in range(HGRN_HEADS):
        ls = slice(hh * LANES, (hh + 1) * LANES)
        parts.append(_rms(o[:, ls]) * on[:, ls])
    o = jnp.concatenate(parts, axis=1) * sg_ref[...].astype(F32)
    y = ((yf_ref[...].astype(F32) + yb_ref[...].astype(F32) + sk_ref[...] * xs_ref[...])
         * sz_ref[...].astype(F32))
    sn = sn_ref[...]
    gw = SSD_INNER // SSD_GROUPS
    parts = []
    for gg in range(SSD_GROUPS):
        ls = slice(gg * gw, (gg + 1) * gw)
        parts.append(_rms(y[:, ls]) * sn[:, ls])
    y = jnp.concatenate(parts, axis=1)
    return (jnp.dot(o.astype(BF16), w_ref[0], preferred_element_type=F32)
            + jnp.dot(y.astype(BF16), w_ref[1], preferred_element_type=F32))


def _mix_mlp_kernel(merge, n_mix, final, *refs):
    mix_refs = refs[:n_mix]
    if final:
        h_ref, mod_ref, g_ref, w1_ref, w2_ref, fg_ref, o_ref = refs[n_mix:]
    else:
        h_ref, mod_ref, g_ref, w1_ref, w2_ref, o_ref = refs[n_mix:]
    h1 = h_ref[...] + mod_ref[2:3, :] * merge(*mix_refs)
    u = _modulate(h1, g_ref[...], mod_ref[3:4, :], mod_ref[4:5, :]).astype(BF16)
    a = jnp.dot(u, w1_ref[...], preferred_element_type=F32)
    a = jnp.square(jnp.maximum(a, 0.0)).astype(BF16)
    out = h1 + mod_ref[5:6, :] * jnp.dot(a, w2_ref[...], preferred_element_type=F32)
    if final:
        out = _rms(out) * fg_ref[...]
    o_ref[...] = out


def _mix_mlp(merge, mix_rows, mix_full, h, mods, gain, layer, w1, w2, final_gain, tr):
    g_, r_, d = h.shape
    final = final_gain is not None
    row = lambda c: pl.BlockSpec((None, tr, c), lambda b, t: (b, t, 0))
    full = lambda a: pl.BlockSpec(a.shape, lambda b, t: (0,) * a.ndim)
    resident = lambda a: pl.BlockSpec((None,) + a.shape[1:], lambda b, t: (layer, 0, 0),
                                      pipeline_mode=pl.Buffered(1))
    in_specs = ([row(a.shape[-1]) for a in mix_rows] + [full(a) for a in mix_full]
                + [row(d), pl.BlockSpec((None, 8, d), lambda b, t: (b, 0, 0)), full(gain),
                   resident(w1), resident(w2)])
    args = list(mix_rows) + list(mix_full) + [h, mods, gain, w1, w2]
    if final:
        in_specs.append(full(final_gain))
        args.append(final_gain)
    return pl.pallas_call(
        functools.partial(_mix_mlp_kernel, merge, len(mix_rows) + len(mix_full), final),
        grid=(g_, r_ // tr),
        in_specs=in_specs,
        out_specs=row(d),
        out_shape=jax.ShapeDtypeStruct(h.shape, F32),
        compiler_params=_cparams(("arbitrary", "arbitrary")),
        name="mix_mlp_final" if final else "mix_mlp",
    )(*args)


def _softplus(x):
    return jnp.maximum(x, 0.0) + jnp.log1p(jnp.exp(-jnp.abs(x)))


def _proj1_kernel(nt, h_ref, hp_ref, hn_ref, mod_ref, g_ref, w_ref, wdt_ref, bl_ref, dtb_ref, cw_ref, cb_ref,
                  q_ref, gf_ref, gb_ref, i_ref, sg_ref, sz_ref, xs_ref, bm_ref, cm_ref, dt_ref, dtt_ref, pad_ref):
    t = pl.program_id(1)
    tr = h_ref.shape[0]
    mod = mod_ref[...]
    gain = g_ref[...]
    ub = _modulate(h_ref[...], gain, mod[0:1], mod[1:2]).astype(BF16)
    u_ext = jnp.concatenate([_modulate(hp_ref[...], gain, mod[0:1], mod[1:2]).astype(BF16), ub,
                             _modulate(hn_ref[...], gain, mod[0:1], mod[1:2]).astype(BF16)], axis=0)
    xbc = jnp.dot(u_ext, w_ref[:, 3072:4096], preferred_element_type=F32)
    ri = lax.broadcasted_iota(jnp.int32, (tr + 2 * HALO, 1), 0)
    inside = ((ri >= HALO) | (t > 0)) & ((ri < tr + HALO) | (t < nt - 1))
    pad_ref[...] = jnp.where(inside, xbc, 0.0)
    half = SSD_CONV_W // 2
    y = jnp.zeros((tr, xbc.shape[1]), F32) + cb_ref[...]
    for j in range(SSD_CONV_W):
        y = y + pad_ref[HALO - half + j:HALO - half + j + tr, :] * cw_ref[j:j + 1, :]
    y = _silu(y)
    xs_ref[...] = y[:, :SSD_INNER]
    bm_ref[...] = y[:, SSD_INNER:SSD_INNER + 256].astype(BF16)
    cm_ref[...] = y[:, SSD_INNER + 256:].astype(BF16)

    def proj(lo, hi):
        return jnp.dot(ub, w_ref[:, lo:hi], preferred_element_type=F32)

    bl = bl_ref[...]
    e = jnp.exp(bl - jnp.max(bl, axis=0, keepdims=True))
    gamma = e / jnp.sum(e, axis=0, keepdims=True)
    lb = (gamma[0:1] + gamma[1:2]) - gamma[0:1]
    q_ref[...] = _silu(proj(0, 512)).astype(BF16)
    for k, out in ((0, gf_ref), (1, gb_ref)):
        lbk = lb[:, 512 * k:512 * (k + 1)]
        f = lbk + (1.0 - lbk) * jax.nn.sigmoid(proj(512 * (k + 1), 512 * (k + 2)))
        out[...] = jnp.log(f)
    i_ref[...] = proj(1536, 2048).astype(BF16)
    sg_ref[...] = _silu(proj(2048, 2560)).astype(BF16)
    sz_ref[...] = _silu(proj(2560, 3072)).astype(BF16)
    dt =_softplus(jnp.dot(ub, wdt_ref[...], preferred_element_type=F32) + dtb_ref[...])
    dt_ref[...] = dt
    dtt_ref[...] = jnp.concatenate([dt[:, :LANES].T[:SSD_HEADS], dt[:, LANES:].T[:SSD_HEADS]], axis=0)


def _proj1(h, mods, gain, w, conv_w, conv_b, tr):
    g_, r_, d = h.shape
    nt = r_ // tr
    nb = tr // HALO
    row = lambda c: pl.BlockSpec((None, tr, c), lambda b, t: (b, t, 0))
    full = lambda a: pl.BlockSpec(a.shape, lambda b, t: (0,) * a.ndim)
    names = ("w", "wdt", "bl", "dtb")
    widths = (512, 512, 512, 512, 512, 512, 512, 256, 256, 2 * LANES)
    dts = (BF16, F32, F32, BF16, BF16, BF16, F32, BF16, BF16, F32)
    return pl.pallas_call(
        functools.partial(_proj1_kernel, nt),
        grid=(g_, nt),
        in_specs=[row(d),
                  pl.BlockSpec((None, HALO, d), lambda b, t: (b, jnp.maximum(t * nb - 1, 0), 0)),
                  pl.BlockSpec((None, HALO, d), lambda b, t: (b, jnp.minimum((t + 1) * nb, nt * nb - 1), 0)),
                  pl.BlockSpec((None, 8, d), lambda b, t: (b, 0, 0)), full(gain)]
        + [full(w[n]) for n in names] + [full(conv_w), full(conv_b)],
        out_specs=[row(c) for c in widths] + [pl.BlockSpec((None, 16, tr), lambda b, t: (b, 0, t))],
        out_shape=[jax.ShapeDtypeStruct((g_, r_, c), t_) for c, t_ in zip(widths, dts)]
        + [jax.ShapeDtypeStruct((g_, 16, r_), F32)],
        scratch_shapes=[pltpu.VMEM((tr + 2 * HALO, SSD_CONV_CH), F32)],
        compiler_params=_cparams(("arbitrary", "arbitrary")),
        name="proj1",
    )(h, h, h, mods, gain, *[w[n] for n in names], conv_w, conv_b)


def _tri(rev, n):
    i = np.arange(n)
    m = i[None, :] >= i[:, None] if rev else i[None, :] <= i[:, None]
    return m


def _ssd_consts(rev, nchunk):
    tri = np.kron(np.eye(nchunk), _tri(rev, SSD_CHUNK).astype(np.float32))
    expand = np.zeros((LANES, SSD_INNER), np.float32)
    for h in range(SSD_HEADS):
        expand[h, h * SSD_HEAD_DIM:(h + 1) * SSD_HEAD_DIM] = 1.0
    return jnp.asarray(tri, BF16), jnp.asarray(tri.T.copy(), BF16), jnp.asarray(expand, BF16)


def _ssd_block_decays(rev, dt_ref, dtt_ref, an, ant, tri, trit):
    dir_ = 1 if rev else 0
    dt = dt_ref[:, dir_ * LANES:(dir_ + 1) * LANES]
    dtt = dtt_ref[8 * dir_:8 * dir_ + 8, :]
    acol = _exact_left(tri, dt * an)
    arow = _exact_right(dtt * ant, trit)
    return dt, acol, arow


def _expand_heads(vals, ex):
    rows = vals[0].shape[0]
    parts = [p for v in vals for p in _split2(v)]
    big = jnp.dot(jnp.concatenate(parts, axis=0), ex, preferred_element_type=F32)
    return [big[2 * i * rows:(2 * i + 1) * rows] + big[(2 * i + 1) * rows:(2 * i + 2) * rows]
            for i in range(len(vals))]


def _ssd_chunk(rev, r0, x_ref, bm_ref, cm_ref, decays, h_ref, y_ref):
    n = SSD_CHUNK
    ti = lax.broadcasted_iota(jnp.int32, (n, n), 0)
    si = lax.broadcasted_iota(jnp.int32, (n, n), 1)
    mask = (si >= ti) if rev else (si <= ti)
    lane = lax.broadcasted_iota(jnp.int32, (n, LANES), 1)
    last = 0 if rev else n - 1
    x = x_ref[r0:r0 + n, :]
    bm = bm_ref[r0:r0 + n, :]
    cm = cm_ref[r0:r0 + n, :]
    acol = decays[0][r0:r0 + n, :]
    arow = decays[1][:, r0:r0 + n]
    dte = decays[2][r0:r0 + n, :]
    ae = decays[3][r0:r0 + n, :]
    ae_last = ae[last:last + 1, :]
    xdt = x * dte
    xw = (xdt * jnp.exp(ae_last - ae)).astype(BF16)
    xdtb = xdt.astype(BF16)
    ys = []
    for g in range(SSD_GROUPS):
        bg = bm[:, g * SSD_STATE:(g + 1) * SSD_STATE]
        cg = cm[:, g * SSD_STATE:(g + 1) * SSD_STATE]
        cb = _dot_nt(cg, bg)
        hs = h_ref[:, g * 256:(g + 1) * 256]
        yoff = jnp.dot(cg, hs.astype(BF16), preferred_element_type=F32) * jnp.exp(ae[:, g * 256:(g + 1) * 256])
        for pr in range(2):
            xp = xdtb[:, g * 256 + pr * LANES:g * 256 + (pr + 1) * LANES]
            res = []
            for k in range(2):
                hd = g * 4 + pr * 2 + k
                dmat = acol[:, hd:hd + 1] - arow[hd:hd + 1, :]
                lmat = jnp.exp(jnp.where(mask, dmat, -jnp.inf))
                res.append(jnp.dot((cb * lmat).astype(BF16), xp, preferred_element_type=F32))
            ys.append(jnp.where(lane < SSD_HEAD_DIM, res[0], res[1]) + yoff[:, pr * LANES:(pr + 1) * LANES])
        h_ref[:, g * 256:(g + 1) * 256] = (hs * jnp.exp(ae_last[:, g * 256:(g + 1) * 256])
                                           + _dot_tn(bg, xw[:, g * 256:(g + 1) * 256]))
    y_ref[r0:r0 + n, :] = jnp.concatenate(ys, axis=1).astype(y_ref.dtype)


def _ssd_kernel(nchunk, xf_ref, bmf_ref, cmf_ref, dtf_ref, dttf_ref, xb_ref, bmb_ref, cmb_ref, dtb_ref, dttb_ref,
                an_ref, ant_ref, tri_ref, trit_ref, ex_ref, h0f_ref, h0b_ref,
                yf_ref, hTf_ref, yb_ref, hTb_ref, hf_ref, hb_ref):
    s = pl.program_id(1)

    @pl.when(s == 0)
    def _():
        hf_ref[...] = h0f_ref[...]
        hb_ref[...] = h0b_ref[...]

    ex = ex_ref[...]
    dt_f, acol_f, arow_f = _ssd_block_decays(False, dtf_ref, dttf_ref, an_ref[0], ant_ref[0], tri_ref[0], trit_ref[0])
    dt_b, acol_b, arow_b = _ssd_block_decays(True, dtb_ref, dttb_ref, an_ref[1], ant_ref[1], tri_ref[1], trit_ref[1])
    dte_f, ae_f, dte_b, ae_b = _expand_heads([dt_f, acol_f, dt_b, acol_b], ex)
    dec_f = (acol_f, arow_f, dte_f, ae_f)
    dec_b = (acol_b, arow_b, dte_b, ae_b)
    for j in range(nchunk):
        _ssd_chunk(False, j * SSD_CHUNK, xf_ref, bmf_ref, cmf_ref, dec_f, hf_ref, yf_ref)
        _ssd_chunk(True, (nchunk - 1 - j) * SSD_CHUNK, xb_ref, bmb_ref, cmb_ref, dec_b, hb_ref, yb_ref)

    @pl.when(s == pl.num_programs(1) - 1)
    def _():
        hTf_ref[...] = hf_ref[...]
        hTb_ref[...] = hb_ref[...]


def _ssd_scan(xs, bm, cm, dt, dtt, a_log, h0f, h0b, rb):
    b_, r_, _ = xs.shape
    nblk = r_ // rb
    trif, tritf, ex = _ssd_consts(False, rb // SSD_CHUNK)
    trib, tritb, _ = _ssd_consts(True, rb // SSD_CHUNK)
    tri = jnp.stack([trif, trib])
    trit = jnp.stack([tritf, tritb])
    a_neg = -jnp.exp(a_log.astype(F32))
    an = jnp.zeros((2, 1, LANES), F32).at[:, 0, :SSD_HEADS].set(a_neg)
    ant = a_neg.reshape(2, SSD_HEADS, 1)
    rowf = lambda c: pl.BlockSpec((None, rb, c), lambda b, s: (b, s, 0))
    rowb = lambda c: pl.BlockSpec((None, rb, c), lambda b, s: (b, nblk - 1 - s, 0))
    full = lambda a: pl.BlockSpec(a.shape, lambda b, s: (0,) * a.ndim)
    st = pl.BlockSpec((None, SSD_STATE, SSD_INNER), lambda b, s: (b, 0, 0))
    y_shape = jax.ShapeDtypeStruct((b_, r_, SSD_INNER), BF16)
    h_shape = jax.ShapeDtypeStruct((b_, SSD_STATE, SSD_INNER), F32)
    return pl.pallas_call(
        functools.partial(_ssd_kernel, rb // SSD_CHUNK),
        grid=(b_, nblk),
        in_specs=[rowf(512), rowf(256), rowf(256), rowf(2 * LANES),
                  pl.BlockSpec((None, 16, rb), lambda b, s: (b, 0, s)),
                  rowb(512), rowb(256), rowb(256), rowb(2 * LANES),
                  pl.BlockSpec((None, 16, rb), lambda b, s: (b, 0, nblk - 1 - s)),
                  full(an), full(ant), full(tri), full(trit), full(ex), st, st],
        out_specs=[rowf(512), st, rowb(512), st],
        out_shape=[y_shape, h_shape, y_shape, h_shape],
        scratch_shapes=[pltpu.VMEM((SSD_STATE, SSD_INNER), F32)] * 2,
        compiler_params=_cparams(("arbitrary", "arbitrary")),
        name="ssd_scan",
    )(xs, bm, cm, dt, dtt, xs, bm, cm, dt, dtt, an, ant, tri, trit, ex, h0f, h0b)


_HGRN_LEVELS = (64, 32, 16, 8, 4, 2, 1)


def _hgrn_consts(rev):
    i = np.arange(CHUNK)
    before = _tri(rev, CHUNK)
    after_strict = ~before
    mats, masks = [], []
    for c in _HGRN_LEVELS:
        same = (i[:, None] // c) == (i[None, :] // c)
        q_side = same & before
        k_side = same & after_strict
        if c == CHUNK:
            mats += [q_side, k_side]
        else:
            blk = i // c
            first = (blk % 2 == 1) if rev else (blk % 2 == 0)
            if c > 1:
                mats.append(np.where(first[:, None], k_side, q_side))
            pair = first[None, :] & (~first)[:, None] & ((blk[:, None] // 2) == (blk[None, :] // 2))
            masks.append(pair)
    masks.append(np.eye(CHUNK, dtype=bool))
    m = np.concatenate(mats, axis=0).astype(np.float32)
    m = np.concatenate([m, m], axis=1)
    mk = np.stack(masks, axis=0).astype(np.float32)
    return jnp.asarray(m, BF16), jnp.asarray(mk, F32)


def _hgrn_chunk(rev, r0, q_ref, g_ref, v_ref, mall, mk_ref, s_ref, o_ref):
    nl = len(_HGRN_LEVELS)
    last = 0 if rev else CHUNK - 1
    g = g_ref[r0:r0 + CHUNK, :]
    g2 = jnp.concatenate(_split2(g), axis=0)
    eall = jnp.exp(jnp.dot(mall, g2, preferred_element_type=F32))
    ff = jnp.exp(g)
    kk = 1.0 - ff
    qq = q_ref[r0:r0 + CHUNK, :].astype(F32)
    vv = v_ref[r0:r0 + CHUNK, :]
    outs = []
    for h in range(HGRN_HEADS):
        ls = slice(h * LANES, (h + 1) * LANES)
        qh, kh, vh = qq[:, ls], kk[:, ls], vv[:, ls]
        khb = kh.astype(BF16)

        def fac(idx, ls=ls):
            return eall[idx * CHUNK:(idx + 1) * CHUNK, ls]

        q01 = jnp.concatenate([qh, qh * ff[:, ls]], axis=0).astype(BF16)
        a01 = _dot_nt(q01, khb)
        att = a01[:CHUNK] * mk_ref[nl - 1] + a01[CHUNK:] * mk_ref[nl - 2]
        for l in range(1, nl - 1):
            e = fac(l + 1)
            att = att + _dot_nt((qh * e).astype(BF16), (kh * e).astype(BF16)) * mk_ref[l - 1]
        st = s_ref[h]
        qs_ = (qh * fac(0)).astype(BF16)
        o = _dot_nt(qs_, st.astype(BF16)) + jnp.dot(att.astype(BF16), vh, preferred_element_type=F32)
        outs.append(o)
        ks_ = (kh * fac(1)).astype(BF16)
        dec = fac(0)[last:last + 1, :]
        s_ref[h] = st * dec + _dot_tn(vh, ks_)
    o_ref[r0:r0 + CHUNK, :] = jnp.concatenate(outs, axis=1).astype(o_ref.dtype)


def _hgrn_kernel(nchunk, qf_ref, gf_ref, vf_ref, qb_ref, gb_ref, vb_ref, mf_ref, mkf_ref, mb_ref, mkb_ref,
                 s0f_ref, s0b_ref, of_ref, sTf_ref, ob_ref, sTb_ref, sf_ref, sb_ref):
    s = pl.program_id(1)

    @pl.when(s == 0)
    def _():
        sf_ref[...] = s0f_ref[...]
        sb_ref[...] = s0b_ref[...]

    mallf = mf_ref[...]
    mallb = mb_ref[...]
    for j in range(nchunk):
        _hgrn_chunk(False, j * CHUNK, qf_ref, gf_ref, vf_ref, mallf, mkf_ref, sf_ref, of_ref)
        _hgrn_chunk(True, (nchunk - 1 - j) * CHUNK, qb_ref, gb_ref, vb_ref, mallb, mkb_ref, sb_ref, ob_ref)

    @pl.when(s == pl.num_programs(1) - 1)
    def _():
        sTf_ref[...] = sf_ref[...]
        sTb_ref[...] = sb_ref[...]


def _hgrn_scan(q, gf, gb, v, s0f, s0b, rb):
    b_, r_, _ = q.shape
    nblk = r_ // rb
    mallf, mkf = _hgrn_consts(False)
    mallb, mkb = _hgrn_consts(True)
    rowf = pl.BlockSpec((None, rb, 512), lambda b, s: (b, s, 0))
    rowb = pl.BlockSpec((None, rb, 512), lambda b, s: (b, nblk - 1 - s, 0))
    full = lambda a: pl.BlockSpec(a.shape, lambda b, s: (0,) * a.ndim)
    st = pl.BlockSpec((None, HGRN_HEADS, LANES, HGRN_DK), lambda b, s: (b, 0, 0, 0))
    o_shape = jax.ShapeDtypeStruct((b_, r_, 512), BF16)
    s_shape = jax.ShapeDtypeStruct((b_, HGRN_HEADS, LANES, HGRN_DK), F32)
    return pl.pallas_call(
        functools.partial(_hgrn_kernel, rb // CHUNK),
        grid=(b_, nblk),
        in_specs=[rowf, rowf, rowf, rowb, rowb, rowb, full(mallf), full(mkf), full(mallb), full(mkb), st, st],
        out_specs=[rowf, st, rowb, st],
        out_shape=[o_shape, s_shape, o_shape, s_shape],
        scratch_shapes=[pltpu.VMEM((HGRN_HEADS, LANES, HGRN_DK), F32)] * 2,
        compiler_params=_cparams(("arbitrary", "arbitrary")),
        name="hgrn_scan",
    )(q, gf, v, q, gb, v, mallf, mkf, mallb, mkb, s0f, s0b)


def _rope_tables(length):
    rows = length // GRID_W
    row = jnp.repeat(jnp.arange(rows, dtype=F32), GRID_W)
    col = jnp.tile(jnp.arange(GRID_W, dtype=F32), rows)

    def cs(rot_dim):
        n_freq = rot_dim // 4
        inv_freq = ROPE_BASE ** (-jnp.arange(n_freq, dtype=F32) / n_freq)
        ang = jnp.concatenate([row[:, None] * inv_freq, col[:, None] * inv_freq], axis=-1)
        return jnp.cos(ang), jnp.sin(ang)

    c, s = cs(DA_HEAD_DIM)
    ca = jnp.concatenate([c, c, c, c], axis=-1)
    sa = jnp.concatenate([-s, -s, s, s], axis=-1)
    c, s = cs(MLA_ROPE)
    one = lambda n: jnp.ones((length, n), F32)
    cqt = jnp.concatenate([c, one(48), c, one(48)], axis=-1)
    sqt = jnp.concatenate([-s, 0.0 * one(48), s, 0.0 * one(48)], axis=-1)
    return ca, sa, cqt, sqt


def _att_weights(att_w_in, mla_q_norm, mla_w_uq, mla_kv_norm, mla_w_ukv):
    d = att_w_in.shape[0]
    r = np.arange

    def place(w, idx):
        wz = jnp.concatenate([w, jnp.zeros(w.shape[:-1] + (1,), w.dtype)], axis=-1)
        return jnp.take(wz, jnp.asarray(np.where(idx < 0, w.shape[-1], idx)), axis=-1)

    da_idx = np.concatenate([r(0, 32), r(64, 96), r(32, 64), r(96, 128)])
    mla_q_idx = np.concatenate([r(64, 80), r(0, 48), r(80, 96), r(48, 64), np.full(32, -1)])
    mla_k_idx = np.concatenate([np.full(16, -1), r(0, 48), np.full(16, -1), r(48, 64), np.full(32, -1)])
    kr_idx = np.concatenate([r(0, 16), np.full(48, -1), r(16, 32), np.full(48, -1)])
    heads = lambda w, n: w.reshape(w.shape[0], n, -1)
    qa_w = place(heads(att_w_in[:, :512], DA_HEADS), da_idx).reshape(d, 512)
    ka_w = place(heads(att_w_in[:, 512:1024], DA_HEADS), da_idx).reshape(d, 512)
    kr_blk = place(att_w_in[:, 2176:2208], kr_idx)
    win = jnp.concatenate([qa_w, ka_w, att_w_in[:, 1536:2176], kr_blk], axis=1).astype(BF16)
    wva = att_w_in[:, 1024:1536].T.astype(BF16)
    wq = place(heads(mla_w_uq, MLA_HEADS), mla_q_idx).reshape(MLA_Q_RANK, MLA_HEADS * LANES)
    wkv = heads(mla_w_ukv, MLA_HEADS)
    wk = place(wkv[:, :, :MLA_NOPE], mla_k_idx).reshape(MLA_KV_RANK, MLA_HEADS * LANES)
    wv = wkv[:, :, MLA_NOPE:].reshape(MLA_KV_RANK, MLA_HEADS * MLA_V)
    return dict(win=win, wva=wva, qn=mla_q_norm.reshape(1, -1), wuq=wq.astype(BF16),
                kvn=mla_kv_norm.reshape(1, -1), wk=wk.astype(BF16), wv=wv.T.astype(BF16))


def _rec_weights(rec_w_in, bound_logits, dt_bias):
    d = rec_w_in.shape[0]
    w = rec_w_in.astype(BF16)
    wdt = rec_w_in[:, 4096:4112]
    pad = jnp.zeros((d, LANES - SSD_HEADS), F32)
    wdt_rows = jnp.concatenate([wdt[:, :SSD_HEADS], pad, wdt[:, SSD_HEADS:], pad], axis=1).astype(BF16)
    zb = jnp.zeros((LANES - SSD_HEADS,), F32)
    dtb = jnp.concatenate([dt_bias[0], zb, dt_bias[1], zb]).reshape(1, 2 * LANES)
    return dict(w=w, wdt=wdt_rows, bl=bound_logits, dtb=dtb)


class _Tiles(NamedTuple):
    proj: int
    ctx: int
    att_q: int
    att_k: int
    mlp0: int
    mlp1: int
    scan: int


def _tiles(length, n_ctx):
    cap = lambda n, full: min(n, full)
    return _Tiles(proj=cap(512, length), ctx=cap(256, n_ctx), att_q=cap(2048, length), att_k=cap(512, length),
                  mlp0=cap(1024, length), mlp1=cap(512, length), scan=cap(512, length))


def kernel(x, c, ctx, c_ctx, w_mod, b_mod, norm_mix, norm_mlp, w_mlp_in, w_mlp_out, att_w_in, att_lambda, att_subnorm, mla_q_norm, mla_w_uq, mla_kv_norm, mla_w_ukv, att_w_out, rec_w_in, hgrn_bound_logits, hgrn_out_norm, ssd_conv_w, ssd_conv_b, ssd_a_log, ssd_dt_bias, ssd_skip, ssd_norm, rec_w_out, final_norm):
    b_, length, d = x.shape
    n_ctx = ctx.shape[1]
    assert w_mod.shape[0] == 2 and d == D_MODEL

    cc = jnp.zeros((8, d), F32).at[:b_].set(c).at[b_].set(c_ctx)
    mods = _mod_vectors(cc, w_mod, b_mod)
    mods = jnp.pad(mods.reshape(2, 8, 6, d), ((0, 0), (0, 0), (0, 2), (0, 0)))
    mods_lat = [mods[l, :b_] for l in range(2)]
    mods_ctx = [jnp.broadcast_to(mods[l, b_], (b_, 8, d)) for l in range(2)]
    row = lambda v: v.reshape(1, -1)

    t = _tiles(length, n_ctx)
    tr, trc = t.proj, t.ctx
    w0 = _att_weights(att_w_in[0], mla_q_norm[0], mla_w_uq[0], mla_kv_norm[0], mla_w_ukv[0])
    tables = _rope_tables(length)
    qa, ka, va, qm, km, vm = _proj0(x, mods_lat[0], row(norm_mix[0]), w0, tables, tr)
    qa_c, ka_c, va_c, qm_c, km_c, vm_c = _proj0(ctx, mods_ctx[0], row(norm_mix[0]), w0, None, trc)
    lam_init = 0.8 - 0.6 * math.exp(-0.3 * 0)
    da_extra = (att_lambda[0], row(att_subnorm[0]))
    ya = _attention("da", qa, ka_c, va_c, ka, va, da_extra, t.att_q, t.att_k, lam_init)
    ym = _attention("mla", qm, km_c, vm_c, km, vm, None, t.att_q, t.att_k, lam_init)
    ya_c = _attention("da", qa_c, ka_c, va_c, None, None, da_extra, trc, t.att_k, lam_init)
    ym_c = _attention("mla", qm_c, km_c, vm_c, None, None, None, trc, t.att_k, lam_init)
    wo0 = (att_w_out[0].astype(BF16).reshape(2, 512, d),)
    w1 = w_mlp_in.astype(BF16)
    w2 = w_mlp_out.astype(BF16)
    h_lat = _mix_mlp(_merge_attention, (ya, ym), wo0, x, mods_lat[0], row(norm_mlp[0]), 0, w1, w2, None, t.mlp0)
    h_ctx = _mix_mlp(_merge_attention, (ya_c, ym_c), wo0, ctx, mods_ctx[0], row(norm_mlp[0]), 0, w1, w2, None, trc)

    w1r = _rec_weights(rec_w_in[0], hgrn_bound_logits, ssd_dt_bias[0])
    cw = ssd_conv_w[0]
    cb = row(ssd_conv_b[0])
    q_c, gf_c, gb_c, i_c, _, _, xs_c, bm_c, cm_c, dt_c, dtt_c = _proj1(h_ctx, mods_ctx[1], row(norm_mix[1]), w1r,
                                                                        cw, cb, trc)
    q_l, gf_l, gb_l, i_l, sg_l, sz_l, xs_l, bm_l, cm_l, dt_l, dtt_l = _proj1(h_lat, mods_lat[1], row(norm_mix[1]),
                                                                             w1r, cw, cb, tr)
    rbc, rbl = t.ctx, t.scan
    zs = jnp.zeros((b_, SSD_STATE, SSD_INNER), F32)
    zh = jnp.zeros((b_, HGRN_HEADS, LANES, HGRN_DK), F32)
    _, hs_f, _, hs_b = _ssd_scan(xs_c, bm_c, cm_c, dt_c, dtt_c, ssd_a_log[0], zs, zs, rbc)
    y_f, _, y_b, _ = _ssd_scan(xs_l, bm_l, cm_l, dt_l, dtt_l, ssd_a_log[0], hs_f, hs_b, rbl)
    _, ss_f, _, ss_b = _hgrn_scan(q_c, gf_c, gb_c, i_c, zh, zh, rbc)
    o_f, _, o_b, _ = _hgrn_scan(q_l, gf_l, gb_l, i_l, ss_f, ss_b, rbl)
    skip = row(jnp.repeat(ssd_skip[0], SSD_HEAD_DIM))
    wro = rec_w_out[0].astype(BF16).reshape(2, 512, d)
    return _mix_mlp(_merge_recurrent, (o_f, o_b, y_f, y_b, sg_l, sz_l, xs_l),
                    (row(hgrn_out_norm[0]), skip, row(ssd_norm[0]), wro),
                    h_lat, mods_lat[1], row(norm_mlp[1]), 1, w1, w2, row(final_norm), t.mlp1)
```

```python
import functools
import math
from typing import NamedTuple

import numpy as np
import jax
import jax.numpy as jnp
from jax import lax
from jax.experimental import pallas as pl
from jax.experimental.pallas import tpu as pltpu

F32 = jnp.float32
BF16 = jnp.bfloat16

D_MODEL = 1024
GRID_W = 64
DA_HEADS = 4
DA_HEAD_DIM = 64
MLA_HEADS = 8
MLA_NOPE = 64
MLA_ROPE = 32
MLA_V = 64
MLA_Q_RANK = 384
MLA_KV_RANK = 256
HGRN_HEADS = 4
HGRN_DK = 128
SSD_HEADS = 8
SSD_HEAD_DIM = 64
SSD_GROUPS = 2
SSD_STATE = 128
SSD_CONV_W = 5
SSD_INNER = SSD_HEADS * SSD_HEAD_DIM
SSD_CONV_CH = SSD_INNER + 2 * SSD_GROUPS * SSD_STATE
MLP_HIDDEN = 4 * D_MODEL
ROPE_BASE = 10000.0
NORM_EPS = 1e-6
CHUNK = 64
SSD_CHUNK = 256

LANES = 128
HALO = 16
LOG2E = 1.4426950408889634
VMEM_LIMIT = 56 * 1024 * 1024


def _cparams(sem):
    return pltpu.CompilerParams(dimension_semantics=sem, vmem_limit_bytes=VMEM_LIMIT)


def _rms(x):
    return x * lax.rsqrt(jnp.mean(x * x, axis=-1, keepdims=True) + NORM_EPS)


def _modulate(x, g, shift, scale):
    return (_rms(x) * g) * (1.0 + scale) + shift


def _silu(x):
    return x * jax.nn.sigmoid(x)


def _dot_nt(a, b):
    return lax.dot_general(a, b, (((1,), (1,)), ((), ())), preferred_element_type=F32)


def _dot_tn(a, b):
    return lax.dot_general(a, b, (((0,), (0,)), ((), ())), preferred_element_type=F32)


def _split2(x):
    hi = x.astype(BF16)
    lo = (x - hi.astype(F32)).astype(BF16)
    return hi, lo


def _exact_left(m01, x):
    hi, lo = _split2(x)
    return (jnp.dot(m01, hi, preferred_element_type=F32) + jnp.dot(m01, lo, preferred_element_type=F32))


def _exact_right(x, m01):
    hi, lo = _split2(x)
    return (jnp.dot(hi, m01, preferred_element_type=F32) + jnp.dot(lo, m01, preferred_element_type=F32))


def _mod_kernel(c_ref, w_ref, b_ref, o_ref):
    a = _silu(c_ref[...]).astype(BF16)
    o_ref[...] = jnp.dot(a, w_ref[...].astype(BF16), preferred_element_type=F32) + b_ref[...]


def _mod_vectors(cc, w_mod, b_mod):
    depth, d, n = w_mod.shape
    tn = 1024
    return pl.pallas_call(
        _mod_kernel,
        grid=(depth, n // tn),
        in_specs=[
            pl.BlockSpec((8, d), lambda l, j: (0, 0)),
            pl.BlockSpec((None, d, tn), lambda l, j: (l, 0, j)),
            pl.BlockSpec((None, 1, tn), lambda l, j: (l, 0, j)),
        ],
        out_specs=pl.BlockSpec((None, 8, tn), lambda l, j: (l, 0, j)),
        out_shape=jax.ShapeDtypeStruct((depth, 8, n), F32),
        compiler_params=_cparams(("arbitrary", "arbitrary")),
        name="mod_vectors",
    )(cc, w_mod, b_mod.reshape(depth, 1, n))


def _rope_partner(x):
    blocks = [pltpu.roll(x[:, i * LANES:(i + 1) * LANES], LANES // 2, 1) for i in range(x.shape[-1] // LANES)]
    return blocks[0] if len(blocks) == 1 else jnp.concatenate(blocks, axis=1)


def _tile_lanes(x, n):
    return jnp.concatenate([x] * n, axis=-1)


def _proj0_kernel(use_rope, *refs):
    if use_rope:
        (h_ref, mod_ref, g_ref, win_ref, wva_ref, qn_ref, wuq_ref, kvn_ref, wk_ref, wv_ref,
         ca_ref, sa_ref, cq_ref, sq_ref,
         qa_ref, ka_ref, va_ref, qm_ref, km_ref, vm_ref) = refs
    else:
        (h_ref, mod_ref, g_ref, win_ref, wva_ref, qn_ref, wuq_ref, kvn_ref, wk_ref, wv_ref,
         qa_ref, ka_ref, va_ref, qm_ref, km_ref, vm_ref) = refs
    mod = mod_ref[...]
    ub = _modulate(h_ref[...], g_ref[...], mod[0:1], mod[1:2]).astype(BF16)

    def proj(lo, hi):
        return jnp.dot(ub, win_ref[:, lo:hi], preferred_element_type=F32)

    qa = proj(0, 512) * (DA_HEAD_DIM ** -0.5 * LOG2E)
    ka = proj(512, 1024)
    va_ref[...] = _dot_nt(wva_ref[...], ub).astype(BF16)
    cq = proj(1024, 1408)
    ckv = proj(1408, 1664)
    kr = proj(1664, 1792)
    cqn = (_rms(cq) * qn_ref[...]).astype(BF16)
    qm = jnp.dot(cqn, wuq_ref[...], preferred_element_type=F32) * ((MLA_NOPE + MLA_ROPE) ** -0.5 * LOG2E)
    ckvn = (_rms(ckv) * kvn_ref[...]).astype(BF16)
    kn = jnp.dot(ckvn, wk_ref[...], preferred_element_type=F32)
    vm_ref[...] = _dot_nt(wv_ref[...], ckvn).astype(BF16)
    if use_rope:
        ca = _tile_lanes(ca_ref[...], DA_HEADS)
        sa = _tile_lanes(sa_ref[...], DA_HEADS)
        qa = qa * ca + _rope_partner(qa) * sa
        ka = ka * ca + _rope_partner(ka) * sa
        cq1 = cq_ref[...]
        sq1 = sq_ref[...]
        qm = qm * _tile_lanes(cq1, MLA_HEADS) + _rope_partner(qm) * _tile_lanes(sq1, MLA_HEADS)
        kr = kr * cq1 + _rope_partner(kr) * sq1
    qa_ref[...] = qa.astype(BF16)
    ka_ref[...] = ka.astype(BF16)
    qm_ref[...] = qm.astype(BF16)
    km_ref[...] = (kn + _tile_lanes(kr, MLA_HEADS)).astype(BF16)


def _proj0(h, mods, gain, w, tables, tr):
    g_, r_, d = h.shape
    use_rope = tables is not None
    row = lambda c: pl.BlockSpec((None, tr, c), lambda b, t: (b, t, 0))
    full = lambda a: pl.BlockSpec(a.shape, lambda b, t: (0,) * a.ndim)
    in_specs = [row(d), pl.BlockSpec((None, 8, d), lambda b, t: (b, 0, 0)), full(gain),
                full(w["win"]), full(w["wva"]), full(w["qn"]), full(w["wuq"]), full(w["kvn"]), full(w["wk"]),
                full(w["wv"])]
    args = [h, mods, gain, w["win"], w["wva"], w["qn"], w["wuq"], w["kvn"], w["wk"], w["wv"]]
    if use_rope:
        in_specs += [pl.BlockSpec((tr, LANES), lambda b, t: (t, 0))] * 4
        args += list(tables)
    col = pl.BlockSpec((None, 512, tr), lambda b, t: (b, 0, t))
    widths = (512, 512, None, 1024, 1024, None)
    return pl.pallas_call(
        functools.partial(_proj0_kernel, use_rope),
        grid=(g_, r_ // tr),
        in_specs=in_specs,
        out_specs=[col if c is None else row(c) for c in widths],
        out_shape=[jax.ShapeDtypeStruct((g_, 512, r_) if c is None else (g_, r_, c), BF16) for c in widths],
        compiler_params=_cparams(("arbitrary", "arbitrary")),
        name="proj0_rope" if use_rope else "proj0_ctx",
    )(*args)


def _attn_kernel(mode, has_lat, tk, nk, lam_init, *refs):
    refs = list(refs)
    q_ref, kc_ref, vc_ref = refs[:3]
    refs = refs[3:]
    if has_lat:
        kl_ref, vl_ref = refs[:2]
        refs = refs[2:]
    if mode == "da":
        lam_ref, sub_ref = refs[:2]
        refs = refs[2:]
    o_ref, ma_ref, la_ref, acca_ref, mb_ref, lb_ref, accb_ref = refs[:7]
    if has_lat:
        s_ref, mx_ref = refs[7:]

    q = q_ref[...]
    if mode == "da":
        lane = lax.broadcasted_iota(jnp.int32, q.shape, 1)
        zero = jnp.zeros_like(q)
        first = (lane % DA_HEAD_DIM) < DA_HEAD_DIM // 2
        qs = (jnp.where(first, q, zero), jnp.where(first, zero, q))
    else:
        qs = (q[:, :LANES], q[:, LANES:])
    stats = ((ma_ref, la_ref, acca_ref), (mb_ref, lb_ref, accb_ref))
    for m_ref, l_ref, acc_ref in stats:
        m_ref[...] = jnp.full(m_ref.shape, -jnp.inf, F32)
        l_ref[...] = jnp.zeros(l_ref.shape, F32)
        acc_ref[...] = jnp.zeros(acc_ref.shape, F32)

    def scores(k):
        ks = (k, k) if mode == "da" else (k[:, :LANES], k[:, LANES:])
        return [_dot_nt(ks[i], qs[i]) for i in range(2)]

    def softmax(i, s):
        m_ref, l_ref, _ = stats[i]
        m_prev = m_ref[...]
        m_new = jnp.maximum(m_prev, jnp.max(s, axis=0, keepdims=True))
        alpha = jnp.exp2(m_prev - m_new)
        p = jnp.exp2(s - m_new)
        l_ref[...] = alpha * l_ref[...] + jnp.sum(p, axis=0, keepdims=True)
        m_ref[...] = m_new
        return p.astype(BF16), alpha

    def accumulate(i, vt, p, alpha):
        acc_ref = stats[i][2]
        acc_ref[...] = alpha * acc_ref[...] + jnp.dot(vt, p, preferred_element_type=F32)

    if not has_lat:
        ss = scores(kc_ref[...])
        pa = [softmax(i, ss[i]) for i in range(2)]
        for i in range(2):
            accumulate(i, vc_ref[...], *pa[i])
    else:
        nc = kc_ref.shape[0]

        def chunk(c):
            return pl.ds(c * tk if isinstance(c, int) else pl.multiple_of(c * tk, tk), tk)

        def keys(st):
            return kc_ref[...] if isinstance(st, int) and st == 0 else kl_ref[chunk(st - 1), :]

        def vals(st):
            return vc_ref[...] if isinstance(st, int) and st == 0 else vl_ref[:, chunk(st - 1)]

        def lookahead(st, slot, n):
            ss = scores(keys(st))
            for i in range(2):
                s_ref[slot, i, 0:n, :] = ss[i]
                mx_ref[1, i] = jnp.maximum(mx_ref[0, i], jnp.max(ss[i], axis=0, keepdims=True))

        def consume(st, slot, n):
            vt = vals(st)
            for i in range(2):
                m_ref, l_ref, acc_ref = stats[i]
                m_cur = mx_ref[0, i]
                alpha = jnp.exp2(m_ref[...] - m_cur)
                p = jnp.exp2(s_ref[slot, i, 0:n, :] - m_cur)
                l_ref[...] = alpha * l_ref[...] + jnp.sum(p, axis=0, keepdims=True)
                acc_ref[...] = alpha * acc_ref[...] + jnp.dot(vt, p.astype(BF16), preferred_element_type=F32)
                m_ref[...] = m_cur

        def advance():
            for i in range(2):
                mx_ref[0, i] = mx_ref[1, i]

        def run(st, slot, has_next):
            n = nc if isinstance(st, int) and st == 0 else tk
            if has_next:
                lookahead(st + 1, 1 - slot, tk)
            consume(st, slot, n)
            if has_next:
                advance()

        for i in range(2):
            mx_ref[0, i] = jnp.full((1, mx_ref.shape[-1]), -jnp.inf, F32)
        lookahead(0, 0, nc)
        advance()
        run(0, 0, True)
        npair = (nk - 1) // 2

        def body(j, carry):
            run(2 * j + 1, 1, True)
            run(2 * j + 2, 0, True)
            return carry

        lax.fori_loop(0, npair, body, 0)
        for st in range(2 * npair + 1, nk + 1):
            run(st, st % 2, st < nk)

    oa = acca_ref[...] / la_ref[...]
    ob = accb_ref[...] / lb_ref[...]
    if mode == "da":
        lp = lam_ref[...]
        lam = (jnp.exp(jnp.sum(lp[0:1] * lp[1:2], axis=1, keepdims=True))
               - jnp.exp(jnp.sum(lp[2:3] * lp[3:4], axis=1, keepdims=True)) + lam_init)
        dlt = (oa - lam * ob).T
        o_ref[...] = ((_rms(dlt) * sub_ref[...]) * (1.0 - lam_init)).astype(o_ref.dtype)
    else:
        chan = lax.broadcasted_iota(jnp.int32, oa.shape, 0)
        o_ref[...] = jnp.where(chan < MLA_V, oa, ob).T.astype(o_ref.dtype)


def _attention(mode, q, kc, vc, kl, vl, extra, tq, tk, lam_init):
    b_, sq, _ = q.shape
    has_lat = kl is not None
    w = LANES if mode == "da" else 2 * LANES
    nh = 4
    kblk = lambda a: pl.BlockSpec((None, a.shape[1], w), lambda b, h, t: (b, 0, h))
    vblk = lambda a: pl.BlockSpec((None, LANES, a.shape[2]), lambda b, h, t: (b, h, 0))
    in_specs = [pl.BlockSpec((None, tq, w), lambda b, h, t: (b, t, h)), kblk(kc), vblk(vc)]
    args = [q, kc, vc]
    nk = 0
    if has_lat:
        in_specs += [kblk(kl), vblk(vl)]
        args += [kl, vl]
        nk = kl.shape[1] // tk
    if mode == "da":
        lam_p, subnorm = extra
        in_specs += [pl.BlockSpec(lam_p.shape, lambda b, h, t: (0, 0)),
                     pl.BlockSpec(subnorm.shape, lambda b, h, t: (0, 0))]
        args += [lam_p, subnorm]
    scratch = []
    for _ in range(2):
        scratch += [pltpu.VMEM((1, tq), F32), pltpu.VMEM((1, tq), F32), pltpu.VMEM((LANES, tq), F32)]
    if has_lat:
        scratch += [pltpu.VMEM((2, 2, tk, tq), F32), pltpu.VMEM((2, 2, 1, tq), F32)]
    return pl.pallas_call(
        functools.partial(_attn_kernel, mode, has_lat, tk, nk, lam_init),
        grid=(b_, nh, sq // tq),
        in_specs=in_specs,
        out_specs=pl.BlockSpec((None, tq, LANES), lambda b, h, t: (b, t, h)),
        out_shape=jax.ShapeDtypeStruct((b_, sq, nh * LANES), BF16),
        scratch_shapes=scratch,
        compiler_params=_cparams(("arbitrary", "arbitrary", "arbitrary")),
        name=f"attn_{mode}_{'lat' if has_lat else 'ctx'}",
    )(*args)


def _merge_attention(ya_ref, ym_ref, w_ref):
    return (jnp.dot(ya_ref[...], w_ref[0], preferred_element_type=F32)
            + jnp.dot(ym_ref[...], w_ref[1], preferred_element_type=F32))


def _merge_recurrent(of_ref, ob_ref, yf_ref, yb_ref, sg_ref, sz_ref, xs_ref, on_ref, sk_ref, sn_ref, w_ref):
    o = of_ref[...].astype(F32) + ob_ref[...].astype(F32)
    on = on_ref[...]
    parts = []
    for hh in range(HGRN_HEADS):
        ls = slice(hh * LANES, (hh + 1) * LANES)
        parts.append(_rms(o[:, ls]) * on[:, ls])
    o = jnp.concatenate(parts, axis=1) * sg_ref[...].astype(F32)
    y = ((yf_ref[...].astype(F32) + yb_ref[...].astype(F32) + sk_ref[...] * xs_ref[...])
         * sz_ref[...].astype(F32))
    sn = sn_ref[...]
    gw = SSD_INNER // SSD_GROUPS
    parts = []
    for gg in range(SSD_GROUPS):
        ls = slice(gg * gw, (gg + 1) * gw)
        parts.append(_rms(y[:, ls]) * sn[:, ls])
    y = jnp.concatenate(parts, axis=1)
    return (jnp.dot(o.astype(BF16), w_ref[0], preferred_element_type=F32)
            + jnp.dot(y.astype(BF16), w_ref[1], preferred_element_type=F32))


def _mix_mlp_kernel(merge, n_mix, final, *refs):
    mix_refs = refs[:n_mix]
    if final:
        h_ref, mod_ref, g_ref, w1_ref, w2_ref, fg_ref, o_ref = refs[n_mix:]
    else:
        h_ref, mod_ref, g_ref, w1_ref, w2_ref, o_ref = refs[n_mix:]
    h1 = h_ref[...] + mod_ref[2:3, :] * merge(*mix_refs)
    u = _modulate(h1, g_ref[...], mod_ref[3:4, :], mod_ref[4:5, :]).astype(BF16)
    a = jnp.dot(u, w1_ref[...], preferred_element_type=F32)
    a = jnp.square(jnp.maximum(a, 0.0)).astype(BF16)
    out = h1 + mod_ref[5:6, :] * jnp.dot(a, w2_ref[...], preferred_element_type=F32)
    if final:
        out = _rms(out) * fg_ref[...]
    o_ref[...] = out


def _mix_mlp(merge, mix_rows, mix_full, h, mods, gain, layer, w1, w2, final_gain, tr):
    g_, r_, d = h.shape
    final = final_gain is not None
    row = lambda c: pl.BlockSpec((None, tr, c), lambda b, t: (b, t, 0))
    full = lambda a: pl.BlockSpec(a.shape, lambda b, t: (0,) * a.ndim)
    resident = lambda a: pl.BlockSpec((None,) + a.shape[1:], lambda b, t: (layer, 0, 0),
                                      pipeline_mode=pl.Buffered(1))
    in_specs = ([row(a.shape[-1]) for a in mix_rows] + [full(a) for a in mix_full]
                + [row(d), pl.BlockSpec((None, 8, d), lambda b, t: (b, 0, 0)), full(gain),
                   resident(w1), resident(w2)])
    args = list(mix_rows) + list(mix_full) + [h, mods, gain, w1, w2]
    if final:
        in_specs.append(full(final_gain))
        args.append(final_gain)
    return pl.pallas_call(
        functools.partial(_mix_mlp_kernel, merge, len(mix_rows) + len(mix_full), final),
        grid=(g_, r_ // tr),
        in_specs=in_specs,
        out_specs=row(d),
        out_shape=jax.ShapeDtypeStruct(h.shape, F32),
        compiler_params=_cparams(("arbitrary", "arbitrary")),
        name="mix_mlp_final" if final else "mix_mlp",
    )(*args)


def _softplus(x):
    return jnp.maximum(x, 0.0) + jnp.log1p(jnp.exp(-jnp.abs(x)))


def _proj1_kernel(nt, h_ref, hp_ref, hn_ref, mod_ref, g_ref, w_ref, wdt_ref, bl_ref, dtb_ref, cw_ref, cb_ref,
                  q_ref, gf_ref, gb_ref, i_ref, sg_ref, sz_ref, xs_ref, bm_ref, cm_ref, dt_ref, dtt_ref, pad_ref):
    t = pl.program_id(1)
    tr = h_ref.shape[0]
    mod = mod_ref[...]
    gain = g_ref[...]
    ub = _modulate(h_ref[...], gain, mod[0:1], mod[1:2]).astype(BF16)
    u_ext = jnp.concatenate([_modulate(hp_ref[...], gain, mod[0:1], mod[1:2]).astype(BF16), ub,
                             _modulate(hn_ref[...], gain, mod[0:1], mod[1:2]).astype(BF16)], axis=0)
    xbc = jnp.dot(u_ext, w_ref[:, 3072:4096], preferred_element_type=F32)
    ri = lax.broadcasted_iota(jnp.int32, (tr + 2 * HALO, 1), 0)
    inside = ((ri >= HALO) | (t > 0)) & ((ri < tr + HALO) | (t < nt - 1))
    pad_ref[...] = jnp.where(inside, xbc, 0.0)
    half = SSD_CONV_W // 2
    y = jnp.zeros((tr, xbc.shape[1]), F32) + cb_ref[...]
    for j in range(SSD_CONV_W):
        y = y + pad_ref[HALO - half + j:HALO - half + j + tr, :] * cw_ref[j:j + 1, :]
    y = _silu(y)
    xs_ref[...] = y[:, :SSD_INNER]
    bm_ref[...] = y[:, SSD_INNER:SSD_INNER + 256].astype(BF16)
    cm_ref[...] = y[:, SSD_INNER + 256:].astype(BF16)

    def proj(lo, hi):
        return jnp.dot(ub, w_ref[:, lo:hi], preferred_element_type=F32)

    bl = bl_ref[...]
    e = jnp.exp(bl - jnp.max(bl, axis=0, keepdims=True))
    gamma = e / jnp.sum(e, axis=0, keepdims=True)
    lb = (gamma[0:1] + gamma[1:2]) - gamma[0:1]
    q_ref[...] = _silu(proj(0, 512)).astype(BF16)
    for k, out in ((0, gf_ref), (1, gb_ref)):
        lbk = lb[:, 512 * k:512 * (k + 1)]
        f = lbk + (1.0 - lbk) * jax.nn.sigmoid(proj(512 * (k + 1), 512 * (k + 2)))
        out[...] = jnp.log(f)
    i_ref[...] = proj(1536, 2048).astype(BF16)
    sg_ref[...] = _silu(proj(2048, 2560)).astype(BF16)
    sz_ref[...] = _silu(proj(2560, 3072)).astype(BF16)
    dt =_softplus(jnp.dot(ub, wdt_ref[...], preferred_element_type=F32) + dtb_ref[...])
    dt_ref[...] = dt
    dtt_ref[...] = jnp.concatenate([dt[:, :LANES].T[:SSD_HEADS], dt[:, LANES:].T[:SSD_HEADS]], axis=0)


def _proj1(h, mods, gain, w, conv_w, conv_b, tr):
    g_, r_, d = h.shape
    nt = r_ // tr
    nb = tr // HALO
    row = lambda c: pl.BlockSpec((None, tr, c), lambda b, t: (b, t, 0))
    full = lambda a: pl.BlockSpec(a.shape, lambda b, t: (0,) * a.ndim)
    names = ("w", "wdt", "bl", "dtb")
    widths = (512, 512, 512, 512, 512, 512, 512, 256, 256, 2 * LANES)
    dts = (BF16, F32, F32, BF16, BF16, BF16, F32, BF16, BF16, F32)
    return pl.pallas_call(
        functools.partial(_proj1_kernel, nt),
        grid=(g_, nt),
        in_specs=[row(d),
                  pl.BlockSpec((None, HALO, d), lambda b, t: (b, jnp.maximum(t * nb - 1, 0), 0)),
                  pl.BlockSpec((None, HALO, d), lambda b, t: (b, jnp.minimum((t + 1) * nb, nt * nb - 1), 0)),
                  pl.BlockSpec((None, 8, d), lambda b, t: (b, 0, 0)), full(gain)]
        + [full(w[n]) for n in names] + [full(conv_w), full(conv_b)],
        out_specs=[row(c) for c in widths] + [pl.BlockSpec((None, 16, tr), lambda b, t: (b, 0, t))],
        out_shape=[jax.ShapeDtypeStruct((g_, r_, c), t_) for c, t_ in zip(widths, dts)]
        + [jax.ShapeDtypeStruct((g_, 16, r_), F32)],
        scratch_shapes=[pltpu.VMEM((tr + 2 * HALO, SSD_CONV_CH), F32)],
        compiler_params=_cparams(("arbitrary", "arbitrary")),
        name="proj1",
    )(h, h, h, mods, gain, *[w[n] for n in names], conv_w, conv_b)


def _tri(rev, n):
    i = np.arange(n)
    m = i[None, :] >= i[:, None] if rev else i[None, :] <= i[:, None]
    return m


def _ssd_consts(rev, nchunk):
    tri = np.kron(np.eye(nchunk), _tri(rev, SSD_CHUNK).astype(np.float32))
    expand = np.zeros((LANES, SSD_INNER), np.float32)
    for h in range(SSD_HEADS):
        expand[h, h * SSD_HEAD_DIM:(h + 1) * SSD_HEAD_DIM] = 1.0
    return jnp.asarray(tri, BF16), jnp.asarray(tri.T.copy(), BF16), jnp.asarray(expand, BF16)


def _ssd_block_decays(rev, dt_ref, dtt_ref, an, ant, tri, trit):
    dir_ = 1 if rev else 0
    dt = dt_ref[:, dir_ * LANES:(dir_ + 1) * LANES]
    dtt = dtt_ref[8 * dir_:8 * dir_ + 8, :]
    acol = _exact_left(tri, dt * an)
    arow = _exact_right(dtt * ant, trit)
    return dt, acol, arow


def _expand_heads(vals, ex):
    rows = vals[0].shape[0]
    parts = [p for v in vals for p in _split2(v)]
    big = jnp.dot(jnp.concatenate(parts, axis=0), ex, preferred_element_type=F32)
    return [big[2 * i * rows:(2 * i + 1) * rows] + big[(2 * i + 1) * rows:(2 * i + 2) * rows]
            for i in range(len(vals))]


def _ssd_chunk(rev, r0, x_ref, bm_ref, cm_ref, decays, h_ref, y_ref):
    n = SSD_CHUNK
    ti = lax.broadcasted_iota(jnp.int32, (n, n), 0)
    si = lax.broadcasted_iota(jnp.int32, (n, n), 1)
    mask = (si >= ti) if rev else (si <= ti)
    lane = lax.broadcasted_iota(jnp.int32, (n, LANES), 1)
    last = 0 if rev else n - 1
    x = x_ref[r0:r0 + n, :]
    bm = bm_ref[r0:r0 + n, :]
    cm = cm_ref[r0:r0 + n, :]
    acol = decays[0][r0:r0 + n, :]
    arow = decays[1][:, r0:r0 + n]
    dte = decays[2][r0:r0 + n, :]
    ae = decays[3][r0:r0 + n, :]
    ae_last = ae[last:last + 1, :]
    xdt = x * dte
    xw = (xdt * jnp.exp(ae_last - ae)).astype(BF16)
    xdtb = xdt.astype(BF16)
    ys = []
    for g in range(SSD_GROUPS):
        bg = bm[:, g * SSD_STATE:(g + 1) * SSD_STATE]
        cg = cm[:, g * SSD_STATE:(g + 1) * SSD_STATE]
        cb = _dot_nt(cg, bg)
        hs = h_ref[:, g * 256:(g + 1) * 256]
        yoff = jnp.dot(cg, hs.astype(BF16), preferred_element_type=F32) * jnp.exp(ae[:, g * 256:(g + 1) * 256])
        for pr in range(2):
            xp = xdtb[:, g * 256 + pr * LANES:g * 256 + (pr + 1) * LANES]
            res = []
            for k in range(2):
                hd = g * 4 + pr * 2 + k
                dmat = acol[:, hd:hd + 1] - arow[hd:hd + 1, :]
                lmat = jnp.exp(jnp.where(mask, dmat, -jnp.inf))
                res.append(jnp.dot((cb * lmat).astype(BF16), xp, preferred_element_type=F32))
            ys.append(jnp.where(lane < SSD_HEAD_DIM, res[0], res[1]) + yoff[:, pr * LANES:(pr + 1) * LANES])
        h_ref[:, g * 256:(g + 1) * 256] = (hs * jnp.exp(ae_last[:, g * 256:(g + 1) * 256])
                                           + _dot_tn(bg, xw[:, g * 256:(g + 1) * 256]))
    y_ref[r0:r0 + n, :] = jnp.concatenate(ys, axis=1).astype(y_ref.dtype)


def _ssd_kernel(nchunk, xf_ref, bmf_ref, cmf_ref, dtf_ref, dttf_ref, xb_ref, bmb_ref, cmb_ref, dtb_ref, dttb_ref,
                an_ref, ant_ref, tri_ref, trit_ref, ex_ref, h0f_ref, h0b_ref,
                yf_ref, hTf_ref, yb_ref, hTb_ref, hf_ref, hb_ref):
    s = pl.program_id(1)

    @pl.when(s == 0)
    def _():
        hf_ref[...] = h0f_ref[...]
        hb_ref[...] = h0b_ref[...]

    ex = ex_ref[...]
    dt_f, acol_f, arow_f = _ssd_block_decays(False, dtf_ref, dttf_ref, an_ref[0], ant_ref[0], tri_ref[0], trit_ref[0])
    dt_b, acol_b, arow_b = _ssd_block_decays(True, dtb_ref, dttb_ref, an_ref[1], ant_ref[1], tri_ref[1], trit_ref[1])
    dte_f, ae_f, dte_b, ae_b = _expand_heads([dt_f, acol_f, dt_b, acol_b], ex)
    dec_f = (acol_f, arow_f, dte_f, ae_f)
    dec_b = (acol_b, arow_b, dte_b, ae_b)
    for j in range(nchunk):
        _ssd_chunk(False, j * SSD_CHUNK, xf_ref, bmf_ref, cmf_ref, dec_f, hf_ref, yf_ref)
        _ssd_chunk(True, (nchunk - 1 - j) * SSD_CHUNK, xb_ref, bmb_ref, cmb_ref, dec_b, hb_ref, yb_ref)

    @pl.when(s == pl.num_programs(1) - 1)
    def _():
        hTf_ref[...] = hf_ref[...]
        hTb_ref[...] = hb_ref[...]


def _ssd_scan(xs, bm, cm, dt, dtt, a_log, h0f, h0b, rb):
    b_, r_, _ = xs.shape
    nblk = r_ // rb
    trif, tritf, ex = _ssd_consts(False, rb // SSD_CHUNK)
    trib, tritb, _ = _ssd_consts(True, rb // SSD_CHUNK)
    tri = jnp.stack([trif, trib])
    trit = jnp.stack([tritf, tritb])
    a_neg = -jnp.exp(a_log.astype(F32))
    an = jnp.zeros((2, 1, LANES), F32).at[:, 0, :SSD_HEADS].set(a_neg)
    ant = a_neg.reshape(2, SSD_HEADS, 1)
    rowf = lambda c: pl.BlockSpec((None, rb, c), lambda b, s: (b, s, 0))
    rowb = lambda c: pl.BlockSpec((None, rb, c), lambda b, s: (b, nblk - 1 - s, 0))
    full = lambda a: pl.BlockSpec(a.shape, lambda b, s: (0,) * a.ndim)
    st = pl.BlockSpec((None, SSD_STATE, SSD_INNER), lambda b, s: (b, 0, 0))
    y_shape = jax.ShapeDtypeStruct((b_, r_, SSD_INNER), BF16)
    h_shape = jax.ShapeDtypeStruct((b_, SSD_STATE, SSD_INNER), F32)
    return pl.pallas_call(
        functools.partial(_ssd_kernel, rb // SSD_CHUNK),
        grid=(b_, nblk),
        in_specs=[rowf(512), rowf(256), rowf(256), rowf(2 * LANES),
                  pl.BlockSpec((None, 16, rb), lambda b, s: (b, 0, s)),
                  rowb(512), rowb(256), rowb(256), rowb(2 * LANES),
                  pl.BlockSpec((None, 16, rb), lambda b, s: (b, 0, nblk - 1 - s)),
                  full(an), full(ant), full(tri), full(trit), full(ex), st, st],
        out_specs=[rowf(512), st, rowb(512), st],
        out_shape=[y_shape, h_shape, y_shape, h_shape],
        scratch_shapes=[pltpu.VMEM((SSD_STATE, SSD_INNER), F32)] * 2,
        compiler_params=_cparams(("arbitrary", "arbitrary")),
        name="ssd_scan",
    )(xs, bm, cm, dt, dtt, xs, bm, cm, dt, dtt, an, ant, tri, trit, ex, h0f, h0b)


_HGRN_LEVELS = (64, 32, 16, 8, 4, 2, 1)


def _hgrn_consts(rev):
    i = np.arange(CHUNK)
    before = _tri(rev, CHUNK)
    after_strict = ~before
    mats, masks = [], []
    for c in _HGRN_LEVELS:
        same = (i[:, None] // c) == (i[None, :] // c)
        q_side = same & before
        k_side = same & after_strict
        if c == CHUNK:
            mats += [q_side, k_side]
        else:
            blk = i // c
            first = (blk % 2 == 1) if rev else (blk % 2 == 0)
            if c > 1:
                mats.append(np.where(first[:, None], k_side, q_side))
            pair = first[None, :] & (~first)[:, None] & ((blk[:, None] // 2) == (blk[None, :] // 2))
            masks.append(pair)
    masks.append(np.eye(CHUNK, dtype=bool))
    m = np.concatenate(mats, axis=0).astype(np.float32)
    m = np.concatenate([m, m], axis=1)
    mk = np.stack(masks, axis=0).astype(np.float32)
    return jnp.asarray(m, BF16), jnp.asarray(mk, F32)


def _hgrn_chunk(rev, r0, q_ref, g_ref, v_ref, mall, mk_ref, s_ref, o_ref):
    nl = len(_HGRN_LEVELS)
    last = 0 if rev else CHUNK - 1
    g = g_ref[r0:r0 + CHUNK, :]
    g2 = jnp.concatenate(_split2(g), axis=0)
    eall = jnp.exp(jnp.dot(mall, g2, preferred_element_type=F32))
    ff = jnp.exp(g)
    kk = 1.0 - ff
    qq = q_ref[r0:r0 + CHUNK, :].astype(F32)
    vv = v_ref[r0:r0 + CHUNK, :]
    outs = []
    for h in range(HGRN_HEADS):
        ls = slice(h * LANES, (h + 1) * LANES)
        qh, kh, vh = qq[:, ls], kk[:, ls], vv[:, ls]
        khb = kh.astype(BF16)

        def fac(idx, ls=ls):
            return eall[idx * CHUNK:(idx + 1) * CHUNK, ls]

        q01 = jnp.concatenate([qh, qh * ff[:, ls]], axis=0).astype(BF16)
        a01 = _dot_nt(q01, khb)
        att = a01[:CHUNK] * mk_ref[nl - 1] + a01[CHUNK:] * mk_ref[nl - 2]
        for l in range(1, nl - 1):
            e = fac(l + 1)
            att = att + _dot_nt((qh * e).astype(BF16), (kh * e).astype(BF16)) * mk_ref[l - 1]
        st = s_ref[h]
        qs_ = (qh * fac(0)).astype(BF16)
        o = _dot_nt(qs_, st.astype(BF16)) + jnp.dot(att.astype(BF16), vh, preferred_element_type=F32)
        outs.append(o)
        ks_ = (kh * fac(1)).astype(BF16)
        dec = fac(0)[last:last + 1, :]
        s_ref[h] = st * dec + _dot_tn(vh, ks_)
    o_ref[r0:r0 + CHUNK, :] = jnp.concatenate(outs, axis=1).astype(o_ref.dtype)


def _hgrn_kernel(nchunk, qf_ref, gf_ref, vf_ref, qb_ref, gb_ref, vb_ref, mf_ref, mkf_ref, mb_ref, mkb_ref,
                 s0f_ref, s0b_ref, of_ref, sTf_ref, ob_ref, sTb_ref, sf_ref, sb_ref):
    s = pl.program_id(1)

    @pl.when(s == 0)
    def _():
        sf_ref[...] = s0f_ref[...]
        sb_ref[...] = s0b_ref[...]

    mallf = mf_ref[...]
    mallb = mb_ref[...]
    for j in range(nchunk):
        _hgrn_chunk(False, j * CHUNK, qf_ref, gf_ref, vf_ref, mallf, mkf_ref, sf_ref, of_ref)
        _hgrn_chunk(True, (nchunk - 1 - j) * CHUNK, qb_ref, gb_ref, vb_ref, mallb, mkb_ref, sb_ref, ob_ref)

    @pl.when(s == pl.num_programs(1) - 1)
    def _():
        sTf_ref[...] = sf_ref[...]
        sTb_ref[...] = sb_ref[...]


def _hgrn_scan(q, gf, gb, v, s0f, s0b, rb):
    b_, r_, _ = q.shape
    nblk = r_ // rb
    mallf, mkf = _hgrn_consts(False)
    mallb, mkb = _hgrn_consts(True)
    rowf = pl.BlockSpec((None, rb, 512), lambda b, s: (b, s, 0))
    rowb = pl.BlockSpec((None, rb, 512), lambda b, s: (b, nblk - 1 - s, 0))
    full = lambda a: pl.BlockSpec(a.shape, lambda b, s: (0,) * a.ndim)
    st = pl.BlockSpec((None, HGRN_HEADS, LANES, HGRN_DK), lambda b, s: (b, 0, 0, 0))
    o_shape = jax.ShapeDtypeStruct((b_, r_, 512), BF16)
    s_shape = jax.ShapeDtypeStruct((b_, HGRN_HEADS, LANES, HGRN_DK), F32)
    return pl.pallas_call(
        functools.partial(_hgrn_kernel, rb // CHUNK),
        grid=(b_, nblk),
        in_specs=[rowf, rowf, rowf, rowb, rowb, rowb, full(mallf), full(mkf), full(mallb), full(mkb), st, st],
        out_specs=[rowf, st, rowb, st],
        out_shape=[o_shape, s_shape, o_shape, s_shape],
        scratch_shapes=[pltpu.VMEM((HGRN_HEADS, LANES, HGRN_DK), F32)] * 2,
        compiler_params=_cparams(("arbitrary", "arbitrary")),
        name="hgrn_scan",
    )(q, gf, v, q, gb, v, mallf, mkf, mallb, mkb, s0f, s0b)


def _rope_tables(length):
    rows = length // GRID_W
    row = jnp.repeat(jnp.arange(rows, dtype=F32), GRID_W)
    col = jnp.tile(jnp.arange(GRID_W, dtype=F32), rows)

    def cs(rot_dim):
        n_freq = rot_dim // 4
        inv_freq = ROPE_BASE ** (-jnp.arange(n_freq, dtype=F32) / n_freq)
        ang = jnp.concatenate([row[:, None] * inv_freq, col[:, None] * inv_freq], axis=-1)
        return jnp.cos(ang), jnp.sin(ang)

    c, s = cs(DA_HEAD_DIM)
    ca = jnp.concatenate([c, c, c, c], axis=-1)
    sa = jnp.concatenate([-s, -s, s, s], axis=-1)
    c, s = cs(MLA_ROPE)
    one = lambda n: jnp.ones((length, n), F32)
    cqt = jnp.concatenate([c, one(48), c, one(48)], axis=-1)
    sqt = jnp.concatenate([-s, 0.0 * one(48), s, 0.0 * one(48)], axis=-1)
    return ca, sa, cqt, sqt


def _att_weights(att_w_in, mla_q_norm, mla_w_uq, mla_kv_norm, mla_w_ukv):
    d = att_w_in.shape[0]
    r = np.arange

    def place(w, idx):
        cuts = [0] + [j for j in range(1, len(idx))
                      if (idx[j] < 0) != (idx[j - 1] < 0) or (idx[j] >= 0 and idx[j] != idx[j - 1] + 1)]
        parts = []
        for a, b in zip(cuts, cuts[1:] + [len(idx)]):
            parts.append(jnp.zeros(w.shape[:-1] + (b - a,), w.dtype) if idx[a] < 0
                         else w[..., int(idx[a]):int(idx[a]) + (b - a)])
        return jnp.concatenate(parts, axis=-1)

    da_idx = np.concatenate([r(0, 32), r(64, 96), r(32, 64), r(96, 128)])
    mla_q_idx = np.concatenate([r(64, 80), r(0, 48), r(80, 96), r(48, 64), np.full(32, -1)])
    mla_k_idx = np.concatenate([np.full(16, -1), r(0, 48), np.full(16, -1), r(48, 64), np.full(32, -1)])
    kr_idx = np.concatenate([r(0, 16), np.full(48, -1), r(16, 32), np.full(48, -1)])
    heads = lambda w, n: w.reshape(w.shape[0], n, -1)
    qa_w = place(heads(att_w_in[:, :512], DA_HEADS), da_idx).reshape(d, 512)
    ka_w = place(heads(att_w_in[:, 512:1024], DA_HEADS), da_idx).reshape(d, 512)
    kr_blk = place(att_w_in[:, 2176:2208], kr_idx)
    win = jnp.concatenate([qa_w, ka_w, att_w_in[:, 1536:2176], kr_blk], axis=1).astype(BF16)
    wva = att_w_in[:, 1024:1536].T.astype(BF16)
    wq = place(heads(mla_w_uq, MLA_HEADS), mla_q_idx).reshape(MLA_Q_RANK, MLA_HEADS * LANES)
    wkv = heads(mla_w_ukv, MLA_HEADS)
    wk = place(wkv[:, :, :MLA_NOPE], mla_k_idx).reshape(MLA_KV_RANK, MLA_HEADS * LANES)
    wv = wkv[:, :, MLA_NOPE:].reshape(MLA_KV_RANK, MLA_HEADS * MLA_V)
    return dict(win=win, wva=wva, qn=mla_q_norm.reshape(1, -1), wuq=wq.astype(BF16),
                kvn=mla_kv_norm.reshape(1, -1), wk=wk.astype(BF16), wv=wv.T.astype(BF16))


def _rec_weights(rec_w_in, bound_logits, dt_bias):
    d = rec_w_in.shape[0]
    w = rec_w_in.astype(BF16)
    wdt = rec_w_in[:, 4096:4112]
    pad = jnp.zeros((d, LANES - SSD_HEADS), F32)
    wdt_rows = jnp.concatenate([wdt[:, :SSD_HEADS], pad, wdt[:, SSD_HEADS:], pad], axis=1).astype(BF16)
    zb = jnp.zeros((LANES - SSD_HEADS,), F32)
    dtb = jnp.concatenate([dt_bias[0], zb, dt_bias[1], zb]).reshape(1, 2 * LANES)
    return dict(w=w, wdt=wdt_rows, bl=bound_logits, dtb=dtb)


class _Tiles(NamedTuple):
    proj: int
    ctx: int
    att_q: int
    att_k: int
    mlp0: int
    mlp1: int
    scan: int


def _tiles(length, n_ctx):
    cap = lambda n, full: min(n, full)
    return _Tiles(proj=cap(512, length), ctx=cap(256, n_ctx), att_q=cap(2048, length), att_k=cap(512, length),
                  mlp0=cap(1024, length), mlp1=cap(512, length), scan=cap(512, length))


def kernel(x, c, ctx, c_ctx, w_mod, b_mod, norm_mix, norm_mlp, w_mlp_in, w_mlp_out, att_w_in, att_lambda, att_subnorm, mla_q_norm, mla_w_uq, mla_kv_norm, mla_w_ukv, att_w_out, rec_w_in, hgrn_bound_logits, hgrn_out_norm, ssd_conv_w, ssd_conv_b, ssd_a_log, ssd_dt_bias, ssd_skip, ssd_norm, rec_w_out, final_norm):
    b_, length, d = x.shape
    n_ctx = ctx.shape[1]
    assert w_mod.shape[0] == 2 and d == D_MODEL

    cc = jnp.zeros((8, d), F32).at[:b_].set(c).at[b_].set(c_ctx)
    mods = _mod_vectors(cc, w_mod, b_mod)
    mods = jnp.pad(mods.reshape(2, 8, 6, d), ((0, 0), (0, 0), (0, 2), (0, 0)))
    mods_lat = [mods[l, :b_] for l in range(2)]
    mods_ctx = [jnp.broadcast_to(mods[l, b_], (b_, 8, d)) for l in range(2)]
    row = lambda v: v.reshape(1, -1)

    t = _tiles(length, n_ctx)
    tr, trc = t.proj, t.ctx
    w0 = _att_weights(att_w_in[0], mla_q_norm[0], mla_w_uq[0], mla_kv_norm[0], mla_w_ukv[0])
    tables = _rope_tables(length)
    qa, ka, va, qm, km, vm = _proj0(x, mods_lat[0], row(norm_mix[0]), w0, tables, tr)
    qa_c, ka_c, va_c, qm_c, km_c, vm_c = _proj0(ctx, mods_ctx[0], row(norm_mix[0]), w0, None, trc)
    lam_init = 0.8 - 0.6 * math.exp(-0.3 * 0)
    da_extra = (att_lambda[0], row(att_subnorm[0]))
    ya = _attention("da", qa, ka_c, va_c, ka, va, da_extra, t.att_q, t.att_k, lam_init)
    ym = _attention("mla", qm, km_c, vm_c, km, vm, None, t.att_q, t.att_k, lam_init)
    ya_c = _attention("da", qa_c, ka_c, va_c, None, None, da_extra, trc, t.att_k, lam_init)
    ym_c = _attention("mla", qm_c, km_c, vm_c, None, None, None, trc, t.att_k, lam_init)
    wo0 = (att_w_out[0].astype(BF16).reshape(2, 512, d),)
    w1 = w_mlp_in.astype(BF16)
    w2 = w_mlp_out.astype(BF16)
    h_lat = _mix_mlp(_merge_attention, (ya, ym), wo0, x, mods_lat[0], row(norm_mlp[0]), 0, w1, w2, None, t.mlp0)
    h_ctx = _mix_mlp(_merge_attention, (ya_c, ym_c), wo0, ctx, mods_ctx[0], row(norm_mlp[0]), 0, w1, w2, None, trc)

    w1r = _rec_weights(rec_w_in[0], hgrn_bound_logits, ssd_dt_bias[0])
    cw = ssd_conv_w[0]
    cb = row(ssd_conv_b[0])
    q_c, gf_c, gb_c, i_c, _, _, xs_c, bm_c, cm_c, dt_c, dtt_c = _proj1(h_ctx, mods_ctx[1], row(norm_mix[1]), w1r,
                                                                        cw, cb, trc)
    q_l, gf_l, gb_l, i_l, sg_l, sz_l, xs_l, bm_l, cm_l, dt_l, dtt_l = _proj1(h_lat, mods_lat[1], row(norm_mix[1]),
                                                                             w1r, cw, cb, tr)
    rbc, rbl = t.ctx, t.scan
    zs = jnp.zeros((b_, SSD_STATE, SSD_INNER), F32)
    zh = jnp.zeros((b_, HGRN_HEADS, LANES, HGRN_DK), F32)
    _, hs_f, _, hs_b = _ssd_scan(xs_c, bm_c, cm_c, dt_c, dtt_c, ssd_a_log[0], zs, zs, rbc)
    y_f, _, y_b, _ = _ssd_scan(xs_l, bm_l, cm_l, dt_l, dtt_l, ssd_a_log[0], hs_f, hs_b, rbl)
    _, ss_f, _, ss_b = _hgrn_scan(q_c, gf_c, gb_c, i_c, zh, zh, rbc)
    o_f, _, o_b, _ = _hgrn_scan(q_l, gf_l, gb_l, i_l, ss_f, ss_b, rbl)
    skip = row(jnp.repeat(ssd_skip[0], SSD_HEAD_DIM))
    wro = rec_w_out[0].astype(BF16).reshape(2, 512, d)
    return _mix_mlp(_merge_recurrent, (o_f, o_b, y_f, y_b, sg_l, sz_l, xs_l),
                    (row(hgrn_out_norm[0]), skip, row(ssd_norm[0]), wro),
                    h_lat, mods_lat[1], row(norm_mlp[1]), 1, w1, w2, row(final_norm), t.mlp1)
```
